```python
import jax, jax.numpy as jnp
from jax import lax
import numpy as np

D_MODEL = 2048
BATCH = 2
SEQ = 4096
DEPTH = 1

CHUNK = 64
MIX_WIDTH = D_MODEL
CONV_WIDTH = MIX_WIDTH // 2
CONV_GROUPS = 16
CONV_K = 3
GLA_WIDTH = MIX_WIDTH - CONV_WIDTH
GLA_HEADS = 4
GLA_DK = GLA_WIDTH // 2
GLA_HEAD_K = GLA_DK // GLA_HEADS
GLA_HEAD_V = GLA_WIDTH // GLA_HEADS
GLA_RANK = 16
GLA_TAU = 16.0
EPS = 1e-6

SPLIT_SIZES = (CONV_WIDTH, CONV_WIDTH, CONV_WIDTH, CONV_WIDTH,
               GLA_DK, GLA_DK, GLA_WIDTH, GLA_WIDTH, GLA_RANK)
IN_COLS = int(sum(SPLIT_SIZES))
SPLIT_IDX = [int(i) for i in np.cumsum(SPLIT_SIZES)[:-1]]

kernel_name = "hybrid_conv_gla_parallel_heads"


def rms_norm(x, g):
    xf = x.astype(jnp.float32)
    y = xf * lax.rsqrt(jnp.mean(xf * xf, axis=-1, keepdims=True) + EPS)
    return (y * g.astype(jnp.float32)).astype(x.dtype)


def short_conv_branch(h, bg, cg, z, conv_w, conv_b):
    u = cg * h
    s = u.shape[1]
    up = jnp.pad(u, ((0, 0), (CONV_K - 1, 0), (0, 0)))
    y = conv_b + sum(conv_w[j] * up[:, j:j + s] for j in range(CONV_K))
    return bg * y * jax.nn.silu(z)


def gla_branch(q, k, v, r, g_down, w_up, b_gate, norm_g):
    bsz, s, _ = q.shape
    n = s // CHUNK
    f32 = jnp.float32
    qc = q.astype(f32).reshape(bsz, n, CHUNK, GLA_HEADS, GLA_HEAD_K) * (GLA_HEAD_K ** -0.5)
    kc = k.astype(f32).reshape(bsz, n, CHUNK, GLA_HEADS, GLA_HEAD_K)
    vc = v.astype(f32).reshape(bsz, n, CHUNK, GLA_HEADS, GLA_HEAD_V)
    glog = jax.nn.log_sigmoid(g_down.astype(f32) @ w_up.astype(f32) + b_gate.astype(f32)) / GLA_TAU
    glog = glog.reshape(bsz, n, CHUNK, GLA_HEADS, GLA_HEAD_K)
    bcum = jnp.cumsum(glog, axis=2)
    b_end = bcum[:, :, -1]
    k_dec = kc * jnp.exp(b_end[:, :, None] - bcum)
    u = jnp.einsum('bnchk,bnchv->bnhkv', k_dec, vc)
    decay = jnp.exp(b_end)

    def step(state, inp):
        d, du = inp
        state = d[..., None] * state + du
        return state, state

    init = jnp.zeros((bsz, GLA_HEADS, GLA_HEAD_K, GLA_HEAD_V), f32)
    _, s_all = lax.scan(step, init, (jnp.moveaxis(decay, 1, 0), jnp.moveaxis(u, 1, 0)))
    s_all = jnp.moveaxis(s_all, 0, 1)
    o = jnp.einsum('bnchk,bnhkv->bnchv', qc, s_all).reshape(bsz, s, GLA_HEADS, GLA_HEAD_V)
    o = o * lax.rsqrt(jnp.mean(o * o, axis=-1, keepdims=True) + EPS) * norm_g.astype(f32)
    o = o.reshape(bsz, s, GLA_WIDTH).astype(q.dtype)
    return o * jax.nn.silu(r)


def setup_inputs(seed: int = 0) -> dict:
    key = jax.random.key(seed)
    ks = jax.random.split(key, 10)
    f32 = jnp.float32
    x = jax.random.normal(ks[0], (BATCH, SEQ, D_MODEL), f32)
    norm_g = 1.0 + 0.02 * jax.random.normal(ks[1], (DEPTH, D_MODEL), f32)
    w_in = jax.random.normal(ks[2], (DEPTH, D_MODEL, IN_COLS), f32) * D_MODEL ** -0.5
    conv_w = jax.random.normal(ks[3], (DEPTH, CONV_K, CONV_WIDTH), f32) * CONV_K ** -0.5
    conv_b = 0.02 * jax.random.normal(ks[4], (DEPTH, CONV_WIDTH), f32)
    gla_w_up = jax.random.normal(ks[5], (DEPTH, GLA_RANK, GLA_DK), f32) * GLA_RANK ** -0.5
    gla_b_gate = 2.0 + 0.1 * jax.random.normal(ks[6], (DEPTH, GLA_DK), f32)
    gla_norm_g = 1.0 + 0.02 * jax.random.normal(ks[7], (DEPTH, GLA_HEADS, GLA_HEAD_V), f32)
    w_out = jax.random.normal(ks[8], (DEPTH, MIX_WIDTH, D_MODEL), f32) * MIX_WIDTH ** -0.5
    final_g = 1.0 + 0.02 * jax.random.normal(ks[9], (D_MODEL,), f32)
    return {"x": x, "norm_g": norm_g, "w_in": w_in, "conv_w": conv_w, "conv_b": conv_b,
            "gla_w_up": gla_w_up, "gla_b_gate": gla_b_gate, "gla_norm_g": gla_norm_g,
            "w_out": w_out, "final_g": final_g}


def reference(x, norm_g, w_in, conv_w, conv_b, gla_w_up, gla_b_gate, gla_norm_g, w_out, final_g):
    for l in range(DEPTH):
        h = rms_norm(x, norm_g[l])
        proj = h @ w_in[l]
        ch, cb, cc, cz, gq, gk, gv, gr, gd = jnp.split(proj, SPLIT_IDX, axis=-1)
        y_conv = short_conv_branch(ch, cb, cc, cz, conv_w[l], conv_b[l])
        y_gla = gla_branch(gq, gk, gv, gr, gd, gla_w_up[l], gla_b_gate[l], gla_norm_g[l])
        y = jnp.concatenate([y_conv, y_gla], axis=-1)
        x = x + y @ w_out[l]
    return rms_norm(x, final_g)
```

```python
import functools

import jax
import jax.numpy as jnp
from jax import lax
from jax.experimental import pallas as pl
from jax.experimental.pallas import tpu as pltpu

F32 = jnp.float32
BF16 = jnp.bfloat16

LANES = 128
EPS = 1e-6
CHUNK = 64
CONV_K = 3
GLA_HEADS = 4
GLA_TAU = 16.0

IN_TM = 512
IN_N_TILES = 3
MIX_T = 256
OUT_TM = 512
VMEM_LIMIT = 56 * 1024 * 1024


def _silu(v):
    return v * jax.nn.sigmoid(v)


def _log_sigmoid(v):
    return -(jnp.maximum(-v, 0.0) + jnp.log1p(jnp.exp(-jnp.abs(v))))


def _in_proj_kernel(x_ref, g_ref, w_ref, o_ref, h_ref):
    @pl.when(pl.program_id(1) == 0)
    def _():
        x = x_ref[...]
        ms = jnp.mean(x * x, axis=-1, keepdims=True)
        h_ref[...] = (x * lax.rsqrt(ms + EPS) * g_ref[...]).astype(BF16)

    o_ref[...] = jnp.dot(h_ref[...], w_ref[...], preferred_element_type=F32).astype(o_ref.dtype)


def _in_proj(x2d, norm_g, w_in_bf16):
    m, d = x2d.shape
    n = w_in_bf16.shape[1]
    tn = n // IN_N_TILES
    return pl.pallas_call(
        _in_proj_kernel,
        grid=(m // IN_TM, IN_N_TILES),
        in_specs=[
            pl.BlockSpec((IN_TM, d), lambda i, j: (i, 0)),
            pl.BlockSpec((1, d), lambda i, j: (0, 0)),
            pl.BlockSpec((d, tn), lambda i, j: (0, j)),
        ],
        out_specs=pl.BlockSpec((IN_TM, tn), lambda i, j: (i, j)),
        out_shape=jax.ShapeDtypeStruct((m, n), BF16),
        scratch_shapes=[pltpu.VMEM((IN_TM, d), BF16)],
        compiler_params=pltpu.CompilerParams(
            dimension_semantics=("arbitrary", "arbitrary"),
            vmem_limit_bytes=VMEM_LIMIT),
        name="in_proj",
    )(x2d, norm_g, w_in_bf16)


def _mixer_kernel(ch_ref, cb_ref, cc_ref, cz_ref, gq_ref, gk_ref, gv_ref, gr_ref, gd_ref,
                  convw_ref, convb_ref, wup_ref, bgate_ref, gng_ref,
                  y_ref, carry_ref, state_ref, *, conv_width, head_k, head_v):
    t_rows = ch_ref.shape[0]

    @pl.when(pl.program_id(1) == 0)
    def _():
        carry_ref[...] = jnp.zeros_like(carry_ref)
        state_ref[...] = jnp.zeros_like(state_ref)

    u = cc_ref[...].astype(F32) * ch_ref[...].astype(F32)
    carry = carry_ref[...]
    prev1 = carry[7:8, :]
    prev2 = carry[6:7, :]
    row = lax.broadcasted_iota(jnp.int32, u.shape, 0)
    u1 = jnp.where(row == 0, prev1, pltpu.roll(u, 1, 0))
    u2 = jnp.where(row == 0, prev2, jnp.where(row == 1, prev1, pltpu.roll(u, 2, 0)))
    carry_ref[...] = u[t_rows - 8:, :]
    w = convw_ref[...]
    conv = convb_ref[...] + w[0:1, :] * u2 + w[1:2, :] * u1 + w[2:3, :] * u
    y_ref[:, :conv_width] = (cb_ref[...].astype(F32) * conv * _silu(cz_ref[...].astype(F32))
                             ).astype(y_ref.dtype)

    pre = jnp.dot(gd_ref[...], wup_ref[...], preferred_element_type=F32) + bgate_ref[...]
    glog = _log_sigmoid(pre) / GLA_TAU
    r_i = lax.broadcasted_iota(jnp.int32, (t_rows, t_rows), 0)
    c_i = lax.broadcasted_iota(jnp.int32, (t_rows, t_rows), 1)
    later = jnp.where((c_i > r_i) & ((c_i // CHUNK) == (r_i // CHUNK)), 1.0, 0.0).astype(BF16)
    g_hi = glog.astype(BF16)
    g_lo = (glog - g_hi.astype(F32)).astype(BF16)
    rev = (jnp.dot(later, g_hi, preferred_element_type=F32)
           + jnp.dot(later, g_lo, preferred_element_type=F32))
    k_dec = (gk_ref[...].astype(F32) * jnp.exp(rev)).astype(BF16)
    scale = head_k ** -0.5

    for c in range(t_rows // CHUNK):
        r0 = c * CHUNK
        b_end = rev[r0:r0 + 1, :] + glog[r0:r0 + 1, :]
        decay = jnp.exp(b_end)
        for h in range(GLA_HEADS):
            ks = slice(h * head_k, (h + 1) * head_k)
            vs = slice(h * head_v, (h + 1) * head_v)
            kd = k_dec[r0:r0 + CHUNK, ks]
            v = gv_ref[r0:r0 + CHUNK, vs]
            inc = lax.dot_general(v, kd, (((0,), (0,)), ((), ())), preferred_element_type=F32)
            s_new = state_ref[h] * decay[:, ks] + inc
            state_ref[h] = s_new
            q = gq_ref[r0:r0 + CHUNK, ks]
            o = lax.dot_general(q, s_new.astype(BF16), (((1,), (1,)), ((), ())),
                                preferred_element_type=F32) * scale
            ms = jnp.mean(o * o, axis=-1, keepdims=True)
            o = o * lax.rsqrt(ms + EPS) * gng_ref[h:h + 1, :]
            r = gr_ref[r0:r0 + CHUNK, vs].astype(F32)
            y_ref[r0:r0 + CHUNK, conv_width + h * head_v:conv_width + (h + 1) * head_v] = (
                o * _silu(r)).astype(y_ref.dtype)


def _mixers(proj, conv_w, conv_b, w_up_pad, b_gate, gla_norm_g, *, batch, seq, conv_width, dk_total,
            gla_width, gd_col_block):
    t = MIX_T
    nt = seq // t
    head_k = dk_total // GLA_HEADS
    head_v = gla_width // GLA_HEADS
    mix_width = conv_width + gla_width

    def col(width, idx):
        return pl.BlockSpec((t, width), lambda b, s: (b * nt + s, idx))

    def whole(shape):
        return pl.BlockSpec(shape, lambda b, s: (0,) * len(shape))

    cw = conv_width
    in_specs = [
        col(cw, 0), col(cw, 1), col(cw, 2), col(cw, 3),
        col(dk_total, (4 * cw) // dk_total), col(dk_total, (4 * cw) // dk_total + 1),
        col(gla_width, (4 * cw + 2 * dk_total) // gla_width),
        col(gla_width, (4 * cw + 2 * dk_total) // gla_width + 1),
        col(LANES, gd_col_block),
        whole(conv_w.shape), whole(conv_b.shape), whole(w_up_pad.shape), whole(b_gate.shape),
        whole(gla_norm_g.shape),
    ]
    kern = functools.partial(_mixer_kernel, conv_width=cw, head_k=head_k, head_v=head_v)
    return pl.pallas_call(
        kern,
        grid=(batch, nt),
        in_specs=in_specs,
        out_specs=pl.BlockSpec((t, mix_width), lambda b, s: (b * nt + s, 0)),
        out_shape=jax.ShapeDtypeStruct((batch * seq, mix_width), BF16),
        scratch_shapes=[pltpu.VMEM((8, cw), F32),
                        pltpu.VMEM((GLA_HEADS, head_v, head_k), F32)],
        compiler_params=pltpu.CompilerParams(
            dimension_semantics=("arbitrary", "arbitrary"),
            vmem_limit_bytes=VMEM_LIMIT),
        name="mixers",
    )(*([proj] * 9), conv_w, conv_b, w_up_pad, b_gate, gla_norm_g)


def _out_proj_kernel(y_ref, w_ref, x_ref, g_ref, o_ref, *, final_norm):
    z = x_ref[...] + jnp.dot(y_ref[...], w_ref[...], preferred_element_type=F32)
    if final_norm:
        ms = jnp.mean(z * z, axis=-1, keepdims=True)
        z = z * lax.rsqrt(ms + EPS) * g_ref[...]
    o_ref[...] = z


def _out_proj(y, w_out_bf16, x2d, final_g, *, final_norm):
    m, d = x2d.shape
    k = y.shape[1]
    return pl.pallas_call(
        functools.partial(_out_proj_kernel, final_norm=final_norm),
        grid=(m // OUT_TM,),
        in_specs=[
            pl.BlockSpec((OUT_TM, k), lambda i: (i, 0)),
            pl.BlockSpec((k, d), lambda i: (0, 0)),
            pl.BlockSpec((OUT_TM, d), lambda i: (i, 0)),
            pl.BlockSpec((1, d), lambda i: (0, 0)),
        ],
        out_specs=pl.BlockSpec((OUT_TM, d), lambda i: (i, 0)),
        out_shape=jax.ShapeDtypeStruct((m, d), F32),
        compiler_params=pltpu.CompilerParams(
            dimension_semantics=("arbitrary",),
            vmem_limit_bytes=VMEM_LIMIT),
        name="out_proj",
    )(y, w_out_bf16, x2d, final_g)


def kernel(x, norm_g, w_in, conv_w, conv_b, gla_w_up, gla_b_gate, gla_norm_g, w_out, final_g):
    batch, seq, d_model = x.shape
    depth = norm_g.shape[0]
    conv_width = conv_w.shape[2]
    rank, dk_total = gla_w_up.shape[1], gla_w_up.shape[2]
    gla_width = gla_norm_g.shape[1] * gla_norm_g.shape[2]
    in_cols = w_in.shape[2]
    main_cols = in_cols - rank
    assert main_cols % LANES == 0 and rank <= LANES
    n_pad = main_cols + LANES
    assert n_pad % (IN_N_TILES * LANES) == 0

    x2d = x.reshape(batch * seq, d_model)
    for l in range(depth):
        w_in_p = jnp.pad(w_in[l], ((0, 0), (0, n_pad - in_cols))).astype(BF16)
        w_up_p = jnp.pad(gla_w_up[l], ((0, LANES - rank), (0, 0))).astype(BF16)
        proj = _in_proj(x2d, norm_g[l][None, :], w_in_p)
        y = _mixers(proj, conv_w[l], conv_b[l][None, :], w_up_p, gla_b_gate[l][None, :],
                    gla_norm_g[l], batch=batch, seq=seq, conv_width=conv_width,
                    dk_total=dk_total, gla_width=gla_width, gd_col_block=main_cols // LANES)
        x2d = _out_proj(y, w_out[l].astype(BF16), x2d, final_g[None, :],
                        final_norm=(l == depth - 1))
    return x2d.reshape(batch, seq, d_model)
```

```python
import functools

import jax
import jax.numpy as jnp
from jax import lax
from jax.experimental import pallas as pl
from jax.experimental.pallas import tpu as pltpu

F32 = jnp.float32
BF16 = jnp.bfloat16

LANES = 128
EPS = 1e-6
CHUNK = 64
CONV_K = 3
GLA_HEADS = 4
GLA_TAU = 16.0

IN_TM = 512
IN_N_TILES = 3
MIX_T = 256
STRIP = 16
COL_CHUNK = 512
OUT_PIECE = 256
VMEM_LIMIT = 56 * 1024 * 1024


def _silu(v):
    return v * jax.nn.sigmoid(v)


def _log_sigmoid(v):
    return -(jnp.maximum(-v, 0.0) + jnp.log1p(jnp.exp(-jnp.abs(v))))


def _in_proj_kernel(x_ref, g_ref, w_ref, o_ref, h_ref):
    @pl.when(pl.program_id(1) == 0)
    def _():
        x = x_ref[...]
        ms = jnp.mean(x * x, axis=-1, keepdims=True)
        h_ref[...] = (x * lax.rsqrt(ms + EPS) * g_ref[...]).astype(BF16)

    o_ref[...] = jnp.dot(h_ref[...], w_ref[...], preferred_element_type=F32).astype(o_ref.dtype)


def _in_proj(x2d, norm_g, w_in_bf16):
    m, d = x2d.shape
    n = w_in_bf16.shape[1]
    tn = n // IN_N_TILES
    return pl.pallas_call(
        _in_proj_kernel,
        grid=(m // IN_TM, IN_N_TILES),
        in_specs=[
            pl.BlockSpec((IN_TM, d), lambda i, j: (i, 0)),
            pl.BlockSpec((1, d), lambda i, j: (0, 0)),
            pl.BlockSpec((d, tn), lambda i, j: (0, j)),
        ],
        out_specs=pl.BlockSpec((IN_TM, tn), lambda i, j: (i, j)),
        out_shape=jax.ShapeDtypeStruct((m, n), BF16),
        scratch_shapes=[pltpu.VMEM((IN_TM, d), BF16)],
        compiler_params=pltpu.CompilerParams(
            dimension_semantics=("arbitrary", "arbitrary"),
            vmem_limit_bytes=VMEM_LIMIT),
        name="in_proj",
    )(x2d, norm_g, w_in_bf16)


def _mixer_body(first_tile_of_sequence, overlapped_pieces,
                ch_ref, cb_ref, cc_ref, cz_ref, gq_ref, gk_ref, gv_ref, gr_ref, gd_ref,
                convw_ref, convb_ref, wup_ref, bgate_ref, gng_ref,
                y_ref, carry_ref, state_ref, glog_ref, hilo_ref, rev_ref, kdec_ref,
                *, conv_width, head_k, head_v):
    t_rows = ch_ref.shape[0]
    dk_total = gk_ref.shape[1]
    n_chunks = t_rows // CHUNK
    n_strips = t_rows // STRIP

    @pl.when(first_tile_of_sequence)
    def _():
        carry_ref[...] = jnp.zeros_like(carry_ref)
        state_ref[...] = jnp.zeros_like(state_ref)

    def strip_loop(body):
        def step(i, c):
            body(pl.ds(pl.multiple_of(i * STRIP, STRIP), STRIP))
            return c
        lax.fori_loop(0, n_strips, step, 0, unroll=2)

    def conv_strip(rows):
        for c0 in range(0, conv_width, COL_CHUNK):
            cols = slice(c0, c0 + COL_CHUNK)
            u = cc_ref[rows, cols].astype(F32) * ch_ref[rows, cols].astype(F32)
            ext = jnp.concatenate([carry_ref[:, cols], u], axis=0)
            u1 = pltpu.roll(ext, 1, 0)[8:, :]
            u2 = pltpu.roll(ext, 2, 0)[8:, :]
            carry_ref[:, cols] = u[STRIP - 8:, :]
            conv = (convb_ref[:, cols] + convw_ref[0:1, cols] * u2 + convw_ref[1:2, cols] * u1
                    + convw_ref[2:3, cols] * u)
            y_ref[rows, cols] = (cb_ref[rows, cols].astype(F32) * conv
                                 * _silu(cz_ref[rows, cols].astype(F32))).astype(y_ref.dtype)

    strip_loop(conv_strip)

    glog_ref[...] = jnp.dot(gd_ref[...], wup_ref[...], preferred_element_type=F32)

    def gate_strip(rows):
        glog = _log_sigmoid(glog_ref[rows, :] + bgate_ref[...]) / GLA_TAU
        g_hi = glog.astype(BF16)
        hilo_ref[rows, :dk_total] = g_hi
        hilo_ref[rows, dk_total:] = (glog - g_hi.astype(F32)).astype(BF16)

    strip_loop(gate_strip)

    r_i = lax.broadcasted_iota(jnp.int32, (t_rows + 8, t_rows), 0)
    c_i = lax.broadcasted_iota(jnp.int32, (t_rows + 8, t_rows), 1)
    c_chunk = c_i // CHUNK
    ones = ((c_i > r_i) & (c_chunk == r_i // CHUNK)) | (c_chunk == r_i - t_rows)
    sums = jnp.dot(jnp.where(ones, 1.0, 0.0).astype(BF16), hilo_ref[...], preferred_element_type=F32)
    rev_ref[...] = sums[:, :dk_total] + sums[:, dk_total:]

    def kdec_strip(rows):
        kdec_ref[rows, :] = (gk_ref[rows, :].astype(F32) * jnp.exp(rev_ref[rows, :])).astype(BF16)

    strip_loop(kdec_strip)

    n_steps = n_chunks * GLA_HEADS
    piece_at = [(j * n_steps) // len(overlapped_pieces) for j in range(len(overlapped_pieces))]
    scale = head_k ** -0.5
    for c in range(n_chunks):
        rows = slice(c * CHUNK, (c + 1) * CHUNK)
        decay = jnp.exp(rev_ref[t_rows + c:t_rows + c + 1, :])
        for h in range(GLA_HEADS):
            for j, piece in enumerate(overlapped_pieces):
                if piece_at[j] == c * GLA_HEADS + h:
                    piece()
            ks = slice(h * head_k, (h + 1) * head_k)
            vs = slice(h * head_v, (h + 1) * head_v)
            inc = lax.dot_general(gv_ref[rows, vs], kdec_ref[rows, ks], (((0,), (0,)), ((), ())),
                                  preferred_element_type=F32)
            s_new = state_ref[h] * decay[:, ks] + inc
            state_ref[h] = s_new
            o = lax.dot_general(gq_ref[rows, ks], s_new.astype(BF16), (((1,), (1,)), ((), ())),
                                preferred_element_type=F32) * scale
            ms = jnp.mean(o * o, axis=-1, keepdims=True)
            o = o * lax.rsqrt(ms + EPS) * gng_ref[h:h + 1, :]
            y_ref[rows, conv_width + h * head_v:conv_width + (h + 1) * head_v] = (
                o * _silu(gr_ref[rows, vs].astype(F32))).astype(y_ref.dtype)


def _mix_out_kernel(*refs, n_tiles, tiles_per_seq, final_norm, **mixer_dims):
    mixer_in = refs[:14]
    w_out_ref, x_ref, fg_ref, o_ref = refs[14:18]
    y_stage, y_prev, ssq_ref = refs[18:21]
    mixer_scratch = refs[21:]
    s = pl.program_id(0)

    @pl.when(s == 0)
    def _():
        y_prev[...] = jnp.zeros_like(y_prev)

    d_model = o_ref.shape[1]

    def column_piece(c0):
        def piece():
            cols = slice(c0, c0 + OUT_PIECE)
            z = x_ref[:, cols] + jnp.dot(y_prev[...], w_out_ref[:, cols], preferred_element_type=F32)
            o_ref[:, cols] = z
            zz = z * z
            part = zz[:, :LANES]
            for l0 in range(LANES, OUT_PIECE, LANES):
                part = part + zz[:, l0:l0 + LANES]
            ssq_ref[...] = part if c0 == 0 else ssq_ref[...] + part
        return piece

    def normalize():
        ms = jnp.sum(ssq_ref[...], axis=-1, keepdims=True) / d_model
        o_ref[...] = o_ref[...] * lax.rsqrt(ms + EPS) * fg_ref[...]

    pieces = [column_piece(c0) for c0 in range(0, d_model, OUT_PIECE)]
    if final_norm:
        pieces.append(normalize)

    tile = jnp.minimum(s, n_tiles - 1)
    _mixer_body(tile % tiles_per_seq == 0, pieces, *mixer_in, y_stage, *mixer_scratch, **mixer_dims)
    y_prev[...] = y_stage[...]


def _mix_out(proj, conv_w, conv_b, w_up_pad, b_gate, gla_norm_g, w_out_bf16, x2d, final_g, *,
             batch, seq, conv_width, dk_total, gla_width, gd_col_block, final_norm):
    t = MIX_T
    nt = seq // t
    n_tiles = batch * nt
    head_k = dk_total // GLA_HEADS
    head_v = gla_width // GLA_HEADS
    mix_width = conv_width + gla_width
    d_model = x2d.shape[1]

    def col(width, idx):
        return pl.BlockSpec((t, width), lambda s: (jnp.minimum(s, n_tiles - 1), idx))

    def whole(shape):
        return pl.BlockSpec(shape, lambda s: (0,) * len(shape))

    def lagged(width):
        return pl.BlockSpec((t, width), lambda s: (jnp.maximum(s - 1, 0), 0))

    cw = conv_width
    in_specs = [
        col(cw, 0), col(cw, 1), col(cw, 2), col(cw, 3),
        col(dk_total, (4 * cw) // dk_total), col(dk_total, (4 * cw) // dk_total + 1),
        col(gla_width, (4 * cw + 2 * dk_total) // gla_width),
        col(gla_width, (4 * cw + 2 * dk_total) // gla_width + 1),
        col(LANES, gd_col_block),
        whole(conv_w.shape), whole(conv_b.shape), whole(w_up_pad.shape), whole(b_gate.shape),
        whole(gla_norm_g.shape),
        whole(w_out_bf16.shape), lagged(d_model), whole(final_g.shape),
    ]
    kern = functools.partial(_mix_out_kernel, n_tiles=n_tiles, tiles_per_seq=nt, final_norm=final_norm,
                             conv_width=cw, head_k=head_k, head_v=head_v)
    return pl.pallas_call(
        kern,
        grid=(n_tiles + 1,),
        in_specs=in_specs,
        out_specs=lagged(d_model),
        out_shape=jax.ShapeDtypeStruct((batch * seq, d_model), F32),
        scratch_shapes=[pltpu.VMEM((t, mix_width), BF16),
                        pltpu.VMEM((t, mix_width), BF16),
                        pltpu.VMEM((t, LANES), F32),
                        pltpu.VMEM((8, cw), F32),
                        pltpu.VMEM((GLA_HEADS, head_v, head_k), F32),
                        pltpu.VMEM((t, dk_total), F32),
                        pltpu.VMEM((t, 2 * dk_total), BF16),
                        pltpu.VMEM((t + 8, dk_total), F32),
                        pltpu.VMEM((t, dk_total), BF16)],
        compiler_params=pltpu.CompilerParams(
            dimension_semantics=("arbitrary",),
            vmem_limit_bytes=VMEM_LIMIT),
        name="mix_out",
    )(*([proj] * 9), conv_w, conv_b, w_up_pad, b_gate, gla_norm_g, w_out_bf16, x2d, final_g)


def kernel(x, norm_g, w_in, conv_w, conv_b, gla_w_up, gla_b_gate, gla_norm_g, w_out, final_g):
    batch, seq, d_model = x.shape
    depth = norm_g.shape[0]
    conv_width = conv_w.shape[2]
    rank, dk_total = gla_w_up.shape[1], gla_w_up.shape[2]
    gla_width = gla_norm_g.shape[1] * gla_norm_g.shape[2]
    in_cols = w_in.shape[2]
    main_cols = in_cols - rank
    assert main_cols % LANES == 0 and rank <= LANES
    n_pad = main_cols + LANES
    assert n_pad % (IN_N_TILES * LANES) == 0

    x2d = x.reshape(batch * seq, d_model)
    for l in range(depth):
        w_in_p = jnp.pad(w_in[l], ((0, 0), (0, n_pad - in_cols))).astype(BF16)
        w_up_p = jnp.pad(gla_w_up[l], ((0, LANES - rank), (0, 0))).astype(BF16)
        proj = _in_proj(x2d, norm_g[l][None, :], w_in_p)
        x2d = _mix_out(proj, conv_w[l], conv_b[l][None, :], w_up_p, gla_b_gate[l][None, :],
                       gla_norm_g[l], w_out[l].astype(BF16), x2d, final_g[None, :],
                       batch=batch, seq=seq, conv_width=conv_width, dk_total=dk_total,
                       gla_width=gla_width, gd_col_block=main_cols // LANES,
                       final_norm=(l == depth - 1))
    return x2d.reshape(batch, seq, d_model)
```

```python
import functools

import jax
import jax.numpy as jnp
from jax import lax
from jax.experimental import pallas as pl
from jax.experimental.pallas import tpu as pltpu

F32 = jnp.float32
BF16 = jnp.bfloat16

LANES = 128
MXU_N = 256
EPS = 1e-6
CHUNK = 64
GLA_HEADS = 4
GLA_TAU = 16.0

MIX_T = 256
STRIP = 16
COL_CHUNK = 512
PIN_LAG = 2
OUT_TM = 512
VMEM_LIMIT = 56 * 1024 * 1024


def _silu(v):
    return v * jax.nn.sigmoid(v)


def _log_sigmoid(v):
    return -(jnp.maximum(-v, 0.0) + jnp.log1p(jnp.exp(-jnp.abs(v))))


def _spread(units, n_slots):
    out = [[] for _ in range(n_slots)]
    for j, u in enumerate(units):
        out[(j * n_slots) // len(units)].append(u)
    return out


def _proj_mix_kernel(x_ref, ng_ref, w_ref, convw_ref, convb_ref, wup_ref, bgate_ref, gng_ref,
                     yc_ref, yg_ref,
                     h_ref, proj_ref, carry_ref, state_ref, sbf_ref, glog_ref, hilo_ref, rev_ref,
                     kdec_ref, *, n_tiles, tiles_per_seq, conv_width, dk_total, gla_width):
    t_rows = x_ref.shape[0]
    n_strips = t_rows // STRIP
    n_chunks = t_rows // CHUNK
    head_k = dk_total // GLA_HEADS
    head_v = gla_width // GLA_HEADS
    off_h, off_b, off_c, off_z = (i * conv_width for i in range(4))
    off_q = 4 * conv_width
    off_k = off_q + dk_total
    off_v = off_k + dk_total
    off_r = off_v + gla_width
    off_gd = off_r + gla_width
    s = pl.program_id(0)

    def strip_rows(i):
        return slice(i * STRIP, (i + 1) * STRIP)

    def conv_strip(rows):
        token = None
        for c0 in range(0, conv_width, COL_CHUNK):
            cols = slice(c0, c0 + COL_CHUNK)

            def col(off):
                return proj_ref[rows, off + c0:off + c0 + COL_CHUNK].astype(F32)

            u = col(off_c) * col(off_h)
            ext = jnp.concatenate([carry_ref[:, cols], u], axis=0)
            u1 = pltpu.roll(ext, 1, 0)[8:, :]
            u2 = pltpu.roll(ext, 2, 0)[8:, :]
            carry_ref[:, cols] = u[STRIP - 8:, :]
            conv = (convb_ref[:, cols] + convw_ref[0:1, cols] * u2 + convw_ref[1:2, cols] * u1
                    + convw_ref[2:3, cols] * u)
            y = (col(off_b) * conv * _silu(col(off_z))).astype(yc_ref.dtype)
            yc_ref[rows, cols] = y
            bits = pltpu.bitcast(y, jnp.uint32)
            for l0 in range(0, COL_CHUNK, LANES):
                part = bits[:, l0:l0 + LANES]
                token = part if token is None else token | part
        return token

    def gate_pre():
        glog_ref[...] = jnp.dot(proj_ref[:, off_gd:off_gd + LANES], wup_ref[...],
                                preferred_element_type=F32)

    def gate_strip(rows):
        glog = _log_sigmoid(glog_ref[rows, :] + bgate_ref[...]) / GLA_TAU
        g_hi = glog.astype(BF16)
        hilo_ref[rows, :dk_total] = g_hi
        hilo_ref[rows, dk_total:] = (glog - g_hi.astype(F32)).astype(BF16)

    def later_sums():
        r_i = lax.broadcasted_iota(jnp.int32, (t_rows + 8, t_rows), 0)
        c_i = lax.broadcasted_iota(jnp.int32, (t_rows + 8, t_rows), 1)
        c_chunk = c_i // CHUNK
        ones = ((c_i > r_i) & (c_chunk == r_i // CHUNK)) | (c_chunk == r_i - t_rows)
        sums = jnp.dot(jnp.where(ones, 1.0, 0.0).astype(BF16), hilo_ref[...],
                       preferred_element_type=F32)
        rev_ref[...] = sums[:, :dk_total] + sums[:, dk_total:]

    def kdec_strip(rows):
        k = proj_ref[rows, off_k:off_k + dk_total].astype(F32)
        kdec_ref[rows, :] = (k * jnp.exp(rev_ref[rows, :])).astype(BF16)

    def state_update(step):
        c, h = divmod(step, GLA_HEADS)
        rows = slice(c * CHUNK, (c + 1) * CHUNK)
        ks = slice(h * head_k, (h + 1) * head_k)
        decay = jnp.exp(rev_ref[t_rows + c:t_rows + c + 1, ks])
        v = proj_ref[rows, off_v + h * head_v:off_v + (h + 1) * head_v]
        inc = lax.dot_general(v, kdec_ref[rows, ks], (((0,), (0,)), ((), ())),
                              preferred_element_type=F32)
        s_new = state_ref[h] * decay + inc
        state_ref[h] = s_new
        sbf_ref[h] = s_new.astype(BF16)

    def readout(step):
        c, h = divmod(step, GLA_HEADS)
        rows = slice(c * CHUNK, (c + 1) * CHUNK)
        q = proj_ref[rows, off_q + h * head_k:off_q + (h + 1) * head_k]
        o = lax.dot_general(q, sbf_ref[h], (((1,), (1,)), ((), ())),
                            preferred_element_type=F32) * (head_k ** -0.5)
        ms = jnp.mean(o * o, axis=-1, keepdims=True)
        o = o * lax.rsqrt(ms + EPS) * gng_ref[h:h + 1, :]
        r = proj_ref[rows, off_r + h * head_v:off_r + (h + 1) * head_v].astype(F32)
        yg_ref[rows, h * head_v:(h + 1) * head_v] = (o * _silu(r)).astype(yg_ref.dtype)

    def pin(tokens):
        acc = tokens[0]
        for t in tokens[1:]:
            acc = acc | t
        zero = pltpu.bitcast((acc >> 16) >> 16, F32)
        zero = jnp.concatenate([zero] * (STRIP // zero.shape[0]), axis=0)
        tile = h_ref[0:STRIP, 0:LANES].astype(F32)
        h_ref[0:STRIP, 0:LANES] = (tile + zero).astype(BF16)

    def proj_piece(c0, width):
        def piece():
            proj_ref[:, c0:c0 + width] = jnp.dot(
                h_ref[...], w_ref[:, c0:c0 + width], preferred_element_type=F32).astype(BF16)
        return piece

    def pieces(off, width):
        return [proj_piece(c0, min(MXU_N, off + width - c0)) for c0 in range(off, off + width, MXU_N)]

    @pl.when(s == 0)
    def _():
        proj_ref[:, :4 * conv_width] = jnp.zeros((t_rows, 4 * conv_width), BF16)

    @pl.when((s == 0) | ((s + tiles_per_seq - 1) % tiles_per_seq == 0))
    def _():
        carry_ref[...] = jnp.zeros_like(carry_ref)

    @pl.when(s < n_tiles)
    def _():
        @pl.when(s % tiles_per_seq == 0)
        def _():
            state_ref[...] = jnp.zeros_like(state_ref)

        def norm_strip(i, c):
            rows = pl.ds(pl.multiple_of(i * STRIP, STRIP), STRIP)
            x = x_ref[rows, :]
            ms = jnp.mean(x * x, axis=-1, keepdims=True)
            h_ref[rows, :] = (x * lax.rsqrt(ms + EPS) * ng_ref[...]).astype(BF16)
            return c

        lax.fori_loop(0, n_strips, norm_strip, 0, unroll=2)

        conv_units = [functools.partial(conv_strip, strip_rows(i)) for i in range(n_strips)]
        gate_units = [functools.partial(gate_strip, strip_rows(i)) for i in range(n_strips)]
        kdec_units = [functools.partial(kdec_strip, strip_rows(i)) for i in range(n_strips)]
        n_steps = n_chunks * GLA_HEADS

        mxu = (pieces(off_gd, LANES) + pieces(off_k, dk_total) + pieces(off_q, dk_total)
               + pieces(off_v, gla_width) + pieces(off_r, gla_width))
        n_k = dk_total // MXU_N
        vpu = [conv_units[:2]]
        vpu += _spread([gate_pre] + gate_units, n_k)
        vpu += [[later_sums, conv_units[2]]] + [kdec_units + [conv_units[3]]]
        vpu += [[] for _ in range(dk_total // MXU_N - 2)]
        rest = _spread(conv_units[4:], len(mxu) - len(vpu))
        rest[-1].append(functools.partial(state_update, 0))
        vpu += rest
        assert len(vpu) == len(mxu)

        mxu2 = pieces(off_h, 4 * conv_width)
        halves = []
        for step in range(n_steps):
            if step + 1 < n_steps:
                halves.append(functools.partial(state_update, step + 1))
            halves.append(functools.partial(readout, step))
        vpu2 = _spread(halves, len(mxu2))

        tokens = []
        for piece, units in zip(mxu + mxu2, vpu + vpu2):
            if len(tokens) >= PIN_LAG and tokens[-PIN_LAG]:
                pin(tokens[-PIN_LAG])
            piece()
            tokens.append([t for t in [unit() for unit in units] if t is not None])

    @pl.when(s == n_tiles)
    def _():
        def step(i, c):
            conv_strip(pl.ds(pl.multiple_of(i * STRIP, STRIP), STRIP))
            return c
        lax.fori_loop(0, n_strips, step, 0, unroll=2)


def _proj_mix(x2d, norm_g, w_in_bf16, conv_w, conv_b, w_up_pad, b_gate, gla_norm_g, *,
              batch, seq, conv_width, dk_total, gla_width):
    t = MIX_T
    nt = seq // t
    n_tiles = batch * nt
    d_model = x2d.shape[1]
    n_cols = w_in_bf16.shape[1]
    head_k = dk_total // GLA_HEADS
    head_v = gla_width // GLA_HEADS

    def whole(shape, **kw):
        return pl.BlockSpec(shape, lambda s: (0,) * len(shape), **kw)

    kern = functools.partial(_proj_mix_kernel, n_tiles=n_tiles, tiles_per_seq=nt,
                             conv_width=conv_width, dk_total=dk_total, gla_width=gla_width)
    return pl.pallas_call(
        kern,
        grid=(n_tiles + 1,),
        in_specs=[
            pl.BlockSpec((t, d_model), lambda s: (jnp.minimum(s, n_tiles - 1), 0)),
            whole(norm_g.shape),
            whole(w_in_bf16.shape, pipeline_mode=pl.Buffered(1)),
            whole(conv_w.shape), whole(conv_b.shape), whole(w_up_pad.shape), whole(b_gate.shape),
            whole(gla_norm_g.shape),
        ],
        out_specs=[
            pl.BlockSpec((t, conv_width), lambda s: (jnp.maximum(s - 1, 0), 0)),
            pl.BlockSpec((t, gla_width), lambda s: (jnp.minimum(s, n_tiles - 1), 0)),
        ],
        out_shape=[jax.ShapeDtypeStruct((batch * seq, conv_width), BF16),
                   jax.ShapeDtypeStruct((batch * seq, gla_width), BF16)],
        scratch_shapes=[pltpu.VMEM((t, d_model), BF16),
                        pltpu.VMEM((t, n_cols), BF16),
                        pltpu.VMEM((8, conv_width), F32),
                        pltpu.VMEM((GLA_HEADS, head_v, head_k), F32),
                        pltpu.VMEM((GLA_HEADS, head_v, head_k), BF16),
                        pltpu.VMEM((t, dk_total), F32),
                        pltpu.VMEM((t, 2 * dk_total), BF16),
                        pltpu.VMEM((t + 8, dk_total), F32),
                        pltpu.VMEM((t, dk_total), BF16)],
        compiler_params=pltpu.CompilerParams(
            dimension_semantics=("arbitrary",),
            vmem_limit_bytes=VMEM_LIMIT),
        name="proj_mix",
    )(x2d, norm_g, w_in_bf16, conv_w, conv_b, w_up_pad, b_gate, gla_norm_g)


def _out_proj_kernel(yc_ref, yg_ref, w_ref, x_ref, g_ref, o_ref, *, final_norm):
    kc = yc_ref.shape[1]
    z = (x_ref[...] + jnp.dot(yc_ref[...], w_ref[:kc, :], preferred_element_type=F32)
         + jnp.dot(yg_ref[...], w_ref[kc:, :], preferred_element_type=F32))
    if final_norm:
        ms = jnp.mean(z * z, axis=-1, keepdims=True)
        z = z * lax.rsqrt(ms + EPS) * g_ref[...]
    o_ref[...] = z


def _out_proj(y_conv, y_gla, w_out_bf16, x2d, final_g, *, final_norm):
    m, d = x2d.shape
    return pl.pallas_call(
        functools.partial(_out_proj_kernel, final_norm=final_norm),
        grid=(m // OUT_TM,),
        in_specs=[
            pl.BlockSpec((OUT_TM, y_conv.shape[1]), lambda i: (i, 0)),
            pl.BlockSpec((OUT_TM, y_gla.shape[1]), lambda i: (i, 0)),
            pl.BlockSpec(w_out_bf16.shape, lambda i: (0, 0)),
            pl.BlockSpec((OUT_TM, d), lambda i: (i, 0)),
            pl.BlockSpec((1, d), lambda i: (0, 0)),
        ],
        out_specs=pl.BlockSpec((OUT_TM, d), lambda i: (i, 0)),
        out_shape=jax.ShapeDtypeStruct((m, d), F32),
        compiler_params=pltpu.CompilerParams(
            dimension_semantics=("arbitrary",),
            vmem_limit_bytes=VMEM_LIMIT),
        name="out_proj",
    )(y_conv, y_gla, w_out_bf16, x2d, final_g)


def kernel(x, norm_g, w_in, conv_w, conv_b, gla_w_up, gla_b_gate, gla_norm_g, w_out, final_g):
    batch, seq, d_model = x.shape
    depth = norm_g.shape[0]
    conv_width = conv_w.shape[2]
    rank, dk_total = gla_w_up.shape[1], gla_w_up.shape[2]
    gla_width = gla_norm_g.shape[1] * gla_norm_g.shape[2]
    in_cols = w_in.shape[2]
    main_cols = in_cols - rank
    assert main_cols == 4 * conv_width + 2 * dk_total + 2 * gla_width
    assert main_cols % MXU_N == 0 and dk_total % MXU_N == 0 and rank <= LANES
    assert seq % MIX_T == 0 and MIX_T % CHUNK == 0 and (batch * seq) % OUT_TM == 0
    n_pad = main_cols + LANES

    x2d = x.reshape(batch * seq, d_model)
    for l in range(depth):
        w_in_p = jnp.pad(w_in[l], ((0, 0), (0, n_pad - in_cols))).astype(BF16)
        w_up_p = jnp.pad(gla_w_up[l], ((0, LANES - rank), (0, 0))).astype(BF16)
        y_conv, y_gla = _proj_mix(x2d, norm_g[l][None, :], w_in_p, conv_w[l], conv_b[l][None, :],
                                  w_up_p, gla_b_gate[l][None, :], gla_norm_g[l],
                                  batch=batch, seq=seq, conv_width=conv_width,
                                  dk_total=dk_total, gla_width=gla_width)
        x2d = _out_proj(y_conv, y_gla, w_out[l].astype(BF16), x2d, final_g[None, :],
                        final_norm=(l == depth - 1))
    return x2d.reshape(batch, seq, d_model)
```

```python
import functools

import jax
import jax.numpy as jnp
from jax import lax
from jax.experimental import pallas as pl
from jax.experimental.pallas import tpu as pltpu

F32 = jnp.float32
BF16 = jnp.bfloat16

LANES = 128
MXU_N = 256
EPS = 1e-6
CHUNK = 64
GLA_HEADS = 4
GLA_TAU = 16.0

MIX_T = 256
STRIP = 16
COL_CHUNK = 512
W_ROWS = 128
PIN_LAG = 2
OUT_TM = 512
VMEM_LIMIT = 56 * 1024 * 1024


def _silu(v):
    return v * jax.nn.sigmoid(v)


def _log_sigmoid(v):
    return -(jnp.maximum(-v, 0.0) + jnp.log1p(jnp.exp(-jnp.abs(v))))


def _spread(units, n_slots):
    out = [[] for _ in range(n_slots)]
    for j, u in enumerate(units):
        out[(j * n_slots) // len(units)].append(u)
    return out


def _proj_mix_kernel(x_ref, ng_ref, w_hbm, convw_ref, convb_ref, wup_ref, bgate_ref, gng_ref, wo_ref,
                     yc_ref, yg_ref, wo_bf_ref,
                     w_ref, stage_ref, dma_sem, h_ref, proj_ref, carry_ref, state_ref, sbf_ref,
                     glog_ref, hilo_ref, rev_ref, kdec_ref,
                     *, n_tiles, tiles_per_seq, conv_width, dk_total, gla_width):
    t_rows = x_ref.shape[0]
    d_model, in_cols = w_hbm.shape
    n_strips = t_rows // STRIP
    n_chunks = t_rows // CHUNK
    head_k = dk_total // GLA_HEADS
    head_v = gla_width // GLA_HEADS
    off_h, off_b, off_c, off_z = (i * conv_width for i in range(4))
    off_q = 4 * conv_width
    off_k = off_q + dk_total
    off_v = off_k + dk_total
    off_r = off_v + gla_width
    off_gd = off_r + gla_width
    s = pl.program_id(0)

    def strip_rows(i):
        return slice(i * STRIP, (i + 1) * STRIP)

    def conv_strip(rows):
        token = None
        for c0 in range(0, conv_width, COL_CHUNK):
            cols = slice(c0, c0 + COL_CHUNK)

            def col(off):
                return proj_ref[rows, off + c0:off + c0 + COL_CHUNK].astype(F32)

            u = col(off_c) * col(off_h)
            ext = jnp.concatenate([carry_ref[:, cols], u], axis=0)
            u1 = pltpu.roll(ext, 1, 0)[8:, :]
            u2 = pltpu.roll(ext, 2, 0)[8:, :]
            carry_ref[:, cols] = u[STRIP - 8:, :]
            conv = (convb_ref[:, cols] + convw_ref[0:1, cols] * u2 + convw_ref[1:2, cols] * u1
                    + convw_ref[2:3, cols] * u)
            y = (col(off_b) * conv * _silu(col(off_z))).astype(yc_ref.dtype)
            yc_ref[rows, cols] = y
            bits = pltpu.bitcast(y, jnp.uint32)
            for l0 in range(0, COL_CHUNK, LANES):
                part = bits[:, l0:l0 + LANES]
                token = part if token is None else token | part
        return token

    def gate_pre():
        glog_ref[...] = jnp.dot(proj_ref[:, off_gd:off_gd + LANES], wup_ref[...],
                                preferred_element_type=F32)

    def gate_strip(rows):
        glog = _log_sigmoid(glog_ref[rows, :] + bgate_ref[...]) / GLA_TAU
        g_hi = glog.astype(BF16)
        hilo_ref[rows, :dk_total] = g_hi
        hilo_ref[rows, dk_total:] = (glog - g_hi.astype(F32)).astype(BF16)

    def later_sums():
        r_i = lax.broadcasted_iota(jnp.int32, (t_rows + 8, t_rows), 0)
        c_i = lax.broadcasted_iota(jnp.int32, (t_rows + 8, t_rows), 1)
        c_chunk = c_i // CHUNK
        ones = ((c_i > r_i) & (c_chunk == r_i // CHUNK)) | (c_chunk == r_i - t_rows)
        sums = jnp.dot(jnp.where(ones, 1.0, 0.0).astype(BF16), hilo_ref[...],
                       preferred_element_type=F32)
        rev_ref[...] = sums[:, :dk_total] + sums[:, dk_total:]

    def kdec_strip(rows):
        k = proj_ref[rows, off_k:off_k + dk_total].astype(F32)
        kdec_ref[rows, :] = (k * jnp.exp(rev_ref[rows, :])).astype(BF16)

    def state_update(step):
        c, h = divmod(step, GLA_HEADS)
        rows = slice(c * CHUNK, (c + 1) * CHUNK)
        ks = slice(h * head_k, (h + 1) * head_k)
        decay = jnp.exp(rev_ref[t_rows + c:t_rows + c + 1, ks])
        v = proj_ref[rows, off_v + h * head_v:off_v + (h + 1) * head_v]
        inc = lax.dot_general(v, kdec_ref[rows, ks], (((0,), (0,)), ((), ())),
                              preferred_element_type=F32)
        s_new = state_ref[h] * decay + inc
        state_ref[h] = s_new
        sbf_ref[h] = s_new.astype(BF16)

    def readout(step):
        c, h = divmod(step, GLA_HEADS)
        rows = slice(c * CHUNK, (c + 1) * CHUNK)
        q = proj_ref[rows, off_q + h * head_k:off_q + (h + 1) * head_k]
        o = lax.dot_general(q, sbf_ref[h], (((1,), (1,)), ((), ())),
                            preferred_element_type=F32) * (head_k ** -0.5)
        ms = jnp.mean(o * o, axis=-1, keepdims=True)
        o = o * lax.rsqrt(ms + EPS) * gng_ref[h:h + 1, :]
        r = proj_ref[rows, off_r + h * head_v:off_r + (h + 1) * head_v].astype(F32)
        yg_ref[rows, h * head_v:(h + 1) * head_v] = (o * _silu(r)).astype(yg_ref.dtype)

    def pin(tokens):
        acc = tokens[0]
        for t in tokens[1:]:
            acc = acc | t
        zero = pltpu.bitcast((acc >> 16) >> 16, F32)
        zero = jnp.concatenate([zero] * (STRIP // zero.shape[0]), axis=0)
        tile = h_ref[0:STRIP, 0:LANES].astype(F32)
        h_ref[0:STRIP, 0:LANES] = (tile + zero).astype(BF16)

    def proj_piece(c0, width):
        def piece():
            proj_ref[:, c0:c0 + width] = jnp.dot(
                h_ref[...], w_ref[:, c0:c0 + width], preferred_element_type=F32).astype(BF16)
        return piece

    def pieces(off, width):
        return [proj_piece(c0, min(MXU_N, off + width - c0)) for c0 in range(off, off + width, MXU_N)]

    @pl.when(s == 0)
    def _():
        proj_ref[:, :4 * conv_width] = jnp.zeros((t_rows, 4 * conv_width), BF16)

        def chunk_copy(r):
            return pltpu.make_async_copy(w_hbm.at[pl.ds(r * W_ROWS, W_ROWS), :],
                                         stage_ref.at[r % 2], dma_sem.at[r % 2])

        pad_cols = w_ref.shape[1] - in_cols
        w_ref[:, in_cols - (LANES - pad_cols):] = jnp.zeros((d_model, LANES), BF16)
        n_w_chunks = d_model // W_ROWS
        chunk_copy(0).start()
        for r in range(n_w_chunks):
            if r + 1 < n_w_chunks:
                chunk_copy(r + 1).start()
            chunk_copy(r).wait()

            def cast_strip(i, c, r=r):
                rows = pl.ds(pl.multiple_of(i * STRIP, STRIP), STRIP)
                w_ref[pl.ds(pl.multiple_of(r * W_ROWS + i * STRIP, STRIP), STRIP), :in_cols] = (
                    stage_ref[r % 2, rows, :].astype(BF16))
                return c

            lax.fori_loop(0, W_ROWS // STRIP, cast_strip, 0)

    @pl.when((s == 0) | ((s + tiles_per_seq - 1) % tiles_per_seq == 0))
    def _():
        carry_ref[...] = jnp.zeros_like(carry_ref)

    @pl.when(s < n_tiles)
    def _():
        @pl.when(s % tiles_per_seq == 0)
        def _():
            state_ref[...] = jnp.zeros_like(state_ref)

        wo_bf_ref[...] = wo_ref[...].astype(BF16)

        def norm_strip(i, c):
            rows = pl.ds(pl.multiple_of(i * STRIP, STRIP), STRIP)
            x = x_ref[rows, :]
            ms = jnp.mean(x * x, axis=-1, keepdims=True)
            h_ref[rows, :] = (x * lax.rsqrt(ms + EPS) * ng_ref[...]).astype(BF16)
            return c

        lax.fori_loop(0, n_strips, norm_strip, 0, unroll=2)

        conv_units = [functools.partial(conv_strip, strip_rows(i)) for i in range(n_strips)]
        gate_units = [functools.partial(gate_strip, strip_rows(i)) for i in range(n_strips)]
        kdec_units = [functools.partial(kdec_strip, strip_rows(i)) for i in range(n_strips)]
        n_steps = n_chunks * GLA_HEADS

        mxu = (pieces(off_gd, LANES) + pieces(off_k, dk_total) + pieces(off_q, dk_total)
               + pieces(off_v, gla_width) + pieces(off_r, gla_width))
        n_k = dk_total // MXU_N
        vpu = [conv_units[:2]]
        vpu += _spread([gate_pre] + gate_units, n_k)
        vpu += [[later_sums, conv_units[2]]] + [kdec_units + [conv_units[3]]]
        vpu += [[] for _ in range(dk_total // MXU_N - 2)]
        rest = _spread(conv_units[4:], len(mxu) - len(vpu))
        rest[-1].append(functools.partial(state_update, 0))
        vpu += rest
        assert len(vpu) == len(mxu)

        mxu2 = pieces(off_h, 4 * conv_width)
        halves = []
        for step in range(n_steps):
            if step + 1 < n_steps:
                halves.append(functools.partial(state_update, step + 1))
            halves.append(functools.partial(readout, step))
        vpu2 = _spread(halves, len(mxu2))

        tokens = []
        for piece, units in zip(mxu + mxu2, vpu + vpu2):
            if len(tokens) >= PIN_LAG and tokens[-PIN_LAG]:
                pin(tokens[-PIN_LAG])
            piece()
            tokens.append([t for t in [unit() for unit in units] if t is not None])

    @pl.when(s == n_tiles)
    def _():
        def step(i, c):
            conv_strip(pl.ds(pl.multiple_of(i * STRIP, STRIP), STRIP))
            return c
        lax.fori_loop(0, n_strips, step, 0, unroll=2)


def _proj_mix(x2d, norm_g, w_in, conv_w, conv_b, w_up_pad, b_gate, gla_norm_g, w_out, *,
              batch, seq, conv_width, dk_total, gla_width, n_cols):
    t = MIX_T
    nt = seq // t
    n_tiles = batch * nt
    d_model = x2d.shape[1]
    wo_rows = w_out.shape[0] // n_tiles
    assert w_out.shape[0] % n_tiles == 0 and wo_rows % STRIP == 0 and d_model % W_ROWS == 0
    head_k = dk_total // GLA_HEADS
    head_v = gla_width // GLA_HEADS

    def whole(shape, **kw):
        return pl.BlockSpec(shape, lambda s: (0,) * len(shape), **kw)

    kern = functools.partial(_proj_mix_kernel, n_tiles=n_tiles, tiles_per_seq=nt,
                             conv_width=conv_width, dk_total=dk_total, gla_width=gla_width)
    return pl.pallas_call(
        kern,
        grid=(n_tiles + 1,),
        in_specs=[
            pl.BlockSpec((t, d_model), lambda s: (jnp.minimum(s, n_tiles - 1), 0)),
            whole(norm_g.shape),
            pl.BlockSpec(memory_space=pl.ANY),
            whole(conv_w.shape), whole(conv_b.shape), whole(w_up_pad.shape), whole(b_gate.shape),
            whole(gla_norm_g.shape),
            pl.BlockSpec((wo_rows, w_out.shape[1]), lambda s: (jnp.minimum(s, n_tiles - 1), 0)),
        ],
        out_specs=[
            pl.BlockSpec((t, conv_width), lambda s: (jnp.maximum(s - 1, 0), 0)),
            pl.BlockSpec((t, gla_width), lambda s: (jnp.minimum(s, n_tiles - 1), 0)),
            pl.BlockSpec((wo_rows, w_out.shape[1]), lambda s: (jnp.minimum(s, n_tiles - 1), 0)),
        ],
        out_shape=[jax.ShapeDtypeStruct((batch * seq, conv_width), BF16),
                   jax.ShapeDtypeStruct((batch * seq, gla_width), BF16),
                   jax.ShapeDtypeStruct(w_out.shape, BF16)],
        scratch_shapes=[pltpu.VMEM((d_model, n_cols), BF16),
                        pltpu.VMEM((2, W_ROWS, w_in.shape[1]), F32),
                        pltpu.SemaphoreType.DMA((2,)),
                        pltpu.VMEM((t, d_model), BF16),
                        pltpu.VMEM((t, n_cols), BF16),
                        pltpu.VMEM((8, conv_width), F32),
                        pltpu.VMEM((GLA_HEADS, head_v, head_k), F32),
                        pltpu.VMEM((GLA_HEADS, head_v, head_k), BF16),
                        pltpu.VMEM((t, dk_total), F32),
                        pltpu.VMEM((t, 2 * dk_total), BF16),
                        pltpu.VMEM((t + 8, dk_total), F32),
                        pltpu.VMEM((t, dk_total), BF16)],
        compiler_params=pltpu.CompilerParams(
            dimension_semantics=("arbitrary",),
            vmem_limit_bytes=VMEM_LIMIT),
        name="proj_mix",
    )(x2d, norm_g, w_in, conv_w, conv_b, w_up_pad, b_gate, gla_norm_g, w_out)


def _out_proj_kernel(yc_ref, yg_ref, w_ref, x_ref, g_ref, o_ref, *, final_norm):
    kc = yc_ref.shape[1]
    z = (x_ref[...] + jnp.dot(yc_ref[...], w_ref[:kc, :], preferred_element_type=F32)
         + jnp.dot(yg_ref[...], w_ref[kc:, :], preferred_element_type=F32))
    if final_norm:
        ms = jnp.mean(z * z, axis=-1, keepdims=True)
        z = z * lax.rsqrt(ms + EPS) * g_ref[...]
    o_ref[...] = z


def _out_proj(y_conv, y_gla, w_out_bf16, x2d, final_g, *, final_norm):
    m, d = x2d.shape
    return pl.pallas_call(
        functools.partial(_out_proj_kernel, final_norm=final_norm),
        grid=(m // OUT_TM,),
        in_specs=[
            pl.BlockSpec((OUT_TM, y_conv.shape[1]), lambda i: (i, 0)),
            pl.BlockSpec((OUT_TM, y_gla.shape[1]), lambda i: (i, 0)),
            pl.BlockSpec(w_out_bf16.shape, lambda i: (0, 0)),
            pl.BlockSpec((OUT_TM, d), lambda i: (i, 0)),
            pl.BlockSpec((1, d), lambda i: (0, 0)),
        ],
        out_specs=pl.BlockSpec((OUT_TM, d), lambda i: (i, 0)),
        out_shape=jax.ShapeDtypeStruct((m, d), F32),
        compiler_params=pltpu.CompilerParams(
            dimension_semantics=("arbitrary",),
            vmem_limit_bytes=VMEM_LIMIT),
        name="out_proj",
    )(y_conv, y_gla, w_out_bf16, x2d, final_g)


def kernel(x, norm_g, w_in, conv_w, conv_b, gla_w_up, gla_b_gate, gla_norm_g, w_out, final_g):
    batch, seq, d_model = x.shape
    depth = norm_g.shape[0]
    conv_width = conv_w.shape[2]
    rank, dk_total = gla_w_up.shape[1], gla_w_up.shape[2]
    gla_width = gla_norm_g.shape[1] * gla_norm_g.shape[2]
    in_cols = w_in.shape[2]
    main_cols = in_cols - rank
    assert main_cols == 4 * conv_width + 2 * dk_total + 2 * gla_width
    assert main_cols % MXU_N == 0 and dk_total % MXU_N == 0 and rank <= LANES
    assert seq % MIX_T == 0 and MIX_T % CHUNK == 0 and (batch * seq) % OUT_TM == 0
    n_pad = main_cols + LANES

    x2d = x.reshape(batch * seq, d_model)
    for l in range(depth):
        w_up_p = jnp.pad(gla_w_up[l], ((0, LANES - rank), (0, 0))).astype(BF16)
        y_conv, y_gla, w_out_bf16 = _proj_mix(
            x2d, norm_g[l][None, :], w_in[l], conv_w[l], conv_b[l][None, :], w_up_p,
            gla_b_gate[l][None, :], gla_norm_g[l], w_out[l], batch=batch, seq=seq,
            conv_width=conv_width, dk_total=dk_total, gla_width=gla_width, n_cols=n_pad)
        x2d = _out_proj(y_conv, y_gla, w_out_bf16, x2d, final_g[None, :],
                        final_norm=(l == depth - 1))
    return x2d.reshape(batch, seq, d_model)
```

```python
import functools

import jax
import jax.numpy as jnp
from jax import lax
from jax.experimental import pallas as pl
from jax.experimental.pallas import tpu as pltpu

F32 = jnp.float32
BF16 = jnp.bfloat16

LANES = 128
MXU_N = 256
EPS = 1e-6
CHUNK = 64
GLA_HEADS = 4
GLA_TAU = 16.0

MIX_T = 256
STRIP = 16
COL_CHUNK = 512
W_ROWS = 512
PIN_LAG = 2
OUT_TM = 512
VMEM_LIMIT = 56 * 1024 * 1024


def _silu(v):
    return v * jax.nn.sigmoid(v)


def _log_sigmoid(v):
    return -(jnp.maximum(-v, 0.0) + jnp.log1p(jnp.exp(-jnp.abs(v))))


def _spread(units, n_slots):
    out = [[] for _ in range(n_slots)]
    for j, u in enumerate(units):
        out[(j * n_slots) // len(units)].append(u)
    return out


def _proj_mix_kernel(x_ref, ng_ref, w_hbm, convw_ref, convb_ref, wup_ref, bgate_ref, gng_ref, wo_ref,
                     yc_ref, yg_ref, wo_bf_ref,
                     w_ref, stage_ref, dma_sem, h_ref, proj_ref, carry_ref, state_ref, sbf_ref,
                     glog_ref, hilo_ref, rev_ref, kdec_ref,
                     *, n_tiles, tiles_per_seq, conv_width, dk_total, gla_width):
    t_rows = x_ref.shape[0]
    in_cols, d_model = w_hbm.shape
    n_strips = t_rows // STRIP
    n_chunks = t_rows // CHUNK
    head_k = dk_total // GLA_HEADS
    head_v = gla_width // GLA_HEADS
    off_h, off_b, off_c, off_z = (i * conv_width for i in range(4))
    off_q = 4 * conv_width
    off_k = off_q + dk_total
    off_v = off_k + dk_total
    off_r = off_v + gla_width
    off_gd = off_r + gla_width
    s = pl.program_id(0)

    def strip_rows(i):
        return slice(i * STRIP, (i + 1) * STRIP)

    def conv_strip(rows):
        token = None
        for c0 in range(0, conv_width, COL_CHUNK):
            cols = slice(c0, c0 + COL_CHUNK)

            def col(off):
                return proj_ref[rows, off + c0:off + c0 + COL_CHUNK].astype(F32)

            u = col(off_c) * col(off_h)
            ext = jnp.concatenate([carry_ref[:, cols], u], axis=0)
            u1 = pltpu.roll(ext, 1, 0)[8:, :]
            u2 = pltpu.roll(ext, 2, 0)[8:, :]
            carry_ref[:, cols] = u[STRIP - 8:, :]
            conv = (convb_ref[:, cols] + convw_ref[0:1, cols] * u2 + convw_ref[1:2, cols] * u1
                    + convw_ref[2:3, cols] * u)
            y = (col(off_b) * conv * _silu(col(off_z))).astype(yc_ref.dtype)
            yc_ref[rows, cols] = y
            bits = pltpu.bitcast(y, jnp.uint32)
            for l0 in range(0, COL_CHUNK, LANES):
                part = bits[:, l0:l0 + LANES]
                token = part if token is None else token | part
        return token

    def gate_pre():
        glog_ref[...] = jnp.dot(proj_ref[:, off_gd:off_gd + LANES], wup_ref[...],
                                preferred_element_type=F32)

    def gate_strip(rows):
        glog = _log_sigmoid(glog_ref[rows, :] + bgate_ref[...]) / GLA_TAU
        g_hi = glog.astype(BF16)
        hilo_ref[rows, :dk_total] = g_hi
        hilo_ref[rows, dk_total:] = (glog - g_hi.astype(F32)).astype(BF16)

    def later_sums():
        r_i = lax.broadcasted_iota(jnp.int32, (t_rows + 8, t_rows), 0)
        c_i = lax.broadcasted_iota(jnp.int32, (t_rows + 8, t_rows), 1)
        c_chunk = c_i // CHUNK
        ones = ((c_i > r_i) & (c_chunk == r_i // CHUNK)) | (c_chunk == r_i - t_rows)
        sums = jnp.dot(jnp.where(ones, 1.0, 0.0).astype(BF16), hilo_ref[...],
                       preferred_element_type=F32)
        rev_ref[...] = sums[:, :dk_total] + sums[:, dk_total:]

    def kdec_strip(rows):
        k = proj_ref[rows, off_k:off_k + dk_total].astype(F32)
        kdec_ref[rows, :] = (k * jnp.exp(rev_ref[rows, :])).astype(BF16)

    def state_update(step):
        c, h = divmod(step, GLA_HEADS)
        rows = slice(c * CHUNK, (c + 1) * CHUNK)
        ks = slice(h * head_k, (h + 1) * head_k)
        decay = jnp.exp(rev_ref[t_rows + c:t_rows + c + 1, ks])
        v = proj_ref[rows, off_v + h * head_v:off_v + (h + 1) * head_v]
        inc = lax.dot_general(v, kdec_ref[rows, ks], (((0,), (0,)), ((), ())),
                              preferred_element_type=F32)
        s_new = state_ref[h] * decay + inc
        state_ref[h] = s_new
        sbf_ref[h] = s_new.astype(BF16)

    def readout(step):
        c, h = divmod(step, GLA_HEADS)
        rows = slice(c * CHUNK, (c + 1) * CHUNK)
        q = proj_ref[rows, off_q + h * head_k:off_q + (h + 1) * head_k]
        o = lax.dot_general(q, sbf_ref[h], (((1,), (1,)), ((), ())),
                            preferred_element_type=F32) * (head_k ** -0.5)
        ms = jnp.mean(o * o, axis=-1, keepdims=True)
        o = o * lax.rsqrt(ms + EPS) * gng_ref[h:h + 1, :]
        r = proj_ref[rows, off_r + h * head_v:off_r + (h + 1) * head_v].astype(F32)
        yg_ref[rows, h * head_v:(h + 1) * head_v] = (o * _silu(r)).astype(yg_ref.dtype)

    def pin(tokens):
        acc = tokens[0]
        for t in tokens[1:]:
            acc = acc | t
        zero = pltpu.bitcast((acc >> 16) >> 16, F32)
        zero = jnp.concatenate([zero] * (STRIP // zero.shape[0]), axis=0)
        tile = h_ref[0:STRIP, 0:LANES].astype(F32)
        h_ref[0:STRIP, 0:LANES] = (tile + zero).astype(BF16)

    def proj_piece(c0, width):
        def piece():
            proj_ref[:, c0:c0 + width] = lax.dot_general(
                h_ref[...], w_ref[c0:c0 + width, :], (((1,), (1,)), ((), ())),
                preferred_element_type=F32).astype(BF16)
        return piece

    def pieces(off, width):
        return [proj_piece(c0, min(MXU_N, off + width - c0)) for c0 in range(off, off + width, MXU_N)]

    @pl.when(s == 0)
    def _():
        proj_ref[:, :4 * conv_width] = jnp.zeros((t_rows, 4 * conv_width), BF16)

        chunks = [(r0, min(W_ROWS, in_cols - r0)) for r0 in range(0, in_cols, W_ROWS)]

        def chunk_copy(i):
            r0, rows = chunks[i]
            return pltpu.make_async_copy(w_hbm.at[pl.ds(r0, rows), :],
                                         stage_ref.at[i % 2, pl.ds(0, rows), :], dma_sem.at[i % 2])

        w_ref[in_cols:, :] = jnp.zeros((w_ref.shape[0] - in_cols, d_model), BF16)
        chunk_copy(0).start()
        for i, (r0, rows) in enumerate(chunks):
            if i + 1 < len(chunks):
                chunk_copy(i + 1).start()
            chunk_copy(i).wait()

            def cast_strip(j, c, i=i, r0=r0):
                src_rows = pl.ds(pl.multiple_of(j * STRIP, STRIP), STRIP)
                dst_rows = pl.ds(pl.multiple_of(r0 + j * STRIP, STRIP), STRIP)
                w_ref[dst_rows, :] = stage_ref[i % 2, src_rows, :].astype(BF16)
                return c

            lax.fori_loop(0, rows // STRIP, cast_strip, 0)

    @pl.when((s == 0) | ((s + tiles_per_seq - 1) % tiles_per_seq == 0))
    def _():
        carry_ref[...] = jnp.zeros_like(carry_ref)

    @pl.when(s < n_tiles)
    def _():
        @pl.when(s % tiles_per_seq == 0)
        def _():
            state_ref[...] = jnp.zeros_like(state_ref)

        wo_bf_ref[...] = wo_ref[...].astype(BF16)

        def norm_strip(i, c):
            rows = pl.ds(pl.multiple_of(i * STRIP, STRIP), STRIP)
            x = x_ref[rows, :]
            ms = jnp.mean(x * x, axis=-1, keepdims=True)
            h_ref[rows, :] = (x * lax.rsqrt(ms + EPS) * ng_ref[...]).astype(BF16)
            return c

        lax.fori_loop(0, n_strips, norm_strip, 0, unroll=2)

        conv_units = [functools.partial(conv_strip, strip_rows(i)) for i in range(n_strips)]
        gate_units = [functools.partial(gate_strip, strip_rows(i)) for i in range(n_strips)]
        kdec_units = [functools.partial(kdec_strip, strip_rows(i)) for i in range(n_strips)]
        n_steps = n_chunks * GLA_HEADS

        mxu = (pieces(off_gd, LANES) + pieces(off_k, dk_total) + pieces(off_q, dk_total)
               + pieces(off_v, gla_width) + pieces(off_r, gla_width))
        n_k = dk_total // MXU_N
        vpu = [conv_units[:2]]
        vpu += _spread([gate_pre] + gate_units, n_k)
        vpu += [[later_sums, conv_units[2]]] + [kdec_units + [conv_units[3]]]
        vpu += [[] for _ in range(dk_total // MXU_N - 2)]
        rest = _spread(conv_units[4:], len(mxu) - len(vpu))
        rest[-1].append(functools.partial(state_update, 0))
        vpu += rest
        assert len(vpu) == len(mxu)

        mxu2 = pieces(off_h, 4 * conv_width)
        halves = []
        for step in range(n_steps):
            if step + 1 < n_steps:
                halves.append(functools.partial(state_update, step + 1))
            halves.append(functools.partial(readout, step))
        vpu2 = _spread(halves, len(mxu2))

        tokens = []
        for piece, units in zip(mxu + mxu2, vpu + vpu2):
            if len(tokens) >= PIN_LAG and tokens[-PIN_LAG]:
                pin(tokens[-PIN_LAG])
            piece()
            tokens.append([t for t in [unit() for unit in units] if t is not None])

    @pl.when(s == n_tiles)
    def _():
        def step(i, c):
            conv_strip(pl.ds(pl.multiple_of(i * STRIP, STRIP), STRIP))
            return c
        lax.fori_loop(0, n_strips, step, 0, unroll=2)


def _proj_mix(x2d, norm_g, w_in_t, conv_w, conv_b, w_up_pad, b_gate, gla_norm_g, w_out, *,
              batch, seq, conv_width, dk_total, gla_width, n_cols):
    t = MIX_T
    nt = seq // t
    n_tiles = batch * nt
    d_model = x2d.shape[1]
    wo_rows = w_out.shape[0] // n_tiles
    assert w_out.shape[0] % n_tiles == 0 and wo_rows % STRIP == 0
    assert w_in_t.shape[0] % STRIP == 0 and n_cols % STRIP == 0
    head_k = dk_total // GLA_HEADS
    head_v = gla_width // GLA_HEADS

    def whole(shape, **kw):
        return pl.BlockSpec(shape, lambda s: (0,) * len(shape), **kw)

    kern = functools.partial(_proj_mix_kernel, n_tiles=n_tiles, tiles_per_seq=nt,
                             conv_width=conv_width, dk_total=dk_total, gla_width=gla_width)
    return pl.pallas_call(
        kern,
        grid=(n_tiles + 1,),
        in_specs=[
            pl.BlockSpec((t, d_model), lambda s: (jnp.minimum(s, n_tiles - 1), 0)),
            whole(norm_g.shape),
            pl.BlockSpec(memory_space=pl.ANY),
            whole(conv_w.shape), whole(conv_b.shape), whole(w_up_pad.shape), whole(b_gate.shape),
            whole(gla_norm_g.shape),
            pl.BlockSpec((wo_rows, w_out.shape[1]), lambda s: (jnp.minimum(s, n_tiles - 1), 0)),
        ],
        out_specs=[
            pl.BlockSpec((t, conv_width), lambda s: (jnp.maximum(s - 1, 0), 0)),
            pl.BlockSpec((t, gla_width), lambda s: (jnp.minimum(s, n_tiles - 1), 0)),
            pl.BlockSpec((wo_rows, w_out.shape[1]), lambda s: (jnp.minimum(s, n_tiles - 1), 0)),
        ],
        out_shape=[jax.ShapeDtypeStruct((batch * seq, conv_width), BF16),
                   jax.ShapeDtypeStruct((batch * seq, gla_width), BF16),
                   jax.ShapeDtypeStruct(w_out.shape, BF16)],
        scratch_shapes=[pltpu.VMEM((n_cols, d_model), BF16),
                        pltpu.VMEM((2, W_ROWS, d_model), F32),
                        pltpu.SemaphoreType.DMA((2,)),
                        pltpu.VMEM((t, d_model), BF16),
                        pltpu.VMEM((t, n_cols), BF16),
                        pltpu.VMEM((8, conv_width), F32),
                        pltpu.VMEM((GLA_HEADS, head_v, head_k), F32),
                        pltpu.VMEM((GLA_HEADS, head_v, head_k), BF16),
                        pltpu.VMEM((t, dk_total), F32),
                        pltpu.VMEM((t, 2 * dk_total), BF16),
                        pltpu.VMEM((t + 8, dk_total), F32),
                        pltpu.VMEM((t, dk_total), BF16)],
        compiler_params=pltpu.CompilerParams(
            dimension_semantics=("arbitrary",),
            vmem_limit_bytes=VMEM_LIMIT),
        name="proj_mix",
    )(x2d, norm_g, w_in_t, conv_w, conv_b, w_up_pad, b_gate, gla_norm_g, w_out)


def _out_proj_kernel(yc_ref, yg_ref, w_ref, x_ref, g_ref, o_ref, *, final_norm):
    kc = yc_ref.shape[1]
    z = (x_ref[...] + jnp.dot(yc_ref[...], w_ref[:kc, :], preferred_element_type=F32)
         + jnp.dot(yg_ref[...], w_ref[kc:, :], preferred_element_type=F32))
    if final_norm:
        ms = jnp.mean(z * z, axis=-1, keepdims=True)
        z = z * lax.rsqrt(ms + EPS) * g_ref[...]
    o_ref[...] = z


def _out_proj(y_conv, y_gla, w_out_bf16, x2d, final_g, *, final_norm):
    m, d = x2d.shape
    return pl.pallas_call(
        functools.partial(_out_proj_kernel, final_norm=final_norm),
        grid=(m // OUT_TM,),
        in_specs=[
            pl.BlockSpec((OUT_TM, y_conv.shape[1]), lambda i: (i, 0)),
            pl.BlockSpec((OUT_TM, y_gla.shape[1]), lambda i: (i, 0)),
            pl.BlockSpec(w_out_bf16.shape, lambda i: (0, 0)),
            pl.BlockSpec((OUT_TM, d), lambda i: (i, 0)),
            pl.BlockSpec((1, d), lambda i: (0, 0)),
        ],
        out_specs=pl.BlockSpec((OUT_TM, d), lambda i: (i, 0)),
        out_shape=jax.ShapeDtypeStruct((m, d), F32),
        compiler_params=pltpu.CompilerParams(
            dimension_semantics=("arbitrary",),
            vmem_limit_bytes=VMEM_LIMIT),
        name="out_proj",
    )(y_conv, y_gla, w_out_bf16, x2d, final_g)


def kernel(x, norm_g, w_in, conv_w, conv_b, gla_w_up, gla_b_gate, gla_norm_g, w_out, final_g):
    batch, seq, d_model = x.shape
    depth = norm_g.shape[0]
    conv_width = conv_w.shape[2]
    rank, dk_total = gla_w_up.shape[1], gla_w_up.shape[2]
    gla_width = gla_norm_g.shape[1] * gla_norm_g.shape[2]
    in_cols = w_in.shape[2]
    main_cols = in_cols - rank
    assert main_cols == 4 * conv_width + 2 * dk_total + 2 * gla_width
    assert main_cols % MXU_N == 0 and dk_total % MXU_N == 0 and rank <= LANES
    assert seq % MIX_T == 0 and MIX_T % CHUNK == 0 and (batch * seq) % OUT_TM == 0
    n_pad = main_cols + LANES

    x2d = x.reshape(batch * seq, d_model)
    for l in range(depth):
        w_up_p = jnp.pad(gla_w_up[l], ((0, LANES - rank), (0, 0))).astype(BF16)
        y_conv, y_gla, w_out_bf16 = _proj_mix(
            x2d, norm_g[l][None, :], w_in[l].T, conv_w[l], conv_b[l][None, :], w_up_p,
            gla_b_gate[l][None, :], gla_norm_g[l], w_out[l], batch=batch, seq=seq,
            conv_width=conv_width, dk_total=dk_total, gla_width=gla_width, n_cols=n_pad)
        x2d = _out_proj(y_conv, y_gla, w_out_bf16, x2d, final_g[None, :],
                        final_norm=(l == depth - 1))
    return x2d.reshape(batch, seq, d_model)
```

```python
import functools

import jax
import jax.numpy as jnp
from jax import lax
from jax.experimental import pallas as pl
from jax.experimental.pallas import tpu as pltpu

F32 = jnp.float32
BF16 = jnp.bfloat16

LANES = 128
MXU_N = 256
EPS = 1e-6
CHUNK = 64
GLA_HEADS = 4
GLA_TAU = 16.0

MIX_T = 256
STRIP = 16
COL_CHUNK = 512
W_ROWS = 512
PIN_LAG = 2
OUT_TM = 512
VMEM_LIMIT = 56 * 1024 * 1024


def _silu(v):
    return v * jax.nn.sigmoid(v)


def _log_sigmoid(v):
    return -(jnp.maximum(-v, 0.0) + jnp.log1p(jnp.exp(-jnp.abs(v))))


def _spread(units, n_slots):
    out = [[] for _ in range(n_slots)]
    for j, u in enumerate(units):
        out[(j * n_slots) // len(units)].append(u)
    return out


def _proj_mix_kernel(x_ref, ng_ref, w_hbm, convw_ref, convb_ref, wup_ref, bgate_ref, gng_ref, wo_ref,
                     yc_ref, yg_ref, wo_bf_ref,
                     w_ref, w_tail_ref, stage_ref, dma_sem, h_ref, proj_ref, carry_ref, state_ref, sbf_ref,
                     glog_ref, hilo_ref, rev_ref, kdec_ref,
                     *, n_tiles, tiles_per_seq, conv_width, dk_total, gla_width):
    t_rows = x_ref.shape[0]
    in_cols, d_model = w_hbm.shape
    n_strips = t_rows // STRIP
    n_chunks = t_rows // CHUNK
    head_k = dk_total // GLA_HEADS
    head_v = gla_width // GLA_HEADS
    off_h, off_b, off_c, off_z = (i * conv_width for i in range(4))
    off_q = 4 * conv_width
    off_k = off_q + dk_total
    off_v = off_k + dk_total
    off_r = off_v + gla_width
    off_gd = off_r + gla_width
    s = pl.program_id(0)

    def strip_rows(i):
        return slice(i * STRIP, (i + 1) * STRIP)

    def conv_strip(rows):
        token = None
        for c0 in range(0, conv_width, COL_CHUNK):
            cols = slice(c0, c0 + COL_CHUNK)

            def col(off):
                return proj_ref[rows, off + c0:off + c0 + COL_CHUNK].astype(F32)

            u = col(off_c) * col(off_h)
            ext = jnp.concatenate([carry_ref[:, cols], u], axis=0)
            u1 = pltpu.roll(ext, 1, 0)[8:, :]
            u2 = pltpu.roll(ext, 2, 0)[8:, :]
            carry_ref[:, cols] = u[STRIP - 8:, :]
            conv = (convb_ref[:, cols] + convw_ref[0:1, cols] * u2 + convw_ref[1:2, cols] * u1
                    + convw_ref[2:3, cols] * u)
            y = (col(off_b) * conv * _silu(col(off_z))).astype(yc_ref.dtype)
            yc_ref[rows, cols] = y
            bits = pltpu.bitcast(y, jnp.uint32)
            for l0 in range(0, COL_CHUNK, LANES):
                part = bits[:, l0:l0 + LANES]
                token = part if token is None else token | part
        return token

    def gate_pre():
        glog_ref[...] = jnp.dot(proj_ref[:, off_gd:off_gd + LANES], wup_ref[...],
                                preferred_element_type=F32)

    def gate_strip(rows):
        glog = _log_sigmoid(glog_ref[rows, :] + bgate_ref[...]) / GLA_TAU
        g_hi = glog.astype(BF16)
        hilo_ref[rows, :dk_total] = g_hi
        hilo_ref[rows, dk_total:] = (glog - g_hi.astype(F32)).astype(BF16)

    def later_sums():
        r_i = lax.broadcasted_iota(jnp.int32, (t_rows + 8, t_rows), 0)
        c_i = lax.broadcasted_iota(jnp.int32, (t_rows + 8, t_rows), 1)
        c_chunk = c_i // CHUNK
        ones = ((c_i > r_i) & (c_chunk == r_i // CHUNK)) | (c_chunk == r_i - t_rows)
        sums = jnp.dot(jnp.where(ones, 1.0, 0.0).astype(BF16), hilo_ref[...],
                       preferred_element_type=F32)
        rev_ref[...] = sums[:, :dk_total] + sums[:, dk_total:]

    def kdec_strip(rows):
        k = proj_ref[rows, off_k:off_k + dk_total].astype(F32)
        kdec_ref[rows, :] = (k * jnp.exp(rev_ref[rows, :])).astype(BF16)

    def state_update(step):
        c, h = divmod(step, GLA_HEADS)
        rows = slice(c * CHUNK, (c + 1) * CHUNK)
        ks = slice(h * head_k, (h + 1) * head_k)
        decay = jnp.exp(rev_ref[t_rows + c:t_rows + c + 1, ks])
        v = proj_ref[rows, off_v + h * head_v:off_v + (h + 1) * head_v]
        inc = lax.dot_general(v, kdec_ref[rows, ks], (((0,), (0,)), ((), ())),
                              preferred_element_type=F32)
        s_new = state_ref[h] * decay + inc
        state_ref[h] = s_new
        sbf_ref[h] = s_new.astype(BF16)

    def readout(step):
        c, h = divmod(step, GLA_HEADS)
        rows = slice(c * CHUNK, (c + 1) * CHUNK)
        q = proj_ref[rows, off_q + h * head_k:off_q + (h + 1) * head_k]
        o = lax.dot_general(q, sbf_ref[h], (((1,), (1,)), ((), ())),
                            preferred_element_type=F32) * (head_k ** -0.5)
        ms = jnp.mean(o * o, axis=-1, keepdims=True)
        o = o * lax.rsqrt(ms + EPS) * gng_ref[h:h + 1, :]
        r = proj_ref[rows, off_r + h * head_v:off_r + (h + 1) * head_v].astype(F32)
        yg_ref[rows, h * head_v:(h + 1) * head_v] = (o * _silu(r)).astype(yg_ref.dtype)

    def pin(tokens):
        acc = tokens[0]
        for t in tokens[1:]:
            acc = acc | t
        zero = pltpu.bitcast((acc >> 16) >> 16, F32)
        zero = jnp.concatenate([zero] * (STRIP // zero.shape[0]), axis=0)
        tile = h_ref[0:STRIP, 0:LANES].astype(F32)
        h_ref[0:STRIP, 0:LANES] = (tile + zero).astype(BF16)

    def proj_piece(c0, width):
        def piece():
            if c0 < w_ref.shape[0] * W_ROWS:
                w = w_ref[c0 // W_ROWS, :, c0 % W_ROWS:c0 % W_ROWS + width]
            else:
                w = w_tail_ref[...]
            proj_ref[:, c0:c0 + width] = jnp.dot(h_ref[...], w,
                                                 preferred_element_type=F32).astype(BF16)
        return piece

    def pieces(off, width):
        return [proj_piece(c0, min(MXU_N, off + width - c0)) for c0 in range(off, off + width, MXU_N)]

    @pl.when(s == 0)
    def _():
        proj_ref[:, :4 * conv_width] = jnp.zeros((t_rows, 4 * conv_width), BF16)

        n_full = in_cols // W_ROWS

        def chunk_copy(i, rows=W_ROWS):
            return pltpu.make_async_copy(w_hbm.at[pl.ds(i * W_ROWS, rows), :],
                                         stage_ref.at[i % 2, pl.ds(0, rows), :], dma_sem.at[i % 2])

        def transpose_chunk(slot, rows, dst):
            for k0 in range(0, d_model, MXU_N):
                blk = stage_ref[slot, 0:rows, k0:k0 + MXU_N]
                dst[k0:k0 + MXU_N, :] = blk.T.astype(BF16)

        chunk_copy(0).start()

        def load_chunk(i, c):
            @pl.when(i + 1 < n_full)
            def _():
                chunk_copy(i + 1).start()
            chunk_copy(i).wait()
            transpose_chunk(i % 2, W_ROWS, w_ref.at[i])
            return c

        lax.fori_loop(0, n_full, load_chunk, 0)

        tail_rows = in_cols - n_full * W_ROWS
        stage_ref[n_full % 2, tail_rows:LANES, :] = jnp.zeros((LANES - tail_rows, d_model), F32)
        chunk_copy(n_full, tail_rows).start()
        chunk_copy(n_full, tail_rows).wait()
        transpose_chunk(n_full % 2, LANES, w_tail_ref)

    @pl.when((s == 0) | ((s + tiles_per_seq - 1) % tiles_per_seq == 0))
    def _():
        carry_ref[...] = jnp.zeros_like(carry_ref)

    @pl.when(s < n_tiles)
    def _():
        @pl.when(s % tiles_per_seq == 0)
        def _():
            state_ref[...] = jnp.zeros_like(state_ref)

        wo_bf_ref[...] = wo_ref[...].astype(BF16)

        def norm_strip(i, c):
            rows = pl.ds(pl.multiple_of(i * STRIP, STRIP), STRIP)
            x = x_ref[rows, :]
            ms = jnp.mean(x * x, axis=-1, keepdims=True)
            h_ref[rows, :] = (x * lax.rsqrt(ms + EPS) * ng_ref[...]).astype(BF16)
            return c

        lax.fori_loop(0, n_strips, norm_strip, 0, unroll=2)

        conv_units = [functools.partial(conv_strip, strip_rows(i)) for i in range(n_strips)]
        gate_units = [functools.partial(gate_strip, strip_rows(i)) for i in range(n_strips)]
        kdec_units = [functools.partial(kdec_strip, strip_rows(i)) for i in range(n_strips)]
        n_steps = n_chunks * GLA_HEADS

        mxu = (pieces(off_gd, LANES) + pieces(off_k, dk_total) + pieces(off_q, dk_total)
               + pieces(off_v, gla_width) + pieces(off_r, gla_width))
        n_k = dk_total // MXU_N
        vpu = [conv_units[:2]]
        vpu += _spread([gate_pre] + gate_units, n_k)
        vpu += [[later_sums, conv_units[2]]] + [kdec_units + [conv_units[3]]]
        vpu += [[] for _ in range(dk_total // MXU_N - 2)]
        rest = _spread(conv_units[4:], len(mxu) - len(vpu))
        rest[-1].append(functools.partial(state_update, 0))
        vpu += rest
        assert len(vpu) == len(mxu)

        mxu2 = pieces(off_h, 4 * conv_width)
        halves = []
        for step in range(n_steps):
            if step + 1 < n_steps:
                halves.append(functools.partial(state_update, step + 1))
            halves.append(functools.partial(readout, step))
        vpu2 = _spread(halves, len(mxu2))

        tokens = []
        for piece, units in zip(mxu + mxu2, vpu + vpu2):
            if len(tokens) >= PIN_LAG and tokens[-PIN_LAG]:
                pin(tokens[-PIN_LAG])
            piece()
            tokens.append([t for t in [unit() for unit in units] if t is not None])

    @pl.when(s == n_tiles)
    def _():
        def step(i, c):
            conv_strip(pl.ds(pl.multiple_of(i * STRIP, STRIP), STRIP))
            return c
        lax.fori_loop(0, n_strips, step, 0, unroll=2)


def _proj_mix(x2d, norm_g, w_in_t, conv_w, conv_b, w_up_pad, b_gate, gla_norm_g, w_out, *,
              batch, seq, conv_width, dk_total, gla_width, n_cols):
    t = MIX_T
    nt = seq // t
    n_tiles = batch * nt
    d_model = x2d.shape[1]
    wo_rows = w_out.shape[0] // n_tiles
    assert w_out.shape[0] % n_tiles == 0 and wo_rows % STRIP == 0
    main_cols = n_cols - LANES
    assert main_cols % W_ROWS == 0 and W_ROWS % MXU_N == 0 and 0 < w_in_t.shape[0] - main_cols <= LANES
    head_k = dk_total // GLA_HEADS
    head_v = gla_width // GLA_HEADS

    def whole(shape, **kw):
        return pl.BlockSpec(shape, lambda s: (0,) * len(shape), **kw)

    kern = functools.partial(_proj_mix_kernel, n_tiles=n_tiles, tiles_per_seq=nt,
                             conv_width=conv_width, dk_total=dk_total, gla_width=gla_width)
    return pl.pallas_call(
        kern,
        grid=(n_tiles + 1,),
        in_specs=[
            pl.BlockSpec((t, d_model), lambda s: (jnp.minimum(s, n_tiles - 1), 0)),
            whole(norm_g.shape),
            pl.BlockSpec(memory_space=pl.ANY),
            whole(conv_w.shape), whole(conv_b.shape), whole(w_up_pad.shape), whole(b_gate.shape),
            whole(gla_norm_g.shape),
            pl.BlockSpec((wo_rows, w_out.shape[1]), lambda s: (jnp.minimum(s, n_tiles - 1), 0)),
        ],
        out_specs=[
            pl.BlockSpec((t, conv_width), lambda s: (jnp.maximum(s - 1, 0), 0)),
            pl.BlockSpec((t, gla_width), lambda s: (jnp.minimum(s, n_tiles - 1), 0)),
            pl.BlockSpec((wo_rows, w_out.shape[1]), lambda s: (jnp.minimum(s, n_tiles - 1), 0)),
        ],
        out_shape=[jax.ShapeDtypeStruct((batch * seq, conv_width), BF16),
                   jax.ShapeDtypeStruct((batch * seq, gla_width), BF16),
                   jax.ShapeDtypeStruct(w_out.shape, BF16)],
        scratch_shapes=[pltpu.VMEM((main_cols // W_ROWS, d_model, W_ROWS), BF16),
                        pltpu.VMEM((d_model, LANES), BF16),
                        pltpu.VMEM((2, W_ROWS, d_model), F32),
                        pltpu.SemaphoreType.DMA((2,)),
                        pltpu.VMEM((t, d_model), BF16),
                        pltpu.VMEM((t, n_cols), BF16),
                        pltpu.VMEM((8, conv_width), F32),
                        pltpu.VMEM((GLA_HEADS, head_v, head_k), F32),
                        pltpu.VMEM((GLA_HEADS, head_v, head_k), BF16),
                        pltpu.VMEM((t, dk_total), F32),
                        pltpu.VMEM((t, 2 * dk_total), BF16),
                        pltpu.VMEM((t + 8, dk_total), F32),
                        pltpu.VMEM((t, dk_total), BF16)],
        compiler_params=pltpu.CompilerParams(
            dimension_semantics=("arbitrary",),
            vmem_limit_bytes=VMEM_LIMIT),
        name="proj_mix",
    )(x2d, norm_g, w_in_t, conv_w, conv_b, w_up_pad, b_gate, gla_norm_g, w_out)


def _out_proj_kernel(yc_ref, yg_ref, w_ref, x_ref, g_ref, o_ref, *, final_norm):
    kc = yc_ref.shape[1]
    z = (x_ref[...] + jnp.dot(yc_ref[...], w_ref[:kc, :], preferred_element_type=F32)
         + jnp.dot(yg_ref[...], w_ref[kc:, :], preferred_element_type=F32))
    if final_norm:
        ms = jnp.mean(z * z, axis=-1, keepdims=True)
        z = z * lax.rsqrt(ms + EPS) * g_ref[...]
    o_ref[...] = z


def _out_proj(y_conv, y_gla, w_out_bf16, x2d, final_g, *, final_norm):
    m, d = x2d.shape
    return pl.pallas_call(
        functools.partial(_out_proj_kernel, final_norm=final_norm),
        grid=(m // OUT_TM,),
        in_specs=[
            pl.BlockSpec((OUT_TM, y_conv.shape[1]), lambda i: (i, 0)),
            pl.BlockSpec((OUT_TM, y_gla.shape[1]), lambda i: (i, 0)),
            pl.BlockSpec(w_out_bf16.shape, lambda i: (0, 0)),
            pl.BlockSpec((OUT_TM, d), lambda i: (i, 0)),
            pl.BlockSpec((1, d), lambda i: (0, 0)),
        ],
        out_specs=pl.BlockSpec((OUT_TM, d), lambda i: (i, 0)),
        out_shape=jax.ShapeDtypeStruct((m, d), F32),
        compiler_params=pltpu.CompilerParams(
            dimension_semantics=("arbitrary",),
            vmem_limit_bytes=VMEM_LIMIT),
        name="out_proj",
    )(y_conv, y_gla, w_out_bf16, x2d, final_g)


def kernel(x, norm_g, w_in, conv_w, conv_b, gla_w_up, gla_b_gate, gla_norm_g, w_out, final_g):
    batch, seq, d_model = x.shape
    depth = norm_g.shape[0]
    conv_width = conv_w.shape[2]
    rank, dk_total = gla_w_up.shape[1], gla_w_up.shape[2]
    gla_width = gla_norm_g.shape[1] * gla_norm_g.shape[2]
    in_cols = w_in.shape[2]
    main_cols = in_cols - rank
    assert main_cols == 4 * conv_width + 2 * dk_total + 2 * gla_width
    assert main_cols % MXU_N == 0 and dk_total % MXU_N == 0 and rank <= LANES
    assert seq % MIX_T == 0 and MIX_T % CHUNK == 0 and (batch * seq) % OUT_TM == 0
    n_pad = main_cols + LANES

    x2d = x.reshape(batch * seq, d_model)
    for l in range(depth):
        w_up_p = jnp.pad(gla_w_up[l], ((0, LANES - rank), (0, 0))).astype(BF16)
        y_conv, y_gla, w_out_bf16 = _proj_mix(
            x2d, norm_g[l][None, :], w_in[l].T, conv_w[l], conv_b[l][None, :], w_up_p,
            gla_b_gate[l][None, :], gla_norm_g[l], w_out[l], batch=batch, seq=seq,
            conv_width=conv_width, dk_total=dk_total, gla_width=gla_width, n_cols=n_pad)
        x2d = _out_proj(y_conv, y_gla, w_out_bf16, x2d, final_g[None, :],
                        final_norm=(l == depth - 1))
    return x2d.reshape(batch, seq, d_model)
```

```python
import functools

import jax
import jax.numpy as jnp
from jax import lax
from jax.experimental import pallas as pl
from jax.experimental.pallas import tpu as pltpu

F32 = jnp.float32
BF16 = jnp.bfloat16

LANES = 128
MXU_N = 256
EPS = 1e-6
CHUNK = 64
GLA_HEADS = 4
GLA_TAU = 16.0

MIX_T = 256
STRIP = 16
COL_CHUNK = 512
W_ROWS = 512
PIN_LAG = 2
OUT_TM = 512
VMEM_LIMIT = 56 * 1024 * 1024


def _silu(v):
    return v * jax.nn.sigmoid(v)


def _log_sigmoid(v):
    return -(jnp.maximum(-v, 0.0) + jnp.log1p(jnp.exp(-jnp.abs(v))))


def _spread(units, n_slots):
    out = [[] for _ in range(n_slots)]
    for j, u in enumerate(units):
        out[(j * n_slots) // len(units)].append(u)
    return out


def _proj_mix_kernel(x_ref, x_hbm, ng_ref, w_hbm, convw_ref, convb_ref, wup_ref, bgate_ref, gng_ref, wo_ref,
                     yc_ref, yg_ref, wo_bf_ref,
                     w_ref, w_tail_ref, stage_ref, dma_sem, h_ref, hn_ref, proj_ref, carry_ref, state_ref, sbf_ref,
                     glog_ref, hilo_ref, rev_ref, kdec_ref,
                     *, n_tiles, tiles_per_seq, conv_width, dk_total, gla_width):
    t_rows = x_ref.shape[0]
    in_cols, d_model = w_hbm.shape
    n_strips = t_rows // STRIP
    n_chunks = t_rows // CHUNK
    head_k = dk_total // GLA_HEADS
    head_v = gla_width // GLA_HEADS
    off_h, off_b, off_c, off_z = (i * conv_width for i in range(4))
    off_q = 4 * conv_width
    off_k = off_q + dk_total
    off_v = off_k + dk_total
    off_r = off_v + gla_width
    off_gd = off_r + gla_width
    s = pl.program_id(0)

    def strip_rows(i):
        return slice(i * STRIP, (i + 1) * STRIP)

    def conv_strip(rows):
        token = None
        for c0 in range(0, conv_width, COL_CHUNK):
            cols = slice(c0, c0 + COL_CHUNK)

            def col(off):
                return proj_ref[rows, off + c0:off + c0 + COL_CHUNK].astype(F32)

            u = col(off_c) * col(off_h)
            ext = jnp.concatenate([carry_ref[:, cols], u], axis=0)
            u1 = pltpu.roll(ext, 1, 0)[8:, :]
            u2 = pltpu.roll(ext, 2, 0)[8:, :]
            carry_ref[:, cols] = u[STRIP - 8:, :]
            conv = (convb_ref[:, cols] + convw_ref[0:1, cols] * u2 + convw_ref[1:2, cols] * u1
                    + convw_ref[2:3, cols] * u)
            y = (col(off_b) * conv * _silu(col(off_z))).astype(yc_ref.dtype)
            yc_ref[rows, cols] = y
            token = token_of(y) if token is None else token | token_of(y)
        return token

    def gate_pre():
        glog_ref[...] = jnp.dot(proj_ref[:, off_gd:off_gd + LANES], wup_ref[...],
                                preferred_element_type=F32)

    def gate_strip(rows):
        glog = _log_sigmoid(glog_ref[rows, :] + bgate_ref[...]) / GLA_TAU
        g_hi = glog.astype(BF16)
        hilo_ref[rows, :dk_total] = g_hi
        hilo_ref[rows, dk_total:] = (glog - g_hi.astype(F32)).astype(BF16)

    def later_sums():
        r_i = lax.broadcasted_iota(jnp.int32, (t_rows + 8, t_rows), 0)
        c_i = lax.broadcasted_iota(jnp.int32, (t_rows + 8, t_rows), 1)
        c_chunk = c_i // CHUNK
        ones = ((c_i > r_i) & (c_chunk == r_i // CHUNK)) | (c_chunk == r_i - t_rows)
        sums = jnp.dot(jnp.where(ones, 1.0, 0.0).astype(BF16), hilo_ref[...],
                       preferred_element_type=F32)
        rev_ref[...] = sums[:, :dk_total] + sums[:, dk_total:]

    def kdec_strip(rows):
        k = proj_ref[rows, off_k:off_k + dk_total].astype(F32)
        kdec_ref[rows, :] = (k * jnp.exp(rev_ref[rows, :])).astype(BF16)

    def state_update(step):
        c, h = divmod(step, GLA_HEADS)
        rows = slice(c * CHUNK, (c + 1) * CHUNK)
        ks = slice(h * head_k, (h + 1) * head_k)
        decay = jnp.exp(rev_ref[t_rows + c:t_rows + c + 1, ks])
        v = proj_ref[rows, off_v + h * head_v:off_v + (h + 1) * head_v]
        inc = lax.dot_general(v, kdec_ref[rows, ks], (((0,), (0,)), ((), ())),
                              preferred_element_type=F32)
        s_new = state_ref[h] * decay + inc
        state_ref[h] = s_new
        sbf_ref[h] = s_new.astype(BF16)

    def readout(step):
        c, h = divmod(step, GLA_HEADS)
        rows = slice(c * CHUNK, (c + 1) * CHUNK)
        q = proj_ref[rows, off_q + h * head_k:off_q + (h + 1) * head_k]
        o = lax.dot_general(q, sbf_ref[h], (((1,), (1,)), ((), ())),
                            preferred_element_type=F32) * (head_k ** -0.5)
        ms = jnp.mean(o * o, axis=-1, keepdims=True)
        o = o * lax.rsqrt(ms + EPS) * gng_ref[h:h + 1, :]
        r = proj_ref[rows, off_r + h * head_v:off_r + (h + 1) * head_v].astype(F32)
        yg_ref[rows, h * head_v:(h + 1) * head_v] = (o * _silu(r)).astype(yg_ref.dtype)

    def token_of(y):
        bits = pltpu.bitcast(y, jnp.uint32)
        token = bits[:, :LANES]
        for l0 in range(LANES, bits.shape[1], LANES):
            token = token | bits[:, l0:l0 + LANES]
        return token

    def norm_strip(src, dst, rows):
        x = src[rows, :]
        ms = jnp.mean(x * x, axis=-1, keepdims=True)
        h = (x * lax.rsqrt(ms + EPS) * ng_ref[...]).astype(BF16)
        dst[rows, :] = h
        return token_of(h)

    def pin(tokens):
        acc = tokens[0]
        for t in tokens[1:]:
            acc = acc | t
        zero = pltpu.bitcast((acc >> 16) >> 16, F32)
        zero = jnp.concatenate([zero] * (STRIP // zero.shape[0]), axis=0)
        tile = h_ref[0:STRIP, 0:LANES].astype(F32)
        h_ref[0:STRIP, 0:LANES] = (tile + zero).astype(BF16)

    def proj_piece(c0, width):
        def piece():
            if c0 < w_ref.shape[0] * W_ROWS:
                w = w_ref[c0 // W_ROWS, :, c0 % W_ROWS:c0 % W_ROWS + width]
            else:
                w = w_tail_ref[...]
            proj_ref[:, c0:c0 + width] = jnp.dot(h_ref[...], w,
                                                 preferred_element_type=F32).astype(BF16)
        return piece

    def pieces(off, width):
        return [proj_piece(c0, min(MXU_N, off + width - c0)) for c0 in range(off, off + width, MXU_N)]

    @pl.when(s == 0)
    def _():
        proj_ref[:, :4 * conv_width] = jnp.zeros((t_rows, 4 * conv_width), BF16)

        n_full = in_cols // W_ROWS

        def chunk_copy(i, rows=W_ROWS):
            return pltpu.make_async_copy(w_hbm.at[pl.ds(i * W_ROWS, rows), :],
                                         stage_ref.at[i % 2, pl.ds(0, rows), :], dma_sem.at[i % 2])

        def transpose_chunk(slot, rows, dst):
            for k0 in range(0, d_model, MXU_N):
                blk = stage_ref[slot, 0:rows, k0:k0 + MXU_N]
                dst[k0:k0 + MXU_N, :] = blk.T.astype(BF16)

        chunk_copy(0).start()

        def load_chunk(i, c):
            @pl.when(i + 1 < n_full)
            def _():
                chunk_copy(i + 1).start()
            chunk_copy(i).wait()
            transpose_chunk(i % 2, W_ROWS, w_ref.at[i])
            return c

        lax.fori_loop(0, n_full, load_chunk, 0)

        tail_rows = in_cols - n_full * W_ROWS
        stage_ref[n_full % 2, tail_rows:LANES, :] = jnp.zeros((LANES - tail_rows, d_model), F32)
        chunk_copy(n_full, tail_rows).start()
        chunk_copy(n_full, tail_rows).wait()
        transpose_chunk(n_full % 2, LANES, w_tail_ref)

        x0_copy = pltpu.make_async_copy(x_hbm.at[pl.ds(0, t_rows), :],
                                        stage_ref.at[0, pl.ds(0, t_rows), :], dma_sem.at[0])
        x0_copy.start()
        x0_copy.wait()

        def norm_first(i, c):
            norm_strip(stage_ref.at[0], h_ref, pl.ds(pl.multiple_of(i * STRIP, STRIP), STRIP))
            return c

        lax.fori_loop(0, n_strips, norm_first, 0, unroll=2)

    @pl.when((s == 0) | ((s + tiles_per_seq - 1) % tiles_per_seq == 0))
    def _():
        carry_ref[...] = jnp.zeros_like(carry_ref)

    @pl.when(s < n_tiles)
    def _():
        @pl.when(s % tiles_per_seq == 0)
        def _():
            state_ref[...] = jnp.zeros_like(state_ref)

        wo_bf_ref[...] = wo_ref[...].astype(BF16)

        @pl.when(s > 0)
        def _():
            h_ref[...] = hn_ref[...]

        norm_units = [functools.partial(norm_strip, x_ref, hn_ref, strip_rows(i))
                      for i in range(n_strips)]
        conv_units = [functools.partial(conv_strip, strip_rows(i)) for i in range(n_strips)]
        gate_units = [functools.partial(gate_strip, strip_rows(i)) for i in range(n_strips)]
        kdec_units = [functools.partial(kdec_strip, strip_rows(i)) for i in range(n_strips)]
        n_steps = n_chunks * GLA_HEADS

        mxu = (pieces(off_gd, LANES) + pieces(off_k, dk_total) + pieces(off_q, dk_total)
               + pieces(off_v, gla_width) + pieces(off_r, gla_width))
        n_k = dk_total // MXU_N
        vpu = [conv_units[:2]]
        vpu += _spread([gate_pre] + gate_units, n_k)
        vpu += [[later_sums, conv_units[2]]] + [kdec_units + [conv_units[3]]]
        vpu += [[] for _ in range(dk_total // MXU_N - 2)]
        rest = _spread(conv_units[4:], len(mxu) - len(vpu))
        rest[-1].append(functools.partial(state_update, 0))
        vpu += rest
        assert len(vpu) == len(mxu)

        mxu2 = pieces(off_h, 4 * conv_width)
        halves = []
        for step in range(n_steps):
            if step + 1 < n_steps:
                halves.append(functools.partial(state_update, step + 1))
            halves.append(functools.partial(readout, step))
        vpu2 = [a + b for a, b in zip(_spread(halves, len(mxu2)), _spread(norm_units, len(mxu2)))]

        tokens = []
        for piece, units in zip(mxu + mxu2, vpu + vpu2):
            if len(tokens) >= PIN_LAG and tokens[-PIN_LAG]:
                pin(tokens[-PIN_LAG])
            piece()
            tokens.append([t for t in [unit() for unit in units] if t is not None])

    @pl.when(s == n_tiles)
    def _():
        def step(i, c):
            conv_strip(pl.ds(pl.multiple_of(i * STRIP, STRIP), STRIP))
            return c
        lax.fori_loop(0, n_strips, step, 0, unroll=2)


def _proj_mix(x2d, norm_g, w_in_t, conv_w, conv_b, w_up_pad, b_gate, gla_norm_g, w_out, *,
              batch, seq, conv_width, dk_total, gla_width, n_cols):
    t = MIX_T
    nt = seq // t
    n_tiles = batch * nt
    d_model = x2d.shape[1]
    wo_rows = w_out.shape[0] // n_tiles
    assert w_out.shape[0] % n_tiles == 0 and wo_rows % STRIP == 0 and t <= W_ROWS
    main_cols = n_cols - LANES
    assert main_cols % W_ROWS == 0 and W_ROWS % MXU_N == 0 and 0 < w_in_t.shape[0] - main_cols <= LANES
    head_k = dk_total // GLA_HEADS
    head_v = gla_width // GLA_HEADS

    def whole(shape, **kw):
        return pl.BlockSpec(shape, lambda s: (0,) * len(shape), **kw)

    kern = functools.partial(_proj_mix_kernel, n_tiles=n_tiles, tiles_per_seq=nt,
                             conv_width=conv_width, dk_total=dk_total, gla_width=gla_width)
    return pl.pallas_call(
        kern,
        grid=(n_tiles + 1,),
        in_specs=[
            pl.BlockSpec((t, d_model), lambda s: (jnp.minimum(s + 1, n_tiles - 1), 0)),
            pl.BlockSpec(memory_space=pl.ANY),
            whole(norm_g.shape),
            pl.BlockSpec(memory_space=pl.ANY),
            whole(conv_w.shape), whole(conv_b.shape), whole(w_up_pad.shape), whole(b_gate.shape),
            whole(gla_norm_g.shape),
            pl.BlockSpec((wo_rows, w_out.shape[1]), lambda s: (jnp.minimum(s, n_tiles - 1), 0)),
        ],
        out_specs=[
            pl.BlockSpec((t, conv_width), lambda s: (jnp.maximum(s - 1, 0), 0)),
            pl.BlockSpec((t, gla_width), lambda s: (jnp.minimum(s, n_tiles - 1), 0)),
            pl.BlockSpec((wo_rows, w_out.shape[1]), lambda s: (jnp.minimum(s, n_tiles - 1), 0)),
        ],
        out_shape=[jax.ShapeDtypeStruct((batch * seq, conv_width), BF16),
                   jax.ShapeDtypeStruct((batch * seq, gla_width), BF16),
                   jax.ShapeDtypeStruct(w_out.shape, BF16)],
        scratch_shapes=[pltpu.VMEM((main_cols // W_ROWS, d_model, W_ROWS), BF16),
                        pltpu.VMEM((d_model, LANES), BF16),
                        pltpu.VMEM((2, W_ROWS, d_model), F32),
                        pltpu.SemaphoreType.DMA((2,)),
                        pltpu.VMEM((t, d_model), BF16),
                        pltpu.VMEM((t, d_model), BF16),
                        pltpu.VMEM((t, n_cols), BF16),
                        pltpu.VMEM((8, conv_width), F32),
                        pltpu.VMEM((GLA_HEADS, head_v, head_k), F32),
                        pltpu.VMEM((GLA_HEADS, head_v, head_k), BF16),
                        pltpu.VMEM((t, dk_total), F32),
                        pltpu.VMEM((t, 2 * dk_total), BF16),
                        pltpu.VMEM((t + 8, dk_total), F32),
                        pltpu.VMEM((t, dk_total), BF16)],
        compiler_params=pltpu.CompilerParams(
            dimension_semantics=("arbitrary",),
            vmem_limit_bytes=VMEM_LIMIT),
        name="proj_mix",
    )(x2d, x2d, norm_g, w_in_t, conv_w, conv_b, w_up_pad, b_gate, gla_norm_g, w_out)


def _out_proj_kernel(yc_ref, yg_ref, w_ref, x_ref, g_ref, o_ref, *, final_norm):
    kc = yc_ref.shape[1]
    z = (x_ref[...] + jnp.dot(yc_ref[...], w_ref[:kc, :], preferred_element_type=F32)
         + jnp.dot(yg_ref[...], w_ref[kc:, :], preferred_element_type=F32))
    if final_norm:
        ms = jnp.mean(z * z, axis=-1, keepdims=True)
        z = z * lax.rsqrt(ms + EPS) * g_ref[...]
    o_ref[...] = z


def _out_proj(y_conv, y_gla, w_out_bf16, x2d, final_g, *, final_norm):
    m, d = x2d.shape
    return pl.pallas_call(
        functools.partial(_out_proj_kernel, final_norm=final_norm),
        grid=(m // OUT_TM,),
        in_specs=[
            pl.BlockSpec((OUT_TM, y_conv.shape[1]), lambda i: (i, 0)),
            pl.BlockSpec((OUT_TM, y_gla.shape[1]), lambda i: (i, 0)),
            pl.BlockSpec(w_out_bf16.shape, lambda i: (0, 0)),
            pl.BlockSpec((OUT_TM, d), lambda i: (i, 0)),
            pl.BlockSpec((1, d), lambda i: (0, 0)),
        ],
        out_specs=pl.BlockSpec((OUT_TM, d), lambda i: (i, 0)),
        out_shape=jax.ShapeDtypeStruct((m, d), F32),
        compiler_params=pltpu.CompilerParams(
            dimension_semantics=("arbitrary",),
            vmem_limit_bytes=VMEM_LIMIT),
        name="out_proj",
    )(y_conv, y_gla, w_out_bf16, x2d, final_g)


def kernel(x, norm_g, w_in, conv_w, conv_b, gla_w_up, gla_b_gate, gla_norm_g, w_out, final_g):
    batch, seq, d_model = x.shape
    depth = norm_g.shape[0]
    conv_width = conv_w.shape[2]
    rank, dk_total = gla_w_up.shape[1], gla_w_up.shape[2]
    gla_width = gla_norm_g.shape[1] * gla_norm_g.shape[2]
    in_cols = w_in.shape[2]
    main_cols = in_cols - rank
    assert main_cols == 4 * conv_width + 2 * dk_total + 2 * gla_width
    assert main_cols % MXU_N == 0 and dk_total % MXU_N == 0 and rank <= LANES
    assert seq % MIX_T == 0 and MIX_T % CHUNK == 0 and (batch * seq) % OUT_TM == 0
    n_pad = main_cols + LANES

    x2d = x.reshape(batch * seq, d_model)
    for l in range(depth):
        w_up_p = jnp.pad(gla_w_up[l], ((0, LANES - rank), (0, 0))).astype(BF16)
        y_conv, y_gla, w_out_bf16 = _proj_mix(
            x2d, norm_g[l][None, :], w_in[l].T, conv_w[l], conv_b[l][None, :], w_up_p,
            gla_b_gate[l][None, :], gla_norm_g[l], w_out[l], batch=batch, seq=seq,
            conv_width=conv_width, dk_total=dk_total, gla_width=gla_width, n_cols=n_pad)
        x2d = _out_proj(y_conv, y_gla, w_out_bf16, x2d, final_g[None, :],
                        final_norm=(l == depth - 1))
    return x2d.reshape(batch, seq, d_model)
```

```python
import functools

import jax
import jax.numpy as jnp
from jax import lax
from jax.experimental import pallas as pl
from jax.experimental.pallas import tpu as pltpu

F32 = jnp.float32
BF16 = jnp.bfloat16

LANES = 128
MXU_N = 256
EPS = 1e-6
CHUNK = 64
GLA_HEADS = 4
GLA_TAU = 16.0

MIX_T = 256
STRIP = 16
COL_CHUNK = 512
W_ROWS = 256
W_SLOTS = 4
IDLE_TAIL_PIECES = 2
PIN_LAG = 2
OUT_TM = 512
VMEM_LIMIT = 56 * 1024 * 1024


def _silu(v):
    return v * jax.nn.sigmoid(v)


def _log_sigmoid(v):
    return -(jnp.maximum(-v, 0.0) + jnp.log1p(jnp.exp(-jnp.abs(v))))


def _spread(units, n_slots):
    out = [[] for _ in range(n_slots)]
    for j, u in enumerate(units):
        out[(j * n_slots) // len(units)].append(u)
    return out


def _proj_mix_kernel(x_ref, x_hbm, ng_ref, w_hbm, convw_ref, convb_ref, wup_ref, bgate_ref, gng_ref, wo_ref,
                     yc_ref, yg_ref, wo_bf_ref,
                     w_ref, w_tail_ref, stage_ref, dma_sem, h_ref, hn_ref, proj_ref, carry_ref, state_ref, sbf_ref,
                     glog_ref, hilo_ref, rev_ref, kdec_ref,
                     *, n_tiles, tiles_per_seq, conv_width, dk_total, gla_width):
    t_rows = x_ref.shape[0]
    in_cols, d_model = w_hbm.shape
    n_strips = t_rows // STRIP
    n_chunks = t_rows // CHUNK
    head_k = dk_total // GLA_HEADS
    head_v = gla_width // GLA_HEADS
    off_h, off_b, off_c, off_z = (i * conv_width for i in range(4))
    off_q = 4 * conv_width
    off_k = off_q + dk_total
    off_v = off_k + dk_total
    off_r = off_v + gla_width
    off_gd = off_r + gla_width
    s = pl.program_id(0)

    def strip_rows(i):
        return slice(i * STRIP, (i + 1) * STRIP)

    def conv_strip(rows):
        token = None
        for c0 in range(0, conv_width, COL_CHUNK):
            cols = slice(c0, c0 + COL_CHUNK)

            def col(off):
                return proj_ref[rows, off + c0:off + c0 + COL_CHUNK].astype(F32)

            u = col(off_c) * col(off_h)
            ext = jnp.concatenate([carry_ref[:, cols], u], axis=0)
            u1 = pltpu.roll(ext, 1, 0)[8:, :]
            u2 = pltpu.roll(ext, 2, 0)[8:, :]
            carry_ref[:, cols] = u[STRIP - 8:, :]
            conv = (convb_ref[:, cols] + convw_ref[0:1, cols] * u2 + convw_ref[1:2, cols] * u1
                    + convw_ref[2:3, cols] * u)
            y = (col(off_b) * conv * _silu(col(off_z))).astype(yc_ref.dtype)
            yc_ref[rows, cols] = y
            token = token_of(y) if token is None else token | token_of(y)
        return token

    def gate_pre():
        glog_ref[...] = jnp.dot(proj_ref[:, off_gd:off_gd + LANES], wup_ref[...],
                                preferred_element_type=F32)

    def gate_strip(rows):
        glog = _log_sigmoid(glog_ref[rows, :] + bgate_ref[...]) / GLA_TAU
        g_hi = glog.astype(BF16)
        hilo_ref[rows, :dk_total] = g_hi
        hilo_ref[rows, dk_total:] = (glog - g_hi.astype(F32)).astype(BF16)

    def later_sums():
        r_i = lax.broadcasted_iota(jnp.int32, (t_rows + 8, t_rows), 0)
        c_i = lax.broadcasted_iota(jnp.int32, (t_rows + 8, t_rows), 1)
        c_chunk = c_i // CHUNK
        ones = ((c_i > r_i) & (c_chunk == r_i // CHUNK)) | (c_chunk == r_i - t_rows)
        sums = jnp.dot(jnp.where(ones, 1.0, 0.0).astype(BF16), hilo_ref[...],
                       preferred_element_type=F32)
        rev_ref[...] = sums[:, :dk_total] + sums[:, dk_total:]

    def kdec_strip(rows):
        k = proj_ref[rows, off_k:off_k + dk_total].astype(F32)
        kdec_ref[rows, :] = (k * jnp.exp(rev_ref[rows, :])).astype(BF16)

    def state_update(step):
        c, h = divmod(step, GLA_HEADS)
        rows = slice(c * CHUNK, (c + 1) * CHUNK)
        ks = slice(h * head_k, (h + 1) * head_k)
        decay = jnp.exp(rev_ref[t_rows + c:t_rows + c + 1, ks])
        v = proj_ref[rows, off_v + h * head_v:off_v + (h + 1) * head_v]
        inc = lax.dot_general(v, kdec_ref[rows, ks], (((0,), (0,)), ((), ())),
                              preferred_element_type=F32)
        s_new = state_ref[h] * decay + inc
        state_ref[h] = s_new
        sbf_ref[h] = s_new.astype(BF16)

    def readout(step):
        c, h = divmod(step, GLA_HEADS)
        rows = slice(c * CHUNK, (c + 1) * CHUNK)
        q = proj_ref[rows, off_q + h * head_k:off_q + (h + 1) * head_k]
        o = lax.dot_general(q, sbf_ref[h], (((1,), (1,)), ((), ())),
                            preferred_element_type=F32) * (head_k ** -0.5)
        ms = jnp.mean(o * o, axis=-1, keepdims=True)
        o = o * lax.rsqrt(ms + EPS) * gng_ref[h:h + 1, :]
        r = proj_ref[rows, off_r + h * head_v:off_r + (h + 1) * head_v].astype(F32)
        yg_ref[rows, h * head_v:(h + 1) * head_v] = (o * _silu(r)).astype(yg_ref.dtype)

    def token_of(y):
        bits = pltpu.bitcast(y, jnp.uint32)
        token = bits[:, :LANES]
        for l0 in range(LANES, bits.shape[1], LANES):
            token = token | bits[:, l0:l0 + LANES]
        return token

    def norm_strip(src, dst, rows):
        x = src[rows, :]
        ms = jnp.mean(x * x, axis=-1, keepdims=True)
        h = (x * lax.rsqrt(ms + EPS) * ng_ref[...]).astype(BF16)
        dst[rows, :] = h
        return token_of(h)

    def pin(tokens):
        acc = tokens[0]
        for t in tokens[1:]:
            acc = acc | t
        zero = pltpu.bitcast((acc >> 16) >> 16, F32)
        zero = jnp.concatenate([zero] * (STRIP // zero.shape[0]), axis=0)
        tile = h_ref[0:STRIP, 0:LANES].astype(F32)
        h_ref[0:STRIP, 0:LANES] = (tile + zero).astype(BF16)

    def proj_piece(c0, width):
        def piece():
            if c0 < w_ref.shape[0] * W_ROWS:
                w = w_ref[c0 // W_ROWS, :, c0 % W_ROWS:c0 % W_ROWS + width]
            else:
                w = w_tail_ref[...]
            proj_ref[:, c0:c0 + width] = jnp.dot(h_ref[...], w,
                                                 preferred_element_type=F32).astype(BF16)
        return piece

    def pieces(off, width):
        return [proj_piece(c0, min(MXU_N, off + width - c0)) for c0 in range(off, off + width, MXU_N)]

    @pl.when(s == 0)
    def _():
        proj_ref[:, :4 * conv_width] = jnp.zeros((t_rows, 4 * conv_width), BF16)

        n_full = in_cols // W_ROWS

        def chunk_copy(i, rows=W_ROWS):
            slot = i % W_SLOTS
            return pltpu.make_async_copy(w_hbm.at[pl.ds(i * W_ROWS, rows), :],
                                         stage_ref.at[slot, pl.ds(0, rows), :], dma_sem.at[slot])

        def transpose_chunk(slot, rows, dst):
            for k0 in range(0, d_model, MXU_N):
                blk = stage_ref[slot, 0:rows, k0:k0 + MXU_N]
                dst[k0:k0 + MXU_N, :] = blk.T.astype(BF16)

        for i in range(W_SLOTS - 1):
            chunk_copy(i).start()

        def load_chunk(i, c):
            @pl.when(i + W_SLOTS - 1 < n_full)
            def _():
                chunk_copy(i + W_SLOTS - 1).start()
            chunk_copy(i).wait()
            transpose_chunk(i % W_SLOTS, W_ROWS, w_ref.at[i])
            return c

        lax.fori_loop(0, n_full, load_chunk, 0)

        tail_rows = in_cols - n_full * W_ROWS
        stage_ref[n_full % W_SLOTS, tail_rows:LANES, :] = jnp.zeros((LANES - tail_rows, d_model), F32)
        chunk_copy(n_full, tail_rows).start()
        chunk_copy(n_full, tail_rows).wait()
        transpose_chunk(n_full % W_SLOTS, LANES, w_tail_ref)

        x0_copy = pltpu.make_async_copy(x_hbm.at[pl.ds(0, t_rows), :],
                                        stage_ref.at[0, pl.ds(0, t_rows), :], dma_sem.at[0])
        x0_copy.start()
        x0_copy.wait()

        def norm_first(i, c):
            norm_strip(stage_ref.at[0], hn_ref, pl.ds(pl.multiple_of(i * STRIP, STRIP), STRIP))
            return c

        lax.fori_loop(0, n_strips, norm_first, 0, unroll=2)

    @pl.when((s == 0) | ((s + tiles_per_seq - 1) % tiles_per_seq == 0))
    def _():
        carry_ref[...] = jnp.zeros_like(carry_ref)

    @pl.when(s < n_tiles)
    def _():
        @pl.when(s % tiles_per_seq == 0)
        def _():
            state_ref[...] = jnp.zeros_like(state_ref)

        wo_bf_ref[...] = wo_ref[...].astype(BF16)

        h_ref[...] = hn_ref[...]

        norm_units = [functools.partial(norm_strip, x_ref, hn_ref, strip_rows(i))
                      for i in range(n_strips)]
        conv_units = [functools.partial(conv_strip, strip_rows(i)) for i in range(n_strips)]
        gate_units = [functools.partial(gate_strip, strip_rows(i)) for i in range(n_strips)]
        kdec_units = [functools.partial(kdec_strip, strip_rows(i)) for i in range(n_strips)]
        n_steps = n_chunks * GLA_HEADS

        mxu = (pieces(off_gd, LANES) + pieces(off_k, dk_total) + pieces(off_q, dk_total)
               + pieces(off_v, gla_width) + pieces(off_r, gla_width))
        n_k = dk_total // MXU_N
        vpu = [conv_units[:2]]
        vpu += _spread([gate_pre] + gate_units, n_k)
        vpu += [[later_sums, conv_units[2]]] + [kdec_units + [conv_units[3]]]
        vpu += [[] for _ in range(dk_total // MXU_N - 2)]
        rest = _spread(conv_units[4:], len(mxu) - len(vpu))
        rest[-1].append(functools.partial(state_update, 0))
        vpu += rest
        assert len(vpu) == len(mxu)

        mxu2 = pieces(off_h, 4 * conv_width)
        halves = []
        for step in range(n_steps):
            if step + 1 < n_steps:
                halves.append(functools.partial(state_update, step + 1))
            halves.append(functools.partial(readout, step))
        busy = len(mxu2) - IDLE_TAIL_PIECES
        vpu2 = [a + b for a, b in zip(_spread(halves, busy), _spread(norm_units, busy))]
        vpu2 += [[] for _ in range(IDLE_TAIL_PIECES)]

        tokens = []
        for piece, units in zip(mxu + mxu2, vpu + vpu2):
            if len(tokens) >= PIN_LAG and tokens[-PIN_LAG]:
                pin(tokens[-PIN_LAG])
            piece()
            tokens.append([t for t in [unit() for unit in units] if t is not None])

    @pl.when(s == n_tiles)
    def _():
        def step(i, c):
            conv_strip(pl.ds(pl.multiple_of(i * STRIP, STRIP), STRIP))
            return c
        lax.fori_loop(0, n_strips, step, 0, unroll=2)


def _proj_mix(x2d, norm_g, w_in_t, conv_w, conv_b, w_up_pad, b_gate, gla_norm_g, w_out, *,
              batch, seq, conv_width, dk_total, gla_width, n_cols):
    t = MIX_T
    nt = seq // t
    n_tiles = batch * nt
    d_model = x2d.shape[1]
    wo_rows = w_out.shape[0] // n_tiles
    assert w_out.shape[0] % n_tiles == 0 and wo_rows % STRIP == 0 and t <= W_ROWS
    main_cols = n_cols - LANES
    assert main_cols % W_ROWS == 0 and W_ROWS % MXU_N == 0 and 0 < w_in_t.shape[0] - main_cols <= LANES
    head_k = dk_total // GLA_HEADS
    head_v = gla_width // GLA_HEADS

    def whole(shape, **kw):
        return pl.BlockSpec(shape, lambda s: (0,) * len(shape), **kw)

    kern = functools.partial(_proj_mix_kernel, n_tiles=n_tiles, tiles_per_seq=nt,
                             conv_width=conv_width, dk_total=dk_total, gla_width=gla_width)
    return pl.pallas_call(
        kern,
        grid=(n_tiles + 1,),
        in_specs=[
            pl.BlockSpec((t, d_model), lambda s: (jnp.minimum(s + 1, n_tiles - 1), 0)),
            pl.BlockSpec(memory_space=pl.ANY),
            whole(norm_g.shape),
            pl.BlockSpec(memory_space=pl.ANY),
            whole(conv_w.shape), whole(conv_b.shape), whole(w_up_pad.shape), whole(b_gate.shape),
            whole(gla_norm_g.shape),
            pl.BlockSpec((wo_rows, w_out.shape[1]), lambda s: (jnp.minimum(s, n_tiles - 1), 0)),
        ],
        out_specs=[
            pl.BlockSpec((t, conv_width), lambda s: (jnp.maximum(s - 1, 0), 0)),
            pl.BlockSpec((t, gla_width), lambda s: (jnp.minimum(s, n_tiles - 1), 0)),
            pl.BlockSpec((wo_rows, w_out.shape[1]), lambda s: (jnp.minimum(s, n_tiles - 1), 0)),
        ],
        out_shape=[jax.ShapeDtypeStruct((batch * seq, conv_width), BF16),
                   jax.ShapeDtypeStruct((batch * seq, gla_width), BF16),
                   jax.ShapeDtypeStruct(w_out.shape, BF16)],
        scratch_shapes=[pltpu.VMEM((main_cols // W_ROWS, d_model, W_ROWS), BF16),
                        pltpu.VMEM((d_model, LANES), BF16),
                        pltpu.VMEM((W_SLOTS, W_ROWS, d_model), F32),
                        pltpu.SemaphoreType.DMA((W_SLOTS,)),
                        pltpu.VMEM((t, d_model), BF16),
                        pltpu.VMEM((t, d_model), BF16),
                        pltpu.VMEM((t, n_cols), BF16),
                        pltpu.VMEM((8, conv_width), F32),
                        pltpu.VMEM((GLA_HEADS, head_v, head_k), F32),
                        pltpu.VMEM((GLA_HEADS, head_v, head_k), BF16),
                        pltpu.VMEM((t, dk_total), F32),
                        pltpu.VMEM((t, 2 * dk_total), BF16),
                        pltpu.VMEM((t + 8, dk_total), F32),
                        pltpu.VMEM((t, dk_total), BF16)],
        compiler_params=pltpu.CompilerParams(
            dimension_semantics=("arbitrary",),
            vmem_limit_bytes=VMEM_LIMIT),
        name="proj_mix",
    )(x2d, x2d, norm_g, w_in_t, conv_w, conv_b, w_up_pad, b_gate, gla_norm_g, w_out)


def _out_proj_kernel(yc_ref, yg_ref, w_ref, x_ref, g_ref, o_ref, *, final_norm):
    kc = yc_ref.shape[1]
    z = (x_ref[...] + jnp.dot(yc_ref[...], w_ref[:kc, :], preferred_element_type=F32)
         + jnp.dot(yg_ref[...], w_ref[kc:, :], preferred_element_type=F32))
    if final_norm:
        ms = jnp.mean(z * z, axis=-1, keepdims=True)
        z = z * lax.rsqrt(ms + EPS) * g_ref[...]
    o_ref[...] = z


def _out_proj(y_conv, y_gla, w_out_bf16, x2d, final_g, *, final_norm):
    m, d = x2d.shape
    return pl.pallas_call(
        functools.partial(_out_proj_kernel, final_norm=final_norm),
        grid=(m // OUT_TM,),
        in_specs=[
            pl.BlockSpec((OUT_TM, y_conv.shape[1]), lambda i: (i, 0)),
            pl.BlockSpec((OUT_TM, y_gla.shape[1]), lambda i: (i, 0)),
            pl.BlockSpec(w_out_bf16.shape, lambda i: (0, 0)),
            pl.BlockSpec((OUT_TM, d), lambda i: (i, 0)),
            pl.BlockSpec((1, d), lambda i: (0, 0)),
        ],
        out_specs=pl.BlockSpec((OUT_TM, d), lambda i: (i, 0)),
        out_shape=jax.ShapeDtypeStruct((m, d), F32),
        compiler_params=pltpu.CompilerParams(
            dimension_semantics=("arbitrary",),
            vmem_limit_bytes=VMEM_LIMIT),
        name="out_proj",
    )(y_conv, y_gla, w_out_bf16, x2d, final_g)


def kernel(x, norm_g, w_in, conv_w, conv_b, gla_w_up, gla_b_gate, gla_norm_g, w_out, final_g):
    batch, seq, d_model = x.shape
    depth = norm_g.shape[0]
    conv_width = conv_w.shape[2]
    rank, dk_total = gla_w_up.shape[1], gla_w_up.shape[2]
    gla_width = gla_norm_g.shape[1] * gla_norm_g.shape[2]
    in_cols = w_in.shape[2]
    main_cols = in_cols - rank
    assert main_cols == 4 * conv_width + 2 * dk_total + 2 * gla_width
    assert main_cols % MXU_N == 0 and dk_total % MXU_N == 0 and rank <= LANES
    assert seq % MIX_T == 0 and MIX_T % CHUNK == 0 and (batch * seq) % OUT_TM == 0
    n_pad = main_cols + LANES

    x2d = x.reshape(batch * seq, d_model)
    for l in range(depth):
        w_up_p = jnp.pad(gla_w_up[l], ((0, LANES - rank), (0, 0))).astype(BF16)
        y_conv, y_gla, w_out_bf16 = _proj_mix(
            x2d, norm_g[l][None, :], w_in[l].T, conv_w[l], conv_b[l][None, :], w_up_p,
            gla_b_gate[l][None, :], gla_norm_g[l], w_out[l], batch=batch, seq=seq,
            conv_width=conv_width, dk_total=dk_total, gla_width=gla_width, n_cols=n_pad)
        x2d = _out_proj(y_conv, y_gla, w_out_bf16, x2d, final_g[None, :],
                        final_norm=(l == depth - 1))
    return x2d.reshape(batch, seq, d_model)
```

```python
import functools

import jax
import jax.numpy as jnp
from jax import lax
from jax.experimental import pallas as pl
from jax.experimental.pallas import tpu as pltpu

F32 = jnp.float32
BF16 = jnp.bfloat16

LANES = 128
MXU_N = 256
EPS = 1e-6
CHUNK = 64
GLA_HEADS = 4
GLA_TAU = 16.0

MIX_T = 256
STRIP = 16
COL_CHUNK = 512
W_ROWS = 256
W_SLOTS = 4
IDLE_TAIL_PIECES = 2
PIN_LAG = 2
OUT_TM = 512
VMEM_LIMIT = 56 * 1024 * 1024


def _silu(v):
    return v * jax.nn.sigmoid(v)


def _log_sigmoid(v):
    return -(jnp.maximum(-v, 0.0) + jnp.log1p(jnp.exp(-jnp.abs(v))))


def _spread(units, n_slots):
    out = [[] for _ in range(n_slots)]
    for j, u in enumerate(units):
        out[(j * n_slots) // len(units)].append(u)
    return out


def _proj_mix_kernel(x_ref, x_hbm, ng_ref, w_hbm, convw_ref, convb_ref, wup_ref, bgate_ref, gng_ref, wo_ref,
                     yc_ref, yg_ref, wo_bf_ref,
                     w_ref, w_tail_ref, stage_ref, dma_sem, h_ref, hn_ref, proj_ref, carry_ref, state_ref, sbf_ref,
                     glog_ref, hilo_ref, rev_ref, kdec_ref,
                     *, n_tiles, tiles_per_seq, conv_width, dk_total, gla_width):
    t_rows = x_ref.shape[0]
    in_cols, d_model = w_hbm.shape
    n_strips = t_rows // STRIP
    n_chunks = t_rows // CHUNK
    head_k = dk_total // GLA_HEADS
    head_v = gla_width // GLA_HEADS
    off_h, off_b, off_c, off_z = (i * conv_width for i in range(4))
    off_q = 4 * conv_width
    off_k = off_q + dk_total
    off_v = off_k + dk_total
    off_r = off_v + gla_width
    off_gd = off_r + gla_width
    s = pl.program_id(0)

    def strip_rows(i):
        return slice(i * STRIP, (i + 1) * STRIP)

    def conv_strip(rows):
        token = None
        for c0 in range(0, conv_width, COL_CHUNK):
            cols = slice(c0, c0 + COL_CHUNK)

            def col(off):
                return proj_ref[rows, off + c0:off + c0 + COL_CHUNK].astype(F32)

            u = col(off_c) * col(off_h)
            ext = jnp.concatenate([carry_ref[:, cols], u], axis=0)
            u1 = pltpu.roll(ext, 1, 0)[8:, :]
            u2 = pltpu.roll(ext, 2, 0)[8:, :]
            carry_ref[:, cols] = u[STRIP - 8:, :]
            conv = (convb_ref[:, cols] + convw_ref[0:1, cols] * u2 + convw_ref[1:2, cols] * u1
                    + convw_ref[2:3, cols] * u)
            y = (col(off_b) * conv * _silu(col(off_z))).astype(yc_ref.dtype)
            yc_ref[rows, cols] = y
            token = token_of(y) if token is None else token | token_of(y)
        return token

    def gate_pre():
        glog_ref[...] = jnp.dot(proj_ref[:, off_gd:off_gd + LANES], wup_ref[...],
                                preferred_element_type=F32)

    def gate_strip(rows):
        glog = _log_sigmoid(glog_ref[rows, :] + bgate_ref[...]) / GLA_TAU
        g_hi = glog.astype(BF16)
        hilo_ref[rows, :dk_total] = g_hi
        hilo_ref[rows, dk_total:] = (glog - g_hi.astype(F32)).astype(BF16)

    def later_sums():
        r_i = lax.broadcasted_iota(jnp.int32, (t_rows + 8, t_rows), 0)
        c_i = lax.broadcasted_iota(jnp.int32, (t_rows + 8, t_rows), 1)
        c_chunk = c_i // CHUNK
        ones = ((c_i > r_i) & (c_chunk == r_i // CHUNK)) | (c_chunk == r_i - t_rows)
        sums = jnp.dot(jnp.where(ones, 1.0, 0.0).astype(BF16), hilo_ref[...],
                       preferred_element_type=F32)
        rev_ref[...] = sums[:, :dk_total] + sums[:, dk_total:]

    def kdec_strip(rows):
        k = proj_ref[rows, off_k:off_k + dk_total].astype(F32)
        kdec_ref[rows, :] = (k * jnp.exp(rev_ref[rows, :])).astype(BF16)

    def state_update(step):
        c, h = divmod(step, GLA_HEADS)
        rows = slice(c * CHUNK, (c + 1) * CHUNK)
        ks = slice(h * head_k, (h + 1) * head_k)
        decay = jnp.exp(rev_ref[t_rows + c:t_rows + c + 1, ks])
        r_i = lax.broadcasted_iota(jnp.int32, (head_k, head_k), 0)
        c_i = lax.broadcasted_iota(jnp.int32, (head_k, head_k), 1)
        decay_col = jnp.sum(jnp.where(r_i == c_i, decay, 0.0), axis=1, keepdims=True)
        v = proj_ref[rows, off_v + h * head_v:off_v + (h + 1) * head_v]
        inc = lax.dot_general(kdec_ref[rows, ks], v, (((0,), (0,)), ((), ())),
                              preferred_element_type=F32)
        s_new = state_ref[h] * decay_col + inc
        state_ref[h] = s_new
        sbf_ref[h] = s_new.astype(BF16)

    def readout(step):
        c, h = divmod(step, GLA_HEADS)
        rows = slice(c * CHUNK, (c + 1) * CHUNK)
        q = proj_ref[rows, off_q + h * head_k:off_q + (h + 1) * head_k]
        o = jnp.dot(q, sbf_ref[h], preferred_element_type=F32) * (head_k ** -0.5)
        ms = jnp.mean(o * o, axis=-1, keepdims=True)
        o = o * lax.rsqrt(ms + EPS) * gng_ref[h:h + 1, :]
        r = proj_ref[rows, off_r + h * head_v:off_r + (h + 1) * head_v].astype(F32)
        yg_ref[rows, h * head_v:(h + 1) * head_v] = (o * _silu(r)).astype(yg_ref.dtype)

    def token_of(y):
        bits = pltpu.bitcast(y, jnp.uint32)
        token = bits[:, :LANES]
        for l0 in range(LANES, bits.shape[1], LANES):
            token = token | bits[:, l0:l0 + LANES]
        return token

    def norm_strip(src, dst, rows):
        x = src[rows, :]
        ms = jnp.mean(x * x, axis=-1, keepdims=True)
        h = (x * lax.rsqrt(ms + EPS) * ng_ref[...]).astype(BF16)
        dst[rows, :] = h
        return token_of(h)

    def pin(tokens):
        acc = tokens[0]
        for t in tokens[1:]:
            acc = acc | t
        zero = pltpu.bitcast((acc >> 16) >> 16, F32)
        zero = jnp.concatenate([zero] * (STRIP // zero.shape[0]), axis=0)
        tile = h_ref[0:STRIP, 0:LANES].astype(F32)
        h_ref[0:STRIP, 0:LANES] = (tile + zero).astype(BF16)

    def proj_piece(c0, width):
        def piece():
            if c0 < w_ref.shape[0] * W_ROWS:
                w = w_ref[c0 // W_ROWS, :, c0 % W_ROWS:c0 % W_ROWS + width]
            else:
                w = w_tail_ref[...]
            proj_ref[:, c0:c0 + width] = jnp.dot(h_ref[...], w,
                                                 preferred_element_type=F32).astype(BF16)
        return piece

    def pieces(off, width):
        return [proj_piece(c0, min(MXU_N, off + width - c0)) for c0 in range(off, off + width, MXU_N)]

    @pl.when(s == 0)
    def _():
        proj_ref[:, :4 * conv_width] = jnp.zeros((t_rows, 4 * conv_width), BF16)

        n_full = in_cols // W_ROWS

        def chunk_copy(i, rows=W_ROWS):
            slot = i % W_SLOTS
            return pltpu.make_async_copy(w_hbm.at[pl.ds(i * W_ROWS, rows), :],
                                         stage_ref.at[slot, pl.ds(0, rows), :], dma_sem.at[slot])

        def transpose_chunk(slot, rows, dst):
            for k0 in range(0, d_model, MXU_N):
                blk = stage_ref[slot, 0:rows, k0:k0 + MXU_N]
                dst[k0:k0 + MXU_N, :] = blk.T.astype(BF16)

        for i in range(W_SLOTS - 1):
            chunk_copy(i).start()

        def load_chunk(i, c):
            @pl.when(i + W_SLOTS - 1 < n_full)
            def _():
                chunk_copy(i + W_SLOTS - 1).start()
            chunk_copy(i).wait()
            transpose_chunk(i % W_SLOTS, W_ROWS, w_ref.at[i])
            return c

        lax.fori_loop(0, n_full, load_chunk, 0)

        tail_rows = in_cols - n_full * W_ROWS
        stage_ref[n_full % W_SLOTS, tail_rows:LANES, :] = jnp.zeros((LANES - tail_rows, d_model), F32)
        chunk_copy(n_full, tail_rows).start()
        chunk_copy(n_full, tail_rows).wait()
        transpose_chunk(n_full % W_SLOTS, LANES, w_tail_ref)

        x0_copy = pltpu.make_async_copy(x_hbm.at[pl.ds(0, t_rows), :],
                                        stage_ref.at[0, pl.ds(0, t_rows), :], dma_sem.at[0])
        x0_copy.start()
        x0_copy.wait()

        def norm_first(i, c):
            norm_strip(stage_ref.at[0], hn_ref, pl.ds(pl.multiple_of(i * STRIP, STRIP), STRIP))
            return c

        lax.fori_loop(0, n_strips, norm_first, 0, unroll=2)

    @pl.when((s == 0) | ((s + tiles_per_seq - 1) % tiles_per_seq == 0))
    def _():
        carry_ref[...] = jnp.zeros_like(carry_ref)

    @pl.when(s < n_tiles)
    def _():
        @pl.when(s % tiles_per_seq == 0)
        def _():
            state_ref[...] = jnp.zeros_like(state_ref)

        wo_bf_ref[...] = wo_ref[...].astype(BF16)

        h_ref[...] = hn_ref[...]

        norm_units = [functools.partial(norm_strip, x_ref, hn_ref, strip_rows(i))
                      for i in range(n_strips)]
        conv_units = [functools.partial(conv_strip, strip_rows(i)) for i in range(n_strips)]
        gate_units = [functools.partial(gate_strip, strip_rows(i)) for i in range(n_strips)]
        kdec_units = [functools.partial(kdec_strip, strip_rows(i)) for i in range(n_strips)]
        n_steps = n_chunks * GLA_HEADS

        mxu = (pieces(off_gd, LANES) + pieces(off_k, dk_total) + pieces(off_q, dk_total)
               + pieces(off_v, gla_width) + pieces(off_r, gla_width))
        n_k = dk_total // MXU_N
        vpu = [conv_units[:2]]
        vpu += _spread([gate_pre] + gate_units, n_k)
        vpu += [[later_sums, conv_units[2]]] + [kdec_units + [conv_units[3]]]
        vpu += [[] for _ in range(dk_total // MXU_N - 2)]
        rest = _spread(conv_units[4:], len(mxu) - len(vpu))
        rest[-1].append(functools.partial(state_update, 0))
        vpu += rest
        assert len(vpu) == len(mxu)

        mxu2 = pieces(off_h, 4 * conv_width)
        halves = []
        for step in range(n_steps):
            if step + 1 < n_steps:
                halves.append(functools.partial(state_update, step + 1))
            halves.append(functools.partial(readout, step))
        busy = len(mxu2) - IDLE_TAIL_PIECES
        vpu2 = [a + b for a, b in zip(_spread(halves, busy), _spread(norm_units, busy))]
        vpu2 += [[] for _ in range(IDLE_TAIL_PIECES)]

        tokens = []
        for piece, units in zip(mxu + mxu2, vpu + vpu2):
            if len(tokens) >= PIN_LAG and tokens[-PIN_LAG]:
                pin(tokens[-PIN_LAG])
            piece()
            tokens.append([t for t in [unit() for unit in units] if t is not None])

    @pl.when(s == n_tiles)
    def _():
        def step(i, c):
            conv_strip(pl.ds(pl.multiple_of(i * STRIP, STRIP), STRIP))
            return c
        lax.fori_loop(0, n_strips, step, 0, unroll=2)


def _proj_mix(x2d, norm_g, w_in_t, conv_w, conv_b, w_up_pad, b_gate, gla_norm_g, w_out, *,
              batch, seq, conv_width, dk_total, gla_width, n_cols):
    t = MIX_T
    nt = seq // t
    n_tiles = batch * nt
    d_model = x2d.shape[1]
    wo_rows = w_out.shape[0] // n_tiles
    assert w_out.shape[0] % n_tiles == 0 and wo_rows % STRIP == 0 and t <= W_ROWS
    main_cols = n_cols - LANES
    assert main_cols % W_ROWS == 0 and W_ROWS % MXU_N == 0 and 0 < w_in_t.shape[0] - main_cols <= LANES
    head_k = dk_total // GLA_HEADS
    head_v = gla_width // GLA_HEADS

    def whole(shape, **kw):
        return pl.BlockSpec(shape, lambda s: (0,) * len(shape), **kw)

    kern = functools.partial(_proj_mix_kernel, n_tiles=n_tiles, tiles_per_seq=nt,
                             conv_width=conv_width, dk_total=dk_total, gla_width=gla_width)
    return pl.pallas_call(
        kern,
        grid=(n_tiles + 1,),
        in_specs=[
            pl.BlockSpec((t, d_model), lambda s: (jnp.minimum(s + 1, n_tiles - 1), 0)),
            pl.BlockSpec(memory_space=pl.ANY),
            whole(norm_g.shape),
            pl.BlockSpec(memory_space=pl.ANY),
            whole(conv_w.shape), whole(conv_b.shape), whole(w_up_pad.shape), whole(b_gate.shape),
            whole(gla_norm_g.shape),
            pl.BlockSpec((wo_rows, w_out.shape[1]), lambda s: (jnp.minimum(s, n_tiles - 1), 0)),
        ],
        out_specs=[
            pl.BlockSpec((t, conv_width), lambda s: (jnp.maximum(s - 1, 0), 0)),
            pl.BlockSpec((t, gla_width), lambda s: (jnp.minimum(s, n_tiles - 1), 0)),
            pl.BlockSpec((wo_rows, w_out.shape[1]), lambda s: (jnp.minimum(s, n_tiles - 1), 0)),
        ],
        out_shape=[jax.ShapeDtypeStruct((batch * seq, conv_width), BF16),
                   jax.ShapeDtypeStruct((batch * seq, gla_width), BF16),
                   jax.ShapeDtypeStruct(w_out.shape, BF16)],
        scratch_shapes=[pltpu.VMEM((main_cols // W_ROWS, d_model, W_ROWS), BF16),
                        pltpu.VMEM((d_model, LANES), BF16),
                        pltpu.VMEM((W_SLOTS, W_ROWS, d_model), F32),
                        pltpu.SemaphoreType.DMA((W_SLOTS,)),
                        pltpu.VMEM((t, d_model), BF16),
                        pltpu.VMEM((t, d_model), BF16),
                        pltpu.VMEM((t, n_cols), BF16),
                        pltpu.VMEM((8, conv_width), F32),
                        pltpu.VMEM((GLA_HEADS, head_k, head_v), F32),
                        pltpu.VMEM((GLA_HEADS, head_k, head_v), BF16),
                        pltpu.VMEM((t, dk_total), F32),
                        pltpu.VMEM((t, 2 * dk_total), BF16),
                        pltpu.VMEM((t + 8, dk_total), F32),
                        pltpu.VMEM((t, dk_total), BF16)],
        compiler_params=pltpu.CompilerParams(
            dimension_semantics=("arbitrary",),
            vmem_limit_bytes=VMEM_LIMIT),
        name="proj_mix",
    )(x2d, x2d, norm_g, w_in_t, conv_w, conv_b, w_up_pad, b_gate, gla_norm_g, w_out)


def _out_proj_kernel(yc_ref, yg_ref, w_ref, x_ref, g_ref, o_ref, *, final_norm):
    kc = yc_ref.shape[1]
    z = (x_ref[...] + jnp.dot(yc_ref[...], w_ref[:kc, :], preferred_element_type=F32)
         + jnp.dot(yg_ref[...], w_ref[kc:, :], preferred_element_type=F32))
    if final_norm:
        ms = jnp.mean(z * z, axis=-1, keepdims=True)
        z = z * lax.rsqrt(ms + EPS) * g_ref[...]
    o_ref[...] = z


def _out_proj(y_conv, y_gla, w_out_bf16, x2d, final_g, *, final_norm):
    m, d = x2d.shape
    return pl.pallas_call(
        functools.partial(_out_proj_kernel, final_norm=final_norm),
        grid=(m // OUT_TM,),
        in_specs=[
            pl.BlockSpec((OUT_TM, y_conv.shape[1]), lambda i: (i, 0)),
            pl.BlockSpec((OUT_TM, y_gla.shape[1]), lambda i: (i, 0)),
            pl.BlockSpec(w_out_bf16.shape, lambda i: (0, 0)),
            pl.BlockSpec((OUT_TM, d), lambda i: (i, 0)),
            pl.BlockSpec((1, d), lambda i: (0, 0)),
        ],
        out_specs=pl.BlockSpec((OUT_TM, d), lambda i: (i, 0)),
        out_shape=jax.ShapeDtypeStruct((m, d), F32),
        compiler_params=pltpu.CompilerParams(
            dimension_semantics=("arbitrary",),
            vmem_limit_bytes=VMEM_LIMIT),
        name="out_proj",
    )(y_conv, y_gla, w_out_bf16, x2d, final_g)


def kernel(x, norm_g, w_in, conv_w, conv_b, gla_w_up, gla_b_gate, gla_norm_g, w_out, final_g):
    batch, seq, d_model = x.shape
    depth = norm_g.shape[0]
    conv_width = conv_w.shape[2]
    rank, dk_total = gla_w_up.shape[1], gla_w_up.shape[2]
    gla_width = gla_norm_g.shape[1] * gla_norm_g.shape[2]
    in_cols = w_in.shape[2]
    main_cols = in_cols - rank
    assert main_cols == 4 * conv_width + 2 * dk_total + 2 * gla_width
    assert main_cols % MXU_N == 0 and dk_total % MXU_N == 0 and rank <= LANES
    assert seq % MIX_T == 0 and MIX_T % CHUNK == 0 and (batch * seq) % OUT_TM == 0
    n_pad = main_cols + LANES

    x2d = x.reshape(batch * seq, d_model)
    for l in range(depth):
        w_up_p = jnp.pad(gla_w_up[l], ((0, LANES - rank), (0, 0))).astype(BF16)
        y_conv, y_gla, w_out_bf16 = _proj_mix(
            x2d, norm_g[l][None, :], w_in[l].T, conv_w[l], conv_b[l][None, :], w_up_p,
            gla_b_gate[l][None, :], gla_norm_g[l], w_out[l], batch=batch, seq=seq,
            conv_width=conv_width, dk_total=dk_total, gla_width=gla_width, n_cols=n_pad)
        x2d = _out_proj(y_conv, y_gla, w_out_bf16, x2d, final_g[None, :],
                        final_norm=(l == depth - 1))
    return x2d.reshape(batch, seq, d_model)
```

```python
import functools

import jax
import jax.numpy as jnp
from jax import lax
from jax.experimental import pallas as pl
from jax.experimental.pallas import tpu as pltpu

F32 = jnp.float32
BF16 = jnp.bfloat16

LANES = 128
SUBLANES = 8
MXU_N = 256
EPS = 1e-6
CHUNK = 64
GLA_HEADS = 4
GLA_TAU = 16.0

MIX_T = 256
STRIP = 16
COL_CHUNK = 256
W_ROWS = 256
W_SLOTS = 4
IDLE_TAIL_PIECES = 2
PIN_LAG = 2
OUT_TM = 512
VMEM_LIMIT = 56 * 1024 * 1024


def _silu(v):
    return v * jax.nn.sigmoid(v)


def _log_sigmoid(v):
    return -(jnp.maximum(-v, 0.0) + jnp.log1p(jnp.exp(-jnp.abs(v))))


def _spread(units, n_slots):
    out = [[] for _ in range(n_slots)]
    for j, u in enumerate(units):
        out[(j * n_slots) // len(units)].append(u)
    return out


def _proj_mix_kernel(x_ref, x_hbm, ng_ref, w_hbm, convw_ref, convb_ref, wup_ref, bgate_ref, gng_ref, wo_ref,
                     yc_ref, yg_ref, wo_bf_ref,
                     w_ref, w_tail_ref, stage_ref, dma_sem, h_ref, hn_ref, pc_ref, pg_ref, carry_ref, state_ref, sbf_ref,
                     glog_ref, hilo_ref, rev_ref, kdec_ref,
                     *, n_tiles, tiles_per_seq, conv_width, dk_total, gla_width):
    t_rows = x_ref.shape[0]
    in_cols, d_model = w_hbm.shape
    n_strips = t_rows // STRIP
    n_chunks = t_rows // CHUNK
    head_k = dk_total // GLA_HEADS
    head_v = gla_width // GLA_HEADS
    off_h, off_b, off_c, off_z = (i * conv_width for i in range(4))
    gla_col0 = 4 * conv_width
    off_q = 0
    off_k = off_q + dk_total
    off_v = off_k + dk_total
    off_r = off_v + gla_width
    off_gd = off_r + gla_width
    s = pl.program_id(0)

    def strip_rows(i):
        return slice(i * STRIP, (i + 1) * STRIP)

    def conv_strip(rows):
        token = None
        for c0 in range(0, conv_width, COL_CHUNK):
            cols = slice(c0, c0 + COL_CHUNK)

            def col(off):
                return pc_ref[rows, off + c0:off + c0 + COL_CHUNK].astype(F32)

            u = col(off_c) * col(off_h)
            ext = jnp.concatenate([carry_ref[:, cols], u], axis=0)
            u1 = pltpu.roll(ext, 1, 0)[8:, :]
            u2 = pltpu.roll(ext, 2, 0)[8:, :]
            carry_ref[:, cols] = u[STRIP - 8:, :]
            conv = (convb_ref[:, cols] + convw_ref[0:1, cols] * u2 + convw_ref[1:2, cols] * u1
                    + convw_ref[2:3, cols] * u)
            y = (col(off_b) * conv * _silu(col(off_z))).astype(yc_ref.dtype)
            yc_ref[rows, cols] = y
            token = token_of(y) if token is None else token | token_of(y)
        return token

    def gate_pre():
        glog_ref[...] = jnp.dot(pg_ref[:, off_gd:off_gd + LANES], wup_ref[...],
                                preferred_element_type=F32)

    def gate_strip(rows):
        glog = _log_sigmoid(glog_ref[rows, :] + bgate_ref[...]) / GLA_TAU
        g_hi = glog.astype(BF16)
        hilo_ref[rows, :dk_total] = g_hi
        hilo_ref[rows, dk_total:] = (glog - g_hi.astype(F32)).astype(BF16)

    def later_sums():
        r_i = lax.broadcasted_iota(jnp.int32, (t_rows + 8, t_rows), 0)
        c_i = lax.broadcasted_iota(jnp.int32, (t_rows + 8, t_rows), 1)
        c_chunk = c_i // CHUNK
        ones = ((c_i > r_i) & (c_chunk == r_i // CHUNK)) | (c_chunk == r_i - t_rows)
        sums = jnp.dot(jnp.where(ones, 1.0, 0.0).astype(BF16), hilo_ref[...],
                       preferred_element_type=F32)
        rev_ref[...] = sums[:, :dk_total] + sums[:, dk_total:]

    def kdec_strip(rows):
        k = pg_ref[rows, off_k:off_k + dk_total].astype(F32)
        kdec_ref[rows, :] = (k * jnp.exp(rev_ref[rows, :])).astype(BF16)

    def state_update(step):
        c, h = divmod(step, GLA_HEADS)
        rows = slice(c * CHUNK, (c + 1) * CHUNK)
        ks = slice(h * head_k, (h + 1) * head_k)
        decay = jnp.exp(rev_ref[t_rows + c:t_rows + c + 1, ks])
        v = pg_ref[rows, off_v + h * head_v:off_v + (h + 1) * head_v]
        inc = lax.dot_general(v, kdec_ref[rows, ks], (((0,), (0,)), ((), ())),
                              preferred_element_type=F32)
        s_new = state_ref[h] * decay + inc
        state_ref[h] = s_new
        sbf_ref[h] = s_new.astype(BF16)

    def readout(step):
        c, h = divmod(step, GLA_HEADS)
        rows = slice(c * CHUNK, (c + 1) * CHUNK)
        q = pg_ref[rows, off_q + h * head_k:off_q + (h + 1) * head_k]
        o = lax.dot_general(q, sbf_ref[h], (((1,), (1,)), ((), ())),
                            preferred_element_type=F32) * (head_k ** -0.5)
        ms = jnp.mean(o * o, axis=-1, keepdims=True)
        o = o * lax.rsqrt(ms + EPS) * gng_ref[h:h + 1, :]
        r = pg_ref[rows, off_r + h * head_v:off_r + (h + 1) * head_v].astype(F32)
        yg_ref[rows, h * head_v:(h + 1) * head_v] = (o * _silu(r)).astype(yg_ref.dtype)

    def token_of(y):
        bits = pltpu.bitcast(y, jnp.uint32)
        token = bits[:, :LANES]
        for l0 in range(LANES, bits.shape[1], LANES):
            token = token | bits[:, l0:l0 + LANES]
        folded = token[:SUBLANES, :]
        for r0 in range(SUBLANES, token.shape[0], SUBLANES):
            folded = folded | token[r0:r0 + SUBLANES, :]
        return folded

    def norm_strip(src, dst, rows):
        x = src[rows, :]
        ms = jnp.mean(x * x, axis=-1, keepdims=True)
        h = (x * lax.rsqrt(ms + EPS) * ng_ref[...]).astype(BF16)
        dst[rows, :] = h
        return token_of(h)

    def pin(tokens):
        acc = tokens[0]
        for t in tokens[1:]:
            acc = acc | t
        zero = pltpu.bitcast((acc >> 16) >> 16, F32)
        zero = jnp.concatenate([zero] * (STRIP // zero.shape[0]), axis=0)
        tile = h_ref[t_rows - STRIP:, d_model - LANES:].astype(F32)
        h_ref[t_rows - STRIP:, d_model - LANES:] = (tile + zero).astype(BF16)

    def proj_piece(dst, dst_col0, c0, width):
        def piece():
            g0 = dst_col0 + c0
            if g0 < w_ref.shape[0] * W_ROWS:
                w = w_ref[g0 // W_ROWS, :, g0 % W_ROWS:g0 % W_ROWS + width]
            else:
                w = w_tail_ref[...]
            y = jnp.dot(h_ref[...], w, preferred_element_type=F32).astype(BF16)
            dst[:, c0:c0 + width] = y
            return token_of(y)
        return piece

    def pieces(dst, dst_col0, off, width):
        return [proj_piece(dst, dst_col0, c0, min(MXU_N, off + width - c0))
                for c0 in range(off, off + width, MXU_N)]

    @pl.when(s == 0)
    def _():
        pc_ref[...] = jnp.zeros_like(pc_ref)

        n_full = in_cols // W_ROWS

        def chunk_copy(i, rows=W_ROWS):
            slot = i % W_SLOTS
            return pltpu.make_async_copy(w_hbm.at[pl.ds(i * W_ROWS, rows), :],
                                         stage_ref.at[slot, pl.ds(0, rows), :], dma_sem.at[slot])

        def transpose_chunk(slot, rows, dst):
            for k0 in range(0, d_model, MXU_N):
                blk = stage_ref[slot, 0:rows, k0:k0 + MXU_N]
                dst[k0:k0 + MXU_N, :] = blk.T.astype(BF16)

        for i in range(W_SLOTS - 1):
            chunk_copy(i).start()

        def load_chunk(i, c):
            @pl.when(i + W_SLOTS - 1 < n_full)
            def _():
                chunk_copy(i + W_SLOTS - 1).start()
            chunk_copy(i).wait()
            transpose_chunk(i % W_SLOTS, W_ROWS, w_ref.at[i])
            return c

        lax.fori_loop(0, n_full, load_chunk, 0)

        tail_rows = in_cols - n_full * W_ROWS
        stage_ref[n_full % W_SLOTS, tail_rows:LANES, :] = jnp.zeros((LANES - tail_rows, d_model), F32)
        chunk_copy(n_full, tail_rows).start()
        chunk_copy(n_full, tail_rows).wait()
        transpose_chunk(n_full % W_SLOTS, LANES, w_tail_ref)

        x0_copy = pltpu.make_async_copy(x_hbm.at[pl.ds(0, t_rows), :],
                                        stage_ref.at[0, pl.ds(0, t_rows), :], dma_sem.at[0])
        x0_copy.start()
        x0_copy.wait()

        def norm_first(i, c):
            norm_strip(stage_ref.at[0], hn_ref, pl.ds(pl.multiple_of(i * STRIP, STRIP), STRIP))
            return c

        lax.fori_loop(0, n_strips, norm_first, 0, unroll=2)

    @pl.when((s == 0) | ((s + tiles_per_seq - 1) % tiles_per_seq == 0))
    def _():
        carry_ref[...] = jnp.zeros_like(carry_ref)

    @pl.when(s < n_tiles)
    def _():
        @pl.when(s % tiles_per_seq == 0)
        def _():
            state_ref[...] = jnp.zeros_like(state_ref)

        wo_bf_ref[...] = wo_ref[...].astype(BF16)

        h_ref[...] = hn_ref[...]

        norm_units = [functools.partial(norm_strip, x_ref, hn_ref, strip_rows(i))
                      for i in range(n_strips)]
        conv_units = [functools.partial(conv_strip, strip_rows(i)) for i in range(n_strips)]
        gate_units = [functools.partial(gate_strip, strip_rows(i)) for i in range(n_strips)]
        kdec_units = [functools.partial(kdec_strip, strip_rows(i)) for i in range(n_strips)]
        n_steps = n_chunks * GLA_HEADS

        mxu = [p for off, width in ((off_gd, LANES), (off_k, dk_total), (off_q, dk_total),
                                    (off_v, gla_width), (off_r, gla_width))
               for p in pieces(pg_ref, gla_col0, off, width)]
        third = (n_strips + 2) // 3
        vpu = [conv_units[0:2],
               conv_units[2:4],
               [gate_pre] + gate_units[:third],
               gate_units[third:2 * third],
               gate_units[2 * third:],
               conv_units[4:5] + [later_sums],
               kdec_units]
        rest = _spread(conv_units[5:], len(mxu) - len(vpu))
        rest[-1].append(functools.partial(state_update, 0))
        vpu += rest
        assert len(vpu) == len(mxu)

        mxu2 = pieces(pc_ref, 0, off_h, 4 * conv_width)
        halves = []
        for step in range(n_steps):
            if step + 1 < n_steps:
                halves.append(functools.partial(state_update, step + 1))
            halves.append(functools.partial(readout, step))
        busy = len(mxu2) - IDLE_TAIL_PIECES
        vpu2 = [a + b for a, b in zip(_spread(halves, busy), _spread(norm_units, busy))]
        vpu2 += [[] for _ in range(IDLE_TAIL_PIECES)]

        tokens = []
        for piece, units in zip(mxu + mxu2, vpu + vpu2):
            if len(tokens) >= PIN_LAG and tokens[-PIN_LAG]:
                pin(tokens[-PIN_LAG])
            results = [piece()] + [unit() for unit in units]
            tokens.append([t for t in results if t is not None])

    @pl.when(s == n_tiles)
    def _():
        def step(i, c):
            conv_strip(pl.ds(pl.multiple_of(i * STRIP, STRIP), STRIP))
            return c
        lax.fori_loop(0, n_strips, step, 0, unroll=2)


def _proj_mix(x2d, norm_g, w_in_t, conv_w, conv_b, w_up_pad, b_gate, gla_norm_g, w_out, *,
              batch, seq, conv_width, dk_total, gla_width, n_cols):
    t = MIX_T
    nt = seq // t
    n_tiles = batch * nt
    d_model = x2d.shape[1]
    wo_rows = w_out.shape[0] // n_tiles
    assert w_out.shape[0] % n_tiles == 0 and wo_rows % STRIP == 0 and t <= W_ROWS
    main_cols = n_cols - LANES
    assert main_cols % W_ROWS == 0 and W_ROWS % MXU_N == 0 and 0 < w_in_t.shape[0] - main_cols <= LANES
    head_k = dk_total // GLA_HEADS
    head_v = gla_width // GLA_HEADS

    def whole(shape, **kw):
        return pl.BlockSpec(shape, lambda s: (0,) * len(shape), **kw)

    kern = functools.partial(_proj_mix_kernel, n_tiles=n_tiles, tiles_per_seq=nt,
                             conv_width=conv_width, dk_total=dk_total, gla_width=gla_width)
    return pl.pallas_call(
        kern,
        grid=(n_tiles + 1,),
        in_specs=[
            pl.BlockSpec((t, d_model), lambda s: (jnp.minimum(s + 1, n_tiles - 1), 0)),
            pl.BlockSpec(memory_space=pl.ANY),
            whole(norm_g.shape),
            pl.BlockSpec(memory_space=pl.ANY),
            whole(conv_w.shape), whole(conv_b.shape), whole(w_up_pad.shape), whole(b_gate.shape),
            whole(gla_norm_g.shape),
            pl.BlockSpec((wo_rows, w_out.shape[1]), lambda s: (jnp.minimum(s, n_tiles - 1), 0)),
        ],
        out_specs=[
            pl.BlockSpec((t, conv_width), lambda s: (jnp.maximum(s - 1, 0), 0)),
            pl.BlockSpec((t, gla_width), lambda s: (jnp.minimum(s, n_tiles - 1), 0)),
            pl.BlockSpec((wo_rows, w_out.shape[1]), lambda s: (jnp.minimum(s, n_tiles - 1), 0)),
        ],
        out_shape=[jax.ShapeDtypeStruct((batch * seq, conv_width), BF16),
                   jax.ShapeDtypeStruct((batch * seq, gla_width), BF16),
                   jax.ShapeDtypeStruct(w_out.shape, BF16)],
        scratch_shapes=[pltpu.VMEM((main_cols // W_ROWS, d_model, W_ROWS), BF16),
                        pltpu.VMEM((d_model, LANES), BF16),
                        pltpu.VMEM((W_SLOTS, W_ROWS, d_model), F32),
                        pltpu.SemaphoreType.DMA((W_SLOTS,)),
                        pltpu.VMEM((t, d_model), BF16),
                        pltpu.VMEM((t, d_model), BF16),
                        pltpu.VMEM((t, 4 * conv_width), BF16),
                        pltpu.VMEM((t, n_cols - 4 * conv_width), BF16),
                        pltpu.VMEM((8, conv_width), F32),
                        pltpu.VMEM((GLA_HEADS, head_v, head_k), F32),
                        pltpu.VMEM((GLA_HEADS, head_v, head_k), BF16),
                        pltpu.VMEM((t, dk_total), F32),
                        pltpu.VMEM((t, 2 * dk_total), BF16),
                        pltpu.VMEM((t + 8, dk_total), F32),
                        pltpu.VMEM((t, dk_total), BF16)],
        compiler_params=pltpu.CompilerParams(
            dimension_semantics=("arbitrary",),
            vmem_limit_bytes=VMEM_LIMIT),
        name="proj_mix",
    )(x2d, x2d, norm_g, w_in_t, conv_w, conv_b, w_up_pad, b_gate, gla_norm_g, w_out)


def _out_proj_kernel(yc_ref, yg_ref, w_ref, x_ref, g_ref, o_ref, *, final_norm):
    kc = yc_ref.shape[1]
    z = (x_ref[...] + jnp.dot(yc_ref[...], w_ref[:kc, :], preferred_element_type=F32)
         + jnp.dot(yg_ref[...], w_ref[kc:, :], preferred_element_type=F32))
    if final_norm:
        ms = jnp.mean(z * z, axis=-1, keepdims=True)
        z = z * lax.rsqrt(ms + EPS) * g_ref[...]
    o_ref[...] = z


def _out_proj(y_conv, y_gla, w_out_bf16, x2d, final_g, *, final_norm):
    m, d = x2d.shape
    return pl.pallas_call(
        functools.partial(_out_proj_kernel, final_norm=final_norm),
        grid=(m // OUT_TM,),
        in_specs=[
            pl.BlockSpec((OUT_TM, y_conv.shape[1]), lambda i: (i, 0)),
            pl.BlockSpec((OUT_TM, y_gla.shape[1]), lambda i: (i, 0)),
            pl.BlockSpec(w_out_bf16.shape, lambda i: (0, 0)),
            pl.BlockSpec((OUT_TM, d), lambda i: (i, 0)),
            pl.BlockSpec((1, d), lambda i: (0, 0)),
        ],
        out_specs=pl.BlockSpec((OUT_TM, d), lambda i: (i, 0)),
        out_shape=jax.ShapeDtypeStruct((m, d), F32),
        compiler_params=pltpu.CompilerParams(
            dimension_semantics=("arbitrary",),
            vmem_limit_bytes=VMEM_LIMIT),
        name="out_proj",
    )(y_conv, y_gla, w_out_bf16, x2d, final_g)


def kernel(x, norm_g, w_in, conv_w, conv_b, gla_w_up, gla_b_gate, gla_norm_g, w_out, final_g):
    batch, seq, d_model = x.shape
    depth = norm_g.shape[0]
    conv_width = conv_w.shape[2]
    rank, dk_total = gla_w_up.shape[1], gla_w_up.shape[2]
    gla_width = gla_norm_g.shape[1] * gla_norm_g.shape[2]
    in_cols = w_in.shape[2]
    main_cols = in_cols - rank
    assert main_cols == 4 * conv_width + 2 * dk_total + 2 * gla_width
    assert main_cols % MXU_N == 0 and dk_total % MXU_N == 0 and rank <= LANES
    assert seq % MIX_T == 0 and MIX_T % CHUNK == 0 and (batch * seq) % OUT_TM == 0
    n_pad = main_cols + LANES

    x2d = x.reshape(batch * seq, d_model)
    for l in range(depth):
        w_up_p = jnp.pad(gla_w_up[l], ((0, LANES - rank), (0, 0))).astype(BF16)
        y_conv, y_gla, w_out_bf16 = _proj_mix(
            x2d, norm_g[l][None, :], w_in[l].T, conv_w[l], conv_b[l][None, :], w_up_p,
            gla_b_gate[l][None, :], gla_norm_g[l], w_out[l], batch=batch, seq=seq,
            conv_width=conv_width, dk_total=dk_total, gla_width=gla_width, n_cols=n_pad)
        x2d = _out_proj(y_conv, y_gla, w_out_bf16, x2d, final_g[None, :],
                        final_norm=(l == depth - 1))
    return x2d.reshape(batch, seq, d_model)
```

```python
import functools

import jax
import jax.numpy as jnp
from jax import lax
from jax.experimental import pallas as pl
from jax.experimental.pallas import tpu as pltpu

F32 = jnp.float32
BF16 = jnp.bfloat16

LANES = 128
MXU_N = 256
EPS = 1e-6
CHUNK = 64
GLA_HEADS = 4
GLA_TAU = 16.0

MIX_T = 256
STRIP = 16
COL_CHUNK = 512
W_ROWS = 512
W_SLOTS = 2
IDLE_TAIL_PIECES = 1
PIECE_N = 512
PIN_LAG = 1
OUT_TM = 512
VMEM_LIMIT = 56 * 1024 * 1024


def _silu(v):
    return v * jax.nn.sigmoid(v)


def _log_sigmoid(v):
    return -(jnp.maximum(-v, 0.0) + jnp.log1p(jnp.exp(-jnp.abs(v))))


def _spread(units, n_slots):
    out = [[] for _ in range(n_slots)]
    for j, u in enumerate(units):
        out[(j * n_slots) // len(units)].append(u)
    return out


def _proj_mix_kernel(x_ref, x_hbm, ng_ref, w_hbm, convw_ref, convb_ref, wup_ref, bgate_ref, gng_ref, wo_ref,
                     yc_ref, yg_ref, wo_bf_ref,
                     w_ref, w_tail_ref, stage_ref, dma_sem, h_ref, hn_ref, proj_ref, carry_ref, state_ref, sbf_ref,
                     glog_ref, hilo_ref, rev_ref, kdec_ref,
                     *, n_tiles, tiles_per_seq, conv_width, dk_total, gla_width):
    t_rows = x_ref.shape[0]
    in_cols, d_model = w_hbm.shape
    n_strips = t_rows // STRIP
    n_chunks = t_rows // CHUNK
    head_k = dk_total // GLA_HEADS
    head_v = gla_width // GLA_HEADS
    off_h, off_b, off_c, off_z = (i * conv_width for i in range(4))
    off_q = 4 * conv_width
    off_k = off_q + dk_total
    off_v = off_k + dk_total
    off_r = off_v + gla_width
    off_gd = off_r + gla_width
    s = pl.program_id(0)

    def strip_rows(i):
        return slice(i * STRIP, (i + 1) * STRIP)

    def conv_strip(rows):
        token = None
        for c0 in range(0, conv_width, COL_CHUNK):
            cols = slice(c0, c0 + COL_CHUNK)

            def col(off):
                return proj_ref[rows, off + c0:off + c0 + COL_CHUNK].astype(F32)

            u = col(off_c) * col(off_h)
            ext = jnp.concatenate([carry_ref[:, cols], u], axis=0)
            u1 = pltpu.roll(ext, 1, 0)[8:, :]
            u2 = pltpu.roll(ext, 2, 0)[8:, :]
            carry_ref[:, cols] = u[STRIP - 8:, :]
            conv = (convb_ref[:, cols] + convw_ref[0:1, cols] * u2 + convw_ref[1:2, cols] * u1
                    + convw_ref[2:3, cols] * u)
            y = (col(off_b) * conv * _silu(col(off_z))).astype(yc_ref.dtype)
            yc_ref[rows, cols] = y
            token = token_of(y) if token is None else token | token_of(y)
        return token

    def gate_pre():
        glog_ref[...] = jnp.dot(proj_ref[:, off_gd:off_gd + LANES], wup_ref[...],
                                preferred_element_type=F32)

    def gate_strip(rows):
        glog = _log_sigmoid(glog_ref[rows, :] + bgate_ref[...]) / GLA_TAU
        g_hi = glog.astype(BF16)
        hilo_ref[rows, :dk_total] = g_hi
        hilo_ref[rows, dk_total:] = (glog - g_hi.astype(F32)).astype(BF16)

    def later_sums():
        r_i = lax.broadcasted_iota(jnp.int32, (t_rows + 8, t_rows), 0)
        c_i = lax.broadcasted_iota(jnp.int32, (t_rows + 8, t_rows), 1)
        c_chunk = c_i // CHUNK
        ones = ((c_i > r_i) & (c_chunk == r_i // CHUNK)) | (c_chunk == r_i - t_rows)
        sums = jnp.dot(jnp.where(ones, 1.0, 0.0).astype(BF16), hilo_ref[...],
                       preferred_element_type=F32)
        rev_ref[...] = sums[:, :dk_total] + sums[:, dk_total:]

    def kdec_strip(rows):
        k = proj_ref[rows, off_k:off_k + dk_total].astype(F32)
        kdec_ref[rows, :] = (k * jnp.exp(rev_ref[rows, :])).astype(BF16)

    def state_update(step):
        c, h = divmod(step, GLA_HEADS)
        rows = slice(c * CHUNK, (c + 1) * CHUNK)
        ks = slice(h * head_k, (h + 1) * head_k)
        decay = jnp.exp(rev_ref[t_rows + c:t_rows + c + 1, ks])
        v = proj_ref[rows, off_v + h * head_v:off_v + (h + 1) * head_v]
        inc = lax.dot_general(v, kdec_ref[rows, ks], (((0,), (0,)), ((), ())),
                              preferred_element_type=F32)
        s_new = state_ref[h] * decay + inc
        state_ref[h] = s_new
        sbf_ref[h] = s_new.astype(BF16)

    def readout(step):
        c, h = divmod(step, GLA_HEADS)
        rows = slice(c * CHUNK, (c + 1) * CHUNK)
        q = proj_ref[rows, off_q + h * head_k:off_q + (h + 1) * head_k]
        o = lax.dot_general(q, sbf_ref[h], (((1,), (1,)), ((), ())),
                            preferred_element_type=F32) * (head_k ** -0.5)
        ms = jnp.mean(o * o, axis=-1, keepdims=True)
        o = o * lax.rsqrt(ms + EPS) * gng_ref[h:h + 1, :]
        r = proj_ref[rows, off_r + h * head_v:off_r + (h + 1) * head_v].astype(F32)
        yg_ref[rows, h * head_v:(h + 1) * head_v] = (o * _silu(r)).astype(yg_ref.dtype)

    def token_of(y):
        bits = pltpu.bitcast(y, jnp.uint32)
        token = bits[:, :LANES]
        for l0 in range(LANES, bits.shape[1], LANES):
            token = token | bits[:, l0:l0 + LANES]
        return token

    def norm_strip(src, dst, rows):
        x = src[rows, :]
        ms = jnp.mean(x * x, axis=-1, keepdims=True)
        h = (x * lax.rsqrt(ms + EPS) * ng_ref[...]).astype(BF16)
        dst[rows, :] = h
        return token_of(h)

    def pin(tokens):
        acc = tokens[0]
        for t in tokens[1:]:
            acc = acc | t
        zero = pltpu.bitcast((acc >> 16) >> 16, F32)
        zero = jnp.concatenate([zero] * (STRIP // zero.shape[0]), axis=0)
        tile = h_ref[0:STRIP, 0:LANES].astype(F32)
        h_ref[0:STRIP, 0:LANES] = (tile + zero).astype(BF16)

    def proj_piece(c0, width):
        def piece():
            if c0 < w_ref.shape[0] * W_ROWS:
                w = w_ref[c0 // W_ROWS, :, c0 % W_ROWS:c0 % W_ROWS + width]
            else:
                w = w_tail_ref[...]
            proj_ref[:, c0:c0 + width] = jnp.dot(h_ref[...], w,
                                                 preferred_element_type=F32).astype(BF16)
        return piece

    def pieces(off, width):
        return [proj_piece(c0, min(PIECE_N, off + width - c0)) for c0 in range(off, off + width, PIECE_N)]

    @pl.when(s == 0)
    def _():
        proj_ref[:, :4 * conv_width] = jnp.zeros((t_rows, 4 * conv_width), BF16)

        n_full = in_cols // W_ROWS

        def chunk_copy(i, rows=W_ROWS):
            slot = i % W_SLOTS
            return pltpu.make_async_copy(w_hbm.at[pl.ds(i * W_ROWS, rows), :],
                                         stage_ref.at[slot, pl.ds(0, rows), :], dma_sem.at[slot])

        def transpose_chunk(slot, rows, dst):
            for k0 in range(0, d_model, MXU_N):
                blk = stage_ref[slot, 0:rows, k0:k0 + MXU_N]
                dst[k0:k0 + MXU_N, :] = blk.T.astype(BF16)

        for i in range(W_SLOTS - 1):
            chunk_copy(i).start()

        def load_chunk(i, c):
            @pl.when(i + W_SLOTS - 1 < n_full)
            def _():
                chunk_copy(i + W_SLOTS - 1).start()
            chunk_copy(i).wait()
            transpose_chunk(i % W_SLOTS, W_ROWS, w_ref.at[i])
            return c

        lax.fori_loop(0, n_full, load_chunk, 0)

        tail_rows = in_cols - n_full * W_ROWS
        stage_ref[n_full % W_SLOTS, tail_rows:LANES, :] = jnp.zeros((LANES - tail_rows, d_model), F32)
        chunk_copy(n_full, tail_rows).start()
        chunk_copy(n_full, tail_rows).wait()
        transpose_chunk(n_full % W_SLOTS, LANES, w_tail_ref)

        x0_copy = pltpu.make_async_copy(x_hbm.at[pl.ds(0, t_rows), :],
                                        stage_ref.at[0, pl.ds(0, t_rows), :], dma_sem.at[0])
        x0_copy.start()
        x0_copy.wait()

        def norm_first(i, c):
            norm_strip(stage_ref.at[0], hn_ref, pl.ds(pl.multiple_of(i * STRIP, STRIP), STRIP))
            return c

        lax.fori_loop(0, n_strips, norm_first, 0, unroll=2)

    @pl.when((s == 0) | ((s + tiles_per_seq - 1) % tiles_per_seq == 0))
    def _():
        carry_ref[...] = jnp.zeros_like(carry_ref)

    @pl.when(s < n_tiles)
    def _():
        @pl.when(s % tiles_per_seq == 0)
        def _():
            state_ref[...] = jnp.zeros_like(state_ref)

        wo_bf_ref[...] = wo_ref[...].astype(BF16)

        h_ref[...] = hn_ref[...]

        norm_units = [functools.partial(norm_strip, x_ref, hn_ref, strip_rows(i))
                      for i in range(n_strips)]
        conv_units = [functools.partial(conv_strip, strip_rows(i)) for i in range(n_strips)]
        gate_units = [functools.partial(gate_strip, strip_rows(i)) for i in range(n_strips)]
        kdec_units = [functools.partial(kdec_strip, strip_rows(i)) for i in range(n_strips)]
        n_steps = n_chunks * GLA_HEADS

        mxu = (pieces(off_gd, LANES) + pieces(off_k, dk_total) + pieces(off_q, dk_total)
               + pieces(off_v, gla_width) + pieces(off_r, gla_width))
        head = [conv_units[:2],
                [gate_pre] + gate_units,
                [later_sums, conv_units[2]] + kdec_units]
        rest = _spread(conv_units[3:], len(mxu) - len(head))
        rest[-1].append(functools.partial(state_update, 0))
        vpu = head + rest
        assert len(vpu) == len(mxu)

        mxu2 = pieces(off_h, 4 * conv_width)
        halves = []
        for step in range(n_steps):
            if step + 1 < n_steps:
                halves.append(functools.partial(state_update, step + 1))
            halves.append(functools.partial(readout, step))
        busy = len(mxu2) - IDLE_TAIL_PIECES
        vpu2 = [a + b for a, b in zip(_spread(halves, busy), _spread(norm_units, busy))]
        vpu2 += [[] for _ in range(IDLE_TAIL_PIECES)]

        tokens = []
        for piece, units in zip(mxu + mxu2, vpu + vpu2):
            if len(tokens) >= PIN_LAG and tokens[-PIN_LAG]:
                pin(tokens[-PIN_LAG])
            piece()
            tokens.append([t for t in [unit() for unit in units] if t is not None])

    @pl.when(s == n_tiles)
    def _():
        def step(i, c):
            conv_strip(pl.ds(pl.multiple_of(i * STRIP, STRIP), STRIP))
            return c
        lax.fori_loop(0, n_strips, step, 0, unroll=2)


def _proj_mix(x2d, norm_g, w_in_t, conv_w, conv_b, w_up_pad, b_gate, gla_norm_g, w_out, *,
              batch, seq, conv_width, dk_total, gla_width, n_cols):
    t = MIX_T
    nt = seq // t
    n_tiles = batch * nt
    d_model = x2d.shape[1]
    wo_rows = w_out.shape[0] // n_tiles
    assert w_out.shape[0] % n_tiles == 0 and wo_rows % STRIP == 0 and t <= W_ROWS
    main_cols = n_cols - LANES
    assert main_cols % W_ROWS == 0 and W_ROWS % PIECE_N == 0 and 0 < w_in_t.shape[0] - main_cols <= LANES
    head_k = dk_total // GLA_HEADS
    head_v = gla_width // GLA_HEADS

    def whole(shape, **kw):
        return pl.BlockSpec(shape, lambda s: (0,) * len(shape), **kw)

    kern = functools.partial(_proj_mix_kernel, n_tiles=n_tiles, tiles_per_seq=nt,
                             conv_width=conv_width, dk_total=dk_total, gla_width=gla_width)
    return pl.pallas_call(
        kern,
        grid=(n_tiles + 1,),
        in_specs=[
            pl.BlockSpec((t, d_model), lambda s: (jnp.minimum(s + 1, n_tiles - 1), 0)),
            pl.BlockSpec(memory_space=pl.ANY),
            whole(norm_g.shape),
            pl.BlockSpec(memory_space=pl.ANY),
            whole(conv_w.shape), whole(conv_b.shape), whole(w_up_pad.shape), whole(b_gate.shape),
            whole(gla_norm_g.shape),
            pl.BlockSpec((wo_rows, w_out.shape[1]), lambda s: (jnp.minimum(s, n_tiles - 1), 0)),
        ],
        out_specs=[
            pl.BlockSpec((t, conv_width), lambda s: (jnp.maximum(s - 1, 0), 0)),
            pl.BlockSpec((t, gla_width), lambda s: (jnp.minimum(s, n_tiles - 1), 0)),
            pl.BlockSpec((wo_rows, w_out.shape[1]), lambda s: (jnp.minimum(s, n_tiles - 1), 0)),
        ],
        out_shape=[jax.ShapeDtypeStruct((batch * seq, conv_width), BF16),
                   jax.ShapeDtypeStruct((batch * seq, gla_width), BF16),
                   jax.ShapeDtypeStruct(w_out.shape, BF16)],
        scratch_shapes=[pltpu.VMEM((main_cols // W_ROWS, d_model, W_ROWS), BF16),
                        pltpu.VMEM((d_model, LANES), BF16),
                        pltpu.VMEM((W_SLOTS, W_ROWS, d_model), F32),
                        pltpu.SemaphoreType.DMA((W_SLOTS,)),
                        pltpu.VMEM((t, d_model), BF16),
                        pltpu.VMEM((t, d_model), BF16),
                        pltpu.VMEM((t, n_cols), BF16),
                        pltpu.VMEM((8, conv_width), F32),
                        pltpu.VMEM((GLA_HEADS, head_v, head_k), F32),
                        pltpu.VMEM((GLA_HEADS, head_v, head_k), BF16),
                        pltpu.VMEM((t, dk_total), F32),
                        pltpu.VMEM((t, 2 * dk_total), BF16),
                        pltpu.VMEM((t + 8, dk_total), F32),
                        pltpu.VMEM((t, dk_total), BF16)],
        compiler_params=pltpu.CompilerParams(
            dimension_semantics=("arbitrary",),
            vmem_limit_bytes=VMEM_LIMIT),
        name="proj_mix",
    )(x2d, x2d, norm_g, w_in_t, conv_w, conv_b, w_up_pad, b_gate, gla_norm_g, w_out)


def _out_proj_kernel(yc_ref, yg_ref, w_ref, x_ref, g_ref, o_ref, *, final_norm):
    kc = yc_ref.shape[1]
    z = (x_ref[...] + jnp.dot(yc_ref[...], w_ref[:kc, :], preferred_element_type=F32)
         + jnp.dot(yg_ref[...], w_ref[kc:, :], preferred_element_type=F32))
    if final_norm:
        ms = jnp.mean(z * z, axis=-1, keepdims=True)
        z = z * lax.rsqrt(ms + EPS) * g_ref[...]
    o_ref[...] = z


def _out_proj(y_conv, y_gla, w_out_bf16, x2d, final_g, *, final_norm):
    m, d = x2d.shape
    return pl.pallas_call(
        functools.partial(_out_proj_kernel, final_norm=final_norm),
        grid=(m // OUT_TM,),
        in_specs=[
            pl.BlockSpec((OUT_TM, y_conv.shape[1]), lambda i: (i, 0)),
            pl.BlockSpec((OUT_TM, y_gla.shape[1]), lambda i: (i, 0)),
            pl.BlockSpec(w_out_bf16.shape, lambda i: (0, 0)),
            pl.BlockSpec((OUT_TM, d), lambda i: (i, 0)),
            pl.BlockSpec((1, d), lambda i: (0, 0)),
        ],
        out_specs=pl.BlockSpec((OUT_TM, d), lambda i: (i, 0)),
        out_shape=jax.ShapeDtypeStruct((m, d), F32),
        compiler_params=pltpu.CompilerParams(
            dimension_semantics=("arbitrary",),
            vmem_limit_bytes=VMEM_LIMIT),
        name="out_proj",
    )(y_conv, y_gla, w_out_bf16, x2d, final_g)


def kernel(x, norm_g, w_in, conv_w, conv_b, gla_w_up, gla_b_gate, gla_norm_g, w_out, final_g):
    batch, seq, d_model = x.shape
    depth = norm_g.shape[0]
    conv_width = conv_w.shape[2]
    rank, dk_total = gla_w_up.shape[1], gla_w_up.shape[2]
    gla_width = gla_norm_g.shape[1] * gla_norm_g.shape[2]
    in_cols = w_in.shape[2]
    main_cols = in_cols - rank
    assert main_cols == 4 * conv_width + 2 * dk_total + 2 * gla_width
    assert main_cols % PIECE_N == 0 and dk_total % PIECE_N == 0 and rank <= LANES
    assert seq % MIX_T == 0 and MIX_T % CHUNK == 0 and (batch * seq) % OUT_TM == 0
    n_pad = main_cols + LANES

    x2d = x.reshape(batch * seq, d_model)
    for l in range(depth):
        w_up_p = jnp.pad(gla_w_up[l], ((0, LANES - rank), (0, 0))).astype(BF16)
        y_conv, y_gla, w_out_bf16 = _proj_mix(
            x2d, norm_g[l][None, :], w_in[l].T, conv_w[l], conv_b[l][None, :], w_up_p,
            gla_b_gate[l][None, :], gla_norm_g[l], w_out[l], batch=batch, seq=seq,
            conv_width=conv_width, dk_total=dk_total, gla_width=gla_width, n_cols=n_pad)
        x2d = _out_proj(y_conv, y_gla, w_out_bf16, x2d, final_g[None, :],
                        final_norm=(l == depth - 1))
    return x2d.reshape(batch, seq, d_model)
```

```python
import functools

import jax
import jax.numpy as jnp
from jax import lax
from jax.experimental import pallas as pl
from jax.experimental.pallas import tpu as pltpu

F32 = jnp.float32
BF16 = jnp.bfloat16

LANES = 128
MXU_N = 256
EPS = 1e-6
CHUNK = 64
GLA_HEADS = 4
GLA_TAU = 16.0

MIX_T = 256
STRIP = 16
COL_CHUNK = 256
W_ROWS = 256
W_SLOTS = 4
IDLE_TAIL_PIECES = 2
PIN_LAG = 2
OUT_TM = 512
VMEM_LIMIT = 56 * 1024 * 1024


def _silu(v):
    return v * jax.nn.sigmoid(v)


def _log_sigmoid(v):
    return -(jnp.maximum(-v, 0.0) + jnp.log1p(jnp.exp(-jnp.abs(v))))


def _spread(units, n_slots):
    out = [[] for _ in range(n_slots)]
    for j, u in enumerate(units):
        out[(j * n_slots) // len(units)].append(u)
    return out


def _proj_mix_kernel(x_ref, x_hbm, ng_ref, w_hbm, convw_ref, convb_ref, wup_ref, bgate_ref, gng_ref, wo_ref,
                     yc_ref, yg_ref, wo_bf_ref,
                     w_ref, w_tail_ref, stage_ref, dma_sem, h_ref, hn_ref, proj_ref, carry_ref, state_ref, sbf_ref,
                     glog_ref, hilo_ref, rev_ref, kdec_ref,
                     *, n_tiles, tiles_per_seq, conv_width, dk_total, gla_width):
    t_rows = x_ref.shape[0]
    in_cols, d_model = w_hbm.shape
    n_strips = t_rows // STRIP
    n_chunks = t_rows // CHUNK
    head_k = dk_total // GLA_HEADS
    head_v = gla_width // GLA_HEADS
    off_h, off_b, off_c, off_z = (i * conv_width for i in range(4))
    off_q = 4 * conv_width
    off_k = off_q + dk_total
    off_v = off_k + dk_total
    off_r = off_v + gla_width
    off_gd = off_r + gla_width
    s = pl.program_id(0)

    def strip_rows(i):
        return slice(i * STRIP, (i + 1) * STRIP)

    def conv_strip(rows):
        token = None
        for c0 in range(0, conv_width, COL_CHUNK):
            cols = slice(c0, c0 + COL_CHUNK)

            def col(off):
                return proj_ref[rows, off + c0:off + c0 + COL_CHUNK].astype(F32)

            u = col(off_c) * col(off_h)
            ext = jnp.concatenate([carry_ref[:, cols], u], axis=0)
            u1 = pltpu.roll(ext, 1, 0)[8:, :]
            u2 = pltpu.roll(ext, 2, 0)[8:, :]
            carry_ref[:, cols] = u[STRIP - 8:, :]
            conv = (convb_ref[:, cols] + convw_ref[0:1, cols] * u2 + convw_ref[1:2, cols] * u1
                    + convw_ref[2:3, cols] * u)
            y = (col(off_b) * conv * _silu(col(off_z))).astype(yc_ref.dtype)
            yc_ref[rows, cols] = y
            token = token_of(y) if token is None else token | token_of(y)
        return token

    def gate_pre():
        glog_ref[...] = jnp.dot(proj_ref[:, off_gd:off_gd + LANES], wup_ref[...],
                                preferred_element_type=F32)

    def gate_strip(rows):
        glog = _log_sigmoid(glog_ref[rows, :] + bgate_ref[...]) / GLA_TAU
        g_hi = glog.astype(BF16)
        hilo_ref[rows, :dk_total] = g_hi
        hilo_ref[rows, dk_total:] = (glog - g_hi.astype(F32)).astype(BF16)

    def later_sums():
        r_i = lax.broadcasted_iota(jnp.int32, (t_rows + 8, t_rows), 0)
        c_i = lax.broadcasted_iota(jnp.int32, (t_rows + 8, t_rows), 1)
        c_chunk = c_i // CHUNK
        ones = ((c_i > r_i) & (c_chunk == r_i // CHUNK)) | (c_chunk == r_i - t_rows)
        sums = jnp.dot(jnp.where(ones, 1.0, 0.0).astype(BF16), hilo_ref[...],
                       preferred_element_type=F32)
        rev_ref[...] = sums[:, :dk_total] + sums[:, dk_total:]

    def kdec_strip(rows):
        k = proj_ref[rows, off_k:off_k + dk_total].astype(F32)
        kdec_ref[rows, :] = (k * jnp.exp(rev_ref[rows, :])).astype(BF16)

    def state_update(step):
        c, h = divmod(step, GLA_HEADS)
        rows = slice(c * CHUNK, (c + 1) * CHUNK)
        ks = slice(h * head_k, (h + 1) * head_k)
        decay = jnp.exp(rev_ref[t_rows + c:t_rows + c + 1, ks])
        v = proj_ref[rows, off_v + h * head_v:off_v + (h + 1) * head_v]
        inc = lax.dot_general(v, kdec_ref[rows, ks], (((0,), (0,)), ((), ())),
                              preferred_element_type=F32)
        s_new = state_ref[h] * decay + inc
        state_ref[h] = s_new
        sbf_ref[h] = s_new.astype(BF16)

    def readout(step):
        c, h = divmod(step, GLA_HEADS)
        rows = slice(c * CHUNK, (c + 1) * CHUNK)
        q = proj_ref[rows, off_q + h * head_k:off_q + (h + 1) * head_k]
        o = lax.dot_general(q, sbf_ref[h], (((1,), (1,)), ((), ())),
                            preferred_element_type=F32) * (head_k ** -0.5)
        ms = jnp.mean(o * o, axis=-1, keepdims=True)
        o = o * lax.rsqrt(ms + EPS) * gng_ref[h:h + 1, :]
        r = proj_ref[rows, off_r + h * head_v:off_r + (h + 1) * head_v].astype(F32)
        yg_ref[rows, h * head_v:(h + 1) * head_v] = (o * _silu(r)).astype(yg_ref.dtype)

    def token_of(y):
        bits = pltpu.bitcast(y, jnp.uint32)
        token = bits[:, :LANES]
        for l0 in range(LANES, bits.shape[1], LANES):
            token = token | bits[:, l0:l0 + LANES]
        return token

    def norm_strip(src, dst, rows):
        x = src[rows, :]
        ms = jnp.mean(x * x, axis=-1, keepdims=True)
        h = (x * lax.rsqrt(ms + EPS) * ng_ref[...]).astype(BF16)
        dst[rows, :] = h
        return token_of(h)

    def pin(tokens):
        acc = tokens[0]
        for t in tokens[1:]:
            acc = acc | t
        zero = pltpu.bitcast((acc >> 16) >> 16, F32)
        zero = jnp.concatenate([zero] * (STRIP // zero.shape[0]), axis=0)
        tile = h_ref[0:STRIP, 0:LANES].astype(F32)
        h_ref[0:STRIP, 0:LANES] = (tile + zero).astype(BF16)

    def proj_piece(c0, width):
        def piece():
            if c0 < w_ref.shape[0] * W_ROWS:
                w = w_ref[c0 // W_ROWS, :, c0 % W_ROWS:c0 % W_ROWS + width]
            else:
                w = w_tail_ref[...]
            proj_ref[:, c0:c0 + width] = jnp.dot(h_ref[...], w,
                                                 preferred_element_type=F32).astype(BF16)
        return piece

    def pieces(off, width):
        return [proj_piece(c0, min(MXU_N, off + width - c0)) for c0 in range(off, off + width, MXU_N)]

    @pl.when(s == 0)
    def _():
        proj_ref[:, :4 * conv_width] = jnp.zeros((t_rows, 4 * conv_width), BF16)

        n_full = in_cols // W_ROWS

        def chunk_copy(i, rows=W_ROWS):
            slot = i % W_SLOTS
            return pltpu.make_async_copy(w_hbm.at[pl.ds(i * W_ROWS, rows), :],
                                         stage_ref.at[slot, pl.ds(0, rows), :], dma_sem.at[slot])

        def transpose_chunk(slot, rows, dst):
            for k0 in range(0, d_model, MXU_N):
                blk = stage_ref[slot, 0:rows, k0:k0 + MXU_N]
                dst[k0:k0 + MXU_N, :] = blk.T.astype(BF16)

        for i in range(W_SLOTS - 1):
            chunk_copy(i).start()

        def load_chunk(i, c):
            @pl.when(i + W_SLOTS - 1 < n_full)
            def _():
                chunk_copy(i + W_SLOTS - 1).start()
            chunk_copy(i).wait()
            transpose_chunk(i % W_SLOTS, W_ROWS, w_ref.at[i])
            return c

        lax.fori_loop(0, n_full, load_chunk, 0)

        tail_rows = in_cols - n_full * W_ROWS
        stage_ref[n_full % W_SLOTS, tail_rows:LANES, :] = jnp.zeros((LANES - tail_rows, d_model), F32)
        chunk_copy(n_full, tail_rows).start()
        chunk_copy(n_full, tail_rows).wait()
        transpose_chunk(n_full % W_SLOTS, LANES, w_tail_ref)

        x0_copy = pltpu.make_async_copy(x_hbm.at[pl.ds(0, t_rows), :],
                                        stage_ref.at[0, pl.ds(0, t_rows), :], dma_sem.at[0])
        x0_copy.start()
        x0_copy.wait()

        def norm_first(i, c):
            norm_strip(stage_ref.at[0], hn_ref, pl.ds(pl.multiple_of(i * STRIP, STRIP), STRIP))
            return c

        lax.fori_loop(0, n_strips, norm_first, 0, unroll=2)

    @pl.when((s == 0) | ((s + tiles_per_seq - 1) % tiles_per_seq == 0))
    def _():
        carry_ref[...] = jnp.zeros_like(carry_ref)

    @pl.when(s < n_tiles)
    def _():
        @pl.when(s % tiles_per_seq == 0)
        def _():
            state_ref[...] = jnp.zeros_like(state_ref)

        wo_bf_ref[...] = wo_ref[...].astype(BF16)

        h_ref[...] = hn_ref[...]

        norm_units = [functools.partial(norm_strip, x_ref, hn_ref, strip_rows(i))
                      for i in range(n_strips)]
        conv_units = [functools.partial(conv_strip, strip_rows(i)) for i in range(n_strips)]
        gate_units = [functools.partial(gate_strip, strip_rows(i)) for i in range(n_strips)]
        kdec_units = [functools.partial(kdec_strip, strip_rows(i)) for i in range(n_strips)]
        n_steps = n_chunks * GLA_HEADS

        mxu = (pieces(off_gd, LANES) + pieces(off_k, dk_total) + pieces(off_q, dk_total)
               + pieces(off_v, gla_width) + pieces(off_r, gla_width))
        n_k = dk_total // MXU_N
        vpu = [conv_units[:2]]
        vpu += _spread([gate_pre] + gate_units, n_k)
        vpu += [[later_sums, conv_units[2]]] + [kdec_units + [conv_units[3]]]
        vpu += [[] for _ in range(dk_total // MXU_N - 2)]
        rest = _spread(conv_units[4:], len(mxu) - len(vpu))
        rest[-1].append(functools.partial(state_update, 0))
        vpu += rest
        assert len(vpu) == len(mxu)

        mxu2 = pieces(off_h, 4 * conv_width)
        halves = []
        for step in range(n_steps):
            if step + 1 < n_steps:
                halves.append(functools.partial(state_update, step + 1))
            halves.append(functools.partial(readout, step))
        busy = len(mxu2) - IDLE_TAIL_PIECES
        vpu2 = [a + b for a, b in zip(_spread(halves, busy), _spread(norm_units, busy))]
        vpu2 += [[] for _ in range(IDLE_TAIL_PIECES)]

        tokens = []
        for piece, units in zip(mxu + mxu2, vpu + vpu2):
            if len(tokens) >= PIN_LAG and tokens[-PIN_LAG]:
                pin(tokens[-PIN_LAG])
            piece()
            tokens.append([t for t in [unit() for unit in units] if t is not None])

    @pl.when(s == n_tiles)
    def _():
        def step(i, c):
            conv_strip(pl.ds(pl.multiple_of(i * STRIP, STRIP), STRIP))
            return c
        lax.fori_loop(0, n_strips, step, 0, unroll=2)


def _proj_mix(x2d, norm_g, w_in_t, conv_w, conv_b, w_up_pad, b_gate, gla_norm_g, w_out, *,
              batch, seq, conv_width, dk_total, gla_width, n_cols):
    t = MIX_T
    nt = seq // t
    n_tiles = batch * nt
    d_model = x2d.shape[1]
    wo_rows = w_out.shape[0] // n_tiles
    assert w_out.shape[0] % n_tiles == 0 and wo_rows % STRIP == 0 and t <= W_ROWS
    main_cols = n_cols - LANES
    assert main_cols % W_ROWS == 0 and W_ROWS % MXU_N == 0 and 0 < w_in_t.shape[0] - main_cols <= LANES
    head_k = dk_total // GLA_HEADS
    head_v = gla_width // GLA_HEADS

    def whole(shape, **kw):
        return pl.BlockSpec(shape, lambda s: (0,) * len(shape), **kw)

    kern = functools.partial(_proj_mix_kernel, n_tiles=n_tiles, tiles_per_seq=nt,
                             conv_width=conv_width, dk_total=dk_total, gla_width=gla_width)
    return pl.pallas_call(
        kern,
        grid=(n_tiles + 1,),
        in_specs=[
            pl.BlockSpec((t, d_model), lambda s: (jnp.minimum(s + 1, n_tiles - 1), 0)),
            pl.BlockSpec(memory_space=pl.ANY),
            whole(norm_g.shape),
            pl.BlockSpec(memory_space=pl.ANY),
            whole(conv_w.shape), whole(conv_b.shape), whole(w_up_pad.shape), whole(b_gate.shape),
            whole(gla_norm_g.shape),
            pl.BlockSpec((wo_rows, w_out.shape[1]), lambda s: (jnp.minimum(s, n_tiles - 1), 0)),
        ],
        out_specs=[
            pl.BlockSpec((t, conv_width), lambda s: (jnp.maximum(s - 1, 0), 0)),
            pl.BlockSpec((t, gla_width), lambda s: (jnp.minimum(s, n_tiles - 1), 0)),
            pl.BlockSpec((wo_rows, w_out.shape[1]), lambda s: (jnp.minimum(s, n_tiles - 1), 0)),
        ],
        out_shape=[jax.ShapeDtypeStruct((batch * seq, conv_width), BF16),
                   jax.ShapeDtypeStruct((batch * seq, gla_width), BF16),
                   jax.ShapeDtypeStruct(w_out.shape, BF16)],
        scratch_shapes=[pltpu.VMEM((main_cols // W_ROWS, d_model, W_ROWS), BF16),
                        pltpu.VMEM((d_model, LANES), BF16),
                        pltpu.VMEM((W_SLOTS, W_ROWS, d_model), F32),
                        pltpu.SemaphoreType.DMA((W_SLOTS,)),
                        pltpu.VMEM((t, d_model), BF16),
                        pltpu.VMEM((t, d_model), BF16),
                        pltpu.VMEM((t, n_cols), BF16),
                        pltpu.VMEM((8, conv_width), F32),
                        pltpu.VMEM((GLA_HEADS, head_v, head_k), F32),
                        pltpu.VMEM((GLA_HEADS, head_v, head_k), BF16),
                        pltpu.VMEM((t, dk_total), F32),
                        pltpu.VMEM((t, 2 * dk_total), BF16),
                        pltpu.VMEM((t + 8, dk_total), F32),
                        pltpu.VMEM((t, dk_total), BF16)],
        compiler_params=pltpu.CompilerParams(
            dimension_semantics=("arbitrary",),
            vmem_limit_bytes=VMEM_LIMIT),
        name="proj_mix",
    )(x2d, x2d, norm_g, w_in_t, conv_w, conv_b, w_up_pad, b_gate, gla_norm_g, w_out)


def _out_proj_kernel(yc_ref, yg_ref, w_ref, x_ref, g_ref, o_ref, *, final_norm):
    kc = yc_ref.shape[1]
    z = (x_ref[...] + jnp.dot(yc_ref[...], w_ref[:kc, :], preferred_element_type=F32)
         + jnp.dot(yg_ref[...], w_ref[kc:, :], preferred_element_type=F32))
    if final_norm:
        ms = jnp.mean(z * z, axis=-1, keepdims=True)
        z = z * lax.rsqrt(ms + EPS) * g_ref[...]
    o_ref[...] = z


def _out_proj(y_conv, y_gla, w_out_bf16, x2d, final_g, *, final_norm):
    m, d = x2d.shape
    return pl.pallas_call(
        functools.partial(_out_proj_kernel, final_norm=final_norm),
        grid=(m // OUT_TM,),
        in_specs=[
            pl.BlockSpec((OUT_TM, y_conv.shape[1]), lambda i: (i, 0)),
            pl.BlockSpec((OUT_TM, y_gla.shape[1]), lambda i: (i, 0)),
            pl.BlockSpec(w_out_bf16.shape, lambda i: (0, 0)),
            pl.BlockSpec((OUT_TM, d), lambda i: (i, 0)),
            pl.BlockSpec((1, d), lambda i: (0, 0)),
        ],
        out_specs=pl.BlockSpec((OUT_TM, d), lambda i: (i, 0)),
        out_shape=jax.ShapeDtypeStruct((m, d), F32),
        compiler_params=pltpu.CompilerParams(
            dimension_semantics=("arbitrary",),
            vmem_limit_bytes=VMEM_LIMIT),
        name="out_proj",
    )(y_conv, y_gla, w_out_bf16, x2d, final_g)


def kernel(x, norm_g, w_in, conv_w, conv_b, gla_w_up, gla_b_gate, gla_norm_g, w_out, final_g):
    batch, seq, d_model = x.shape
    depth = norm_g.shape[0]
    conv_width = conv_w.shape[2]
    rank, dk_total = gla_w_up.shape[1], gla_w_up.shape[2]
    gla_width = gla_norm_g.shape[1] * gla_norm_g.shape[2]
    in_cols = w_in.shape[2]
    main_cols = in_cols - rank
    assert main_cols == 4 * conv_width + 2 * dk_total + 2 * gla_width
    assert main_cols % MXU_N == 0 and dk_total % MXU_N == 0 and rank <= LANES
    assert seq % MIX_T == 0 and MIX_T % CHUNK == 0 and (batch * seq) % OUT_TM == 0
    n_pad = main_cols + LANES

    x2d = x.reshape(batch * seq, d_model)
    for l in range(depth):
        w_up_p = jnp.pad(gla_w_up[l], ((0, LANES - rank), (0, 0))).astype(BF16)
        y_conv, y_gla, w_out_bf16 = _proj_mix(
            x2d, norm_g[l][None, :], w_in[l].T, conv_w[l], conv_b[l][None, :], w_up_p,
            gla_b_gate[l][None, :], gla_norm_g[l], w_out[l], batch=batch, seq=seq,
            conv_width=conv_width, dk_total=dk_total, gla_width=gla_width, n_cols=n_pad)
        x2d = _out_proj(y_conv, y_gla, w_out_bf16, x2d, final_g[None, :],
                        final_norm=(l == depth - 1))
    return x2d.reshape(batch, seq, d_model)
```

```python
import functools

import jax
import jax.numpy as jnp
from jax import lax
from jax.experimental import pallas as pl
from jax.experimental.pallas import tpu as pltpu

F32 = jnp.float32
BF16 = jnp.bfloat16

LANES = 128
SUBLANES = 8
MXU_N = 256
EPS = 1e-6
CHUNK = 64
GLA_HEADS = 4
GLA_TAU = 16.0

MIX_T = 256
STRIP = 16
COL_CHUNK = 256
W_ROWS = 256
W_SLOTS = 4
IDLE_TAIL_PIECES = 2
PIN_LAG = 2
OUT_TM = 512
VMEM_LIMIT = 56 * 1024 * 1024


def _silu(v):
    return v * jax.nn.sigmoid(v)


def _log_sigmoid(v):
    return -(jnp.maximum(-v, 0.0) + jnp.log1p(jnp.exp(-jnp.abs(v))))


def _spread(units, n_slots):
    out = [[] for _ in range(n_slots)]
    for j, u in enumerate(units):
        out[(j * n_slots) // len(units)].append(u)
    return out


def _proj_mix_kernel(x_ref, x_hbm, ng_ref, w_hbm, convw_ref, convb_ref, wup_ref, bgate_ref, gng_ref, wo_ref,
                     yc_ref, yg_ref, wo_bf_ref,
                     w_ref, w_tail_ref, stage_ref, dma_sem, h_ref, hn_ref, proj_ref, carry_ref, state_ref, sbf_ref,
                     glog_ref, hilo_ref, rev_ref, kdec_ref,
                     *, n_tiles, tiles_per_seq, conv_width, dk_total, gla_width):
    t_rows = x_ref.shape[0]
    in_cols, d_model = w_hbm.shape
    n_strips = t_rows // STRIP
    n_chunks = t_rows // CHUNK
    head_k = dk_total // GLA_HEADS
    head_v = gla_width // GLA_HEADS
    off_h, off_b, off_c, off_z = (i * conv_width for i in range(4))
    off_q = 4 * conv_width
    off_k = off_q + dk_total
    off_v = off_k + dk_total
    off_r = off_v + gla_width
    off_gd = off_r + gla_width
    s = pl.program_id(0)

    def strip_rows(i):
        return slice(i * STRIP, (i + 1) * STRIP)

    def conv_strip(rows):
        token = None
        for c0 in range(0, conv_width, COL_CHUNK):
            cols = slice(c0, c0 + COL_CHUNK)

            def col(off):
                return proj_ref[rows, off + c0:off + c0 + COL_CHUNK].astype(F32)

            u = col(off_c) * col(off_h)
            ext = jnp.concatenate([carry_ref[:, cols], u], axis=0)
            u1 = pltpu.roll(ext, 1, 0)[8:, :]
            u2 = pltpu.roll(ext, 2, 0)[8:, :]
            carry_ref[:, cols] = u[STRIP - 8:, :]
            conv = (convb_ref[:, cols] + convw_ref[0:1, cols] * u2 + convw_ref[1:2, cols] * u1
                    + convw_ref[2:3, cols] * u)
            y = (col(off_b) * conv * _silu(col(off_z))).astype(yc_ref.dtype)
            yc_ref[rows, cols] = y
            token = token_of(y) if token is None else token | token_of(y)
        return token

    def gate_pre():
        glog_ref[...] = jnp.dot(proj_ref[:, off_gd:off_gd + LANES], wup_ref[...],
                                preferred_element_type=F32)

    def gate_strip(rows):
        glog = _log_sigmoid(glog_ref[rows, :] + bgate_ref[...]) / GLA_TAU
        g_hi = glog.astype(BF16)
        hilo_ref[rows, :dk_total] = g_hi
        hilo_ref[rows, dk_total:] = (glog - g_hi.astype(F32)).astype(BF16)

    def later_sums():
        r_i = lax.broadcasted_iota(jnp.int32, (t_rows + 8, t_rows), 0)
        c_i = lax.broadcasted_iota(jnp.int32, (t_rows + 8, t_rows), 1)
        c_chunk = c_i // CHUNK
        ones = ((c_i > r_i) & (c_chunk == r_i // CHUNK)) | (c_chunk == r_i - t_rows)
        sums = jnp.dot(jnp.where(ones, 1.0, 0.0).astype(BF16), hilo_ref[...],
                       preferred_element_type=F32)
        rev_ref[...] = sums[:, :dk_total] + sums[:, dk_total:]

    def kdec_strip(rows):
        k = proj_ref[rows, off_k:off_k + dk_total].astype(F32)
        kdec_ref[rows, :] = (k * jnp.exp(rev_ref[rows, :])).astype(BF16)

    def state_update(step):
        c, h = divmod(step, GLA_HEADS)
        rows = slice(c * CHUNK, (c + 1) * CHUNK)
        ks = slice(h * head_k, (h + 1) * head_k)
        decay = jnp.exp(rev_ref[t_rows + c:t_rows + c + 1, ks])
        v = proj_ref[rows, off_v + h * head_v:off_v + (h + 1) * head_v]
        inc = lax.dot_general(v, kdec_ref[rows, ks], (((0,), (0,)), ((), ())),
                              preferred_element_type=F32)
        s_new = state_ref[h] * decay + inc
        state_ref[h] = s_new
        sbf_ref[h] = s_new.astype(BF16)

    def readout(step):
        c, h = divmod(step, GLA_HEADS)
        rows = slice(c * CHUNK, (c + 1) * CHUNK)
        q = proj_ref[rows, off_q + h * head_k:off_q + (h + 1) * head_k]
        o = lax.dot_general(q, sbf_ref[h], (((1,), (1,)), ((), ())),
                            preferred_element_type=F32) * (head_k ** -0.5)
        ms = jnp.mean(o * o, axis=-1, keepdims=True)
        o = o * lax.rsqrt(ms + EPS) * gng_ref[h:h + 1, :]
        r = proj_ref[rows, off_r + h * head_v:off_r + (h + 1) * head_v].astype(F32)
        yg_ref[rows, h * head_v:(h + 1) * head_v] = (o * _silu(r)).astype(yg_ref.dtype)

    def token_of(y):
        bits = pltpu.bitcast(y, jnp.uint32)
        token = bits[:, :LANES]
        for l0 in range(LANES, bits.shape[1], LANES):
            token = token | bits[:, l0:l0 + LANES]
        folded = token[:SUBLANES, :]
        for r0 in range(SUBLANES, token.shape[0], SUBLANES):
            folded = folded | token[r0:r0 + SUBLANES, :]
        return folded

    def norm_strip(src, dst, rows):
        x = src[rows, :]
        ms = jnp.mean(x * x, axis=-1, keepdims=True)
        h = (x * lax.rsqrt(ms + EPS) * ng_ref[...]).astype(BF16)
        dst[rows, :] = h
        return token_of(h)

    def pin(tokens):
        acc = tokens[0]
        for t in tokens[1:]:
            acc = acc | t
        zero = pltpu.bitcast((acc >> 16) >> 16, F32)
        zero = jnp.concatenate([zero] * (STRIP // zero.shape[0]), axis=0)
        tile = h_ref[0:STRIP, 0:LANES].astype(F32)
        h_ref[0:STRIP, 0:LANES] = (tile + zero).astype(BF16)

    def proj_piece(c0, width):
        def piece():
            if c0 < w_ref.shape[0] * W_ROWS:
                w = w_ref[c0 // W_ROWS, :, c0 % W_ROWS:c0 % W_ROWS + width]
            else:
                w = w_tail_ref[...]
            y = jnp.dot(h_ref[...], w, preferred_element_type=F32).astype(BF16)
            proj_ref[:, c0:c0 + width] = y
            return token_of(y)
        return piece

    def pieces(off, width):
        return [proj_piece(c0, min(MXU_N, off + width - c0)) for c0 in range(off, off + width, MXU_N)]

    @pl.when(s == 0)
    def _():
        proj_ref[:, :4 * conv_width] = jnp.zeros((t_rows, 4 * conv_width), BF16)

        n_full = in_cols // W_ROWS

        def chunk_copy(i, rows=W_ROWS):
            slot = i % W_SLOTS
            return pltpu.make_async_copy(w_hbm.at[pl.ds(i * W_ROWS, rows), :],
                                         stage_ref.at[slot, pl.ds(0, rows), :], dma_sem.at[slot])

        def transpose_chunk(slot, rows, dst):
            for k0 in range(0, d_model, MXU_N):
                blk = stage_ref[slot, 0:rows, k0:k0 + MXU_N]
                dst[k0:k0 + MXU_N, :] = blk.T.astype(BF16)

        for i in range(W_SLOTS - 1):
            chunk_copy(i).start()

        def load_chunk(i, c):
            @pl.when(i + W_SLOTS - 1 < n_full)
            def _():
                chunk_copy(i + W_SLOTS - 1).start()
            chunk_copy(i).wait()
            transpose_chunk(i % W_SLOTS, W_ROWS, w_ref.at[i])
            return c

        lax.fori_loop(0, n_full, load_chunk, 0)

        tail_rows = in_cols - n_full * W_ROWS
        stage_ref[n_full % W_SLOTS, tail_rows:LANES, :] = jnp.zeros((LANES - tail_rows, d_model), F32)
        chunk_copy(n_full, tail_rows).start()
        chunk_copy(n_full, tail_rows).wait()
        transpose_chunk(n_full % W_SLOTS, LANES, w_tail_ref)

        x0_copy = pltpu.make_async_copy(x_hbm.at[pl.ds(0, t_rows), :],
                                        stage_ref.at[0, pl.ds(0, t_rows), :], dma_sem.at[0])
        x0_copy.start()
        x0_copy.wait()

        def norm_first(i, c):
            norm_strip(stage_ref.at[0], hn_ref, pl.ds(pl.multiple_of(i * STRIP, STRIP), STRIP))
            return c

        lax.fori_loop(0, n_strips, norm_first, 0, unroll=2)

    @pl.when((s == 0) | ((s + tiles_per_seq - 1) % tiles_per_seq == 0))
    def _():
        carry_ref[...] = jnp.zeros_like(carry_ref)

    @pl.when(s < n_tiles)
    def _():
        @pl.when(s % tiles_per_seq == 0)
        def _():
            state_ref[...] = jnp.zeros_like(state_ref)

        wo_bf_ref[...] = wo_ref[...].astype(BF16)

        h_ref[...] = hn_ref[...]

        norm_units = [functools.partial(norm_strip, x_ref, hn_ref, strip_rows(i))
                      for i in range(n_strips)]
        conv_units = [functools.partial(conv_strip, strip_rows(i)) for i in range(n_strips)]
        gate_units = [functools.partial(gate_strip, strip_rows(i)) for i in range(n_strips)]
        kdec_units = [functools.partial(kdec_strip, strip_rows(i)) for i in range(n_strips)]
        n_steps = n_chunks * GLA_HEADS

        mxu = (pieces(off_gd, LANES) + pieces(off_k, dk_total) + pieces(off_q, dk_total)
               + pieces(off_v, gla_width) + pieces(off_r, gla_width))
        n_k = dk_total // MXU_N
        vpu = [conv_units[:2]]
        vpu += _spread([gate_pre] + gate_units, n_k)
        vpu += [[later_sums, conv_units[2]]] + [kdec_units + [conv_units[3]]]
        vpu += [[] for _ in range(dk_total // MXU_N - 2)]
        rest = _spread(conv_units[4:], len(mxu) - len(vpu))
        rest[-1].append(functools.partial(state_update, 0))
        vpu += rest
        assert len(vpu) == len(mxu)

        mxu2 = pieces(off_h, 4 * conv_width)
        halves = []
        for step in range(n_steps):
            if step + 1 < n_steps:
                halves.append(functools.partial(state_update, step + 1))
            halves.append(functools.partial(readout, step))
        busy = len(mxu2) - IDLE_TAIL_PIECES
        vpu2 = [a + b for a, b in zip(_spread(halves, busy), _spread(norm_units, busy))]
        vpu2 += [[] for _ in range(IDLE_TAIL_PIECES)]

        tokens = []
        for piece, units in zip(mxu + mxu2, vpu + vpu2):
            if len(tokens) >= PIN_LAG and tokens[-PIN_LAG]:
                pin(tokens[-PIN_LAG])
            results = [piece()] + [unit() for unit in units]
            tokens.append([t for t in results if t is not None])

    @pl.when(s == n_tiles)
    def _():
        def step(i, c):
            conv_strip(pl.ds(pl.multiple_of(i * STRIP, STRIP), STRIP))
            return c
        lax.fori_loop(0, n_strips, step, 0, unroll=2)


def _proj_mix(x2d, norm_g, w_in_t, conv_w, conv_b, w_up_pad, b_gate, gla_norm_g, w_out, *,
              batch, seq, conv_width, dk_total, gla_width, n_cols):
    t = MIX_T
    nt = seq // t
    n_tiles = batch * nt
    d_model = x2d.shape[1]
    wo_rows = w_out.shape[0] // n_tiles
    assert w_out.shape[0] % n_tiles == 0 and wo_rows % STRIP == 0 and t <= W_ROWS
    main_cols = n_cols - LANES
    assert main_cols % W_ROWS == 0 and W_ROWS % MXU_N == 0 and 0 < w_in_t.shape[0] - main_cols <= LANES
    head_k = dk_total // GLA_HEADS
    head_v = gla_width // GLA_HEADS

    def whole(shape, **kw):
        return pl.BlockSpec(shape, lambda s: (0,) * len(shape), **kw)

    kern = functools.partial(_proj_mix_kernel, n_tiles=n_tiles, tiles_per_seq=nt,
                             conv_width=conv_width, dk_total=dk_total, gla_width=gla_width)
    return pl.pallas_call(
        kern,
        grid=(n_tiles + 1,),
        in_specs=[
            pl.BlockSpec((t, d_model), lambda s: (jnp.minimum(s + 1, n_tiles - 1), 0)),
            pl.BlockSpec(memory_space=pl.ANY),
            whole(norm_g.shape),
            pl.BlockSpec(memory_space=pl.ANY),
            whole(conv_w.shape), whole(conv_b.shape), whole(w_up_pad.shape), whole(b_gate.shape),
            whole(gla_norm_g.shape),
            pl.BlockSpec((wo_rows, w_out.shape[1]), lambda s: (jnp.minimum(s, n_tiles - 1), 0)),
        ],
        out_specs=[
            pl.BlockSpec((t, conv_width), lambda s: (jnp.maximum(s - 1, 0), 0)),
            pl.BlockSpec((t, gla_width), lambda s: (jnp.minimum(s, n_tiles - 1), 0)),
            pl.BlockSpec((wo_rows, w_out.shape[1]), lambda s: (jnp.minimum(s, n_tiles - 1), 0)),
        ],
        out_shape=[jax.ShapeDtypeStruct((batch * seq, conv_width), BF16),
                   jax.ShapeDtypeStruct((batch * seq, gla_width), BF16),
                   jax.ShapeDtypeStruct(w_out.shape, BF16)],
        scratch_shapes=[pltpu.VMEM((main_cols // W_ROWS, d_model, W_ROWS), BF16),
                        pltpu.VMEM((d_model, LANES), BF16),
                        pltpu.VMEM((W_SLOTS, W_ROWS, d_model), F32),
                        pltpu.SemaphoreType.DMA((W_SLOTS,)),
                        pltpu.VMEM((t, d_model), BF16),
                        pltpu.VMEM((t, d_model), BF16),
                        pltpu.VMEM((t, n_cols), BF16),
                        pltpu.VMEM((8, conv_width), F32),
                        pltpu.VMEM((GLA_HEADS, head_v, head_k), F32),
                        pltpu.VMEM((GLA_HEADS, head_v, head_k), BF16),
                        pltpu.VMEM((t, dk_total), F32),
                        pltpu.VMEM((t, 2 * dk_total), BF16),
                        pltpu.VMEM((t + 8, dk_total), F32),
                        pltpu.VMEM((t, dk_total), BF16)],
        compiler_params=pltpu.CompilerParams(
            dimension_semantics=("arbitrary",),
            vmem_limit_bytes=VMEM_LIMIT),
        name="proj_mix",
    )(x2d, x2d, norm_g, w_in_t, conv_w, conv_b, w_up_pad, b_gate, gla_norm_g, w_out)


def _out_proj_kernel(yc_ref, yg_ref, w_ref, x_ref, g_ref, o_ref, *, final_norm):
    kc = yc_ref.shape[1]
    z = (x_ref[...] + jnp.dot(yc_ref[...], w_ref[:kc, :], preferred_element_type=F32)
         + jnp.dot(yg_ref[...], w_ref[kc:, :], preferred_element_type=F32))
    if final_norm:
        ms = jnp.mean(z * z, axis=-1, keepdims=True)
        z = z * lax.rsqrt(ms + EPS) * g_ref[...]
    o_ref[...] = z


def _out_proj(y_conv, y_gla, w_out_bf16, x2d, final_g, *, final_norm):
    m, d = x2d.shape
    return pl.pallas_call(
        functools.partial(_out_proj_kernel, final_norm=final_norm),
        grid=(m // OUT_TM,),
        in_specs=[
            pl.BlockSpec((OUT_TM, y_conv.shape[1]), lambda i: (i, 0)),
            pl.BlockSpec((OUT_TM, y_gla.shape[1]), lambda i: (i, 0)),
            pl.BlockSpec(w_out_bf16.shape, lambda i: (0, 0)),
            pl.BlockSpec((OUT_TM, d), lambda i: (i, 0)),
            pl.BlockSpec((1, d), lambda i: (0, 0)),
        ],
        out_specs=pl.BlockSpec((OUT_TM, d), lambda i: (i, 0)),
        out_shape=jax.ShapeDtypeStruct((m, d), F32),
        compiler_params=pltpu.CompilerParams(
            dimension_semantics=("arbitrary",),
            vmem_limit_bytes=VMEM_LIMIT),
        name="out_proj",
    )(y_conv, y_gla, w_out_bf16, x2d, final_g)


def kernel(x, norm_g, w_in, conv_w, conv_b, gla_w_up, gla_b_gate, gla_norm_g, w_out, final_g):
    batch, seq, d_model = x.shape
    depth = norm_g.shape[0]
    conv_width = conv_w.shape[2]
    rank, dk_total = gla_w_up.shape[1], gla_w_up.shape[2]
    gla_width = gla_norm_g.shape[1] * gla_norm_g.shape[2]
    in_cols = w_in.shape[2]
    main_cols = in_cols - rank
    assert main_cols == 4 * conv_width + 2 * dk_total + 2 * gla_width
    assert main_cols % MXU_N == 0 and dk_total % MXU_N == 0 and rank <= LANES
    assert seq % MIX_T == 0 and MIX_T % CHUNK == 0 and (batch * seq) % OUT_TM == 0
    n_pad = main_cols + LANES

    x2d = x.reshape(batch * seq, d_model)
    for l in range(depth):
        w_up_p = jnp.pad(gla_w_up[l], ((0, LANES - rank), (0, 0))).astype(BF16)
        y_conv, y_gla, w_out_bf16 = _proj_mix(
            x2d, norm_g[l][None, :], w_in[l].T, conv_w[l], conv_b[l][None, :], w_up_p,
            gla_b_gate[l][None, :], gla_norm_g[l], w_out[l], batch=batch, seq=seq,
            conv_width=conv_width, dk_total=dk_total, gla_width=gla_width, n_cols=n_pad)
        x2d = _out_proj(y_conv, y_gla, w_out_bf16, x2d, final_g[None, :],
                        final_norm=(l == depth - 1))
    return x2d.reshape(batch, seq, d_model)
```

```python
import functools

import jax
import jax.numpy as jnp
from jax import lax
from jax.experimental import pallas as pl
from jax.experimental.pallas import tpu as pltpu

F32 = jnp.float32
BF16 = jnp.bfloat16

LANES = 128
MXU_N = 256
EPS = 1e-6
CHUNK = 64
GLA_HEADS = 4
GLA_TAU = 16.0

MIX_T = 256
STRIP = 16
COL_CHUNK = 256
W_ROWS = 256
W_SLOTS = 3
IDLE_TAIL_PIECES = 2
PIN_LAG = 2
OUT_TM = 512
VMEM_LIMIT = 56 * 1024 * 1024


def _silu(v):
    return v * jax.nn.sigmoid(v)


def _log_sigmoid(v):
    return -(jnp.maximum(-v, 0.0) + jnp.log1p(jnp.exp(-jnp.abs(v))))


def _spread(units, n_slots):
    out = [[] for _ in range(n_slots)]
    for j, u in enumerate(units):
        out[(j * n_slots) // len(units)].append(u)
    return out


def _proj_mix_kernel(x_ref, x_hbm, ng_ref, w_hbm, convw_ref, convb_ref, wup_ref, bgate_ref, gng_ref, wo_ref,
                     yc_ref, yg_ref, wo_bf_ref,
                     w_ref, w_tail_ref, stage_ref, dma_sem, h_ref, hn_ref, pc_ref, pg_ref, carry_ref, state_ref, sbf_ref,
                     glog_ref, hilo_ref, rev_ref, kdec_ref,
                     *, n_tiles, tiles_per_seq, conv_width, dk_total, gla_width):
    t_rows = x_ref.shape[0]
    in_cols, d_model = w_hbm.shape
    n_strips = t_rows // STRIP
    n_chunks = t_rows // CHUNK
    head_k = dk_total // GLA_HEADS
    head_v = gla_width // GLA_HEADS
    off_h, off_b, off_c, off_z = (i * conv_width for i in range(4))
    gla_col0 = 4 * conv_width
    off_q = 0
    off_k = off_q + dk_total
    off_v = off_k + dk_total
    off_r = off_v + gla_width
    off_gd = off_r + gla_width
    s = pl.program_id(0)
    cur = s % 2

    def strip_rows(i):
        return slice(i * STRIP, (i + 1) * STRIP)

    def conv_strip(rows):
        token = None
        for c0 in range(0, conv_width, COL_CHUNK):
            cols = slice(c0, c0 + COL_CHUNK)

            def col(off):
                return pc_ref[1 - cur, rows, off + c0:off + c0 + COL_CHUNK].astype(F32)

            u = col(off_c) * col(off_h)
            ext = jnp.concatenate([carry_ref[:, cols], u], axis=0)
            u1 = pltpu.roll(ext, 1, 0)[8:, :]
            u2 = pltpu.roll(ext, 2, 0)[8:, :]
            carry_ref[:, cols] = u[STRIP - 8:, :]
            conv = (convb_ref[:, cols] + convw_ref[0:1, cols] * u2 + convw_ref[1:2, cols] * u1
                    + convw_ref[2:3, cols] * u)
            y = (col(off_b) * conv * _silu(col(off_z))).astype(yc_ref.dtype)
            yc_ref[rows, cols] = y
            token = token_of(y) if token is None else token | token_of(y)
        return token

    def gate_pre():
        glog_ref[...] = jnp.dot(pg_ref[:, off_gd:off_gd + LANES], wup_ref[...],
                                preferred_element_type=F32)

    def gate_strip(rows):
        glog = _log_sigmoid(glog_ref[rows, :] + bgate_ref[...]) / GLA_TAU
        g_hi = glog.astype(BF16)
        hilo_ref[rows, :dk_total] = g_hi
        hilo_ref[rows, dk_total:] = (glog - g_hi.astype(F32)).astype(BF16)

    def later_sums():
        r_i = lax.broadcasted_iota(jnp.int32, (t_rows + 8, t_rows), 0)
        c_i = lax.broadcasted_iota(jnp.int32, (t_rows + 8, t_rows), 1)
        c_chunk = c_i // CHUNK
        ones = ((c_i > r_i) & (c_chunk == r_i // CHUNK)) | (c_chunk == r_i - t_rows)
        sums = jnp.dot(jnp.where(ones, 1.0, 0.0).astype(BF16), hilo_ref[...],
                       preferred_element_type=F32)
        rev_ref[...] = sums[:, :dk_total] + sums[:, dk_total:]

    def kdec_strip(rows):
        k = pg_ref[rows, off_k:off_k + dk_total].astype(F32)
        kdec_ref[rows, :] = (k * jnp.exp(rev_ref[rows, :])).astype(BF16)

    def state_update(step):
        c, h = divmod(step, GLA_HEADS)
        rows = slice(c * CHUNK, (c + 1) * CHUNK)
        ks = slice(h * head_k, (h + 1) * head_k)
        decay = jnp.exp(rev_ref[t_rows + c:t_rows + c + 1, ks])
        v = pg_ref[rows, off_v + h * head_v:off_v + (h + 1) * head_v]
        inc = lax.dot_general(v, kdec_ref[rows, ks], (((0,), (0,)), ((), ())),
                              preferred_element_type=F32)
        s_new = state_ref[h] * decay + inc
        state_ref[h] = s_new
        sbf_ref[h] = s_new.astype(BF16)

    def readout(step):
        c, h = divmod(step, GLA_HEADS)
        rows = slice(c * CHUNK, (c + 1) * CHUNK)
        q = pg_ref[rows, off_q + h * head_k:off_q + (h + 1) * head_k]
        o = lax.dot_general(q, sbf_ref[h], (((1,), (1,)), ((), ())),
                            preferred_element_type=F32) * (head_k ** -0.5)
        ms = jnp.mean(o * o, axis=-1, keepdims=True)
        o = o * lax.rsqrt(ms + EPS) * gng_ref[h:h + 1, :]
        r = pg_ref[rows, off_r + h * head_v:off_r + (h + 1) * head_v].astype(F32)
        yg_ref[rows, h * head_v:(h + 1) * head_v] = (o * _silu(r)).astype(yg_ref.dtype)

    def token_of(y):
        bits = pltpu.bitcast(y, jnp.uint32)
        token = bits[:, :LANES]
        for l0 in range(LANES, bits.shape[1], LANES):
            token = token | bits[:, l0:l0 + LANES]
        return token

    def norm_strip(src, dst, rows):
        x = src[rows, :]
        ms = jnp.mean(x * x, axis=-1, keepdims=True)
        h = (x * lax.rsqrt(ms + EPS) * ng_ref[...]).astype(BF16)
        dst[rows, :] = h
        return token_of(h)

    def pin(tokens):
        acc = tokens[0]
        for t in tokens[1:]:
            acc = acc | t
        zero = pltpu.bitcast((acc >> 16) >> 16, F32)
        zero = jnp.concatenate([zero] * (STRIP // zero.shape[0]), axis=0)
        tile = h_ref[0:STRIP, 0:LANES].astype(F32)
        h_ref[0:STRIP, 0:LANES] = (tile + zero).astype(BF16)

    def proj_piece(dst, dst_col0, c0, width):
        def piece():
            g0 = dst_col0 + c0
            if g0 < w_ref.shape[0] * W_ROWS:
                w = w_ref[g0 // W_ROWS, :, g0 % W_ROWS:g0 % W_ROWS + width]
            else:
                w = w_tail_ref[...]
            dst[:, c0:c0 + width] = jnp.dot(h_ref[...], w, preferred_element_type=F32).astype(BF16)
        return piece

    def pieces(dst, dst_col0, off, width):
        return [proj_piece(dst, dst_col0, c0, min(MXU_N, off + width - c0))
                for c0 in range(off, off + width, MXU_N)]

    @pl.when(s == 0)
    def _():
        pc_ref[1] = jnp.zeros(pc_ref.shape[1:], BF16)

        n_full = in_cols // W_ROWS

        def chunk_copy(i, rows=W_ROWS):
            slot = i % W_SLOTS
            return pltpu.make_async_copy(w_hbm.at[pl.ds(i * W_ROWS, rows), :],
                                         stage_ref.at[slot, pl.ds(0, rows), :], dma_sem.at[slot])

        def transpose_chunk(slot, rows, dst):
            for k0 in range(0, d_model, MXU_N):
                blk = stage_ref[slot, 0:rows, k0:k0 + MXU_N]
                dst[k0:k0 + MXU_N, :] = blk.T.astype(BF16)

        for i in range(W_SLOTS - 1):
            chunk_copy(i).start()

        def load_chunk(i, c):
            @pl.when(i + W_SLOTS - 1 < n_full)
            def _():
                chunk_copy(i + W_SLOTS - 1).start()
            chunk_copy(i).wait()
            transpose_chunk(i % W_SLOTS, W_ROWS, w_ref.at[i])
            return c

        lax.fori_loop(0, n_full, load_chunk, 0)

        tail_rows = in_cols - n_full * W_ROWS
        stage_ref[n_full % W_SLOTS, tail_rows:LANES, :] = jnp.zeros((LANES - tail_rows, d_model), F32)
        chunk_copy(n_full, tail_rows).start()
        chunk_copy(n_full, tail_rows).wait()
        transpose_chunk(n_full % W_SLOTS, LANES, w_tail_ref)

        x0_copy = pltpu.make_async_copy(x_hbm.at[pl.ds(0, t_rows), :],
                                        stage_ref.at[0, pl.ds(0, t_rows), :], dma_sem.at[0])
        x0_copy.start()
        x0_copy.wait()

        def norm_first(i, c):
            norm_strip(stage_ref.at[0], hn_ref, pl.ds(pl.multiple_of(i * STRIP, STRIP), STRIP))
            return c

        lax.fori_loop(0, n_strips, norm_first, 0, unroll=2)

    @pl.when((s == 0) | ((s + tiles_per_seq - 1) % tiles_per_seq == 0))
    def _():
        carry_ref[...] = jnp.zeros_like(carry_ref)

    @pl.when(s < n_tiles)
    def _():
        @pl.when(s % tiles_per_seq == 0)
        def _():
            state_ref[...] = jnp.zeros_like(state_ref)

        wo_bf_ref[...] = wo_ref[...].astype(BF16)

        h_ref[...] = hn_ref[...]

        norm_units = [functools.partial(norm_strip, x_ref, hn_ref, strip_rows(i))
                      for i in range(n_strips)]
        conv_units = [functools.partial(conv_strip, strip_rows(i)) for i in range(n_strips)]
        gate_units = [functools.partial(gate_strip, strip_rows(i)) for i in range(n_strips)]
        kdec_units = [functools.partial(kdec_strip, strip_rows(i)) for i in range(n_strips)]
        n_steps = n_chunks * GLA_HEADS

        mxu = [p for off, width in ((off_gd, LANES), (off_k, dk_total), (off_q, dk_total),
                                    (off_v, gla_width), (off_r, gla_width))
               for p in pieces(pg_ref, gla_col0, off, width)]
        n_k = dk_total // MXU_N
        vpu = [[]] + _spread([gate_pre] + gate_units, n_k) + [[later_sums], kdec_units]
        vpu += [[] for _ in range(len(mxu) - len(vpu))]
        vpu[-1].append(functools.partial(state_update, 0))

        mxu2 = pieces(pc_ref.at[cur], 0, off_h, 4 * conv_width)
        halves = []
        for step in range(n_steps):
            if step + 1 < n_steps:
                halves.append(functools.partial(state_update, step + 1))
            halves.append(functools.partial(readout, step))
        busy = len(mxu2) - IDLE_TAIL_PIECES
        vpu2 = [a + b for a, b in zip(_spread(halves, busy), _spread(norm_units, busy))]
        vpu2 += [[] for _ in range(IDLE_TAIL_PIECES)]
        busy_all = len(mxu) + busy
        for slot, extra in zip((vpu + vpu2)[:busy_all], _spread(conv_units, busy_all)):
            slot.extend(extra)

        tokens = []
        for piece, units in zip(mxu + mxu2, vpu + vpu2):
            if len(tokens) >= PIN_LAG and tokens[-PIN_LAG]:
                pin(tokens[-PIN_LAG])
            piece()
            tokens.append([t for t in [unit() for unit in units] if t is not None])

    @pl.when(s == n_tiles)
    def _():
        def step(i, c):
            conv_strip(pl.ds(pl.multiple_of(i * STRIP, STRIP), STRIP))
            return c
        lax.fori_loop(0, n_strips, step, 0, unroll=2)


def _proj_mix(x2d, norm_g, w_in_t, conv_w, conv_b, w_up_pad, b_gate, gla_norm_g, w_out, *,
              batch, seq, conv_width, dk_total, gla_width, n_cols):
    t = MIX_T
    nt = seq // t
    n_tiles = batch * nt
    d_model = x2d.shape[1]
    wo_rows = w_out.shape[0] // n_tiles
    assert w_out.shape[0] % n_tiles == 0 and wo_rows % STRIP == 0 and t <= W_ROWS
    main_cols = n_cols - LANES
    assert main_cols % W_ROWS == 0 and W_ROWS % MXU_N == 0 and 0 < w_in_t.shape[0] - main_cols <= LANES
    head_k = dk_total // GLA_HEADS
    head_v = gla_width // GLA_HEADS

    def whole(shape, **kw):
        return pl.BlockSpec(shape, lambda s: (0,) * len(shape), **kw)

    kern = functools.partial(_proj_mix_kernel, n_tiles=n_tiles, tiles_per_seq=nt,
                             conv_width=conv_width, dk_total=dk_total, gla_width=gla_width)
    return pl.pallas_call(
        kern,
        grid=(n_tiles + 1,),
        in_specs=[
            pl.BlockSpec((t, d_model), lambda s: (jnp.minimum(s + 1, n_tiles - 1), 0)),
            pl.BlockSpec(memory_space=pl.ANY),
            whole(norm_g.shape),
            pl.BlockSpec(memory_space=pl.ANY),
            whole(conv_w.shape), whole(conv_b.shape), whole(w_up_pad.shape), whole(b_gate.shape),
            whole(gla_norm_g.shape),
            pl.BlockSpec((wo_rows, w_out.shape[1]), lambda s: (jnp.minimum(s, n_tiles - 1), 0)),
        ],
        out_specs=[
            pl.BlockSpec((t, conv_width), lambda s: (jnp.maximum(s - 1, 0), 0)),
            pl.BlockSpec((t, gla_width), lambda s: (jnp.minimum(s, n_tiles - 1), 0)),
            pl.BlockSpec((wo_rows, w_out.shape[1]), lambda s: (jnp.minimum(s, n_tiles - 1), 0)),
        ],
        out_shape=[jax.ShapeDtypeStruct((batch * seq, conv_width), BF16),
                   jax.ShapeDtypeStruct((batch * seq, gla_width), BF16),
                   jax.ShapeDtypeStruct(w_out.shape, BF16)],
        scratch_shapes=[pltpu.VMEM((main_cols // W_ROWS, d_model, W_ROWS), BF16),
                        pltpu.VMEM((d_model, LANES), BF16),
                        pltpu.VMEM((W_SLOTS, W_ROWS, d_model), F32),
                        pltpu.SemaphoreType.DMA((W_SLOTS,)),
                        pltpu.VMEM((t, d_model), BF16),
                        pltpu.VMEM((t, d_model), BF16),
                        pltpu.VMEM((2, t, 4 * conv_width), BF16),
                        pltpu.VMEM((t, n_cols - 4 * conv_width), BF16),
                        pltpu.VMEM((8, conv_width), F32),
                        pltpu.VMEM((GLA_HEADS, head_v, head_k), F32),
                        pltpu.VMEM((GLA_HEADS, head_v, head_k), BF16),
                        pltpu.VMEM((t, dk_total), F32),
                        pltpu.VMEM((t, 2 * dk_total), BF16),
                        pltpu.VMEM((t + 8, dk_total), F32),
                        pltpu.VMEM((t, dk_total), BF16)],
        compiler_params=pltpu.CompilerParams(
            dimension_semantics=("arbitrary",),
            vmem_limit_bytes=VMEM_LIMIT),
        name="proj_mix",
    )(x2d, x2d, norm_g, w_in_t, conv_w, conv_b, w_up_pad, b_gate, gla_norm_g, w_out)


def _out_proj_kernel(yc_ref, yg_ref, w_ref, x_ref, g_ref, o_ref, *, final_norm):
    kc = yc_ref.shape[1]
    z = (x_ref[...] + jnp.dot(yc_ref[...], w_ref[:kc, :], preferred_element_type=F32)
         + jnp.dot(yg_ref[...], w_ref[kc:, :], preferred_element_type=F32))
    if final_norm:
        ms = jnp.mean(z * z, axis=-1, keepdims=True)
        z = z * lax.rsqrt(ms + EPS) * g_ref[...]
    o_ref[...] = z


def _out_proj(y_conv, y_gla, w_out_bf16, x2d, final_g, *, final_norm):
    m, d = x2d.shape
    return pl.pallas_call(
        functools.partial(_out_proj_kernel, final_norm=final_norm),
        grid=(m // OUT_TM,),
        in_specs=[
            pl.BlockSpec((OUT_TM, y_conv.shape[1]), lambda i: (i, 0)),
            pl.BlockSpec((OUT_TM, y_gla.shape[1]), lambda i: (i, 0)),
            pl.BlockSpec(w_out_bf16.shape, lambda i: (0, 0)),
            pl.BlockSpec((OUT_TM, d), lambda i: (i, 0)),
            pl.BlockSpec((1, d), lambda i: (0, 0)),
        ],
        out_specs=pl.BlockSpec((OUT_TM, d), lambda i: (i, 0)),
        out_shape=jax.ShapeDtypeStruct((m, d), F32),
        compiler_params=pltpu.CompilerParams(
            dimension_semantics=("arbitrary",),
            vmem_limit_bytes=VMEM_LIMIT),
        name="out_proj",
    )(y_conv, y_gla, w_out_bf16, x2d, final_g)


def kernel(x, norm_g, w_in, conv_w, conv_b, gla_w_up, gla_b_gate, gla_norm_g, w_out, final_g):
    batch, seq, d_model = x.shape
    depth = norm_g.shape[0]
    conv_width = conv_w.shape[2]
    rank, dk_total = gla_w_up.shape[1], gla_w_up.shape[2]
    gla_width = gla_norm_g.shape[1] * gla_norm_g.shape[2]
    in_cols = w_in.shape[2]
    main_cols = in_cols - rank
    assert main_cols == 4 * conv_width + 2 * dk_total + 2 * gla_width
    assert main_cols % MXU_N == 0 and dk_total % MXU_N == 0 and rank <= LANES
    assert seq % MIX_T == 0 and MIX_T % CHUNK == 0 and (batch * seq) % OUT_TM == 0
    n_pad = main_cols + LANES

    x2d = x.reshape(batch * seq, d_model)
    for l in range(depth):
        w_up_p = jnp.pad(gla_w_up[l], ((0, LANES - rank), (0, 0))).astype(BF16)
        y_conv, y_gla, w_out_bf16 = _proj_mix(
            x2d, norm_g[l][None, :], w_in[l].T, conv_w[l], conv_b[l][None, :], w_up_p,
            gla_b_gate[l][None, :], gla_norm_g[l], w_out[l], batch=batch, seq=seq,
            conv_width=conv_width, dk_total=dk_total, gla_width=gla_width, n_cols=n_pad)
        x2d = _out_proj(y_conv, y_gla, w_out_bf16, x2d, final_g[None, :],
                        final_norm=(l == depth - 1))
    return x2d.reshape(batch, seq, d_model)
```

```python
import functools

import jax
import jax.numpy as jnp
from jax import lax
from jax.experimental import pallas as pl
from jax.experimental.pallas import tpu as pltpu

F32 = jnp.float32
BF16 = jnp.bfloat16

LANES = 128
MXU_N = 256
EPS = 1e-6
CHUNK = 64
GLA_HEADS = 4
GLA_TAU = 16.0

MIX_T = 256
STRIP = 16
COL_CHUNK = 256
W_ROWS = 256
W_SLOTS = 4
IDLE_TAIL_PIECES = 2
PIN_LAG = 2
OUT_TM = 512
VMEM_LIMIT = 56 * 1024 * 1024


def _silu(v):
    return v * jax.nn.sigmoid(v)


def _log_sigmoid(v):
    return -(jnp.maximum(-v, 0.0) + jnp.log1p(jnp.exp(-jnp.abs(v))))


def _spread(units, n_slots):
    out = [[] for _ in range(n_slots)]
    for j, u in enumerate(units):
        out[(j * n_slots) // len(units)].append(u)
    return out


def _proj_mix_kernel(x_ref, x_hbm, ng_ref, w_hbm, convw_ref, convb_ref, wup_ref, bgate_ref, gng_ref, wo_ref,
                     yc_ref, yg_ref, wo_bf_ref,
                     w_ref, w_tail_ref, stage_ref, dma_sem, h_ref, hn_ref, proj_ref, carry_ref, state_ref, sbf_ref,
                     glog_ref, hilo_ref, rev_ref, kdec_ref,
                     *, n_tiles, tiles_per_seq, conv_width, dk_total, gla_width):
    t_rows = x_ref.shape[0]
    in_cols, d_model = w_hbm.shape
    n_strips = t_rows // STRIP
    n_chunks = t_rows // CHUNK
    head_k = dk_total // GLA_HEADS
    head_v = gla_width // GLA_HEADS
    off_h, off_b, off_c, off_z = (i * conv_width for i in range(4))
    off_q = 4 * conv_width
    off_k = off_q + dk_total
    off_v = off_k + dk_total
    off_r = off_v + gla_width
    off_gd = off_r + gla_width
    s = pl.program_id(0)

    def strip_rows(i):
        return slice(i * STRIP, (i + 1) * STRIP)

    def conv_strip(rows):
        token = None
        for c0 in range(0, conv_width, COL_CHUNK):
            cols = slice(c0, c0 + COL_CHUNK)

            def col(off):
                return proj_ref[rows, off + c0:off + c0 + COL_CHUNK].astype(F32)

            u = col(off_c) * col(off_h)
            ext = jnp.concatenate([carry_ref[:, cols], u], axis=0)
            u1 = pltpu.roll(ext, 1, 0)[8:, :]
            u2 = pltpu.roll(ext, 2, 0)[8:, :]
            carry_ref[:, cols] = u[STRIP - 8:, :]
            conv = (convb_ref[:, cols] + convw_ref[0:1, cols] * u2 + convw_ref[1:2, cols] * u1
                    + convw_ref[2:3, cols] * u)
            y = (col(off_b) * conv * _silu(col(off_z))).astype(yc_ref.dtype)
            yc_ref[rows, cols] = y
            token = token_of(y) if token is None else token | token_of(y)
        return token

    def gate_pre():
        glog_ref[...] = jnp.dot(proj_ref[:, off_gd:off_gd + LANES], wup_ref[...],
                                preferred_element_type=F32)

    def gate_strip(rows):
        glog = _log_sigmoid(glog_ref[rows, :] + bgate_ref[...]) / GLA_TAU
        g_hi = glog.astype(BF16)
        hilo_ref[rows, :dk_total] = g_hi
        hilo_ref[rows, dk_total:] = (glog - g_hi.astype(F32)).astype(BF16)

    def later_sums():
        r_i = lax.broadcasted_iota(jnp.int32, (t_rows + 8, t_rows), 0)
        c_i = lax.broadcasted_iota(jnp.int32, (t_rows + 8, t_rows), 1)
        c_chunk = c_i // CHUNK
        ones = ((c_i > r_i) & (c_chunk == r_i // CHUNK)) | (c_chunk == r_i - t_rows)
        sums = jnp.dot(jnp.where(ones, 1.0, 0.0).astype(BF16), hilo_ref[...],
                       preferred_element_type=F32)
        rev_ref[...] = sums[:, :dk_total] + sums[:, dk_total:]

    def kdec_strip(rows):
        k = proj_ref[rows, off_k:off_k + dk_total].astype(F32)
        kdec_ref[rows, :] = (k * jnp.exp(rev_ref[rows, :])).astype(BF16)

    def state_update(step):
        c, h = divmod(step, GLA_HEADS)
        rows = slice(c * CHUNK, (c + 1) * CHUNK)
        ks = slice(h * head_k, (h + 1) * head_k)
        decay = jnp.exp(rev_ref[t_rows + c:t_rows + c + 1, ks])
        v = proj_ref[rows, off_v + h * head_v:off_v + (h + 1) * head_v]
        inc = lax.dot_general(v, kdec_ref[rows, ks], (((0,), (0,)), ((), ())),
                              preferred_element_type=F32)
        s_new = state_ref[h] * decay + inc
        state_ref[h] = s_new
        sbf_ref[h] = s_new.astype(BF16)

    def readout(step):
        c, h = divmod(step, GLA_HEADS)
        rows = slice(c * CHUNK, (c + 1) * CHUNK)
        q = proj_ref[rows, off_q + h * head_k:off_q + (h + 1) * head_k]
        o = lax.dot_general(q, sbf_ref[h], (((1,), (1,)), ((), ())),
                            preferred_element_type=F32) * (head_k ** -0.5)
        ms = jnp.mean(o * o, axis=-1, keepdims=True)
        o = o * lax.rsqrt(ms + EPS) * gng_ref[h:h + 1, :]
        r = proj_ref[rows, off_r + h * head_v:off_r + (h + 1) * head_v].astype(F32)
        yg_ref[rows, h * head_v:(h + 1) * head_v] = (o * _silu(r)).astype(yg_ref.dtype)

    def token_of(y):
        bits = pltpu.bitcast(y, jnp.uint32)
        token = bits[:, :LANES]
        for l0 in range(LANES, bits.shape[1], LANES):
            token = token | bits[:, l0:l0 + LANES]
        return token

    def norm_strip(src, dst, rows):
        x = src[rows, :]
        ms = jnp.mean(x * x, axis=-1, keepdims=True)
        h = (x * lax.rsqrt(ms + EPS) * ng_ref[...]).astype(BF16)
        dst[rows, :] = h
        return token_of(h)

    def pin(tokens):
        acc = tokens[0]
        for t in tokens[1:]:
            acc = acc | t
        zero = pltpu.bitcast((acc >> 16) >> 16, F32)
        zero = jnp.concatenate([zero] * (STRIP // zero.shape[0]), axis=0)
        tile = h_ref[t_rows - STRIP:, d_model - LANES:].astype(F32)
        h_ref[t_rows - STRIP:, d_model - LANES:] = (tile + zero).astype(BF16)

    def proj_piece(c0, width):
        def piece():
            if c0 < w_ref.shape[0] * W_ROWS:
                w = w_ref[c0 // W_ROWS, :, c0 % W_ROWS:c0 % W_ROWS + width]
            else:
                w = w_tail_ref[...]
            proj_ref[:, c0:c0 + width] = jnp.dot(h_ref[...], w,
                                                 preferred_element_type=F32).astype(BF16)
        return piece

    def pieces(off, width):
        return [proj_piece(c0, min(MXU_N, off + width - c0)) for c0 in range(off, off + width, MXU_N)]

    @pl.when(s == 0)
    def _():
        proj_ref[:, :4 * conv_width] = jnp.zeros((t_rows, 4 * conv_width), BF16)

        n_full = in_cols // W_ROWS

        def chunk_copy(i, rows=W_ROWS):
            slot = i % W_SLOTS
            return pltpu.make_async_copy(w_hbm.at[pl.ds(i * W_ROWS, rows), :],
                                         stage_ref.at[slot, pl.ds(0, rows), :], dma_sem.at[slot])

        def transpose_chunk(slot, rows, dst):
            for k0 in range(0, d_model, MXU_N):
                blk = stage_ref[slot, 0:rows, k0:k0 + MXU_N]
                dst[k0:k0 + MXU_N, :] = blk.T.astype(BF16)

        for i in range(W_SLOTS - 1):
            chunk_copy(i).start()

        def load_chunk(i, c):
            @pl.when(i + W_SLOTS - 1 < n_full)
            def _():
                chunk_copy(i + W_SLOTS - 1).start()
            chunk_copy(i).wait()
            transpose_chunk(i % W_SLOTS, W_ROWS, w_ref.at[i])
            return c

        lax.fori_loop(0, n_full, load_chunk, 0)

        tail_rows = in_cols - n_full * W_ROWS
        stage_ref[n_full % W_SLOTS, tail_rows:LANES, :] = jnp.zeros((LANES - tail_rows, d_model), F32)
        chunk_copy(n_full, tail_rows).start()
        chunk_copy(n_full, tail_rows).wait()
        transpose_chunk(n_full % W_SLOTS, LANES, w_tail_ref)

        x0_copy = pltpu.make_async_copy(x_hbm.at[pl.ds(0, t_rows), :],
                                        stage_ref.at[0, pl.ds(0, t_rows), :], dma_sem.at[0])
        x0_copy.start()
        x0_copy.wait()

        def norm_first(i, c):
            norm_strip(stage_ref.at[0], hn_ref, pl.ds(pl.multiple_of(i * STRIP, STRIP), STRIP))
            return c

        lax.fori_loop(0, n_strips, norm_first, 0, unroll=2)

    @pl.when((s == 0) | ((s + tiles_per_seq - 1) % tiles_per_seq == 0))
    def _():
        carry_ref[...] = jnp.zeros_like(carry_ref)

    @pl.when(s < n_tiles)
    def _():
        @pl.when(s % tiles_per_seq == 0)
        def _():
            state_ref[...] = jnp.zeros_like(state_ref)

        wo_bf_ref[...] = wo_ref[...].astype(BF16)

        h_ref[...] = hn_ref[...]

        norm_units = [functools.partial(norm_strip, x_ref, hn_ref, strip_rows(i))
                      for i in range(n_strips)]
        conv_units = [functools.partial(conv_strip, strip_rows(i)) for i in range(n_strips)]
        gate_units = [functools.partial(gate_strip, strip_rows(i)) for i in range(n_strips)]
        kdec_units = [functools.partial(kdec_strip, strip_rows(i)) for i in range(n_strips)]
        n_steps = n_chunks * GLA_HEADS

        mxu = (pieces(off_gd, LANES) + pieces(off_k, dk_total) + pieces(off_q, dk_total)
               + pieces(off_v, gla_width) + pieces(off_r, gla_width))
        n_k = dk_total // MXU_N
        vpu = [conv_units[:2]]
        vpu += _spread([gate_pre] + gate_units, n_k)
        vpu += [[later_sums, conv_units[2]]] + [kdec_units + [conv_units[3]]]
        vpu += [[] for _ in range(dk_total // MXU_N - 2)]
        rest = _spread(conv_units[4:], len(mxu) - len(vpu))
        rest[-1].append(functools.partial(state_update, 0))
        vpu += rest
        assert len(vpu) == len(mxu)

        mxu2 = pieces(off_h, 4 * conv_width)
        halves = []
        for step in range(n_steps):
            if step + 1 < n_steps:
                halves.append(functools.partial(state_update, step + 1))
            halves.append(functools.partial(readout, step))
        busy = len(mxu2) - IDLE_TAIL_PIECES
        vpu2 = [a + b for a, b in zip(_spread(halves, busy), _spread(norm_units, busy))]
        vpu2 += [[] for _ in range(IDLE_TAIL_PIECES)]

        tokens = []
        for piece, units in zip(mxu + mxu2, vpu + vpu2):
            if len(tokens) >= PIN_LAG and tokens[-PIN_LAG]:
                pin(tokens[-PIN_LAG])
            piece()
            tokens.append([t for t in [unit() for unit in units] if t is not None])

    @pl.when(s == n_tiles)
    def _():
        def step(i, c):
            conv_strip(pl.ds(pl.multiple_of(i * STRIP, STRIP), STRIP))
            return c
        lax.fori_loop(0, n_strips, step, 0, unroll=2)


def _proj_mix(x2d, norm_g, w_in_t, conv_w, conv_b, w_up_pad, b_gate, gla_norm_g, w_out, *,
              batch, seq, conv_width, dk_total, gla_width, n_cols):
    t = MIX_T
    nt = seq // t
    n_tiles = batch * nt
    d_model = x2d.shape[1]
    wo_rows = w_out.shape[0] // n_tiles
    assert w_out.shape[0] % n_tiles == 0 and wo_rows % STRIP == 0 and t <= W_ROWS
    main_cols = n_cols - LANES
    assert main_cols % W_ROWS == 0 and W_ROWS % MXU_N == 0 and 0 < w_in_t.shape[0] - main_cols <= LANES
    head_k = dk_total // GLA_HEADS
    head_v = gla_width // GLA_HEADS

    def whole(shape, **kw):
        return pl.BlockSpec(shape, lambda s: (0,) * len(shape), **kw)

    kern = functools.partial(_proj_mix_kernel, n_tiles=n_tiles, tiles_per_seq=nt,
                             conv_width=conv_width, dk_total=dk_total, gla_width=gla_width)
    return pl.pallas_call(
        kern,
        grid=(n_tiles + 1,),
        in_specs=[
            pl.BlockSpec((t, d_model), lambda s: (jnp.minimum(s + 1, n_tiles - 1), 0)),
            pl.BlockSpec(memory_space=pl.ANY),
            whole(norm_g.shape),
            pl.BlockSpec(memory_space=pl.ANY),
            whole(conv_w.shape), whole(conv_b.shape), whole(w_up_pad.shape), whole(b_gate.shape),
            whole(gla_norm_g.shape),
            pl.BlockSpec((wo_rows, w_out.shape[1]), lambda s: (jnp.minimum(s, n_tiles - 1), 0)),
        ],
        out_specs=[
            pl.BlockSpec((t, conv_width), lambda s: (jnp.maximum(s - 1, 0), 0)),
            pl.BlockSpec((t, gla_width), lambda s: (jnp.minimum(s, n_tiles - 1), 0)),
            pl.BlockSpec((wo_rows, w_out.shape[1]), lambda s: (jnp.minimum(s, n_tiles - 1), 0)),
        ],
        out_shape=[jax.ShapeDtypeStruct((batch * seq, conv_width), BF16),
                   jax.ShapeDtypeStruct((batch * seq, gla_width), BF16),
                   jax.ShapeDtypeStruct(w_out.shape, BF16)],
        scratch_shapes=[pltpu.VMEM((main_cols // W_ROWS, d_model, W_ROWS), BF16),
                        pltpu.VMEM((d_model, LANES), BF16),
                        pltpu.VMEM((W_SLOTS, W_ROWS, d_model), F32),
                        pltpu.SemaphoreType.DMA((W_SLOTS,)),
                        pltpu.VMEM((t, d_model), BF16),
                        pltpu.VMEM((t, d_model), BF16),
                        pltpu.VMEM((t, n_cols), BF16),
                        pltpu.VMEM((8, conv_width), F32),
                        pltpu.VMEM((GLA_HEADS, head_v, head_k), F32),
                        pltpu.VMEM((GLA_HEADS, head_v, head_k), BF16),
                        pltpu.VMEM((t, dk_total), F32),
                        pltpu.VMEM((t, 2 * dk_total), BF16),
                        pltpu.VMEM((t + 8, dk_total), F32),
                        pltpu.VMEM((t, dk_total), BF16)],
        compiler_params=pltpu.CompilerParams(
            dimension_semantics=("arbitrary",),
            vmem_limit_bytes=VMEM_LIMIT),
        name="proj_mix",
    )(x2d, x2d, norm_g, w_in_t, conv_w, conv_b, w_up_pad, b_gate, gla_norm_g, w_out)


def _out_proj_kernel(yc_ref, yg_ref, w_ref, x_ref, g_ref, o_ref, *, final_norm):
    kc = yc_ref.shape[1]
    z = (x_ref[...] + jnp.dot(yc_ref[...], w_ref[:kc, :], preferred_element_type=F32)
         + jnp.dot(yg_ref[...], w_ref[kc:, :], preferred_element_type=F32))
    if final_norm:
        ms = jnp.mean(z * z, axis=-1, keepdims=True)
        z = z * lax.rsqrt(ms + EPS) * g_ref[...]
    o_ref[...] = z


def _out_proj(y_conv, y_gla, w_out_bf16, x2d, final_g, *, final_norm):
    m, d = x2d.shape
    return pl.pallas_call(
        functools.partial(_out_proj_kernel, final_norm=final_norm),
        grid=(m // OUT_TM,),
        in_specs=[
            pl.BlockSpec((OUT_TM, y_conv.shape[1]), lambda i: (i, 0)),
            pl.BlockSpec((OUT_TM, y_gla.shape[1]), lambda i: (i, 0)),
            pl.BlockSpec(w_out_bf16.shape, lambda i: (0, 0)),
            pl.BlockSpec((OUT_TM, d), lambda i: (i, 0)),
            pl.BlockSpec((1, d), lambda i: (0, 0)),
        ],
        out_specs=pl.BlockSpec((OUT_TM, d), lambda i: (i, 0)),
        out_shape=jax.ShapeDtypeStruct((m, d), F32),
        compiler_params=pltpu.CompilerParams(
            dimension_semantics=("arbitrary",),
            vmem_limit_bytes=VMEM_LIMIT),
        name="out_proj",
    )(y_conv, y_gla, w_out_bf16, x2d, final_g)


def kernel(x, norm_g, w_in, conv_w, conv_b, gla_w_up, gla_b_gate, gla_norm_g, w_out, final_g):
    batch, seq, d_model = x.shape
    depth = norm_g.shape[0]
    conv_width = conv_w.shape[2]
    rank, dk_total = gla_w_up.shape[1], gla_w_up.shape[2]
    gla_width = gla_norm_g.shape[1] * gla_norm_g.shape[2]
    in_cols = w_in.shape[2]
    main_cols = in_cols - rank
    assert main_cols == 4 * conv_width + 2 * dk_total + 2 * gla_width
    assert main_cols % MXU_N == 0 and dk_total % MXU_N == 0 and rank <= LANES
    assert seq % MIX_T == 0 and MIX_T % CHUNK == 0 and (batch * seq) % OUT_TM == 0
    n_pad = main_cols + LANES

    x2d = x.reshape(batch * seq, d_model)
    for l in range(depth):
        w_up_p = jnp.pad(gla_w_up[l], ((0, LANES - rank), (0, 0))).astype(BF16)
        y_conv, y_gla, w_out_bf16 = _proj_mix(
            x2d, norm_g[l][None, :], w_in[l].T, conv_w[l], conv_b[l][None, :], w_up_p,
            gla_b_gate[l][None, :], gla_norm_g[l], w_out[l], batch=batch, seq=seq,
            conv_width=conv_width, dk_total=dk_total, gla_width=gla_width, n_cols=n_pad)
        x2d = _out_proj(y_conv, y_gla, w_out_bf16, x2d, final_g[None, :],
                        final_norm=(l == depth - 1))
    return x2d.reshape(batch, seq, d_model)
```

```python
import functools

import jax
import jax.numpy as jnp
from jax import lax
from jax.experimental import pallas as pl
from jax.experimental.pallas import tpu as pltpu

F32 = jnp.float32
BF16 = jnp.bfloat16

LANES = 128
MXU_N = 256
EPS = 1e-6
CHUNK = 64
GLA_HEADS = 4
GLA_TAU = 16.0

MIX_T = 256
STRIP = 16
COL_CHUNK = 256
W_ROWS = 256
W_SLOTS = 4
IDLE_TAIL_PIECES = 2
PIN_LAG = 2
OUT_TM = 512
VMEM_LIMIT = 56 * 1024 * 1024


def _silu(v):
    return v * jax.nn.sigmoid(v)


def _log_sigmoid(v):
    return -(jnp.maximum(-v, 0.0) + jnp.log1p(jnp.exp(-jnp.abs(v))))


def _spread(units, n_slots):
    out = [[] for _ in range(n_slots)]
    for j, u in enumerate(units):
        out[(j * n_slots) // len(units)].append(u)
    return out


def _proj_mix_kernel(x_ref, x_hbm, ng_ref, w_hbm, convw_ref, convb_ref, wup_ref, bgate_ref, gng_ref, wo_ref,
                     yc_ref, yg_ref, wo_bf_ref,
                     w_ref, w_tail_ref, stage_ref, dma_sem, h_ref, hn_ref, proj_ref, carry_ref, state_ref, sbf_ref,
                     glog_ref, hilo_ref, rev_ref, kdec_ref,
                     *, n_tiles, tiles_per_seq, conv_width, dk_total, gla_width):
    t_rows = x_ref.shape[0]
    in_cols, d_model = w_hbm.shape
    n_strips = t_rows // STRIP
    n_chunks = t_rows // CHUNK
    head_k = dk_total // GLA_HEADS
    head_v = gla_width // GLA_HEADS
    off_h, off_b, off_c, off_z = (i * conv_width for i in range(4))
    off_q = 4 * conv_width
    off_k = off_q + dk_total
    off_v = off_k + dk_total
    off_r = off_v + gla_width
    off_gd = off_r + gla_width
    s = pl.program_id(0)

    def strip_rows(i):
        return slice(i * STRIP, (i + 1) * STRIP)

    def conv_strip(rows):
        token = None
        for c0 in range(0, conv_width, COL_CHUNK):
            cols = slice(c0, c0 + COL_CHUNK)

            def col(off):
                return proj_ref[rows, off + c0:off + c0 + COL_CHUNK].astype(F32)

            u = col(off_c) * col(off_h)
            ext = jnp.concatenate([carry_ref[:, cols], u], axis=0)
            u1 = pltpu.roll(ext, 1, 0)[8:, :]
            u2 = pltpu.roll(ext, 2, 0)[8:, :]
            carry_ref[:, cols] = u[STRIP - 8:, :]
            conv = (convb_ref[:, cols] + convw_ref[0:1, cols] * u2 + convw_ref[1:2, cols] * u1
                    + convw_ref[2:3, cols] * u)
            y = (col(off_b) * conv * _silu(col(off_z))).astype(yc_ref.dtype)
            yc_ref[rows, cols] = y
            token = token_of(y) if token is None else token | token_of(y)
        return token

    def gate_pre():
        glog_ref[...] = jnp.dot(proj_ref[:, off_gd:off_gd + LANES], wup_ref[...],
                                preferred_element_type=F32)

    def gate_strip(rows):
        glog = _log_sigmoid(glog_ref[rows, :] + bgate_ref[...]) / GLA_TAU
        g_hi = glog.astype(BF16)
        hilo_ref[rows, :dk_total] = g_hi
        hilo_ref[rows, dk_total:] = (glog - g_hi.astype(F32)).astype(BF16)

    def later_sums():
        r_i = lax.broadcasted_iota(jnp.int32, (t_rows + 8, t_rows), 0)
        c_i = lax.broadcasted_iota(jnp.int32, (t_rows + 8, t_rows), 1)
        c_chunk = c_i // CHUNK
        ones = ((c_i > r_i) & (c_chunk == r_i // CHUNK)) | (c_chunk == r_i - t_rows)
        sums = jnp.dot(jnp.where(ones, 1.0, 0.0).astype(BF16), hilo_ref[...],
                       preferred_element_type=F32)
        rev_ref[...] = sums[:, :dk_total] + sums[:, dk_total:]

    def kdec_strip(rows):
        k = proj_ref[rows, off_k:off_k + dk_total].astype(F32)
        kdec_ref[rows, :] = (k * jnp.exp(rev_ref[rows, :])).astype(BF16)

    def state_update(step):
        c, h = divmod(step, GLA_HEADS)
        rows = slice(c * CHUNK, (c + 1) * CHUNK)
        ks = slice(h * head_k, (h + 1) * head_k)
        decay = jnp.exp(rev_ref[t_rows + c:t_rows + c + 1, ks])
        v = proj_ref[rows, off_v + h * head_v:off_v + (h + 1) * head_v]
        inc = lax.dot_general(v, kdec_ref[rows, ks], (((0,), (0,)), ((), ())),
                              preferred_element_type=F32)
        s_new = state_ref[h] * decay + inc
        state_ref[h] = s_new
        sbf_ref[h] = s_new.astype(BF16)

    def readout(step):
        c, h = divmod(step, GLA_HEADS)
        rows = slice(c * CHUNK, (c + 1) * CHUNK)
        q = proj_ref[rows, off_q + h * head_k:off_q + (h + 1) * head_k]
        o = lax.dot_general(q, sbf_ref[h], (((1,), (1,)), ((), ())),
                            preferred_element_type=F32) * (head_k ** -0.5)
        ms = jnp.mean(o * o, axis=-1, keepdims=True)
        o = o * lax.rsqrt(ms + EPS) * gng_ref[h:h + 1, :]
        r = proj_ref[rows, off_r + h * head_v:off_r + (h + 1) * head_v].astype(F32)
        yg_ref[rows, h * head_v:(h + 1) * head_v] = (o * _silu(r)).astype(yg_ref.dtype)

    def token_of(y):
        bits = pltpu.bitcast(y, jnp.uint32)
        token = bits[:, :LANES]
        for l0 in range(LANES, bits.shape[1], LANES):
            token = token | bits[:, l0:l0 + LANES]
        return token

    def norm_strip(src, dst, rows):
        x = src[rows, :]
        ms = jnp.mean(x * x, axis=-1, keepdims=True)
        h = (x * lax.rsqrt(ms + EPS) * ng_ref[...]).astype(BF16)
        dst[rows, :] = h
        return token_of(h)

    def pin(tokens):
        acc = tokens[0]
        for t in tokens[1:]:
            acc = acc | t
        zero = pltpu.bitcast((acc >> 16) >> 16, F32)
        zero = jnp.concatenate([zero] * (STRIP // zero.shape[0]), axis=0)
        tile = h_ref[0:STRIP, 0:LANES].astype(F32)
        h_ref[0:STRIP, 0:LANES] = (tile + zero).astype(BF16)

    def proj_piece(c0, width):
        def piece():
            if c0 < w_ref.shape[0] * W_ROWS:
                w = w_ref[c0 // W_ROWS, :, c0 % W_ROWS:c0 % W_ROWS + width]
            else:
                w = w_tail_ref[...]
            proj_ref[:, c0:c0 + width] = jnp.dot(h_ref[...], w,
                                                 preferred_element_type=F32).astype(BF16)
        return piece

    def pieces(off, width):
        return [proj_piece(c0, min(MXU_N, off + width - c0)) for c0 in range(off, off + width, MXU_N)]

    @pl.when(s == 0)
    def _():
        proj_ref[:, :4 * conv_width] = jnp.zeros((t_rows, 4 * conv_width), BF16)

        n_full = in_cols // W_ROWS

        def chunk_copy(i, rows=W_ROWS):
            slot = i % W_SLOTS
            return pltpu.make_async_copy(w_hbm.at[pl.ds(i * W_ROWS, rows), :],
                                         stage_ref.at[slot, pl.ds(0, rows), :], dma_sem.at[slot])

        def transpose_chunk(slot, rows, dst):
            for k0 in range(0, d_model, MXU_N):
                blk = stage_ref[slot, 0:rows, k0:k0 + MXU_N]
                dst[k0:k0 + MXU_N, :] = blk.T.astype(BF16)

        for i in range(W_SLOTS - 1):
            chunk_copy(i).start()

        def load_chunk(i, c):
            @pl.when(i + W_SLOTS - 1 < n_full)
            def _():
                chunk_copy(i + W_SLOTS - 1).start()
            chunk_copy(i).wait()
            transpose_chunk(i % W_SLOTS, W_ROWS, w_ref.at[i])
            return c

        lax.fori_loop(0, n_full, load_chunk, 0)

        tail_rows = in_cols - n_full * W_ROWS
        stage_ref[n_full % W_SLOTS, tail_rows:LANES, :] = jnp.zeros((LANES - tail_rows, d_model), F32)
        chunk_copy(n_full, tail_rows).start()
        chunk_copy(n_full, tail_rows).wait()
        transpose_chunk(n_full % W_SLOTS, LANES, w_tail_ref)

        x0_copy = pltpu.make_async_copy(x_hbm.at[pl.ds(0, t_rows), :],
                                        stage_ref.at[0, pl.ds(0, t_rows), :], dma_sem.at[0])
        x0_copy.start()
        x0_copy.wait()

        def norm_first(i, c):
            norm_strip(stage_ref.at[0], hn_ref, pl.ds(pl.multiple_of(i * STRIP, STRIP), STRIP))
            return c

        lax.fori_loop(0, n_strips, norm_first, 0, unroll=2)

    @pl.when((s == 0) | ((s + tiles_per_seq - 1) % tiles_per_seq == 0))
    def _():
        carry_ref[...] = jnp.zeros_like(carry_ref)

    @pl.when(s < n_tiles)
    def _():
        @pl.when(s % tiles_per_seq == 0)
        def _():
            state_ref[...] = jnp.zeros_like(state_ref)

        wo_bf_ref[...] = wo_ref[...].astype(BF16)

        h_ref[...] = hn_ref[...]

        def unpinned(unit):
            def run():
                unit()
            return run

        norm_units = [unpinned(functools.partial(norm_strip, x_ref, hn_ref, strip_rows(i)))
                      for i in range(n_strips)]
        conv_units = [functools.partial(conv_strip, strip_rows(i)) for i in range(n_strips)]
        gate_units = [functools.partial(gate_strip, strip_rows(i)) for i in range(n_strips)]
        kdec_units = [functools.partial(kdec_strip, strip_rows(i)) for i in range(n_strips)]
        n_steps = n_chunks * GLA_HEADS

        mxu = (pieces(off_gd, LANES) + pieces(off_k, dk_total) + pieces(off_q, dk_total)
               + pieces(off_v, gla_width) + pieces(off_r, gla_width))
        n_k = dk_total // MXU_N
        vpu = [conv_units[:2]]
        vpu += _spread([gate_pre] + gate_units, n_k)
        vpu += [[later_sums, conv_units[2]]] + [kdec_units + [conv_units[3]]]
        vpu += [[] for _ in range(dk_total // MXU_N - 2)]
        rest = _spread(conv_units[4:], len(mxu) - len(vpu))
        rest[-1].append(functools.partial(state_update, 0))
        vpu += rest
        assert len(vpu) == len(mxu)

        mxu2 = pieces(off_h, 4 * conv_width)
        halves = []
        for step in range(n_steps):
            if step + 1 < n_steps:
                halves.append(functools.partial(state_update, step + 1))
            halves.append(functools.partial(readout, step))
        busy = len(mxu2) - IDLE_TAIL_PIECES
        vpu2 = [a + b for a, b in zip(_spread(halves, busy), _spread(norm_units, busy))]
        vpu2 += [[] for _ in range(IDLE_TAIL_PIECES)]

        tokens = []
        for piece, units in zip(mxu + mxu2, vpu + vpu2):
            if len(tokens) >= PIN_LAG and tokens[-PIN_LAG]:
                pin(tokens[-PIN_LAG])
            piece()
            tokens.append([t for t in [unit() for unit in units] if t is not None])

    @pl.when(s == n_tiles)
    def _():
        def step(i, c):
            conv_strip(pl.ds(pl.multiple_of(i * STRIP, STRIP), STRIP))
            return c
        lax.fori_loop(0, n_strips, step, 0, unroll=2)


def _proj_mix(x2d, norm_g, w_in_t, conv_w, conv_b, w_up_pad, b_gate, gla_norm_g, w_out, *,
              batch, seq, conv_width, dk_total, gla_width, n_cols):
    t = MIX_T
    nt = seq // t
    n_tiles = batch * nt
    d_model = x2d.shape[1]
    wo_rows = w_out.shape[0] // n_tiles
    assert w_out.shape[0] % n_tiles == 0 and wo_rows % STRIP == 0 and t <= W_ROWS
    main_cols = n_cols - LANES
    assert main_cols % W_ROWS == 0 and W_ROWS % MXU_N == 0 and 0 < w_in_t.shape[0] - main_cols <= LANES
    head_k = dk_total // GLA_HEADS
    head_v = gla_width // GLA_HEADS

    def whole(shape, **kw):
        return pl.BlockSpec(shape, lambda s: (0,) * len(shape), **kw)

    kern = functools.partial(_proj_mix_kernel, n_tiles=n_tiles, tiles_per_seq=nt,
                             conv_width=conv_width, dk_total=dk_total, gla_width=gla_width)
    return pl.pallas_call(
        kern,
        grid=(n_tiles + 1,),
        in_specs=[
            pl.BlockSpec((t, d_model), lambda s: (jnp.minimum(s + 1, n_tiles - 1), 0)),
            pl.BlockSpec(memory_space=pl.ANY),
            whole(norm_g.shape),
            pl.BlockSpec(memory_space=pl.ANY),
            whole(conv_w.shape), whole(conv_b.shape), whole(w_up_pad.shape), whole(b_gate.shape),
            whole(gla_norm_g.shape),
            pl.BlockSpec((wo_rows, w_out.shape[1]), lambda s: (jnp.minimum(s, n_tiles - 1), 0)),
        ],
        out_specs=[
            pl.BlockSpec((t, conv_width), lambda s: (jnp.maximum(s - 1, 0), 0)),
            pl.BlockSpec((t, gla_width), lambda s: (jnp.minimum(s, n_tiles - 1), 0)),
            pl.BlockSpec((wo_rows, w_out.shape[1]), lambda s: (jnp.minimum(s, n_tiles - 1), 0)),
        ],
        out_shape=[jax.ShapeDtypeStruct((batch * seq, conv_width), BF16),
                   jax.ShapeDtypeStruct((batch * seq, gla_width), BF16),
                   jax.ShapeDtypeStruct(w_out.shape, BF16)],
        scratch_shapes=[pltpu.VMEM((main_cols // W_ROWS, d_model, W_ROWS), BF16),
                        pltpu.VMEM((d_model, LANES), BF16),
                        pltpu.VMEM((W_SLOTS, W_ROWS, d_model), F32),
                        pltpu.SemaphoreType.DMA((W_SLOTS,)),
                        pltpu.VMEM((t, d_model), BF16),
                        pltpu.VMEM((t, d_model), BF16),
                        pltpu.VMEM((t, n_cols), BF16),
                        pltpu.VMEM((8, conv_width), F32),
                        pltpu.VMEM((GLA_HEADS, head_v, head_k), F32),
                        pltpu.VMEM((GLA_HEADS, head_v, head_k), BF16),
                        pltpu.VMEM((t, dk_total), F32),
                        pltpu.VMEM((t, 2 * dk_total), BF16),
                        pltpu.VMEM((t + 8, dk_total), F32),
                        pltpu.VMEM((t, dk_total), BF16)],
        compiler_params=pltpu.CompilerParams(
            dimension_semantics=("arbitrary",),
            vmem_limit_bytes=VMEM_LIMIT),
        name="proj_mix",
    )(x2d, x2d, norm_g, w_in_t, conv_w, conv_b, w_up_pad, b_gate, gla_norm_g, w_out)


def _out_proj_kernel(yc_ref, yg_ref, w_ref, x_ref, g_ref, o_ref, *, final_norm):
    kc = yc_ref.shape[1]
    z = (x_ref[...] + jnp.dot(yc_ref[...], w_ref[:kc, :], preferred_element_type=F32)
         + jnp.dot(yg_ref[...], w_ref[kc:, :], preferred_element_type=F32))
    if final_norm:
        ms = jnp.mean(z * z, axis=-1, keepdims=True)
        z = z * lax.rsqrt(ms + EPS) * g_ref[...]
    o_ref[...] = z


def _out_proj(y_conv, y_gla, w_out_bf16, x2d, final_g, *, final_norm):
    m, d = x2d.shape
    return pl.pallas_call(
        functools.partial(_out_proj_kernel, final_norm=final_norm),
        grid=(m // OUT_TM,),
        in_specs=[
            pl.BlockSpec((OUT_TM, y_conv.shape[1]), lambda i: (i, 0)),
            pl.BlockSpec((OUT_TM, y_gla.shape[1]), lambda i: (i, 0)),
            pl.BlockSpec(w_out_bf16.shape, lambda i: (0, 0)),
            pl.BlockSpec((OUT_TM, d), lambda i: (i, 0)),
            pl.BlockSpec((1, d), lambda i: (0, 0)),
        ],
        out_specs=pl.BlockSpec((OUT_TM, d), lambda i: (i, 0)),
        out_shape=jax.ShapeDtypeStruct((m, d), F32),
        compiler_params=pltpu.CompilerParams(
            dimension_semantics=("arbitrary",),
            vmem_limit_bytes=VMEM_LIMIT),
        name="out_proj",
    )(y_conv, y_gla, w_out_bf16, x2d, final_g)


def kernel(x, norm_g, w_in, conv_w, conv_b, gla_w_up, gla_b_gate, gla_norm_g, w_out, final_g):
    batch, seq, d_model = x.shape
    depth = norm_g.shape[0]
    conv_width = conv_w.shape[2]
    rank, dk_total = gla_w_up.shape[1], gla_w_up.shape[2]
    gla_width = gla_norm_g.shape[1] * gla_norm_g.shape[2]
    in_cols = w_in.shape[2]
    main_cols = in_cols - rank
    assert main_cols == 4 * conv_width + 2 * dk_total + 2 * gla_width
    assert main_cols % MXU_N == 0 and dk_total % MXU_N == 0 and rank <= LANES
    assert seq % MIX_T == 0 and MIX_T % CHUNK == 0 and (batch * seq) % OUT_TM == 0
    n_pad = main_cols + LANES

    x2d = x.reshape(batch * seq, d_model)
    for l in range(depth):
        w_up_p = jnp.pad(gla_w_up[l], ((0, LANES - rank), (0, 0))).astype(BF16)
        y_conv, y_gla, w_out_bf16 = _proj_mix(
            x2d, norm_g[l][None, :], w_in[l].T, conv_w[l], conv_b[l][None, :], w_up_p,
            gla_b_gate[l][None, :], gla_norm_g[l], w_out[l], batch=batch, seq=seq,
            conv_width=conv_width, dk_total=dk_total, gla_width=gla_width, n_cols=n_pad)
        x2d = _out_proj(y_conv, y_gla, w_out_bf16, x2d, final_g[None, :],
                        final_norm=(l == depth - 1))
    return x2d.reshape(batch, seq, d_model)
```

```python
import functools

import jax
import jax.numpy as jnp
from jax import lax
from jax.experimental import pallas as pl
from jax.experimental.pallas import tpu as pltpu

F32 = jnp.float32
BF16 = jnp.bfloat16

LANES = 128
MXU_N = 256
EPS = 1e-6
CHUNK = 64
GLA_HEADS = 4
GLA_TAU = 16.0

MIX_T = 256
STRIP = 16
COL_CHUNK = 256
W_ROWS = 256
W_SLOTS = 4
IDLE_TAIL_PIECES = 2
PIN_LAG = 2
OUT_TM = 512
VMEM_LIMIT = 56 * 1024 * 1024


def _silu(v):
    return v * jax.nn.sigmoid(v)


def _log_sigmoid(v):
    return -(jnp.maximum(-v, 0.0) + jnp.log(1.0 + jnp.exp(-jnp.abs(v))))


def _spread(units, n_slots):
    out = [[] for _ in range(n_slots)]
    for j, u in enumerate(units):
        out[(j * n_slots) // len(units)].append(u)
    return out


def _proj_mix_kernel(x_ref, x_hbm, ng_ref, w_hbm, convw_ref, convb_ref, wup_ref, bgate_ref, gng_ref, wo_ref,
                     yc_ref, yg_ref, wo_bf_ref,
                     w_ref, w_tail_ref, stage_ref, dma_sem, h_ref, hn_ref, proj_ref, carry_ref, state_ref, sbf_ref,
                     glog_ref, hilo_ref, rev_ref, kdec_ref,
                     *, n_tiles, tiles_per_seq, conv_width, dk_total, gla_width):
    t_rows = x_ref.shape[0]
    in_cols, d_model = w_hbm.shape
    n_strips = t_rows // STRIP
    n_chunks = t_rows // CHUNK
    head_k = dk_total // GLA_HEADS
    head_v = gla_width // GLA_HEADS
    off_h, off_b, off_c, off_z = (i * conv_width for i in range(4))
    off_q = 4 * conv_width
    off_k = off_q + dk_total
    off_v = off_k + dk_total
    off_r = off_v + gla_width
    off_gd = off_r + gla_width
    s = pl.program_id(0)

    def strip_rows(i):
        return slice(i * STRIP, (i + 1) * STRIP)

    def conv_strip(rows):
        token = None
        for c0 in range(0, conv_width, COL_CHUNK):
            cols = slice(c0, c0 + COL_CHUNK)

            def col(off):
                return proj_ref[rows, off + c0:off + c0 + COL_CHUNK].astype(F32)

            u = col(off_c) * col(off_h)
            ext = jnp.concatenate([carry_ref[:, cols], u], axis=0)
            u1 = pltpu.roll(ext, 1, 0)[8:, :]
            u2 = pltpu.roll(ext, 2, 0)[8:, :]
            carry_ref[:, cols] = u[STRIP - 8:, :]
            conv = (convb_ref[:, cols] + convw_ref[0:1, cols] * u2 + convw_ref[1:2, cols] * u1
                    + convw_ref[2:3, cols] * u)
            y = (col(off_b) * conv * _silu(col(off_z))).astype(yc_ref.dtype)
            yc_ref[rows, cols] = y
            token = token_of(y) if token is None else token | token_of(y)
        return token

    def gate_pre():
        glog_ref[...] = jnp.dot(proj_ref[:, off_gd:off_gd + LANES], wup_ref[...],
                                preferred_element_type=F32)

    def gate_strip(rows):
        glog = _log_sigmoid(glog_ref[rows, :] + bgate_ref[...]) / GLA_TAU
        g_hi = glog.astype(BF16)
        hilo_ref[rows, :dk_total] = g_hi
        hilo_ref[rows, dk_total:] = (glog - g_hi.astype(F32)).astype(BF16)

    def later_sums():
        r_i = lax.broadcasted_iota(jnp.int32, (t_rows + 8, t_rows), 0)
        c_i = lax.broadcasted_iota(jnp.int32, (t_rows + 8, t_rows), 1)
        c_chunk = c_i // CHUNK
        ones = ((c_i > r_i) & (c_chunk == r_i // CHUNK)) | (c_chunk == r_i - t_rows)
        sums = jnp.dot(jnp.where(ones, 1.0, 0.0).astype(BF16), hilo_ref[...],
                       preferred_element_type=F32)
        rev_ref[...] = sums[:, :dk_total] + sums[:, dk_total:]

    def kdec_strip(rows):
        k = proj_ref[rows, off_k:off_k + dk_total].astype(F32)
        kdec_ref[rows, :] = (k * jnp.exp(rev_ref[rows, :])).astype(BF16)

    def state_update(step):
        c, h = divmod(step, GLA_HEADS)
        rows = slice(c * CHUNK, (c + 1) * CHUNK)
        ks = slice(h * head_k, (h + 1) * head_k)
        decay = jnp.exp(rev_ref[t_rows + c:t_rows + c + 1, ks])
        v = proj_ref[rows, off_v + h * head_v:off_v + (h + 1) * head_v]
        inc = lax.dot_general(v, kdec_ref[rows, ks], (((0,), (0,)), ((), ())),
                              preferred_element_type=F32)
        s_new = state_ref[h] * decay + inc
        state_ref[h] = s_new
        sbf_ref[h] = s_new.astype(BF16)

    def readout(step):
        c, h = divmod(step, GLA_HEADS)
        rows = slice(c * CHUNK, (c + 1) * CHUNK)
        q = proj_ref[rows, off_q + h * head_k:off_q + (h + 1) * head_k]
        o = lax.dot_general(q, sbf_ref[h], (((1,), (1,)), ((), ())),
                            preferred_element_type=F32) * (head_k ** -0.5)
        ms = jnp.mean(o * o, axis=-1, keepdims=True)
        o = o * lax.rsqrt(ms + EPS) * gng_ref[h:h + 1, :]
        r = proj_ref[rows, off_r + h * head_v:off_r + (h + 1) * head_v].astype(F32)
        yg_ref[rows, h * head_v:(h + 1) * head_v] = (o * _silu(r)).astype(yg_ref.dtype)

    def token_of(y):
        bits = pltpu.bitcast(y, jnp.uint32)
        token = bits[:, :LANES]
        for l0 in range(LANES, bits.shape[1], LANES):
            token = token | bits[:, l0:l0 + LANES]
        return token

    def norm_strip(src, dst, rows):
        x = src[rows, :]
        ms = jnp.mean(x * x, axis=-1, keepdims=True)
        h = (x * lax.rsqrt(ms + EPS) * ng_ref[...]).astype(BF16)
        dst[rows, :] = h
        return token_of(h)

    def pin(tokens):
        acc = tokens[0]
        for t in tokens[1:]:
            acc = acc | t
        zero = pltpu.bitcast((acc >> 16) >> 16, F32)
        zero = jnp.concatenate([zero] * (STRIP // zero.shape[0]), axis=0)
        tile = h_ref[0:STRIP, 0:LANES].astype(F32)
        h_ref[0:STRIP, 0:LANES] = (tile + zero).astype(BF16)

    def proj_piece(c0, width):
        def piece():
            if c0 < w_ref.shape[0] * W_ROWS:
                w = w_ref[c0 // W_ROWS, :, c0 % W_ROWS:c0 % W_ROWS + width]
            else:
                w = w_tail_ref[...]
            proj_ref[:, c0:c0 + width] = jnp.dot(h_ref[...], w,
                                                 preferred_element_type=F32).astype(BF16)
        return piece

    def pieces(off, width):
        return [proj_piece(c0, min(MXU_N, off + width - c0)) for c0 in range(off, off + width, MXU_N)]

    @pl.when(s == 0)
    def _():
        proj_ref[:, :4 * conv_width] = jnp.zeros((t_rows, 4 * conv_width), BF16)

        n_full = in_cols // W_ROWS

        def chunk_copy(i, rows=W_ROWS):
            slot = i % W_SLOTS
            return pltpu.make_async_copy(w_hbm.at[pl.ds(i * W_ROWS, rows), :],
                                         stage_ref.at[slot, pl.ds(0, rows), :], dma_sem.at[slot])

        def transpose_chunk(slot, rows, dst):
            for k0 in range(0, d_model, MXU_N):
                blk = stage_ref[slot, 0:rows, k0:k0 + MXU_N]
                dst[k0:k0 + MXU_N, :] = blk.T.astype(BF16)

        for i in range(W_SLOTS - 1):
            chunk_copy(i).start()

        def load_chunk(i, c):
            @pl.when(i + W_SLOTS - 1 < n_full)
            def _():
                chunk_copy(i + W_SLOTS - 1).start()
            chunk_copy(i).wait()
            transpose_chunk(i % W_SLOTS, W_ROWS, w_ref.at[i])
            return c

        lax.fori_loop(0, n_full, load_chunk, 0)

        tail_rows = in_cols - n_full * W_ROWS
        stage_ref[n_full % W_SLOTS, tail_rows:LANES, :] = jnp.zeros((LANES - tail_rows, d_model), F32)
        chunk_copy(n_full, tail_rows).start()
        chunk_copy(n_full, tail_rows).wait()
        transpose_chunk(n_full % W_SLOTS, LANES, w_tail_ref)

        x0_copy = pltpu.make_async_copy(x_hbm.at[pl.ds(0, t_rows), :],
                                        stage_ref.at[0, pl.ds(0, t_rows), :], dma_sem.at[0])
        x0_copy.start()
        x0_copy.wait()

        def norm_first(i, c):
            norm_strip(stage_ref.at[0], hn_ref, pl.ds(pl.multiple_of(i * STRIP, STRIP), STRIP))
            return c

        lax.fori_loop(0, n_strips, norm_first, 0, unroll=2)

    @pl.when((s == 0) | ((s + tiles_per_seq - 1) % tiles_per_seq == 0))
    def _():
        carry_ref[...] = jnp.zeros_like(carry_ref)

    @pl.when(s < n_tiles)
    def _():
        @pl.when(s % tiles_per_seq == 0)
        def _():
            state_ref[...] = jnp.zeros_like(state_ref)

        wo_bf_ref[...] = wo_ref[...].astype(BF16)

        h_ref[...] = hn_ref[...]

        norm_units = [functools.partial(norm_strip, x_ref, hn_ref, strip_rows(i))
                      for i in range(n_strips)]
        conv_units = [functools.partial(conv_strip, strip_rows(i)) for i in range(n_strips)]
        gate_units = [functools.partial(gate_strip, strip_rows(i)) for i in range(n_strips)]
        kdec_units = [functools.partial(kdec_strip, strip_rows(i)) for i in range(n_strips)]
        n_steps = n_chunks * GLA_HEADS

        mxu = (pieces(off_gd, LANES) + pieces(off_k, dk_total) + pieces(off_q, dk_total)
               + pieces(off_v, gla_width) + pieces(off_r, gla_width))
        n_k = dk_total // MXU_N
        vpu = [conv_units[:2]]
        vpu += _spread([gate_pre] + gate_units, n_k)
        vpu += [[later_sums, conv_units[2]]] + [kdec_units + [conv_units[3]]]
        vpu += [[] for _ in range(dk_total // MXU_N - 2)]
        rest = _spread(conv_units[4:], len(mxu) - len(vpu))
        rest[-1].append(functools.partial(state_update, 0))
        vpu += rest
        assert len(vpu) == len(mxu)

        mxu2 = pieces(off_h, 4 * conv_width)
        halves = []
        for step in range(n_steps):
            if step + 1 < n_steps:
                halves.append(functools.partial(state_update, step + 1))
            halves.append(functools.partial(readout, step))
        busy = len(mxu2) - IDLE_TAIL_PIECES
        vpu2 = [a + b for a, b in zip(_spread(halves, busy), _spread(norm_units, busy))]
        vpu2 += [[] for _ in range(IDLE_TAIL_PIECES)]

        tokens = []
        for piece, units in zip(mxu + mxu2, vpu + vpu2):
            if len(tokens) >= PIN_LAG and tokens[-PIN_LAG]:
                pin(tokens[-PIN_LAG])
            piece()
            tokens.append([t for t in [unit() for unit in units] if t is not None])

    @pl.when(s == n_tiles)
    def _():
        def step(i, c):
            conv_strip(pl.ds(pl.multiple_of(i * STRIP, STRIP), STRIP))
            return c
        lax.fori_loop(0, n_strips, step, 0, unroll=2)


def _proj_mix(x2d, norm_g, w_in_t, conv_w, conv_b, w_up_pad, b_gate, gla_norm_g, w_out, *,
              batch, seq, conv_width, dk_total, gla_width, n_cols):
    t = MIX_T
    nt = seq // t
    n_tiles = batch * nt
    d_model = x2d.shape[1]
    wo_rows = w_out.shape[0] // n_tiles
    assert w_out.shape[0] % n_tiles == 0 and wo_rows % STRIP == 0 and t <= W_ROWS
    main_cols = n_cols - LANES
    assert main_cols % W_ROWS == 0 and W_ROWS % MXU_N == 0 and 0 < w_in_t.shape[0] - main_cols <= LANES
    head_k = dk_total // GLA_HEADS
    head_v = gla_width // GLA_HEADS

    def whole(shape, **kw):
        return pl.BlockSpec(shape, lambda s: (0,) * len(shape), **kw)

    kern = functools.partial(_proj_mix_kernel, n_tiles=n_tiles, tiles_per_seq=nt,
                             conv_width=conv_width, dk_total=dk_total, gla_width=gla_width)
    return pl.pallas_call(
        kern,
        grid=(n_tiles + 1,),
        in_specs=[
            pl.BlockSpec((t, d_model), lambda s: (jnp.minimum(s + 1, n_tiles - 1), 0)),
            pl.BlockSpec(memory_space=pl.ANY),
            whole(norm_g.shape),
            pl.BlockSpec(memory_space=pl.ANY),
            whole(conv_w.shape), whole(conv_b.shape), whole(w_up_pad.shape), whole(b_gate.shape),
            whole(gla_norm_g.shape),
            pl.BlockSpec((wo_rows, w_out.shape[1]), lambda s: (jnp.minimum(s, n_tiles - 1), 0)),
        ],
        out_specs=[
            pl.BlockSpec((t, conv_width), lambda s: (jnp.maximum(s - 1, 0), 0)),
            pl.BlockSpec((t, gla_width), lambda s: (jnp.minimum(s, n_tiles - 1), 0)),
            pl.BlockSpec((wo_rows, w_out.shape[1]), lambda s: (jnp.minimum(s, n_tiles - 1), 0)),
        ],
        out_shape=[jax.ShapeDtypeStruct((batch * seq, conv_width), BF16),
                   jax.ShapeDtypeStruct((batch * seq, gla_width), BF16),
                   jax.ShapeDtypeStruct(w_out.shape, BF16)],
        scratch_shapes=[pltpu.VMEM((main_cols // W_ROWS, d_model, W_ROWS), BF16),
                        pltpu.VMEM((d_model, LANES), BF16),
                        pltpu.VMEM((W_SLOTS, W_ROWS, d_model), F32),
                        pltpu.SemaphoreType.DMA((W_SLOTS,)),
                        pltpu.VMEM((t, d_model), BF16),
                        pltpu.VMEM((t, d_model), BF16),
                        pltpu.VMEM((t, n_cols), BF16),
                        pltpu.VMEM((8, conv_width), F32),
                        pltpu.VMEM((GLA_HEADS, head_v, head_k), F32),
                        pltpu.VMEM((GLA_HEADS, head_v, head_k), BF16),
                        pltpu.VMEM((t, dk_total), F32),
                        pltpu.VMEM((t, 2 * dk_total), BF16),
                        pltpu.VMEM((t + 8, dk_total), F32),
                        pltpu.VMEM((t, dk_total), BF16)],
        compiler_params=pltpu.CompilerParams(
            dimension_semantics=("arbitrary",),
            vmem_limit_bytes=VMEM_LIMIT),
        name="proj_mix",
    )(x2d, x2d, norm_g, w_in_t, conv_w, conv_b, w_up_pad, b_gate, gla_norm_g, w_out)


def _out_proj_kernel(yc_ref, yg_ref, w_ref, x_ref, g_ref, o_ref, *, final_norm):
    kc = yc_ref.shape[1]
    z = (x_ref[...] + jnp.dot(yc_ref[...], w_ref[:kc, :], preferred_element_type=F32)
         + jnp.dot(yg_ref[...], w_ref[kc:, :], preferred_element_type=F32))
    if final_norm:
        ms = jnp.mean(z * z, axis=-1, keepdims=True)
        z = z * lax.rsqrt(ms + EPS) * g_ref[...]
    o_ref[...] = z


def _out_proj(y_conv, y_gla, w_out_bf16, x2d, final_g, *, final_norm):
    m, d = x2d.shape
    return pl.pallas_call(
        functools.partial(_out_proj_kernel, final_norm=final_norm),
        grid=(m // OUT_TM,),
        in_specs=[
            pl.BlockSpec((OUT_TM, y_conv.shape[1]), lambda i: (i, 0)),
            pl.BlockSpec((OUT_TM, y_gla.shape[1]), lambda i: (i, 0)),
            pl.BlockSpec(w_out_bf16.shape, lambda i: (0, 0)),
            pl.BlockSpec((OUT_TM, d), lambda i: (i, 0)),
            pl.BlockSpec((1, d), lambda i: (0, 0)),
        ],
        out_specs=pl.BlockSpec((OUT_TM, d), lambda i: (i, 0)),
        out_shape=jax.ShapeDtypeStruct((m, d), F32),
        compiler_params=pltpu.CompilerParams(
            dimension_semantics=("arbitrary",),
            vmem_limit_bytes=VMEM_LIMIT),
        name="out_proj",
    )(y_conv, y_gla, w_out_bf16, x2d, final_g)


def kernel(x, norm_g, w_in, conv_w, conv_b, gla_w_up, gla_b_gate, gla_norm_g, w_out, final_g):
    batch, seq, d_model = x.shape
    depth = norm_g.shape[0]
    conv_width = conv_w.shape[2]
    rank, dk_total = gla_w_up.shape[1], gla_w_up.shape[2]
    gla_width = gla_norm_g.shape[1] * gla_norm_g.shape[2]
    in_cols = w_in.shape[2]
    main_cols = in_cols - rank
    assert main_cols == 4 * conv_width + 2 * dk_total + 2 * gla_width
    assert main_cols % MXU_N == 0 and dk_total % MXU_N == 0 and rank <= LANES
    assert seq % MIX_T == 0 and MIX_T % CHUNK == 0 and (batch * seq) % OUT_TM == 0
    n_pad = main_cols + LANES

    x2d = x.reshape(batch * seq, d_model)
    for l in range(depth):
        w_up_p = jnp.pad(gla_w_up[l], ((0, LANES - rank), (0, 0))).astype(BF16)
        y_conv, y_gla, w_out_bf16 = _proj_mix(
            x2d, norm_g[l][None, :], w_in[l].T, conv_w[l], conv_b[l][None, :], w_up_p,
            gla_b_gate[l][None, :], gla_norm_g[l], w_out[l], batch=batch, seq=seq,
            conv_width=conv_width, dk_total=dk_total, gla_width=gla_width, n_cols=n_pad)
        x2d = _out_proj(y_conv, y_gla, w_out_bf16, x2d, final_g[None, :],
                        final_norm=(l == depth - 1))
    return x2d.reshape(batch, seq, d_model)
```

```python
import functools

import jax
import jax.numpy as jnp
from jax import lax
from jax.experimental import pallas as pl
from jax.experimental.pallas import tpu as pltpu

F32 = jnp.float32
BF16 = jnp.bfloat16

LANES = 128
MXU_N = 256
EPS = 1e-6
CHUNK = 64
GLA_HEADS = 4
GLA_TAU = 16.0

MIX_T = 256
STRIP = 16
COL_CHUNK = 256
W_ROWS = 256
W_SLOTS = 2
IDLE_TAIL_PIECES = 2
PIN_LAG = 2
OUT_TM = 512
VMEM_LIMIT = 60 * 1024 * 1024


def _silu(v):
    return v * jax.nn.sigmoid(v)


def _log_sigmoid(v):
    return -(jnp.maximum(-v, 0.0) + jnp.log1p(jnp.exp(-jnp.abs(v))))


def _spread(units, n_slots):
    out = [[] for _ in range(n_slots)]
    for j, u in enumerate(units):
        out[(j * n_slots) // len(units)].append(u)
    return out


def _proj_mix_kernel(xa_ref, xb_ref, x_hbm, ng_ref, w_hbm, convw_ref, convb_ref, wup_ref, bgate_ref, gng_ref,
                     wo_ref, yc_ref, yg_ref, wo_bf_ref,
                     w_ref, w_tail_ref, stage_ref, dma_sem, ha_ref, hb_ref, proj_ref, ycs_ref, carry_ref, state_ref,
                     sbf_ref, glog_ref, hilo_ref, rev_ref, kdec_ref,
                     *, n_tiles, tiles_per_seq, conv_width, dk_total, gla_width):
    t_rows = xa_ref.shape[0]
    in_cols, d_model = w_hbm.shape
    n_strips = t_rows // STRIP
    n_chunks = t_rows // CHUNK
    head_k = dk_total // GLA_HEADS
    head_v = gla_width // GLA_HEADS
    off_h, off_b, off_c, off_z = (i * conv_width for i in range(4))
    off_q = 4 * conv_width
    off_k = off_q + dk_total
    off_v = off_k + dk_total
    off_r = off_v + gla_width
    off_gd = off_r + gla_width
    g = pl.program_id(0)
    n_steps = n_tiles // 2

    def strip_rows(i):
        return slice(i * STRIP, (i + 1) * STRIP)

    def conv_strip(dst, rows):
        token = None
        for c0 in range(0, conv_width, COL_CHUNK):
            cols = slice(c0, c0 + COL_CHUNK)

            def col(off):
                return proj_ref[rows, off + c0:off + c0 + COL_CHUNK].astype(F32)

            u = col(off_c) * col(off_h)
            ext = jnp.concatenate([carry_ref[:, cols], u], axis=0)
            u1 = pltpu.roll(ext, 1, 0)[8:, :]
            u2 = pltpu.roll(ext, 2, 0)[8:, :]
            carry_ref[:, cols] = u[STRIP - 8:, :]
            conv = (convb_ref[:, cols] + convw_ref[0:1, cols] * u2 + convw_ref[1:2, cols] * u1
                    + convw_ref[2:3, cols] * u)
            y = (col(off_b) * conv * _silu(col(off_z))).astype(dst.dtype)
            dst[rows, cols] = y
            token = token_of(y) if token is None else token | token_of(y)
        return token

    def gate_pre():
        glog_ref[...] = jnp.dot(proj_ref[:, off_gd:off_gd + LANES], wup_ref[...],
                                preferred_element_type=F32)

    def gate_strip(rows):
        glog = _log_sigmoid(glog_ref[rows, :] + bgate_ref[...]) / GLA_TAU
        g_hi = glog.astype(BF16)
        hilo_ref[rows, :dk_total] = g_hi
        hilo_ref[rows, dk_total:] = (glog - g_hi.astype(F32)).astype(BF16)

    def later_sums():
        r_i = lax.broadcasted_iota(jnp.int32, (t_rows + 8, t_rows), 0)
        c_i = lax.broadcasted_iota(jnp.int32, (t_rows + 8, t_rows), 1)
        c_chunk = c_i // CHUNK
        ones = ((c_i > r_i) & (c_chunk == r_i // CHUNK)) | (c_chunk == r_i - t_rows)
        sums = jnp.dot(jnp.where(ones, 1.0, 0.0).astype(BF16), hilo_ref[...],
                       preferred_element_type=F32)
        rev_ref[...] = sums[:, :dk_total] + sums[:, dk_total:]

    def kdec_strip(rows):
        k = proj_ref[rows, off_k:off_k + dk_total].astype(F32)
        kdec_ref[rows, :] = (k * jnp.exp(rev_ref[rows, :])).astype(BF16)

    def state_update(step):
        c, h = divmod(step, GLA_HEADS)
        rows = slice(c * CHUNK, (c + 1) * CHUNK)
        ks = slice(h * head_k, (h + 1) * head_k)
        decay = jnp.exp(rev_ref[t_rows + c:t_rows + c + 1, ks])
        v = proj_ref[rows, off_v + h * head_v:off_v + (h + 1) * head_v]
        inc = lax.dot_general(v, kdec_ref[rows, ks], (((0,), (0,)), ((), ())),
                              preferred_element_type=F32)
        s_new = state_ref[h] * decay + inc
        state_ref[h] = s_new
        sbf_ref[h] = s_new.astype(BF16)

    def readout(dst, step):
        c, h = divmod(step, GLA_HEADS)
        rows = slice(c * CHUNK, (c + 1) * CHUNK)
        q = proj_ref[rows, off_q + h * head_k:off_q + (h + 1) * head_k]
        o = lax.dot_general(q, sbf_ref[h], (((1,), (1,)), ((), ())),
                            preferred_element_type=F32) * (head_k ** -0.5)
        ms = jnp.mean(o * o, axis=-1, keepdims=True)
        o = o * lax.rsqrt(ms + EPS) * gng_ref[h:h + 1, :]
        r = proj_ref[rows, off_r + h * head_v:off_r + (h + 1) * head_v].astype(F32)
        dst[rows, h * head_v:(h + 1) * head_v] = (o * _silu(r)).astype(dst.dtype)

    def token_of(y):
        bits = pltpu.bitcast(y, jnp.uint32)
        token = bits[:, :LANES]
        for l0 in range(LANES, bits.shape[1], LANES):
            token = token | bits[:, l0:l0 + LANES]
        return token

    def norm_strip(src, dst, rows):
        x = src[rows, :]
        ms = jnp.mean(x * x, axis=-1, keepdims=True)
        h = (x * lax.rsqrt(ms + EPS) * ng_ref[...]).astype(BF16)
        dst[rows, :] = h
        return token_of(h)

    def pin(h_ref, tokens):
        acc = tokens[0]
        for t in tokens[1:]:
            acc = acc | t
        zero = pltpu.bitcast((acc >> 16) >> 16, F32)
        zero = jnp.concatenate([zero] * (STRIP // zero.shape[0]), axis=0)
        tile = h_ref[0:STRIP, 0:LANES].astype(F32)
        h_ref[0:STRIP, 0:LANES] = (tile + zero).astype(BF16)

    def proj_piece(h_ref, c0, width):
        def piece():
            if c0 < w_ref.shape[0] * W_ROWS:
                w = w_ref[c0 // W_ROWS, :, c0 % W_ROWS:c0 % W_ROWS + width]
            else:
                w = w_tail_ref[...]
            proj_ref[:, c0:c0 + width] = jnp.dot(h_ref[...], w,
                                                 preferred_element_type=F32).astype(BF16)
        return piece

    def pieces(h_ref, off, width):
        return [proj_piece(h_ref, c0, min(MXU_N, off + width - c0))
                for c0 in range(off, off + width, MXU_N)]

    @pl.when(g == 0)
    def _():
        proj_ref[:, :4 * conv_width] = jnp.zeros((t_rows, 4 * conv_width), BF16)
        ycs_ref[...] = jnp.zeros_like(ycs_ref)

        n_full = in_cols // W_ROWS

        def chunk_copy(i, rows=W_ROWS):
            slot = i % W_SLOTS
            return pltpu.make_async_copy(w_hbm.at[pl.ds(i * W_ROWS, rows), :],
                                         stage_ref.at[slot, pl.ds(0, rows), :], dma_sem.at[slot])

        def transpose_chunk(slot, rows, dst):
            for k0 in range(0, d_model, MXU_N):
                blk = stage_ref[slot, 0:rows, k0:k0 + MXU_N]
                dst[k0:k0 + MXU_N, :] = blk.T.astype(BF16)

        for i in range(W_SLOTS - 1):
            chunk_copy(i).start()

        def load_chunk(i, c):
            @pl.when(i + W_SLOTS - 1 < n_full)
            def _():
                chunk_copy(i + W_SLOTS - 1).start()
            chunk_copy(i).wait()
            transpose_chunk(i % W_SLOTS, W_ROWS, w_ref.at[i])
            return c

        lax.fori_loop(0, n_full, load_chunk, 0)

        tail_rows = in_cols - n_full * W_ROWS
        stage_ref[n_full % W_SLOTS, tail_rows:LANES, :] = jnp.zeros((LANES - tail_rows, d_model), F32)
        chunk_copy(n_full, tail_rows).start()
        chunk_copy(n_full, tail_rows).wait()
        transpose_chunk(n_full % W_SLOTS, LANES, w_tail_ref)

        x0_copy = pltpu.make_async_copy(x_hbm.at[pl.ds(0, t_rows), :],
                                        stage_ref.at[0, pl.ds(0, t_rows), :], dma_sem.at[0])
        x0_copy.start()
        x0_copy.wait()

        def norm_first(i, c):
            norm_strip(stage_ref.at[0], ha_ref, pl.ds(pl.multiple_of(i * STRIP, STRIP), STRIP))
            return c

        lax.fori_loop(0, n_strips, norm_first, 0, unroll=2)

    def half(idx):
        tile = 2 * g + idx
        h_ref, hn_ref, x_ref = (ha_ref, hb_ref, xa_ref) if idx == 0 else (hb_ref, ha_ref, xb_ref)
        conv_dst = yc_ref.at[pl.ds(t_rows, t_rows)] if idx == 0 else ycs_ref
        gla_dst = yg_ref.at[pl.ds(idx * t_rows, t_rows)]

        @pl.when((tile == 0) | ((tile + tiles_per_seq - 1) % tiles_per_seq == 0))
        def _():
            carry_ref[...] = jnp.zeros_like(carry_ref)

        @pl.when(tile % tiles_per_seq == 0)
        def _():
            state_ref[...] = jnp.zeros_like(state_ref)

        norm_units = [functools.partial(norm_strip, x_ref, hn_ref, strip_rows(i))
                      for i in range(n_strips)]
        conv_units = [functools.partial(conv_strip, conv_dst, strip_rows(i)) for i in range(n_strips)]
        gate_units = [functools.partial(gate_strip, strip_rows(i)) for i in range(n_strips)]
        kdec_units = [functools.partial(kdec_strip, strip_rows(i)) for i in range(n_strips)]
        n_rec = n_chunks * GLA_HEADS

        mxu = (pieces(h_ref, off_gd, LANES) + pieces(h_ref, off_k, dk_total)
               + pieces(h_ref, off_q, dk_total) + pieces(h_ref, off_v, gla_width)
               + pieces(h_ref, off_r, gla_width))
        n_k = dk_total // MXU_N
        vpu = [conv_units[:2]]
        vpu += _spread([gate_pre] + gate_units, n_k)
        vpu += [[later_sums, conv_units[2]]] + [kdec_units + [conv_units[3]]]
        vpu += [[] for _ in range(dk_total // MXU_N - 2)]
        rest = _spread(conv_units[4:], len(mxu) - len(vpu))
        rest[-1].append(functools.partial(state_update, 0))
        vpu += rest
        assert len(vpu) == len(mxu)

        mxu2 = pieces(h_ref, off_h, 4 * conv_width)
        halves = []
        for step in range(n_rec):
            if step + 1 < n_rec:
                halves.append(functools.partial(state_update, step + 1))
            halves.append(functools.partial(readout, gla_dst, step))
        busy = len(mxu2) - IDLE_TAIL_PIECES
        vpu2 = [a + b for a, b in zip(_spread(halves, busy), _spread(norm_units, busy))]
        vpu2 += [[] for _ in range(IDLE_TAIL_PIECES)]

        tokens = []
        for piece, units in zip(mxu + mxu2, vpu + vpu2):
            if len(tokens) >= PIN_LAG and tokens[-PIN_LAG]:
                pin(h_ref, tokens[-PIN_LAG])
            piece()
            tokens.append([t for t in [unit() for unit in units] if t is not None])

    @pl.when(g < n_steps)
    def _():
        wo_bf_ref[...] = wo_ref[...].astype(BF16)
        yc_ref[0:t_rows, :] = ycs_ref[...]
        half(0)
        half(1)

    @pl.when(g == n_steps)
    def _():
        yc_ref[0:t_rows, :] = ycs_ref[...]
        if (n_tiles - 1) % tiles_per_seq == 0:
            carry_ref[...] = jnp.zeros_like(carry_ref)

        def step(i, c):
            conv_strip(yc_ref.at[pl.ds(t_rows, t_rows)], pl.ds(pl.multiple_of(i * STRIP, STRIP), STRIP))
            return c
        lax.fori_loop(0, n_strips, step, 0, unroll=2)


def _proj_mix(x2d, norm_g, w_in_t, conv_w, conv_b, w_up_pad, b_gate, gla_norm_g, w_out, *,
              batch, seq, conv_width, dk_total, gla_width, n_cols):
    t = MIX_T
    nt = seq // t
    n_tiles = batch * nt
    d_model = x2d.shape[1]
    assert n_tiles % 2 == 0
    n_steps = n_tiles // 2
    wo_rows = w_out.shape[0] // n_steps
    assert w_out.shape[0] % n_steps == 0 and wo_rows % STRIP == 0 and t <= W_ROWS
    main_cols = n_cols - LANES
    assert main_cols % W_ROWS == 0 and W_ROWS % MXU_N == 0 and 0 < w_in_t.shape[0] - main_cols <= LANES
    head_k = dk_total // GLA_HEADS
    head_v = gla_width // GLA_HEADS

    def whole(shape, **kw):
        return pl.BlockSpec(shape, lambda s: (0,) * len(shape), **kw)

    kern = functools.partial(_proj_mix_kernel, n_tiles=n_tiles, tiles_per_seq=nt,
                             conv_width=conv_width, dk_total=dk_total, gla_width=gla_width)
    return pl.pallas_call(
        kern,
        grid=(n_steps + 1,),
        in_specs=[
            pl.BlockSpec((t, d_model), lambda s: (jnp.minimum(2 * s + 1, n_tiles - 1), 0)),
            pl.BlockSpec((t, d_model), lambda s: (jnp.minimum(2 * s + 2, n_tiles - 1), 0)),
            pl.BlockSpec(memory_space=pl.ANY),
            whole(norm_g.shape),
            pl.BlockSpec(memory_space=pl.ANY),
            whole(conv_w.shape), whole(conv_b.shape), whole(w_up_pad.shape), whole(b_gate.shape),
            whole(gla_norm_g.shape),
            pl.BlockSpec((wo_rows, w_out.shape[1]), lambda s: (jnp.minimum(s, n_steps - 1), 0)),
        ],
        out_specs=[
            pl.BlockSpec((2 * t, conv_width), lambda s: (jnp.maximum(s - 1, 0), 0)),
            pl.BlockSpec((2 * t, gla_width), lambda s: (jnp.minimum(s, n_steps - 1), 0)),
            pl.BlockSpec((wo_rows, w_out.shape[1]), lambda s: (jnp.minimum(s, n_steps - 1), 0)),
        ],
        out_shape=[jax.ShapeDtypeStruct((batch * seq, conv_width), BF16),
                   jax.ShapeDtypeStruct((batch * seq, gla_width), BF16),
                   jax.ShapeDtypeStruct(w_out.shape, BF16)],
        scratch_shapes=[pltpu.VMEM((main_cols // W_ROWS, d_model, W_ROWS), BF16),
                        pltpu.VMEM((d_model, LANES), BF16),
                        pltpu.VMEM((W_SLOTS, W_ROWS, d_model), F32),
                        pltpu.SemaphoreType.DMA((W_SLOTS,)),
                        pltpu.VMEM((t, d_model), BF16),
                        pltpu.VMEM((t, d_model), BF16),
                        pltpu.VMEM((t, n_cols), BF16),
                        pltpu.VMEM((t, conv_width), BF16),
                        pltpu.VMEM((8, conv_width), F32),
                        pltpu.VMEM((GLA_HEADS, head_v, head_k), F32),
                        pltpu.VMEM((GLA_HEADS, head_v, head_k), BF16),
                        pltpu.VMEM((t, dk_total), F32),
                        pltpu.VMEM((t, 2 * dk_total), BF16),
                        pltpu.VMEM((t + 8, dk_total), F32),
                        pltpu.VMEM((t, dk_total), BF16)],
        compiler_params=pltpu.CompilerParams(
            dimension_semantics=("arbitrary",),
            vmem_limit_bytes=VMEM_LIMIT),
        name="proj_mix",
    )(x2d, x2d, x2d, norm_g, w_in_t, conv_w, conv_b, w_up_pad, b_gate, gla_norm_g, w_out)


def _out_proj_kernel(yc_ref, yg_ref, w_ref, x_ref, g_ref, o_ref, *, final_norm):
    kc = yc_ref.shape[1]
    z = (x_ref[...] + jnp.dot(yc_ref[...], w_ref[:kc, :], preferred_element_type=F32)
         + jnp.dot(yg_ref[...], w_ref[kc:, :], preferred_element_type=F32))
    if final_norm:
        ms = jnp.mean(z * z, axis=-1, keepdims=True)
        z = z * lax.rsqrt(ms + EPS) * g_ref[...]
    o_ref[...] = z


def _out_proj(y_conv, y_gla, w_out_bf16, x2d, final_g, *, final_norm):
    m, d = x2d.shape
    return pl.pallas_call(
        functools.partial(_out_proj_kernel, final_norm=final_norm),
        grid=(m // OUT_TM,),
        in_specs=[
            pl.BlockSpec((OUT_TM, y_conv.shape[1]), lambda i: (i, 0)),
            pl.BlockSpec((OUT_TM, y_gla.shape[1]), lambda i: (i, 0)),
            pl.BlockSpec(w_out_bf16.shape, lambda i: (0, 0)),
            pl.BlockSpec((OUT_TM, d), lambda i: (i, 0)),
            pl.BlockSpec((1, d), lambda i: (0, 0)),
        ],
        out_specs=pl.BlockSpec((OUT_TM, d), lambda i: (i, 0)),
        out_shape=jax.ShapeDtypeStruct((m, d), F32),
        compiler_params=pltpu.CompilerParams(
            dimension_semantics=("arbitrary",),
            vmem_limit_bytes=VMEM_LIMIT),
        name="out_proj",
    )(y_conv, y_gla, w_out_bf16, x2d, final_g)


def kernel(x, norm_g, w_in, conv_w, conv_b, gla_w_up, gla_b_gate, gla_norm_g, w_out, final_g):
    batch, seq, d_model = x.shape
    depth = norm_g.shape[0]
    conv_width = conv_w.shape[2]
    rank, dk_total = gla_w_up.shape[1], gla_w_up.shape[2]
    gla_width = gla_norm_g.shape[1] * gla_norm_g.shape[2]
    in_cols = w_in.shape[2]
    main_cols = in_cols - rank
    assert main_cols == 4 * conv_width + 2 * dk_total + 2 * gla_width
    assert main_cols % MXU_N == 0 and dk_total % MXU_N == 0 and rank <= LANES
    assert seq % MIX_T == 0 and MIX_T % CHUNK == 0 and (batch * seq) % OUT_TM == 0
    n_pad = main_cols + LANES

    x2d = x.reshape(batch * seq, d_model)
    for l in range(depth):
        w_up_p = jnp.pad(gla_w_up[l], ((0, LANES - rank), (0, 0))).astype(BF16)
        y_conv, y_gla, w_out_bf16 = _proj_mix(
            x2d, norm_g[l][None, :], w_in[l].T, conv_w[l], conv_b[l][None, :], w_up_p,
            gla_b_gate[l][None, :], gla_norm_g[l], w_out[l], batch=batch, seq=seq,
            conv_width=conv_width, dk_total=dk_total, gla_width=gla_width, n_cols=n_pad)
        x2d = _out_proj(y_conv, y_gla, w_out_bf16, x2d, final_g[None, :],
                        final_norm=(l == depth - 1))
    return x2d.reshape(batch, seq, d_model)
```

```python
import functools

import jax
import jax.numpy as jnp
from jax import lax
from jax.experimental import pallas as pl
from jax.experimental.pallas import tpu as pltpu

F32 = jnp.float32
BF16 = jnp.bfloat16

LANES = 128
MXU_N = 256
EPS = 1e-6
CHUNK = 64
GLA_HEADS = 4
GLA_TAU = 16.0

MIX_T = 256
STRIP = 16
COL_CHUNK = 256
W_ROWS = 256
W_SLOTS = 4
IDLE_TAIL_PIECES = 2
PIN_LAG = 2
OUT_TM = 512
VMEM_LIMIT = 56 * 1024 * 1024


def _silu(v):
    return v * jax.nn.sigmoid(v)


def _log_sigmoid(v):
    return -(jnp.maximum(-v, 0.0) + jnp.log1p(jnp.exp(-jnp.abs(v))))


def _spread(units, n_slots):
    out = [[] for _ in range(n_slots)]
    for j, u in enumerate(units):
        out[(j * n_slots) // len(units)].append(u)
    return out


def _proj_mix_kernel(x_ref, x_hbm, ng_ref, w_hbm, convw_ref, convb_ref, wup_ref, bgate_ref, gng_ref, wo_ref,
                     yc_ref, yg_ref, wo_bf_ref,
                     w_ref, w_tail_ref, stage_ref, dma_sem, h_ref, hn_ref, proj_ref, carry_ref, state_ref, sbf_ref,
                     glog_ref, hilo_ref, rev_ref, kdec_ref,
                     *, n_tiles, tiles_per_seq, conv_width, dk_total, gla_width):
    t_rows = x_ref.shape[0]
    in_cols, d_model = w_hbm.shape
    n_strips = t_rows // STRIP
    n_chunks = t_rows // CHUNK
    head_k = dk_total // GLA_HEADS
    head_v = gla_width // GLA_HEADS
    off_h, off_b, off_c, off_z = (i * conv_width for i in range(4))
    off_q = 4 * conv_width
    off_k = off_q + dk_total
    off_v = off_k + dk_total
    off_r = off_v + gla_width
    off_gd = off_r + gla_width
    s = pl.program_id(0)

    def strip_rows(i):
        return slice(i * STRIP, (i + 1) * STRIP)

    def conv_strip(rows):
        token = None
        for c0 in range(0, conv_width, COL_CHUNK):
            cols = slice(c0, c0 + COL_CHUNK)

            def col(off):
                return proj_ref[rows, off + c0:off + c0 + COL_CHUNK].astype(F32)

            u = col(off_c) * col(off_h)
            ext = jnp.concatenate([carry_ref[:, cols], u], axis=0)
            u1 = pltpu.roll(ext, 1, 0)[8:, :]
            u2 = pltpu.roll(ext, 2, 0)[8:, :]
            carry_ref[:, cols] = u[STRIP - 8:, :]
            conv = (convb_ref[:, cols] + convw_ref[0:1, cols] * u2 + convw_ref[1:2, cols] * u1
                    + convw_ref[2:3, cols] * u)
            y = (col(off_b) * conv * _silu(col(off_z))).astype(yc_ref.dtype)
            yc_ref[rows, cols] = y
            token = token_of(y) if token is None else token | token_of(y)
        return token

    def gate_pre():
        glog_ref[...] = jnp.dot(proj_ref[:, off_gd:off_gd + LANES], wup_ref[...],
                                preferred_element_type=F32)

    def gate_strip(rows):
        glog = _log_sigmoid(glog_ref[rows, :] + bgate_ref[...]) / GLA_TAU
        g_hi = glog.astype(BF16)
        hilo_ref[rows, :dk_total] = g_hi
        hilo_ref[rows, dk_total:] = (glog - g_hi.astype(F32)).astype(BF16)

    def later_sums():
        r_i = lax.broadcasted_iota(jnp.int32, (t_rows + 8, t_rows), 0)
        c_i = lax.broadcasted_iota(jnp.int32, (t_rows + 8, t_rows), 1)
        c_chunk = c_i // CHUNK
        ones = ((c_i > r_i) & (c_chunk == r_i // CHUNK)) | (c_chunk == r_i - t_rows)
        sums = jnp.dot(jnp.where(ones, 1.0, 0.0).astype(BF16), hilo_ref[...],
                       preferred_element_type=F32)
        rev_ref[...] = sums[:, :dk_total] + sums[:, dk_total:]

    def kdec_strip(rows):
        k = proj_ref[rows, off_k:off_k + dk_total].astype(F32)
        kdec_ref[rows, :] = (k * jnp.exp(rev_ref[rows, :])).astype(BF16)

    def state_update(step):
        c, h = divmod(step, GLA_HEADS)
        rows = slice(c * CHUNK, (c + 1) * CHUNK)
        ks = slice(h * head_k, (h + 1) * head_k)
        decay = jnp.exp(rev_ref[t_rows + c:t_rows + c + 1, ks])
        v = proj_ref[rows, off_v + h * head_v:off_v + (h + 1) * head_v]
        inc = lax.dot_general(v, kdec_ref[rows, ks], (((0,), (0,)), ((), ())),
                              preferred_element_type=F32)
        s_new = state_ref[h] * decay + inc
        state_ref[h] = s_new
        sbf_ref[h] = s_new.astype(BF16)

    def readout(step):
        c, h = divmod(step, GLA_HEADS)
        rows = slice(c * CHUNK, (c + 1) * CHUNK)
        q = proj_ref[rows, off_q + h * head_k:off_q + (h + 1) * head_k]
        o = lax.dot_general(q, sbf_ref[h], (((1,), (1,)), ((), ())),
                            preferred_element_type=F32) * (head_k ** -0.5)
        ms = jnp.mean(o * o, axis=-1, keepdims=True)
        o = o * lax.rsqrt(ms + EPS) * gng_ref[h:h + 1, :]
        r = proj_ref[rows, off_r + h * head_v:off_r + (h + 1) * head_v].astype(F32)
        yg_ref[rows, h * head_v:(h + 1) * head_v] = (o * _silu(r)).astype(yg_ref.dtype)

    def token_of(y):
        bits = pltpu.bitcast(y, jnp.uint32)
        token = bits[:, :LANES]
        for l0 in range(LANES, bits.shape[1], LANES):
            token = token | bits[:, l0:l0 + LANES]
        return token

    def norm_strip(src, dst, rows):
        x = src[rows, :]
        ms = jnp.mean(x * x, axis=-1, keepdims=True)
        h = (x * lax.rsqrt(ms + EPS) * ng_ref[...]).astype(BF16)
        dst[rows, :] = h
        return token_of(h)

    def pin(tokens):
        acc = tokens[0]
        for t in tokens[1:]:
            acc = acc | t
        zero = pltpu.bitcast((acc >> 16) >> 16, F32)
        zero = jnp.concatenate([zero] * (STRIP // zero.shape[0]), axis=0)
        tile = h_ref[0:STRIP, 0:LANES].astype(F32)
        h_ref[0:STRIP, 0:LANES] = (tile + zero).astype(BF16)

    def proj_piece(c0, width):
        def piece():
            if c0 < w_ref.shape[0] * W_ROWS:
                w = w_ref[c0 // W_ROWS, :, c0 % W_ROWS:c0 % W_ROWS + width]
            else:
                w = w_tail_ref[...]
            proj_ref[:, c0:c0 + width] = jnp.dot(h_ref[...], w,
                                                 preferred_element_type=F32).astype(BF16)
        return piece

    def pieces(off, width):
        return [proj_piece(c0, min(MXU_N, off + width - c0)) for c0 in range(off, off + width, MXU_N)]

    @pl.when(s == 0)
    def _():
        proj_ref[:, :4 * conv_width] = jnp.zeros((t_rows, 4 * conv_width), BF16)

        n_full = in_cols // W_ROWS

        def chunk_copy(i, rows=W_ROWS):
            slot = i % W_SLOTS
            return pltpu.make_async_copy(w_hbm.at[pl.ds(i * W_ROWS, rows), :],
                                         stage_ref.at[slot, pl.ds(0, rows), :], dma_sem.at[slot])

        def transpose_chunk(slot, rows, dst):
            for k0 in range(0, d_model, MXU_N):
                blk = stage_ref[slot, 0:rows, k0:k0 + MXU_N]
                dst[k0:k0 + MXU_N, :] = blk.T.astype(BF16)

        for i in range(W_SLOTS - 1):
            chunk_copy(i).start(priority=i % 2)

        def load_pair(j, c):
            for p in range(2):
                i = 2 * j + p

                @pl.when(i + W_SLOTS - 1 < n_full)
                def _():
                    chunk_copy(i + W_SLOTS - 1).start(priority=(p + W_SLOTS - 1) % 2)
                chunk_copy(i).wait()
                transpose_chunk(i % W_SLOTS, W_ROWS, w_ref.at[i])
            return c

        assert n_full % 2 == 0
        lax.fori_loop(0, n_full // 2, load_pair, 0)

        tail_rows = in_cols - n_full * W_ROWS
        stage_ref[n_full % W_SLOTS, tail_rows:LANES, :] = jnp.zeros((LANES - tail_rows, d_model), F32)
        chunk_copy(n_full, tail_rows).start()
        chunk_copy(n_full, tail_rows).wait()
        transpose_chunk(n_full % W_SLOTS, LANES, w_tail_ref)

        x0_copy = pltpu.make_async_copy(x_hbm.at[pl.ds(0, t_rows), :],
                                        stage_ref.at[0, pl.ds(0, t_rows), :], dma_sem.at[0])
        x0_copy.start()
        x0_copy.wait()

        def norm_first(i, c):
            norm_strip(stage_ref.at[0], hn_ref, pl.ds(pl.multiple_of(i * STRIP, STRIP), STRIP))
            return c

        lax.fori_loop(0, n_strips, norm_first, 0, unroll=2)

    @pl.when((s == 0) | ((s + tiles_per_seq - 1) % tiles_per_seq == 0))
    def _():
        carry_ref[...] = jnp.zeros_like(carry_ref)

    @pl.when(s < n_tiles)
    def _():
        @pl.when(s % tiles_per_seq == 0)
        def _():
            state_ref[...] = jnp.zeros_like(state_ref)

        wo_bf_ref[...] = wo_ref[...].astype(BF16)

        h_ref[...] = hn_ref[...]

        norm_units = [functools.partial(norm_strip, x_ref, hn_ref, strip_rows(i))
                      for i in range(n_strips)]
        conv_units = [functools.partial(conv_strip, strip_rows(i)) for i in range(n_strips)]
        gate_units = [functools.partial(gate_strip, strip_rows(i)) for i in range(n_strips)]
        kdec_units = [functools.partial(kdec_strip, strip_rows(i)) for i in range(n_strips)]
        n_steps = n_chunks * GLA_HEADS

        mxu = (pieces(off_gd, LANES) + pieces(off_k, dk_total) + pieces(off_q, dk_total)
               + pieces(off_v, gla_width) + pieces(off_r, gla_width))
        n_k = dk_total // MXU_N
        vpu = [conv_units[:2]]
        vpu += _spread([gate_pre] + gate_units, n_k)
        vpu += [[later_sums, conv_units[2]]] + [kdec_units + [conv_units[3]]]
        vpu += [[] for _ in range(dk_total // MXU_N - 2)]
        rest = _spread(conv_units[4:], len(mxu) - len(vpu))
        rest[-1].append(functools.partial(state_update, 0))
        vpu += rest
        assert len(vpu) == len(mxu)

        mxu2 = pieces(off_h, 4 * conv_width)
        halves = []
        for step in range(n_steps):
            if step + 1 < n_steps:
                halves.append(functools.partial(state_update, step + 1))
            halves.append(functools.partial(readout, step))
        busy = len(mxu2) - IDLE_TAIL_PIECES
        vpu2 = [a + b for a, b in zip(_spread(halves, busy), _spread(norm_units, busy))]
        vpu2 += [[] for _ in range(IDLE_TAIL_PIECES)]

        tokens = []
        for piece, units in zip(mxu + mxu2, vpu + vpu2):
            if len(tokens) >= PIN_LAG and tokens[-PIN_LAG]:
                pin(tokens[-PIN_LAG])
            piece()
            tokens.append([t for t in [unit() for unit in units] if t is not None])

    @pl.when(s == n_tiles)
    def _():
        def step(i, c):
            conv_strip(pl.ds(pl.multiple_of(i * STRIP, STRIP), STRIP))
            return c
        lax.fori_loop(0, n_strips, step, 0, unroll=2)


def _proj_mix(x2d, norm_g, w_in_t, conv_w, conv_b, w_up_pad, b_gate, gla_norm_g, w_out, *,
              batch, seq, conv_width, dk_total, gla_width, n_cols):
    t = MIX_T
    nt = seq // t
    n_tiles = batch * nt
    d_model = x2d.shape[1]
    wo_rows = w_out.shape[0] // n_tiles
    assert w_out.shape[0] % n_tiles == 0 and wo_rows % STRIP == 0 and t <= W_ROWS
    main_cols = n_cols - LANES
    assert main_cols % W_ROWS == 0 and W_ROWS % MXU_N == 0 and 0 < w_in_t.shape[0] - main_cols <= LANES
    head_k = dk_total // GLA_HEADS
    head_v = gla_width // GLA_HEADS

    def whole(shape, **kw):
        return pl.BlockSpec(shape, lambda s: (0,) * len(shape), **kw)

    kern = functools.partial(_proj_mix_kernel, n_tiles=n_tiles, tiles_per_seq=nt,
                             conv_width=conv_width, dk_total=dk_total, gla_width=gla_width)
    return pl.pallas_call(
        kern,
        grid=(n_tiles + 1,),
        in_specs=[
            pl.BlockSpec((t, d_model), lambda s: (jnp.minimum(s + 1, n_tiles - 1), 0)),
            pl.BlockSpec(memory_space=pl.ANY),
            whole(norm_g.shape),
            pl.BlockSpec(memory_space=pl.ANY),
            whole(conv_w.shape), whole(conv_b.shape), whole(w_up_pad.shape), whole(b_gate.shape),
            whole(gla_norm_g.shape),
            pl.BlockSpec((wo_rows, w_out.shape[1]), lambda s: (jnp.minimum(s, n_tiles - 1), 0)),
        ],
        out_specs=[
            pl.BlockSpec((t, conv_width), lambda s: (jnp.maximum(s - 1, 0), 0)),
            pl.BlockSpec((t, gla_width), lambda s: (jnp.minimum(s, n_tiles - 1), 0)),
            pl.BlockSpec((wo_rows, w_out.shape[1]), lambda s: (jnp.minimum(s, n_tiles - 1), 0)),
        ],
        out_shape=[jax.ShapeDtypeStruct((batch * seq, conv_width), BF16),
                   jax.ShapeDtypeStruct((batch * seq, gla_width), BF16),
                   jax.ShapeDtypeStruct(w_out.shape, BF16)],
        scratch_shapes=[pltpu.VMEM((main_cols // W_ROWS, d_model, W_ROWS), BF16),
                        pltpu.VMEM((d_model, LANES), BF16),
                        pltpu.VMEM((W_SLOTS, W_ROWS, d_model), F32),
                        pltpu.SemaphoreType.DMA((W_SLOTS,)),
                        pltpu.VMEM((t, d_model), BF16),
                        pltpu.VMEM((t, d_model), BF16),
                        pltpu.VMEM((t, n_cols), BF16),
                        pltpu.VMEM((8, conv_width), F32),
                        pltpu.VMEM((GLA_HEADS, head_v, head_k), F32),
                        pltpu.VMEM((GLA_HEADS, head_v, head_k), BF16),
                        pltpu.VMEM((t, dk_total), F32),
                        pltpu.VMEM((t, 2 * dk_total), BF16),
                        pltpu.VMEM((t + 8, dk_total), F32),
                        pltpu.VMEM((t, dk_total), BF16)],
        compiler_params=pltpu.CompilerParams(
            dimension_semantics=("arbitrary",),
            vmem_limit_bytes=VMEM_LIMIT),
        name="proj_mix",
    )(x2d, x2d, norm_g, w_in_t, conv_w, conv_b, w_up_pad, b_gate, gla_norm_g, w_out)


def _out_proj_kernel(yc_ref, yg_ref, w_ref, x_ref, g_ref, o_ref, *, final_norm):
    kc = yc_ref.shape[1]
    z = (x_ref[...] + jnp.dot(yc_ref[...], w_ref[:kc, :], preferred_element_type=F32)
         + jnp.dot(yg_ref[...], w_ref[kc:, :], preferred_element_type=F32))
    if final_norm:
        ms = jnp.mean(z * z, axis=-1, keepdims=True)
        z = z * lax.rsqrt(ms + EPS) * g_ref[...]
    o_ref[...] = z


def _out_proj(y_conv, y_gla, w_out_bf16, x2d, final_g, *, final_norm):
    m, d = x2d.shape
    return pl.pallas_call(
        functools.partial(_out_proj_kernel, final_norm=final_norm),
        grid=(m // OUT_TM,),
        in_specs=[
            pl.BlockSpec((OUT_TM, y_conv.shape[1]), lambda i: (i, 0)),
            pl.BlockSpec((OUT_TM, y_gla.shape[1]), lambda i: (i, 0)),
            pl.BlockSpec(w_out_bf16.shape, lambda i: (0, 0)),
            pl.BlockSpec((OUT_TM, d), lambda i: (i, 0)),
            pl.BlockSpec((1, d), lambda i: (0, 0)),
        ],
        out_specs=pl.BlockSpec((OUT_TM, d), lambda i: (i, 0)),
        out_shape=jax.ShapeDtypeStruct((m, d), F32),
        compiler_params=pltpu.CompilerParams(
            dimension_semantics=("arbitrary",),
            vmem_limit_bytes=VMEM_LIMIT),
        name="out_proj",
    )(y_conv, y_gla, w_out_bf16, x2d, final_g)


def kernel(x, norm_g, w_in, conv_w, conv_b, gla_w_up, gla_b_gate, gla_norm_g, w_out, final_g):
    batch, seq, d_model = x.shape
    depth = norm_g.shape[0]
    conv_width = conv_w.shape[2]
    rank, dk_total = gla_w_up.shape[1], gla_w_up.shape[2]
    gla_width = gla_norm_g.shape[1] * gla_norm_g.shape[2]
    in_cols = w_in.shape[2]
    main_cols = in_cols - rank
    assert main_cols == 4 * conv_width + 2 * dk_total + 2 * gla_width
    assert main_cols % MXU_N == 0 and dk_total % MXU_N == 0 and rank <= LANES
    assert seq % MIX_T == 0 and MIX_T % CHUNK == 0 and (batch * seq) % OUT_TM == 0
    n_pad = main_cols + LANES

    x2d = x.reshape(batch * seq, d_model)
    for l in range(depth):
        w_up_p = jnp.pad(gla_w_up[l], ((0, LANES - rank), (0, 0))).astype(BF16)
        y_conv, y_gla, w_out_bf16 = _proj_mix(
            x2d, norm_g[l][None, :], w_in[l].T, conv_w[l], conv_b[l][None, :], w_up_p,
            gla_b_gate[l][None, :], gla_norm_g[l], w_out[l], batch=batch, seq=seq,
            conv_width=conv_width, dk_total=dk_total, gla_width=gla_width, n_cols=n_pad)
        x2d = _out_proj(y_conv, y_gla, w_out_bf16, x2d, final_g[None, :],
                        final_norm=(l == depth - 1))
    return x2d.reshape(batch, seq, d_model)
```

```python
import functools

import jax
import jax.numpy as jnp
from jax import lax
from jax.experimental import pallas as pl
from jax.experimental.pallas import tpu as pltpu

F32 = jnp.float32
BF16 = jnp.bfloat16

LANES = 128
MXU_N = 256
EPS = 1e-6
CHUNK = 64
GLA_HEADS = 4
GLA_TAU = 16.0

MIX_T = 256
STRIP = 16
COL_CHUNK = 256
W_ROWS = 256
W_SLOTS = 4
IDLE_TAIL_PIECES = 2
PIN_LAG = 3
OUT_TM = 512
VMEM_LIMIT = 56 * 1024 * 1024


def _silu(v):
    return v * jax.nn.sigmoid(v)


def _log_sigmoid(v):
    return -(jnp.maximum(-v, 0.0) + jnp.log1p(jnp.exp(-jnp.abs(v))))


def _spread(units, n_slots):
    out = [[] for _ in range(n_slots)]
    for j, u in enumerate(units):
        out[(j * n_slots) // len(units)].append(u)
    return out


def _proj_mix_kernel(x_ref, x_hbm, ng_ref, w_hbm, convw_ref, convb_ref, wup_ref, bgate_ref, gng_ref, wo_ref,
                     yc_ref, yg_ref, wo_bf_ref,
                     w_ref, w_tail_ref, stage_ref, dma_sem, h_ref, hn_ref, proj_ref, carry_ref, state_ref, sbf_ref,
                     glog_ref, hilo_ref, rev_ref, kdec_ref,
                     *, n_tiles, tiles_per_seq, conv_width, dk_total, gla_width):
    t_rows = x_ref.shape[0]
    in_cols, d_model = w_hbm.shape
    n_strips = t_rows // STRIP
    n_chunks = t_rows // CHUNK
    head_k = dk_total // GLA_HEADS
    head_v = gla_width // GLA_HEADS
    off_h, off_b, off_c, off_z = (i * conv_width for i in range(4))
    off_q = 4 * conv_width
    off_k = off_q + dk_total
    off_v = off_k + dk_total
    off_r = off_v + gla_width
    off_gd = off_r + gla_width
    s = pl.program_id(0)

    def strip_rows(i):
        return slice(i * STRIP, (i + 1) * STRIP)

    def conv_strip(rows):
        token = None
        for c0 in range(0, conv_width, COL_CHUNK):
            cols = slice(c0, c0 + COL_CHUNK)

            def col(off):
                return proj_ref[rows, off + c0:off + c0 + COL_CHUNK].astype(F32)

            u = col(off_c) * col(off_h)
            ext = jnp.concatenate([carry_ref[:, cols], u], axis=0)
            u1 = pltpu.roll(ext, 1, 0)[8:, :]
            u2 = pltpu.roll(ext, 2, 0)[8:, :]
            carry_ref[:, cols] = u[STRIP - 8:, :]
            conv = (convb_ref[:, cols] + convw_ref[0:1, cols] * u2 + convw_ref[1:2, cols] * u1
                    + convw_ref[2:3, cols] * u)
            y = (col(off_b) * conv * _silu(col(off_z))).astype(yc_ref.dtype)
            yc_ref[rows, cols] = y
            token = token_of(y) if token is None else token | token_of(y)
        return token

    def gate_pre():
        glog_ref[...] = jnp.dot(proj_ref[:, off_gd:off_gd + LANES], wup_ref[...],
                                preferred_element_type=F32)

    def gate_strip(rows):
        glog = _log_sigmoid(glog_ref[rows, :] + bgate_ref[...]) / GLA_TAU
        g_hi = glog.astype(BF16)
        hilo_ref[rows, :dk_total] = g_hi
        hilo_ref[rows, dk_total:] = (glog - g_hi.astype(F32)).astype(BF16)

    def later_sums():
        r_i = lax.broadcasted_iota(jnp.int32, (t_rows + 8, t_rows), 0)
        c_i = lax.broadcasted_iota(jnp.int32, (t_rows + 8, t_rows), 1)
        c_chunk = c_i // CHUNK
        ones = ((c_i > r_i) & (c_chunk == r_i // CHUNK)) | (c_chunk == r_i - t_rows)
        sums = jnp.dot(jnp.where(ones, 1.0, 0.0).astype(BF16), hilo_ref[...],
                       preferred_element_type=F32)
        rev_ref[...] = sums[:, :dk_total] + sums[:, dk_total:]

    def kdec_strip(rows):
        k = proj_ref[rows, off_k:off_k + dk_total].astype(F32)
        kdec_ref[rows, :] = (k * jnp.exp(rev_ref[rows, :])).astype(BF16)

    def state_update(step):
        c, h = divmod(step, GLA_HEADS)
        rows = slice(c * CHUNK, (c + 1) * CHUNK)
        ks = slice(h * head_k, (h + 1) * head_k)
        decay = jnp.exp(rev_ref[t_rows + c:t_rows + c + 1, ks])
        v = proj_ref[rows, off_v + h * head_v:off_v + (h + 1) * head_v]
        inc = lax.dot_general(v, kdec_ref[rows, ks], (((0,), (0,)), ((), ())),
                              preferred_element_type=F32)
        s_new = state_ref[h] * decay + inc
        state_ref[h] = s_new
        sbf_ref[h] = s_new.astype(BF16)

    def readout(step):
        c, h = divmod(step, GLA_HEADS)
        rows = slice(c * CHUNK, (c + 1) * CHUNK)
        q = proj_ref[rows, off_q + h * head_k:off_q + (h + 1) * head_k]
        o = lax.dot_general(q, sbf_ref[h], (((1,), (1,)), ((), ())),
                            preferred_element_type=F32) * (head_k ** -0.5)
        ms = jnp.mean(o * o, axis=-1, keepdims=True)
        o = o * lax.rsqrt(ms + EPS) * gng_ref[h:h + 1, :]
        r = proj_ref[rows, off_r + h * head_v:off_r + (h + 1) * head_v].astype(F32)
        yg_ref[rows, h * head_v:(h + 1) * head_v] = (o * _silu(r)).astype(yg_ref.dtype)

    def token_of(y):
        bits = pltpu.bitcast(y, jnp.uint32)
        token = bits[:, :LANES]
        for l0 in range(LANES, bits.shape[1], LANES):
            token = token | bits[:, l0:l0 + LANES]
        return token

    def norm_strip(src, dst, rows):
        x = src[rows, :]
        ms = jnp.mean(x * x, axis=-1, keepdims=True)
        h = (x * lax.rsqrt(ms + EPS) * ng_ref[...]).astype(BF16)
        dst[rows, :] = h
        return token_of(h)

    def pin(tokens):
        acc = tokens[0]
        for t in tokens[1:]:
            acc = acc | t
        zero = pltpu.bitcast((acc >> 16) >> 16, F32)
        zero = jnp.concatenate([zero] * (STRIP // zero.shape[0]), axis=0)
        tile = h_ref[0:STRIP, 0:LANES].astype(F32)
        h_ref[0:STRIP, 0:LANES] = (tile + zero).astype(BF16)

    def proj_piece(c0, width):
        def piece():
            if c0 < w_ref.shape[0] * W_ROWS:
                w = w_ref[c0 // W_ROWS, :, c0 % W_ROWS:c0 % W_ROWS + width]
            else:
                w = w_tail_ref[...]
            proj_ref[:, c0:c0 + width] = jnp.dot(h_ref[...], w,
                                                 preferred_element_type=F32).astype(BF16)
        return piece

    def pieces(off, width):
        return [proj_piece(c0, min(MXU_N, off + width - c0)) for c0 in range(off, off + width, MXU_N)]

    @pl.when(s == 0)
    def _():
        proj_ref[:, :4 * conv_width] = jnp.zeros((t_rows, 4 * conv_width), BF16)

        n_full = in_cols // W_ROWS

        def chunk_copy(i, rows=W_ROWS):
            slot = i % W_SLOTS
            return pltpu.make_async_copy(w_hbm.at[pl.ds(i * W_ROWS, rows), :],
                                         stage_ref.at[slot, pl.ds(0, rows), :], dma_sem.at[slot])

        def transpose_chunk(slot, rows, dst):
            for k0 in range(0, d_model, MXU_N):
                blk = stage_ref[slot, 0:rows, k0:k0 + MXU_N]
                dst[k0:k0 + MXU_N, :] = blk.T.astype(BF16)

        for i in range(W_SLOTS - 1):
            chunk_copy(i).start()

        def load_chunk(i, c):
            @pl.when(i + W_SLOTS - 1 < n_full)
            def _():
                chunk_copy(i + W_SLOTS - 1).start()
            chunk_copy(i).wait()
            transpose_chunk(i % W_SLOTS, W_ROWS, w_ref.at[i])
            return c

        lax.fori_loop(0, n_full, load_chunk, 0)

        tail_rows = in_cols - n_full * W_ROWS
        stage_ref[n_full % W_SLOTS, tail_rows:LANES, :] = jnp.zeros((LANES - tail_rows, d_model), F32)
        chunk_copy(n_full, tail_rows).start()
        chunk_copy(n_full, tail_rows).wait()
        transpose_chunk(n_full % W_SLOTS, LANES, w_tail_ref)

        x0_copy = pltpu.make_async_copy(x_hbm.at[pl.ds(0, t_rows), :],
                                        stage_ref.at[0, pl.ds(0, t_rows), :], dma_sem.at[0])
        x0_copy.start()
        x0_copy.wait()

        def norm_first(i, c):
            norm_strip(stage_ref.at[0], hn_ref, pl.ds(pl.multiple_of(i * STRIP, STRIP), STRIP))
            return c

        lax.fori_loop(0, n_strips, norm_first, 0, unroll=2)

    @pl.when((s == 0) | ((s + tiles_per_seq - 1) % tiles_per_seq == 0))
    def _():
        carry_ref[...] = jnp.zeros_like(carry_ref)

    @pl.when(s < n_tiles)
    def _():
        @pl.when(s % tiles_per_seq == 0)
        def _():
            state_ref[...] = jnp.zeros_like(state_ref)

        wo_bf_ref[...] = wo_ref[...].astype(BF16)

        h_ref[...] = hn_ref[...]

        norm_units = [functools.partial(norm_strip, x_ref, hn_ref, strip_rows(i))
                      for i in range(n_strips)]
        conv_units = [functools.partial(conv_strip, strip_rows(i)) for i in range(n_strips)]
        gate_units = [functools.partial(gate_strip, strip_rows(i)) for i in range(n_strips)]
        kdec_units = [functools.partial(kdec_strip, strip_rows(i)) for i in range(n_strips)]
        n_steps = n_chunks * GLA_HEADS

        mxu = (pieces(off_gd, LANES) + pieces(off_k, dk_total) + pieces(off_q, dk_total)
               + pieces(off_v, gla_width) + pieces(off_r, gla_width))
        n_k = dk_total // MXU_N
        vpu = [conv_units[:2]]
        vpu += _spread([gate_pre] + gate_units, n_k)
        vpu += [[later_sums, conv_units[2]]] + [kdec_units + [conv_units[3]]]
        vpu += [[] for _ in range(dk_total // MXU_N - 2)]
        rest = _spread(conv_units[4:], len(mxu) - len(vpu))
        rest[-1].append(functools.partial(state_update, 0))
        vpu += rest
        assert len(vpu) == len(mxu)

        mxu2 = pieces(off_h, 4 * conv_width)
        halves = []
        for step in range(n_steps):
            if step + 1 < n_steps:
                halves.append(functools.partial(state_update, step + 1))
            halves.append(functools.partial(readout, step))
        busy = len(mxu2) - IDLE_TAIL_PIECES
        vpu2 = [a + b for a, b in zip(_spread(halves, busy), _spread(norm_units, busy))]
        vpu2 += [[] for _ in range(IDLE_TAIL_PIECES)]

        tokens = []
        for piece, units in zip(mxu + mxu2, vpu + vpu2):
            if len(tokens) >= PIN_LAG and tokens[-PIN_LAG]:
                pin(tokens[-PIN_LAG])
            piece()
            tokens.append([t for t in [unit() for unit in units] if t is not None])

    @pl.when(s == n_tiles)
    def _():
        def step(i, c):
            conv_strip(pl.ds(pl.multiple_of(i * STRIP, STRIP), STRIP))
            return c
        lax.fori_loop(0, n_strips, step, 0, unroll=2)


def _proj_mix(x2d, norm_g, w_in_t, conv_w, conv_b, w_up_pad, b_gate, gla_norm_g, w_out, *,
              batch, seq, conv_width, dk_total, gla_width, n_cols):
    t = MIX_T
    nt = seq // t
    n_tiles = batch * nt
    d_model = x2d.shape[1]
    wo_rows = w_out.shape[0] // n_tiles
    assert w_out.shape[0] % n_tiles == 0 and wo_rows % STRIP == 0 and t <= W_ROWS
    main_cols = n_cols - LANES
    assert main_cols % W_ROWS == 0 and W_ROWS % MXU_N == 0 and 0 < w_in_t.shape[0] - main_cols <= LANES
    head_k = dk_total // GLA_HEADS
    head_v = gla_width // GLA_HEADS

    def whole(shape, **kw):
        return pl.BlockSpec(shape, lambda s: (0,) * len(shape), **kw)

    kern = functools.partial(_proj_mix_kernel, n_tiles=n_tiles, tiles_per_seq=nt,
                             conv_width=conv_width, dk_total=dk_total, gla_width=gla_width)
    return pl.pallas_call(
        kern,
        grid=(n_tiles + 1,),
        in_specs=[
            pl.BlockSpec((t, d_model), lambda s: (jnp.minimum(s + 1, n_tiles - 1), 0)),
            pl.BlockSpec(memory_space=pl.ANY),
            whole(norm_g.shape),
            pl.BlockSpec(memory_space=pl.ANY),
            whole(conv_w.shape), whole(conv_b.shape), whole(w_up_pad.shape), whole(b_gate.shape),
            whole(gla_norm_g.shape),
            pl.BlockSpec((wo_rows, w_out.shape[1]), lambda s: (jnp.minimum(s, n_tiles - 1), 0)),
        ],
        out_specs=[
            pl.BlockSpec((t, conv_width), lambda s: (jnp.maximum(s - 1, 0), 0)),
            pl.BlockSpec((t, gla_width), lambda s: (jnp.minimum(s, n_tiles - 1), 0)),
            pl.BlockSpec((wo_rows, w_out.shape[1]), lambda s: (jnp.minimum(s, n_tiles - 1), 0)),
        ],
        out_shape=[jax.ShapeDtypeStruct((batch * seq, conv_width), BF16),
                   jax.ShapeDtypeStruct((batch * seq, gla_width), BF16),
                   jax.ShapeDtypeStruct(w_out.shape, BF16)],
        scratch_shapes=[pltpu.VMEM((main_cols // W_ROWS, d_model, W_ROWS), BF16),
                        pltpu.VMEM((d_model, LANES), BF16),
                        pltpu.VMEM((W_SLOTS, W_ROWS, d_model), F32),
                        pltpu.SemaphoreType.DMA((W_SLOTS,)),
                        pltpu.VMEM((t, d_model), BF16),
                        pltpu.VMEM((t, d_model), BF16),
                        pltpu.VMEM((t, n_cols), BF16),
                        pltpu.VMEM((8, conv_width), F32),
                        pltpu.VMEM((GLA_HEADS, head_v, head_k), F32),
                        pltpu.VMEM((GLA_HEADS, head_v, head_k), BF16),
                        pltpu.VMEM((t, dk_total), F32),
                        pltpu.VMEM((t, 2 * dk_total), BF16),
                        pltpu.VMEM((t + 8, dk_total), F32),
                        pltpu.VMEM((t, dk_total), BF16)],
        compiler_params=pltpu.CompilerParams(
            dimension_semantics=("arbitrary",),
            vmem_limit_bytes=VMEM_LIMIT),
        name="proj_mix",
    )(x2d, x2d, norm_g, w_in_t, conv_w, conv_b, w_up_pad, b_gate, gla_norm_g, w_out)


def _out_proj_kernel(yc_ref, yg_ref, w_ref, x_ref, g_ref, o_ref, *, final_norm):
    kc = yc_ref.shape[1]
    z = (x_ref[...] + jnp.dot(yc_ref[...], w_ref[:kc, :], preferred_element_type=F32)
         + jnp.dot(yg_ref[...], w_ref[kc:, :], preferred_element_type=F32))
    if final_norm:
        ms = jnp.mean(z * z, axis=-1, keepdims=True)
        z = z * lax.rsqrt(ms + EPS) * g_ref[...]
    o_ref[...] = z


def _out_proj(y_conv, y_gla, w_out_bf16, x2d, final_g, *, final_norm):
    m, d = x2d.shape
    return pl.pallas_call(
        functools.partial(_out_proj_kernel, final_norm=final_norm),
        grid=(m // OUT_TM,),
        in_specs=[
            pl.BlockSpec((OUT_TM, y_conv.shape[1]), lambda i: (i, 0)),
            pl.BlockSpec((OUT_TM, y_gla.shape[1]), lambda i: (i, 0)),
            pl.BlockSpec(w_out_bf16.shape, lambda i: (0, 0)),
            pl.BlockSpec((OUT_TM, d), lambda i: (i, 0)),
            pl.BlockSpec((1, d), lambda i: (0, 0)),
        ],
        out_specs=pl.BlockSpec((OUT_TM, d), lambda i: (i, 0)),
        out_shape=jax.ShapeDtypeStruct((m, d), F32),
        compiler_params=pltpu.CompilerParams(
            dimension_semantics=("arbitrary",),
            vmem_limit_bytes=VMEM_LIMIT),
        name="out_proj",
    )(y_conv, y_gla, w_out_bf16, x2d, final_g)


def kernel(x, norm_g, w_in, conv_w, conv_b, gla_w_up, gla_b_gate, gla_norm_g, w_out, final_g):
    batch, seq, d_model = x.shape
    depth = norm_g.shape[0]
    conv_width = conv_w.shape[2]
    rank, dk_total = gla_w_up.shape[1], gla_w_up.shape[2]
    gla_width = gla_norm_g.shape[1] * gla_norm_g.shape[2]
    in_cols = w_in.shape[2]
    main_cols = in_cols - rank
    assert main_cols == 4 * conv_width + 2 * dk_total + 2 * gla_width
    assert main_cols % MXU_N == 0 and dk_total % MXU_N == 0 and rank <= LANES
    assert seq % MIX_T == 0 and MIX_T % CHUNK == 0 and (batch * seq) % OUT_TM == 0
    n_pad = main_cols + LANES

    x2d = x.reshape(batch * seq, d_model)
    for l in range(depth):
        w_up_p = jnp.pad(gla_w_up[l], ((0, LANES - rank), (0, 0))).astype(BF16)
        y_conv, y_gla, w_out_bf16 = _proj_mix(
            x2d, norm_g[l][None, :], w_in[l].T, conv_w[l], conv_b[l][None, :], w_up_p,
            gla_b_gate[l][None, :], gla_norm_g[l], w_out[l], batch=batch, seq=seq,
            conv_width=conv_width, dk_total=dk_total, gla_width=gla_width, n_cols=n_pad)
        x2d = _out_proj(y_conv, y_gla, w_out_bf16, x2d, final_g[None, :],
                        final_norm=(l == depth - 1))
    return x2d.reshape(batch, seq, d_model)
```

```python
import functools

import jax
import jax.numpy as jnp
from jax import lax
from jax.experimental import pallas as pl
from jax.experimental.pallas import tpu as pltpu

F32 = jnp.float32
BF16 = jnp.bfloat16

LANES = 128
MXU_N = 256
EPS = 1e-6
CHUNK = 64
GLA_HEADS = 4
GLA_TAU = 16.0

MIX_T = 256
STRIP = 16
COL_CHUNK = 256
W_ROWS = 256
W_SLOTS = 4
IDLE_TAIL_PIECES = 2
PIN_LAG = 1
OUT_TM = 512
VMEM_LIMIT = 56 * 1024 * 1024


def _silu(v):
    return v * jax.nn.sigmoid(v)


def _log_sigmoid(v):
    return -(jnp.maximum(-v, 0.0) + jnp.log1p(jnp.exp(-jnp.abs(v))))


def _spread(units, n_slots):
    out = [[] for _ in range(n_slots)]
    for j, u in enumerate(units):
        out[(j * n_slots) // len(units)].append(u)
    return out


def _proj_mix_kernel(x_ref, x_hbm, ng_ref, w_hbm, convw_ref, convb_ref, wup_ref, bgate_ref, gng_ref, wo_ref,
                     yc_ref, yg_ref, wo_bf_ref,
                     w_ref, w_tail_ref, stage_ref, dma_sem, h_ref, hn_ref, proj_ref, carry_ref, state_ref, sbf_ref,
                     glog_ref, hilo_ref, rev_ref, kdec_ref,
                     *, n_tiles, tiles_per_seq, conv_width, dk_total, gla_width):
    t_rows = x_ref.shape[0]
    in_cols, d_model = w_hbm.shape
    n_strips = t_rows // STRIP
    n_chunks = t_rows // CHUNK
    head_k = dk_total // GLA_HEADS
    head_v = gla_width // GLA_HEADS
    off_h, off_b, off_c, off_z = (i * conv_width for i in range(4))
    off_q = 4 * conv_width
    off_k = off_q + dk_total
    off_v = off_k + dk_total
    off_r = off_v + gla_width
    off_gd = off_r + gla_width
    s = pl.program_id(0)

    def strip_rows(i):
        return slice(i * STRIP, (i + 1) * STRIP)

    def conv_strip(rows):
        token = None
        for c0 in range(0, conv_width, COL_CHUNK):
            cols = slice(c0, c0 + COL_CHUNK)

            def col(off):
                return proj_ref[rows, off + c0:off + c0 + COL_CHUNK].astype(F32)

            u = col(off_c) * col(off_h)
            ext = jnp.concatenate([carry_ref[:, cols], u], axis=0)
            u1 = pltpu.roll(ext, 1, 0)[8:, :]
            u2 = pltpu.roll(ext, 2, 0)[8:, :]
            carry_ref[:, cols] = u[STRIP - 8:, :]
            conv = (convb_ref[:, cols] + convw_ref[0:1, cols] * u2 + convw_ref[1:2, cols] * u1
                    + convw_ref[2:3, cols] * u)
            y = (col(off_b) * conv * _silu(col(off_z))).astype(yc_ref.dtype)
            yc_ref[rows, cols] = y
            token = token_of(y) if token is None else token | token_of(y)
        return token

    def gate_pre():
        glog_ref[...] = jnp.dot(proj_ref[:, off_gd:off_gd + LANES], wup_ref[...],
                                preferred_element_type=F32)

    def gate_strip(rows):
        glog = _log_sigmoid(glog_ref[rows, :] + bgate_ref[...]) / GLA_TAU
        g_hi = glog.astype(BF16)
        hilo_ref[rows, :dk_total] = g_hi
        hilo_ref[rows, dk_total:] = (glog - g_hi.astype(F32)).astype(BF16)

    def later_sums():
        r_i = lax.broadcasted_iota(jnp.int32, (t_rows + 8, t_rows), 0)
        c_i = lax.broadcasted_iota(jnp.int32, (t_rows + 8, t_rows), 1)
        c_chunk = c_i // CHUNK
        ones = ((c_i > r_i) & (c_chunk == r_i // CHUNK)) | (c_chunk == r_i - t_rows)
        sums = jnp.dot(jnp.where(ones, 1.0, 0.0).astype(BF16), hilo_ref[...],
                       preferred_element_type=F32)
        rev_ref[...] = sums[:, :dk_total] + sums[:, dk_total:]

    def kdec_strip(rows):
        k = proj_ref[rows, off_k:off_k + dk_total].astype(F32)
        kdec_ref[rows, :] = (k * jnp.exp(rev_ref[rows, :])).astype(BF16)

    def state_update(step):
        c, h = divmod(step, GLA_HEADS)
        rows = slice(c * CHUNK, (c + 1) * CHUNK)
        ks = slice(h * head_k, (h + 1) * head_k)
        decay = jnp.exp(rev_ref[t_rows + c:t_rows + c + 1, ks])
        v = proj_ref[rows, off_v + h * head_v:off_v + (h + 1) * head_v]
        inc = lax.dot_general(v, kdec_ref[rows, ks], (((0,), (0,)), ((), ())),
                              preferred_element_type=F32)
        s_new = state_ref[h] * decay + inc
        state_ref[h] = s_new
        sbf_ref[h] = s_new.astype(BF16)

    def readout(step):
        c, h = divmod(step, GLA_HEADS)
        rows = slice(c * CHUNK, (c + 1) * CHUNK)
        q = proj_ref[rows, off_q + h * head_k:off_q + (h + 1) * head_k]
        o = lax.dot_general(q, sbf_ref[h], (((1,), (1,)), ((), ())),
                            preferred_element_type=F32) * (head_k ** -0.5)
        ms = jnp.mean(o * o, axis=-1, keepdims=True)
        o = o * lax.rsqrt(ms + EPS) * gng_ref[h:h + 1, :]
        r = proj_ref[rows, off_r + h * head_v:off_r + (h + 1) * head_v].astype(F32)
        yg_ref[rows, h * head_v:(h + 1) * head_v] = (o * _silu(r)).astype(yg_ref.dtype)

    def token_of(y):
        bits = pltpu.bitcast(y, jnp.uint32)
        token = bits[:, :LANES]
        for l0 in range(LANES, bits.shape[1], LANES):
            token = token | bits[:, l0:l0 + LANES]
        return token

    def norm_strip(src, dst, rows):
        x = src[rows, :]
        ms = jnp.mean(x * x, axis=-1, keepdims=True)
        h = (x * lax.rsqrt(ms + EPS) * ng_ref[...]).astype(BF16)
        dst[rows, :] = h
        return token_of(h)

    def pin(tokens):
        acc = tokens[0]
        for t in tokens[1:]:
            acc = acc | t
        zero = pltpu.bitcast((acc >> 16) >> 16, F32)
        zero = jnp.concatenate([zero] * (STRIP // zero.shape[0]), axis=0)
        tile = h_ref[0:STRIP, 0:LANES].astype(F32)
        h_ref[0:STRIP, 0:LANES] = (tile + zero).astype(BF16)

    def proj_piece(c0, width):
        def piece():
            if c0 < w_ref.shape[0] * W_ROWS:
                w = w_ref[c0 // W_ROWS, :, c0 % W_ROWS:c0 % W_ROWS + width]
            else:
                w = w_tail_ref[...]
            proj_ref[:, c0:c0 + width] = jnp.dot(h_ref[...], w,
                                                 preferred_element_type=F32).astype(BF16)
        return piece

    def pieces(off, width):
        return [proj_piece(c0, min(MXU_N, off + width - c0)) for c0 in range(off, off + width, MXU_N)]

    @pl.when(s == 0)
    def _():
        proj_ref[:, :4 * conv_width] = jnp.zeros((t_rows, 4 * conv_width), BF16)

        n_full = in_cols // W_ROWS

        def chunk_copy(i, rows=W_ROWS):
            slot = i % W_SLOTS
            return pltpu.make_async_copy(w_hbm.at[pl.ds(i * W_ROWS, rows), :],
                                         stage_ref.at[slot, pl.ds(0, rows), :], dma_sem.at[slot])

        def transpose_chunk(slot, rows, dst):
            for k0 in range(0, d_model, MXU_N):
                blk = stage_ref[slot, 0:rows, k0:k0 + MXU_N]
                dst[k0:k0 + MXU_N, :] = blk.T.astype(BF16)

        for i in range(W_SLOTS - 1):
            chunk_copy(i).start()

        def load_chunk(i, c):
            @pl.when(i + W_SLOTS - 1 < n_full)
            def _():
                chunk_copy(i + W_SLOTS - 1).start()
            chunk_copy(i).wait()
            transpose_chunk(i % W_SLOTS, W_ROWS, w_ref.at[i])
            return c

        lax.fori_loop(0, n_full, load_chunk, 0)

        tail_rows = in_cols - n_full * W_ROWS
        stage_ref[n_full % W_SLOTS, tail_rows:LANES, :] = jnp.zeros((LANES - tail_rows, d_model), F32)
        chunk_copy(n_full, tail_rows).start()
        chunk_copy(n_full, tail_rows).wait()
        transpose_chunk(n_full % W_SLOTS, LANES, w_tail_ref)

        x0_copy = pltpu.make_async_copy(x_hbm.at[pl.ds(0, t_rows), :],
                                        stage_ref.at[0, pl.ds(0, t_rows), :], dma_sem.at[0])
        x0_copy.start()
        x0_copy.wait()

        def norm_first(i, c):
            norm_strip(stage_ref.at[0], hn_ref, pl.ds(pl.multiple_of(i * STRIP, STRIP), STRIP))
            return c

        lax.fori_loop(0, n_strips, norm_first, 0, unroll=2)

    @pl.when((s == 0) | ((s + tiles_per_seq - 1) % tiles_per_seq == 0))
    def _():
        carry_ref[...] = jnp.zeros_like(carry_ref)

    @pl.when(s < n_tiles)
    def _():
        @pl.when(s % tiles_per_seq == 0)
        def _():
            state_ref[...] = jnp.zeros_like(state_ref)

        wo_bf_ref[...] = wo_ref[...].astype(BF16)

        h_ref[...] = hn_ref[...]

        norm_units = [functools.partial(norm_strip, x_ref, hn_ref, strip_rows(i))
                      for i in range(n_strips)]
        conv_units = [functools.partial(conv_strip, strip_rows(i)) for i in range(n_strips)]
        gate_units = [functools.partial(gate_strip, strip_rows(i)) for i in range(n_strips)]
        kdec_units = [functools.partial(kdec_strip, strip_rows(i)) for i in range(n_strips)]
        n_steps = n_chunks * GLA_HEADS

        mxu = (pieces(off_gd, LANES) + pieces(off_k, dk_total) + pieces(off_q, dk_total)
               + pieces(off_v, gla_width) + pieces(off_r, gla_width))
        n_k = dk_total // MXU_N
        vpu = [conv_units[:2]]
        vpu += _spread([gate_pre] + gate_units, n_k)
        vpu += [[later_sums, conv_units[2]]] + [kdec_units + [conv_units[3]]]
        vpu += [[] for _ in range(dk_total // MXU_N - 2)]
        rest = _spread(conv_units[4:], len(mxu) - len(vpu))
        rest[-1].append(functools.partial(state_update, 0))
        vpu += rest
        assert len(vpu) == len(mxu)

        mxu2 = pieces(off_h, 4 * conv_width)
        halves = []
        for step in range(n_steps):
            if step + 1 < n_steps:
                halves.append(functools.partial(state_update, step + 1))
            halves.append(functools.partial(readout, step))
        busy = len(mxu2) - IDLE_TAIL_PIECES
        vpu2 = [a + b for a, b in zip(_spread(halves, busy), _spread(norm_units, busy))]
        vpu2 += [[] for _ in range(IDLE_TAIL_PIECES)]

        tokens = []
        for piece, units in zip(mxu + mxu2, vpu + vpu2):
            if len(tokens) >= PIN_LAG and tokens[-PIN_LAG]:
                pin(tokens[-PIN_LAG])
            piece()
            tokens.append([t for t in [unit() for unit in units] if t is not None])

    @pl.when(s == n_tiles)
    def _():
        def step(i, c):
            conv_strip(pl.ds(pl.multiple_of(i * STRIP, STRIP), STRIP))
            return c
        lax.fori_loop(0, n_strips, step, 0, unroll=2)


def _proj_mix(x2d, norm_g, w_in_t, conv_w, conv_b, w_up_pad, b_gate, gla_norm_g, w_out, *,
              batch, seq, conv_width, dk_total, gla_width, n_cols):
    t = MIX_T
    nt = seq // t
    n_tiles = batch * nt
    d_model = x2d.shape[1]
    wo_rows = w_out.shape[0] // n_tiles
    assert w_out.shape[0] % n_tiles == 0 and wo_rows % STRIP == 0 and t <= W_ROWS
    main_cols = n_cols - LANES
    assert main_cols % W_ROWS == 0 and W_ROWS % MXU_N == 0 and 0 < w_in_t.shape[0] - main_cols <= LANES
    head_k = dk_total // GLA_HEADS
    head_v = gla_width // GLA_HEADS

    def whole(shape, **kw):
        return pl.BlockSpec(shape, lambda s: (0,) * len(shape), **kw)

    kern = functools.partial(_proj_mix_kernel, n_tiles=n_tiles, tiles_per_seq=nt,
                             conv_width=conv_width, dk_total=dk_total, gla_width=gla_width)
    return pl.pallas_call(
        kern,
        grid=(n_tiles + 1,),
        in_specs=[
            pl.BlockSpec((t, d_model), lambda s: (jnp.minimum(s + 1, n_tiles - 1), 0)),
            pl.BlockSpec(memory_space=pl.ANY),
            whole(norm_g.shape),
            pl.BlockSpec(memory_space=pl.ANY),
            whole(conv_w.shape), whole(conv_b.shape), whole(w_up_pad.shape), whole(b_gate.shape),
            whole(gla_norm_g.shape),
            pl.BlockSpec((wo_rows, w_out.shape[1]), lambda s: (jnp.minimum(s, n_tiles - 1), 0)),
        ],
        out_specs=[
            pl.BlockSpec((t, conv_width), lambda s: (jnp.maximum(s - 1, 0), 0)),
            pl.BlockSpec((t, gla_width), lambda s: (jnp.minimum(s, n_tiles - 1), 0)),
            pl.BlockSpec((wo_rows, w_out.shape[1]), lambda s: (jnp.minimum(s, n_tiles - 1), 0)),
        ],
        out_shape=[jax.ShapeDtypeStruct((batch * seq, conv_width), BF16),
                   jax.ShapeDtypeStruct((batch * seq, gla_width), BF16),
                   jax.ShapeDtypeStruct(w_out.shape, BF16)],
        scratch_shapes=[pltpu.VMEM((main_cols // W_ROWS, d_model, W_ROWS), BF16),
                        pltpu.VMEM((d_model, LANES), BF16),
                        pltpu.VMEM((W_SLOTS, W_ROWS, d_model), F32),
                        pltpu.SemaphoreType.DMA((W_SLOTS,)),
                        pltpu.VMEM((t, d_model), BF16),
                        pltpu.VMEM((t, d_model), BF16),
                        pltpu.VMEM((t, n_cols), BF16),
                        pltpu.VMEM((8, conv_width), F32),
                        pltpu.VMEM((GLA_HEADS, head_v, head_k), F32),
                        pltpu.VMEM((GLA_HEADS, head_v, head_k), BF16),
                        pltpu.VMEM((t, dk_total), F32),
                        pltpu.VMEM((t, 2 * dk_total), BF16),
                        pltpu.VMEM((t + 8, dk_total), F32),
                        pltpu.VMEM((t, dk_total), BF16)],
        compiler_params=pltpu.CompilerParams(
            dimension_semantics=("arbitrary",),
            vmem_limit_bytes=VMEM_LIMIT),
        name="proj_mix",
    )(x2d, x2d, norm_g, w_in_t, conv_w, conv_b, w_up_pad, b_gate, gla_norm_g, w_out)


def _out_proj_kernel(yc_ref, yg_ref, w_ref, x_ref, g_ref, o_ref, *, final_norm):
    kc = yc_ref.shape[1]
    z = (x_ref[...] + jnp.dot(yc_ref[...], w_ref[:kc, :], preferred_element_type=F32)
         + jnp.dot(yg_ref[...], w_ref[kc:, :], preferred_element_type=F32))
    if final_norm:
        ms = jnp.mean(z * z, axis=-1, keepdims=True)
        z = z * lax.rsqrt(ms + EPS) * g_ref[...]
    o_ref[...] = z


def _out_proj(y_conv, y_gla, w_out_bf16, x2d, final_g, *, final_norm):
    m, d = x2d.shape
    return pl.pallas_call(
        functools.partial(_out_proj_kernel, final_norm=final_norm),
        grid=(m // OUT_TM,),
        in_specs=[
            pl.BlockSpec((OUT_TM, y_conv.shape[1]), lambda i: (i, 0)),
            pl.BlockSpec((OUT_TM, y_gla.shape[1]), lambda i: (i, 0)),
            pl.BlockSpec(w_out_bf16.shape, lambda i: (0, 0)),
            pl.BlockSpec((OUT_TM, d), lambda i: (i, 0)),
            pl.BlockSpec((1, d), lambda i: (0, 0)),
        ],
        out_specs=pl.BlockSpec((OUT_TM, d), lambda i: (i, 0)),
        out_shape=jax.ShapeDtypeStruct((m, d), F32),
        compiler_params=pltpu.CompilerParams(
            dimension_semantics=("arbitrary",),
            vmem_limit_bytes=VMEM_LIMIT),
        name="out_proj",
    )(y_conv, y_gla, w_out_bf16, x2d, final_g)


def kernel(x, norm_g, w_in, conv_w, conv_b, gla_w_up, gla_b_gate, gla_norm_g, w_out, final_g):
    batch, seq, d_model = x.shape
    depth = norm_g.shape[0]
    conv_width = conv_w.shape[2]
    rank, dk_total = gla_w_up.shape[1], gla_w_up.shape[2]
    gla_width = gla_norm_g.shape[1] * gla_norm_g.shape[2]
    in_cols = w_in.shape[2]
    main_cols = in_cols - rank
    assert main_cols == 4 * conv_width + 2 * dk_total + 2 * gla_width
    assert main_cols % MXU_N == 0 and dk_total % MXU_N == 0 and rank <= LANES
    assert seq % MIX_T == 0 and MIX_T % CHUNK == 0 and (batch * seq) % OUT_TM == 0
    n_pad = main_cols + LANES

    x2d = x.reshape(batch * seq, d_model)
    for l in range(depth):
        w_up_p = jnp.pad(gla_w_up[l], ((0, LANES - rank), (0, 0))).astype(BF16)
        y_conv, y_gla, w_out_bf16 = _proj_mix(
            x2d, norm_g[l][None, :], w_in[l].T, conv_w[l], conv_b[l][None, :], w_up_p,
            gla_b_gate[l][None, :], gla_norm_g[l], w_out[l], batch=batch, seq=seq,
            conv_width=conv_width, dk_total=dk_total, gla_width=gla_width, n_cols=n_pad)
        x2d = _out_proj(y_conv, y_gla, w_out_bf16, x2d, final_g[None, :],
                        final_norm=(l == depth - 1))
    return x2d.reshape(batch, seq, d_model)
```

```python
import functools

import jax
import jax.numpy as jnp
from jax import lax
from jax.experimental import pallas as pl
from jax.experimental.pallas import tpu as pltpu

F32 = jnp.float32
BF16 = jnp.bfloat16

LANES = 128
MXU_N = 256
EPS = 1e-6
CHUNK = 64
GLA_HEADS = 4
GLA_TAU = 16.0

MIX_T = 256
STRIP = 16
COL_CHUNK = 256
W_ROWS = 256
W_SLOTS = 4
IDLE_TAIL_PIECES = 1
PIN_LAG = 1
OUT_TM = 512
VMEM_LIMIT = 56 * 1024 * 1024


def _silu(v):
    return v * jax.nn.sigmoid(v)


def _log_sigmoid(v):
    return -(jnp.maximum(-v, 0.0) + jnp.log1p(jnp.exp(-jnp.abs(v))))


def _spread(units, n_slots):
    out = [[] for _ in range(n_slots)]
    for j, u in enumerate(units):
        out[(j * n_slots) // len(units)].append(u)
    return out


def _proj_mix_kernel(x_ref, x_hbm, ng_ref, w_hbm, convw_ref, convb_ref, wup_ref, bgate_ref, gng_ref, wo_ref,
                     yc_ref, yg_ref, wo_bf_ref,
                     w_ref, w_tail_ref, stage_ref, dma_sem, h_ref, hn_ref, proj_ref, carry_ref, state_ref, sbf_ref,
                     glog_ref, hilo_ref, rev_ref, kdec_ref,
                     *, n_tiles, tiles_per_seq, conv_width, dk_total, gla_width):
    t_rows = x_ref.shape[0]
    in_cols, d_model = w_hbm.shape
    n_strips = t_rows // STRIP
    n_chunks = t_rows // CHUNK
    head_k = dk_total // GLA_HEADS
    head_v = gla_width // GLA_HEADS
    off_h, off_b, off_c, off_z = (i * conv_width for i in range(4))
    off_q = 4 * conv_width
    off_k = off_q + dk_total
    off_v = off_k + dk_total
    off_r = off_v + gla_width
    off_gd = off_r + gla_width
    s = pl.program_id(0)

    def strip_rows(i):
        return slice(i * STRIP, (i + 1) * STRIP)

    def conv_strip(rows):
        token = None
        for c0 in range(0, conv_width, COL_CHUNK):
            cols = slice(c0, c0 + COL_CHUNK)

            def col(off):
                return proj_ref[rows, off + c0:off + c0 + COL_CHUNK].astype(F32)

            u = col(off_c) * col(off_h)
            ext = jnp.concatenate([carry_ref[:, cols], u], axis=0)
            u1 = pltpu.roll(ext, 1, 0)[8:, :]
            u2 = pltpu.roll(ext, 2, 0)[8:, :]
            carry_ref[:, cols] = u[STRIP - 8:, :]
            conv = (convb_ref[:, cols] + convw_ref[0:1, cols] * u2 + convw_ref[1:2, cols] * u1
                    + convw_ref[2:3, cols] * u)
            y = (col(off_b) * conv * _silu(col(off_z))).astype(yc_ref.dtype)
            yc_ref[rows, cols] = y
            token = token_of(y) if token is None else token | token_of(y)
        return token

    def gate_pre():
        glog_ref[...] = jnp.dot(proj_ref[:, off_gd:off_gd + LANES], wup_ref[...],
                                preferred_element_type=F32)

    def gate_strip(rows):
        glog = _log_sigmoid(glog_ref[rows, :] + bgate_ref[...]) / GLA_TAU
        g_hi = glog.astype(BF16)
        hilo_ref[rows, :dk_total] = g_hi
        hilo_ref[rows, dk_total:] = (glog - g_hi.astype(F32)).astype(BF16)

    def later_sums():
        r_i = lax.broadcasted_iota(jnp.int32, (t_rows + 8, t_rows), 0)
        c_i = lax.broadcasted_iota(jnp.int32, (t_rows + 8, t_rows), 1)
        c_chunk = c_i // CHUNK
        ones = ((c_i > r_i) & (c_chunk == r_i // CHUNK)) | (c_chunk == r_i - t_rows)
        sums = jnp.dot(jnp.where(ones, 1.0, 0.0).astype(BF16), hilo_ref[...],
                       preferred_element_type=F32)
        rev_ref[...] = sums[:, :dk_total] + sums[:, dk_total:]

    def kdec_strip(rows):
        k = proj_ref[rows, off_k:off_k + dk_total].astype(F32)
        kdec_ref[rows, :] = (k * jnp.exp(rev_ref[rows, :])).astype(BF16)

    def state_update(step):
        c, h = divmod(step, GLA_HEADS)
        rows = slice(c * CHUNK, (c + 1) * CHUNK)
        ks = slice(h * head_k, (h + 1) * head_k)
        decay = jnp.exp(rev_ref[t_rows + c:t_rows + c + 1, ks])
        v = proj_ref[rows, off_v + h * head_v:off_v + (h + 1) * head_v]
        inc = lax.dot_general(v, kdec_ref[rows, ks], (((0,), (0,)), ((), ())),
                              preferred_element_type=F32)
        s_new = state_ref[h] * decay + inc
        state_ref[h] = s_new
        sbf_ref[h] = s_new.astype(BF16)

    def readout(step):
        c, h = divmod(step, GLA_HEADS)
        rows = slice(c * CHUNK, (c + 1) * CHUNK)
        q = proj_ref[rows, off_q + h * head_k:off_q + (h + 1) * head_k]
        o = lax.dot_general(q, sbf_ref[h], (((1,), (1,)), ((), ())),
                            preferred_element_type=F32) * (head_k ** -0.5)
        ms = jnp.mean(o * o, axis=-1, keepdims=True)
        o = o * lax.rsqrt(ms + EPS) * gng_ref[h:h + 1, :]
        r = proj_ref[rows, off_r + h * head_v:off_r + (h + 1) * head_v].astype(F32)
        yg_ref[rows, h * head_v:(h + 1) * head_v] = (o * _silu(r)).astype(yg_ref.dtype)

    def token_of(y):
        bits = pltpu.bitcast(y, jnp.uint32)
        token = bits[:, :LANES]
        for l0 in range(LANES, bits.shape[1], LANES):
            token = token | bits[:, l0:l0 + LANES]
        return token

    def norm_strip(src, dst, rows):
        x = src[rows, :]
        ms = jnp.mean(x * x, axis=-1, keepdims=True)
        h = (x * lax.rsqrt(ms + EPS) * ng_ref[...]).astype(BF16)
        dst[rows, :] = h
        return token_of(h)

    def pin(tokens):
        acc = tokens[0]
        for t in tokens[1:]:
            acc = acc | t
        zero = pltpu.bitcast((acc >> 16) >> 16, F32)
        zero = jnp.concatenate([zero] * (STRIP // zero.shape[0]), axis=0)
        tile = h_ref[0:STRIP, 0:LANES].astype(F32)
        h_ref[0:STRIP, 0:LANES] = (tile + zero).astype(BF16)

    def proj_piece(c0, width):
        def piece():
            if c0 < w_ref.shape[0] * W_ROWS:
                w = w_ref[c0 // W_ROWS, :, c0 % W_ROWS:c0 % W_ROWS + width]
            else:
                w = w_tail_ref[...]
            proj_ref[:, c0:c0 + width] = jnp.dot(h_ref[...], w,
                                                 preferred_element_type=F32).astype(BF16)
        return piece

    def pieces(off, width):
        return [proj_piece(c0, min(MXU_N, off + width - c0)) for c0 in range(off, off + width, MXU_N)]

    @pl.when(s == 0)
    def _():
        proj_ref[:, :4 * conv_width] = jnp.zeros((t_rows, 4 * conv_width), BF16)

        n_full = in_cols // W_ROWS

        def chunk_copy(i, rows=W_ROWS):
            slot = i % W_SLOTS
            return pltpu.make_async_copy(w_hbm.at[pl.ds(i * W_ROWS, rows), :],
                                         stage_ref.at[slot, pl.ds(0, rows), :], dma_sem.at[slot])

        def transpose_chunk(slot, rows, dst):
            for k0 in range(0, d_model, MXU_N):
                blk = stage_ref[slot, 0:rows, k0:k0 + MXU_N]
                dst[k0:k0 + MXU_N, :] = blk.T.astype(BF16)

        for i in range(W_SLOTS - 1):
            chunk_copy(i).start()

        def load_chunk(i, c):
            @pl.when(i + W_SLOTS - 1 < n_full)
            def _():
                chunk_copy(i + W_SLOTS - 1).start()
            chunk_copy(i).wait()
            transpose_chunk(i % W_SLOTS, W_ROWS, w_ref.at[i])
            return c

        lax.fori_loop(0, n_full, load_chunk, 0)

        tail_rows = in_cols - n_full * W_ROWS
        stage_ref[n_full % W_SLOTS, tail_rows:LANES, :] = jnp.zeros((LANES - tail_rows, d_model), F32)
        chunk_copy(n_full, tail_rows).start()
        chunk_copy(n_full, tail_rows).wait()
        transpose_chunk(n_full % W_SLOTS, LANES, w_tail_ref)

        x0_copy = pltpu.make_async_copy(x_hbm.at[pl.ds(0, t_rows), :],
                                        stage_ref.at[0, pl.ds(0, t_rows), :], dma_sem.at[0])
        x0_copy.start()
        x0_copy.wait()

        def norm_first(i, c):
            norm_strip(stage_ref.at[0], hn_ref, pl.ds(pl.multiple_of(i * STRIP, STRIP), STRIP))
            return c

        lax.fori_loop(0, n_strips, norm_first, 0, unroll=2)

    @pl.when((s == 0) | ((s + tiles_per_seq - 1) % tiles_per_seq == 0))
    def _():
        carry_ref[...] = jnp.zeros_like(carry_ref)

    @pl.when(s < n_tiles)
    def _():
        @pl.when(s % tiles_per_seq == 0)
        def _():
            state_ref[...] = jnp.zeros_like(state_ref)

        wo_bf_ref[...] = wo_ref[...].astype(BF16)

        h_ref[...] = hn_ref[...]

        norm_units = [functools.partial(norm_strip, x_ref, hn_ref, strip_rows(i))
                      for i in range(n_strips)]
        conv_units = [functools.partial(conv_strip, strip_rows(i)) for i in range(n_strips)]
        gate_units = [functools.partial(gate_strip, strip_rows(i)) for i in range(n_strips)]
        kdec_units = [functools.partial(kdec_strip, strip_rows(i)) for i in range(n_strips)]
        n_steps = n_chunks * GLA_HEADS

        mxu = (pieces(off_gd, LANES) + pieces(off_k, dk_total) + pieces(off_q, dk_total)
               + pieces(off_v, gla_width) + pieces(off_r, gla_width))
        n_k = dk_total // MXU_N
        vpu = [conv_units[:2]]
        vpu += _spread([gate_pre] + gate_units, n_k)
        vpu += [[later_sums, conv_units[2]]] + [kdec_units + [conv_units[3]]]
        vpu += [[] for _ in range(dk_total // MXU_N - 2)]
        rest = _spread(conv_units[4:], len(mxu) - len(vpu))
        rest[-1].append(functools.partial(state_update, 0))
        vpu += rest
        assert len(vpu) == len(mxu)

        mxu2 = pieces(off_h, 4 * conv_width)
        halves = []
        for step in range(n_steps):
            if step + 1 < n_steps:
                halves.append(functools.partial(state_update, step + 1))
            halves.append(functools.partial(readout, step))
        busy = len(mxu2) - IDLE_TAIL_PIECES
        vpu2 = [a + b for a, b in zip(_spread(halves, busy), _spread(norm_units, busy))]
        vpu2 += [[] for _ in range(IDLE_TAIL_PIECES)]

        tokens = []
        for piece, units in zip(mxu + mxu2, vpu + vpu2):
            if len(tokens) >= PIN_LAG and tokens[-PIN_LAG]:
                pin(tokens[-PIN_LAG])
            piece()
            tokens.append([t for t in [unit() for unit in units] if t is not None])

    @pl.when(s == n_tiles)
    def _():
        def step(i, c):
            conv_strip(pl.ds(pl.multiple_of(i * STRIP, STRIP), STRIP))
            return c
        lax.fori_loop(0, n_strips, step, 0, unroll=2)


def _proj_mix(x2d, norm_g, w_in_t, conv_w, conv_b, w_up_pad, b_gate, gla_norm_g, w_out, *,
              batch, seq, conv_width, dk_total, gla_width, n_cols):
    t = MIX_T
    nt = seq // t
    n_tiles = batch * nt
    d_model = x2d.shape[1]
    wo_rows = w_out.shape[0] // n_tiles
    assert w_out.shape[0] % n_tiles == 0 and wo_rows % STRIP == 0 and t <= W_ROWS
    main_cols = n_cols - LANES
    assert main_cols % W_ROWS == 0 and W_ROWS % MXU_N == 0 and 0 < w_in_t.shape[0] - main_cols <= LANES
    head_k = dk_total // GLA_HEADS
    head_v = gla_width // GLA_HEADS

    def whole(shape, **kw):
        return pl.BlockSpec(shape, lambda s: (0,) * len(shape), **kw)

    kern = functools.partial(_proj_mix_kernel, n_tiles=n_tiles, tiles_per_seq=nt,
                             conv_width=conv_width, dk_total=dk_total, gla_width=gla_width)
    return pl.pallas_call(
        kern,
        grid=(n_tiles + 1,),
        in_specs=[
            pl.BlockSpec((t, d_model), lambda s: (jnp.minimum(s + 1, n_tiles - 1), 0)),
            pl.BlockSpec(memory_space=pl.ANY),
            whole(norm_g.shape),
            pl.BlockSpec(memory_space=pl.ANY),
            whole(conv_w.shape), whole(conv_b.shape), whole(w_up_pad.shape), whole(b_gate.shape),
            whole(gla_norm_g.shape),
            pl.BlockSpec((wo_rows, w_out.shape[1]), lambda s: (jnp.minimum(s, n_tiles - 1), 0)),
        ],
        out_specs=[
            pl.BlockSpec((t, conv_width), lambda s: (jnp.maximum(s - 1, 0), 0)),
            pl.BlockSpec((t, gla_width), lambda s: (jnp.minimum(s, n_tiles - 1), 0)),
            pl.BlockSpec((wo_rows, w_out.shape[1]), lambda s: (jnp.minimum(s, n_tiles - 1), 0)),
        ],
        out_shape=[jax.ShapeDtypeStruct((batch * seq, conv_width), BF16),
                   jax.ShapeDtypeStruct((batch * seq, gla_width), BF16),
                   jax.ShapeDtypeStruct(w_out.shape, BF16)],
        scratch_shapes=[pltpu.VMEM((main_cols // W_ROWS, d_model, W_ROWS), BF16),
                        pltpu.VMEM((d_model, LANES), BF16),
                        pltpu.VMEM((W_SLOTS, W_ROWS, d_model), F32),
                        pltpu.SemaphoreType.DMA((W_SLOTS,)),
                        pltpu.VMEM((t, d_model), BF16),
                        pltpu.VMEM((t, d_model), BF16),
                        pltpu.VMEM((t, n_cols), BF16),
                        pltpu.VMEM((8, conv_width), F32),
                        pltpu.VMEM((GLA_HEADS, head_v, head_k), F32),
                        pltpu.VMEM((GLA_HEADS, head_v, head_k), BF16),
                        pltpu.VMEM((t, dk_total), F32),
                        pltpu.VMEM((t, 2 * dk_total), BF16),
                        pltpu.VMEM((t + 8, dk_total), F32),
                        pltpu.VMEM((t, dk_total), BF16)],
        compiler_params=pltpu.CompilerParams(
            dimension_semantics=("arbitrary",),
            vmem_limit_bytes=VMEM_LIMIT),
        name="proj_mix",
    )(x2d, x2d, norm_g, w_in_t, conv_w, conv_b, w_up_pad, b_gate, gla_norm_g, w_out)


def _out_proj_kernel(yc_ref, yg_ref, w_ref, x_ref, g_ref, o_ref, *, final_norm):
    kc = yc_ref.shape[1]
    z = (x_ref[...] + jnp.dot(yc_ref[...], w_ref[:kc, :], preferred_element_type=F32)
         + jnp.dot(yg_ref[...], w_ref[kc:, :], preferred_element_type=F32))
    if final_norm:
        ms = jnp.mean(z * z, axis=-1, keepdims=True)
        z = z * lax.rsqrt(ms + EPS) * g_ref[...]
    o_ref[...] = z


def _out_proj(y_conv, y_gla, w_out_bf16, x2d, final_g, *, final_norm):
    m, d = x2d.shape
    return pl.pallas_call(
        functools.partial(_out_proj_kernel, final_norm=final_norm),
        grid=(m // OUT_TM,),
        in_specs=[
            pl.BlockSpec((OUT_TM, y_conv.shape[1]), lambda i: (i, 0)),
            pl.BlockSpec((OUT_TM, y_gla.shape[1]), lambda i: (i, 0)),
            pl.BlockSpec(w_out_bf16.shape, lambda i: (0, 0)),
            pl.BlockSpec((OUT_TM, d), lambda i: (i, 0)),
            pl.BlockSpec((1, d), lambda i: (0, 0)),
        ],
        out_specs=pl.BlockSpec((OUT_TM, d), lambda i: (i, 0)),
        out_shape=jax.ShapeDtypeStruct((m, d), F32),
        compiler_params=pltpu.CompilerParams(
            dimension_semantics=("arbitrary",),
            vmem_limit_bytes=VMEM_LIMIT),
        name="out_proj",
    )(y_conv, y_gla, w_out_bf16, x2d, final_g)


def kernel(x, norm_g, w_in, conv_w, conv_b, gla_w_up, gla_b_gate, gla_norm_g, w_out, final_g):
    batch, seq, d_model = x.shape
    depth = norm_g.shape[0]
    conv_width = conv_w.shape[2]
    rank, dk_total = gla_w_up.shape[1], gla_w_up.shape[2]
    gla_width = gla_norm_g.shape[1] * gla_norm_g.shape[2]
    in_cols = w_in.shape[2]
    main_cols = in_cols - rank
    assert main_cols == 4 * conv_width + 2 * dk_total + 2 * gla_width
    assert main_cols % MXU_N == 0 and dk_total % MXU_N == 0 and rank <= LANES
    assert seq % MIX_T == 0 and MIX_T % CHUNK == 0 and (batch * seq) % OUT_TM == 0
    n_pad = main_cols + LANES

    x2d = x.reshape(batch * seq, d_model)
    for l in range(depth):
        w_up_p = jnp.pad(gla_w_up[l], ((0, LANES - rank), (0, 0))).astype(BF16)
        y_conv, y_gla, w_out_bf16 = _proj_mix(
            x2d, norm_g[l][None, :], w_in[l].T, conv_w[l], conv_b[l][None, :], w_up_p,
            gla_b_gate[l][None, :], gla_norm_g[l], w_out[l], batch=batch, seq=seq,
            conv_width=conv_width, dk_total=dk_total, gla_width=gla_width, n_cols=n_pad)
        x2d = _out_proj(y_conv, y_gla, w_out_bf16, x2d, final_g[None, :],
                        final_norm=(l == depth - 1))
    return x2d.reshape(batch, seq, d_model)
```

```python
import functools

import jax
import jax.numpy as jnp
from jax import lax
from jax.experimental import pallas as pl
from jax.experimental.pallas import tpu as pltpu

F32 = jnp.float32
BF16 = jnp.bfloat16

LANES = 128
MXU_N = 256
EPS = 1e-6
CHUNK = 64
GLA_HEADS = 4
GLA_TAU = 16.0

MIX_T = 256
STRIP = 16
COL_CHUNK = 256
W_ROWS = 256
W_SLOTS = 4
IDLE_TAIL_PIECES = 3
PIN_LAG = 1
OUT_TM = 512
VMEM_LIMIT = 56 * 1024 * 1024


def _silu(v):
    return v * jax.nn.sigmoid(v)


def _log_sigmoid(v):
    return -(jnp.maximum(-v, 0.0) + jnp.log1p(jnp.exp(-jnp.abs(v))))


def _spread(units, n_slots):
    out = [[] for _ in range(n_slots)]
    for j, u in enumerate(units):
        out[(j * n_slots) // len(units)].append(u)
    return out


def _proj_mix_kernel(x_ref, x_hbm, ng_ref, w_hbm, convw_ref, convb_ref, wup_ref, bgate_ref, gng_ref, wo_ref,
                     yc_ref, yg_ref, wo_bf_ref,
                     w_ref, w_tail_ref, stage_ref, dma_sem, h_ref, hn_ref, proj_ref, carry_ref, state_ref, sbf_ref,
                     glog_ref, hilo_ref, rev_ref, kdec_ref,
                     *, n_tiles, tiles_per_seq, conv_width, dk_total, gla_width):
    t_rows = x_ref.shape[0]
    in_cols, d_model = w_hbm.shape
    n_strips = t_rows // STRIP
    n_chunks = t_rows // CHUNK
    head_k = dk_total // GLA_HEADS
    head_v = gla_width // GLA_HEADS
    off_h, off_b, off_c, off_z = (i * conv_width for i in range(4))
    off_q = 4 * conv_width
    off_k = off_q + dk_total
    off_v = off_k + dk_total
    off_r = off_v + gla_width
    off_gd = off_r + gla_width
    s = pl.program_id(0)

    def strip_rows(i):
        return slice(i * STRIP, (i + 1) * STRIP)

    def conv_strip(rows):
        token = None
        for c0 in range(0, conv_width, COL_CHUNK):
            cols = slice(c0, c0 + COL_CHUNK)

            def col(off):
                return proj_ref[rows, off + c0:off + c0 + COL_CHUNK].astype(F32)

            u = col(off_c) * col(off_h)
            ext = jnp.concatenate([carry_ref[:, cols], u], axis=0)
            u1 = pltpu.roll(ext, 1, 0)[8:, :]
            u2 = pltpu.roll(ext, 2, 0)[8:, :]
            carry_ref[:, cols] = u[STRIP - 8:, :]
            conv = (convb_ref[:, cols] + convw_ref[0:1, cols] * u2 + convw_ref[1:2, cols] * u1
                    + convw_ref[2:3, cols] * u)
            y = (col(off_b) * conv * _silu(col(off_z))).astype(yc_ref.dtype)
            yc_ref[rows, cols] = y
            token = token_of(y) if token is None else token | token_of(y)
        return token

    def gate_pre():
        glog_ref[...] = jnp.dot(proj_ref[:, off_gd:off_gd + LANES], wup_ref[...],
                                preferred_element_type=F32)

    def gate_strip(rows):
        glog = _log_sigmoid(glog_ref[rows, :] + bgate_ref[...]) / GLA_TAU
        g_hi = glog.astype(BF16)
        hilo_ref[rows, :dk_total] = g_hi
        hilo_ref[rows, dk_total:] = (glog - g_hi.astype(F32)).astype(BF16)

    def later_sums():
        r_i = lax.broadcasted_iota(jnp.int32, (t_rows + 8, t_rows), 0)
        c_i = lax.broadcasted_iota(jnp.int32, (t_rows + 8, t_rows), 1)
        c_chunk = c_i // CHUNK
        ones = ((c_i > r_i) & (c_chunk == r_i // CHUNK)) | (c_chunk == r_i - t_rows)
        sums = jnp.dot(jnp.where(ones, 1.0, 0.0).astype(BF16), hilo_ref[...],
                       preferred_element_type=F32)
        rev_ref[...] = sums[:, :dk_total] + sums[:, dk_total:]

    def kdec_strip(rows):
        k = proj_ref[rows, off_k:off_k + dk_total].astype(F32)
        kdec_ref[rows, :] = (k * jnp.exp(rev_ref[rows, :])).astype(BF16)

    def state_update(step):
        c, h = divmod(step, GLA_HEADS)
        rows = slice(c * CHUNK, (c + 1) * CHUNK)
        ks = slice(h * head_k, (h + 1) * head_k)
        decay = jnp.exp(rev_ref[t_rows + c:t_rows + c + 1, ks])
        v = proj_ref[rows, off_v + h * head_v:off_v + (h + 1) * head_v]
        inc = lax.dot_general(v, kdec_ref[rows, ks], (((0,), (0,)), ((), ())),
                              preferred_element_type=F32)
        s_new = state_ref[h] * decay + inc
        state_ref[h] = s_new
        sbf_ref[h] = s_new.astype(BF16)

    def readout(step):
        c, h = divmod(step, GLA_HEADS)
        rows = slice(c * CHUNK, (c + 1) * CHUNK)
        q = proj_ref[rows, off_q + h * head_k:off_q + (h + 1) * head_k]
        o = lax.dot_general(q, sbf_ref[h], (((1,), (1,)), ((), ())),
                            preferred_element_type=F32) * (head_k ** -0.5)
        ms = jnp.mean(o * o, axis=-1, keepdims=True)
        o = o * lax.rsqrt(ms + EPS) * gng_ref[h:h + 1, :]
        r = proj_ref[rows, off_r + h * head_v:off_r + (h + 1) * head_v].astype(F32)
        yg_ref[rows, h * head_v:(h + 1) * head_v] = (o * _silu(r)).astype(yg_ref.dtype)

    def token_of(y):
        bits = pltpu.bitcast(y, jnp.uint32)
        token = bits[:, :LANES]
        for l0 in range(LANES, bits.shape[1], LANES):
            token = token | bits[:, l0:l0 + LANES]
        return token

    def norm_strip(src, dst, rows):
        x = src[rows, :]
        ms = jnp.mean(x * x, axis=-1, keepdims=True)
        h = (x * lax.rsqrt(ms + EPS) * ng_ref[...]).astype(BF16)
        dst[rows, :] = h
        return token_of(h)

    def pin(tokens):
        acc = tokens[0]
        for t in tokens[1:]:
            acc = acc | t
        zero = pltpu.bitcast((acc >> 16) >> 16, F32)
        zero = jnp.concatenate([zero] * (STRIP // zero.shape[0]), axis=0)
        tile = h_ref[0:STRIP, 0:LANES].astype(F32)
        h_ref[0:STRIP, 0:LANES] = (tile + zero).astype(BF16)

    def proj_piece(c0, width):
        def piece():
            if c0 < w_ref.shape[0] * W_ROWS:
                w = w_ref[c0 // W_ROWS, :, c0 % W_ROWS:c0 % W_ROWS + width]
            else:
                w = w_tail_ref[...]
            proj_ref[:, c0:c0 + width] = jnp.dot(h_ref[...], w,
                                                 preferred_element_type=F32).astype(BF16)
        return piece

    def pieces(off, width):
        return [proj_piece(c0, min(MXU_N, off + width - c0)) for c0 in range(off, off + width, MXU_N)]

    @pl.when(s == 0)
    def _():
        proj_ref[:, :4 * conv_width] = jnp.zeros((t_rows, 4 * conv_width), BF16)

        n_full = in_cols // W_ROWS

        def chunk_copy(i, rows=W_ROWS):
            slot = i % W_SLOTS
            return pltpu.make_async_copy(w_hbm.at[pl.ds(i * W_ROWS, rows), :],
                                         stage_ref.at[slot, pl.ds(0, rows), :], dma_sem.at[slot])

        def transpose_chunk(slot, rows, dst):
            for k0 in range(0, d_model, MXU_N):
                blk = stage_ref[slot, 0:rows, k0:k0 + MXU_N]
                dst[k0:k0 + MXU_N, :] = blk.T.astype(BF16)

        for i in range(W_SLOTS - 1):
            chunk_copy(i).start()

        def load_chunk(i, c):
            @pl.when(i + W_SLOTS - 1 < n_full)
            def _():
                chunk_copy(i + W_SLOTS - 1).start()
            chunk_copy(i).wait()
            transpose_chunk(i % W_SLOTS, W_ROWS, w_ref.at[i])
            return c

        lax.fori_loop(0, n_full, load_chunk, 0)

        tail_rows = in_cols - n_full * W_ROWS
        stage_ref[n_full % W_SLOTS, tail_rows:LANES, :] = jnp.zeros((LANES - tail_rows, d_model), F32)
        chunk_copy(n_full, tail_rows).start()
        chunk_copy(n_full, tail_rows).wait()
        transpose_chunk(n_full % W_SLOTS, LANES, w_tail_ref)

        x0_copy = pltpu.make_async_copy(x_hbm.at[pl.ds(0, t_rows), :],
                                        stage_ref.at[0, pl.ds(0, t_rows), :], dma_sem.at[0])
        x0_copy.start()
        x0_copy.wait()

        def norm_first(i, c):
            norm_strip(stage_ref.at[0], hn_ref, pl.ds(pl.multiple_of(i * STRIP, STRIP), STRIP))
            return c

        lax.fori_loop(0, n_strips, norm_first, 0, unroll=2)

    @pl.when((s == 0) | ((s + tiles_per_seq - 1) % tiles_per_seq == 0))
    def _():
        carry_ref[...] = jnp.zeros_like(carry_ref)

    @pl.when(s < n_tiles)
    def _():
        @pl.when(s % tiles_per_seq == 0)
        def _():
            state_ref[...] = jnp.zeros_like(state_ref)

        wo_bf_ref[...] = wo_ref[...].astype(BF16)

        h_ref[...] = hn_ref[...]

        norm_units = [functools.partial(norm_strip, x_ref, hn_ref, strip_rows(i))
                      for i in range(n_strips)]
        conv_units = [functools.partial(conv_strip, strip_rows(i)) for i in range(n_strips)]
        gate_units = [functools.partial(gate_strip, strip_rows(i)) for i in range(n_strips)]
        kdec_units = [functools.partial(kdec_strip, strip_rows(i)) for i in range(n_strips)]
        n_steps = n_chunks * GLA_HEADS

        mxu = (pieces(off_gd, LANES) + pieces(off_k, dk_total) + pieces(off_q, dk_total)
               + pieces(off_v, gla_width) + pieces(off_r, gla_width))
        n_k = dk_total // MXU_N
        vpu = [conv_units[:2]]
        vpu += _spread([gate_pre] + gate_units, n_k)
        vpu += [[later_sums, conv_units[2]]] + [kdec_units + [conv_units[3]]]
        vpu += [[] for _ in range(dk_total // MXU_N - 2)]
        rest = _spread(conv_units[4:], len(mxu) - len(vpu))
        rest[-1].append(functools.partial(state_update, 0))
        vpu += rest
        assert len(vpu) == len(mxu)

        mxu2 = pieces(off_h, 4 * conv_width)
        halves = []
        for step in range(n_steps):
            if step + 1 < n_steps:
                halves.append(functools.partial(state_update, step + 1))
            halves.append(functools.partial(readout, step))
        busy = len(mxu2) - IDLE_TAIL_PIECES
        vpu2 = [a + b for a, b in zip(_spread(halves, busy), _spread(norm_units, busy))]
        vpu2 += [[] for _ in range(IDLE_TAIL_PIECES)]

        tokens = []
        for piece, units in zip(mxu + mxu2, vpu + vpu2):
            if len(tokens) >= PIN_LAG and tokens[-PIN_LAG]:
                pin(tokens[-PIN_LAG])
            piece()
            tokens.append([t for t in [unit() for unit in units] if t is not None])

    @pl.when(s == n_tiles)
    def _():
        def step(i, c):
            conv_strip(pl.ds(pl.multiple_of(i * STRIP, STRIP), STRIP))
            return c
        lax.fori_loop(0, n_strips, step, 0, unroll=2)


def _proj_mix(x2d, norm_g, w_in_t, conv_w, conv_b, w_up_pad, b_gate, gla_norm_g, w_out, *,
              batch, seq, conv_width, dk_total, gla_width, n_cols):
    t = MIX_T
    nt = seq // t
    n_tiles = batch * nt
    d_model = x2d.shape[1]
    wo_rows = w_out.shape[0] // n_tiles
    assert w_out.shape[0] % n_tiles == 0 and wo_rows % STRIP == 0 and t <= W_ROWS
    main_cols = n_cols - LANES
    assert main_cols % W_ROWS == 0 and W_ROWS % MXU_N == 0 and 0 < w_in_t.shape[0] - main_cols <= LANES
    head_k = dk_total // GLA_HEADS
    head_v = gla_width // GLA_HEADS

    def whole(shape, **kw):
        return pl.BlockSpec(shape, lambda s: (0,) * len(shape), **kw)

    kern = functools.partial(_proj_mix_kernel, n_tiles=n_tiles, tiles_per_seq=nt,
                             conv_width=conv_width, dk_total=dk_total, gla_width=gla_width)
    return pl.pallas_call(
        kern,
        grid=(n_tiles + 1,),
        in_specs=[
            pl.BlockSpec((t, d_model), lambda s: (jnp.minimum(s + 1, n_tiles - 1), 0)),
            pl.BlockSpec(memory_space=pl.ANY),
            whole(norm_g.shape),
            pl.BlockSpec(memory_space=pl.ANY),
            whole(conv_w.shape), whole(conv_b.shape), whole(w_up_pad.shape), whole(b_gate.shape),
            whole(gla_norm_g.shape),
            pl.BlockSpec((wo_rows, w_out.shape[1]), lambda s: (jnp.minimum(s, n_tiles - 1), 0)),
        ],
        out_specs=[
            pl.BlockSpec((t, conv_width), lambda s: (jnp.maximum(s - 1, 0), 0)),
            pl.BlockSpec((t, gla_width), lambda s: (jnp.minimum(s, n_tiles - 1), 0)),
            pl.BlockSpec((wo_rows, w_out.shape[1]), lambda s: (jnp.minimum(s, n_tiles - 1), 0)),
        ],
        out_shape=[jax.ShapeDtypeStruct((batch * seq, conv_width), BF16),
                   jax.ShapeDtypeStruct((batch * seq, gla_width), BF16),
                   jax.ShapeDtypeStruct(w_out.shape, BF16)],
        scratch_shapes=[pltpu.VMEM((main_cols // W_ROWS, d_model, W_ROWS), BF16),
                        pltpu.VMEM((d_model, LANES), BF16),
                        pltpu.VMEM((W_SLOTS, W_ROWS, d_model), F32),
                        pltpu.SemaphoreType.DMA((W_SLOTS,)),
                        pltpu.VMEM((t, d_model), BF16),
                        pltpu.VMEM((t, d_model), BF16),
                        pltpu.VMEM((t, n_cols), BF16),
                        pltpu.VMEM((8, conv_width), F32),
                        pltpu.VMEM((GLA_HEADS, head_v, head_k), F32),
                        pltpu.VMEM((GLA_HEADS, head_v, head_k), BF16),
                        pltpu.VMEM((t, dk_total), F32),
                        pltpu.VMEM((t, 2 * dk_total), BF16),
                        pltpu.VMEM((t + 8, dk_total), F32),
                        pltpu.VMEM((t, dk_total), BF16)],
        compiler_params=pltpu.CompilerParams(
            dimension_semantics=("arbitrary",),
            vmem_limit_bytes=VMEM_LIMIT),
        name="proj_mix",
    )(x2d, x2d, norm_g, w_in_t, conv_w, conv_b, w_up_pad, b_gate, gla_norm_g, w_out)


def _out_proj_kernel(yc_ref, yg_ref, w_ref, x_ref, g_ref, o_ref, *, final_norm):
    kc = yc_ref.shape[1]
    z = (x_ref[...] + jnp.dot(yc_ref[...], w_ref[:kc, :], preferred_element_type=F32)
         + jnp.dot(yg_ref[...], w_ref[kc:, :], preferred_element_type=F32))
    if final_norm:
        ms = jnp.mean(z * z, axis=-1, keepdims=True)
        z = z * lax.rsqrt(ms + EPS) * g_ref[...]
    o_ref[...] = z


def _out_proj(y_conv, y_gla, w_out_bf16, x2d, final_g, *, final_norm):
    m, d = x2d.shape
    return pl.pallas_call(
        functools.partial(_out_proj_kernel, final_norm=final_norm),
        grid=(m // OUT_TM,),
        in_specs=[
            pl.BlockSpec((OUT_TM, y_conv.shape[1]), lambda i: (i, 0)),
            pl.BlockSpec((OUT_TM, y_gla.shape[1]), lambda i: (i, 0)),
            pl.BlockSpec(w_out_bf16.shape, lambda i: (0, 0)),
            pl.BlockSpec((OUT_TM, d), lambda i: (i, 0)),
            pl.BlockSpec((1, d), lambda i: (0, 0)),
        ],
        out_specs=pl.BlockSpec((OUT_TM, d), lambda i: (i, 0)),
        out_shape=jax.ShapeDtypeStruct((m, d), F32),
        compiler_params=pltpu.CompilerParams(
            dimension_semantics=("arbitrary",),
            vmem_limit_bytes=VMEM_LIMIT),
        name="out_proj",
    )(y_conv, y_gla, w_out_bf16, x2d, final_g)


def kernel(x, norm_g, w_in, conv_w, conv_b, gla_w_up, gla_b_gate, gla_norm_g, w_out, final_g):
    batch, seq, d_model = x.shape
    depth = norm_g.shape[0]
    conv_width = conv_w.shape[2]
    rank, dk_total = gla_w_up.shape[1], gla_w_up.shape[2]
    gla_width = gla_norm_g.shape[1] * gla_norm_g.shape[2]
    in_cols = w_in.shape[2]
    main_cols = in_cols - rank
    assert main_cols == 4 * conv_width + 2 * dk_total + 2 * gla_width
    assert main_cols % MXU_N == 0 and dk_total % MXU_N == 0 and rank <= LANES
    assert seq % MIX_T == 0 and MIX_T % CHUNK == 0 and (batch * seq) % OUT_TM == 0
    n_pad = main_cols + LANES

    x2d = x.reshape(batch * seq, d_model)
    for l in range(depth):
        w_up_p = jnp.pad(gla_w_up[l], ((0, LANES - rank), (0, 0))).astype(BF16)
        y_conv, y_gla, w_out_bf16 = _proj_mix(
            x2d, norm_g[l][None, :], w_in[l].T, conv_w[l], conv_b[l][None, :], w_up_p,
            gla_b_gate[l][None, :], gla_norm_g[l], w_out[l], batch=batch, seq=seq,
            conv_width=conv_width, dk_total=dk_total, gla_width=gla_width, n_cols=n_pad)
        x2d = _out_proj(y_conv, y_gla, w_out_bf16, x2d, final_g[None, :],
                        final_norm=(l == depth - 1))
    return x2d.reshape(batch, seq, d_model)
```

```python
import functools

import jax
import jax.numpy as jnp
from jax import lax
from jax.experimental import pallas as pl
from jax.experimental.pallas import tpu as pltpu

F32 = jnp.float32
BF16 = jnp.bfloat16

LANES = 128
MXU_N = 256
EPS = 1e-6
CHUNK = 64
GLA_HEADS = 4
GLA_TAU = 16.0

MIX_T = 256
STRIP = 16
COL_CHUNK = 256
W_ROWS = 256
W_SLOTS = 4
IDLE_TAIL_PIECES = 2
PIN_LAG = 1
OUT_TM = 512
VMEM_LIMIT = 56 * 1024 * 1024


def _silu(v):
    return v * jax.nn.sigmoid(v)


def _log_sigmoid(v):
    return -(jnp.maximum(-v, 0.0) + jnp.log1p(jnp.exp(-jnp.abs(v))))


def _spread(units, n_slots):
    out = [[] for _ in range(n_slots)]
    for j, u in enumerate(units):
        out[(j * n_slots) // len(units)].append(u)
    return out


def _proj_mix_kernel(x_ref, x_hbm, ng_ref, w_hbm, convw_ref, convb_ref, wup_ref, bgate_ref, gng_ref, wo_ref,
                     yc_ref, yg_ref, wo_bf_ref,
                     w_ref, w_tail_ref, stage_ref, dma_sem, h_ref, hn_ref, proj_ref, carry_ref, state_ref, sbf_ref,
                     glog_ref, hilo_ref, rev_ref, kdec_ref,
                     *, n_tiles, tiles_per_seq, conv_width, dk_total, gla_width):
    t_rows = x_ref.shape[0]
    in_cols, d_model = w_hbm.shape
    n_strips = t_rows // STRIP
    n_chunks = t_rows // CHUNK
    head_k = dk_total // GLA_HEADS
    head_v = gla_width // GLA_HEADS
    off_h, off_b, off_c, off_z = (i * conv_width for i in range(4))
    off_q = 4 * conv_width
    off_k = off_q + dk_total
    off_v = off_k + dk_total
    off_r = off_v + gla_width
    off_gd = off_r + gla_width
    s = pl.program_id(0)

    def strip_rows(i):
        return slice(i * STRIP, (i + 1) * STRIP)

    def conv_strip(rows):
        token = None
        for c0 in range(0, conv_width, COL_CHUNK):
            cols = slice(c0, c0 + COL_CHUNK)

            def col(off):
                return proj_ref[rows, off + c0:off + c0 + COL_CHUNK].astype(F32)

            u = col(off_c) * col(off_h)
            ext = jnp.concatenate([carry_ref[:, cols], u], axis=0)
            u1 = pltpu.roll(ext, 1, 0)[8:, :]
            u2 = pltpu.roll(ext, 2, 0)[8:, :]
            carry_ref[:, cols] = u[STRIP - 8:, :]
            conv = (convb_ref[:, cols] + convw_ref[0:1, cols] * u2 + convw_ref[1:2, cols] * u1
                    + convw_ref[2:3, cols] * u)
            y = (col(off_b) * conv * _silu(col(off_z))).astype(yc_ref.dtype)
            yc_ref[rows, cols] = y
            token = token_of(y) if token is None else token | token_of(y)
        return token

    def gate_pre():
        glog_ref[...] = jnp.dot(proj_ref[:, off_gd:off_gd + LANES], wup_ref[...],
                                preferred_element_type=F32)

    def gate_strip(rows):
        glog = _log_sigmoid(glog_ref[rows, :] + bgate_ref[...]) / GLA_TAU
        g_hi = glog.astype(BF16)
        hilo_ref[rows, :dk_total] = g_hi
        hilo_ref[rows, dk_total:] = (glog - g_hi.astype(F32)).astype(BF16)

    def later_sums():
        r_i = lax.broadcasted_iota(jnp.int32, (t_rows + 8, t_rows), 0)
        c_i = lax.broadcasted_iota(jnp.int32, (t_rows + 8, t_rows), 1)
        c_chunk = c_i // CHUNK
        ones = ((c_i > r_i) & (c_chunk == r_i // CHUNK)) | (c_chunk == r_i - t_rows)
        sums = jnp.dot(jnp.where(ones, 1.0, 0.0).astype(BF16), hilo_ref[...],
                       preferred_element_type=F32)
        rev_ref[...] = sums[:, :dk_total] + sums[:, dk_total:]

    def kdec_strip(rows):
        k = proj_ref[rows, off_k:off_k + dk_total].astype(F32)
        kdec_ref[rows, :] = (k * jnp.exp(rev_ref[rows, :])).astype(BF16)

    def state_update(step):
        c, h = divmod(step, GLA_HEADS)
        rows = slice(c * CHUNK, (c + 1) * CHUNK)
        ks = slice(h * head_k, (h + 1) * head_k)
        decay = jnp.exp(rev_ref[t_rows + c:t_rows + c + 1, ks])
        v = proj_ref[rows, off_v + h * head_v:off_v + (h + 1) * head_v]
        inc = lax.dot_general(v, kdec_ref[rows, ks], (((0,), (0,)), ((), ())),
                              preferred_element_type=F32)
        s_new = state_ref[h] * decay + inc
        state_ref[h] = s_new
        sbf_ref[h] = s_new.astype(BF16)

    def readout(step):
        c, h = divmod(step, GLA_HEADS)
        rows = slice(c * CHUNK, (c + 1) * CHUNK)
        q = proj_ref[rows, off_q + h * head_k:off_q + (h + 1) * head_k]
        o = lax.dot_general(q, sbf_ref[h], (((1,), (1,)), ((), ())),
                            preferred_element_type=F32) * (head_k ** -0.5)
        ms = jnp.mean(o * o, axis=-1, keepdims=True)
        o = o * lax.rsqrt(ms + EPS) * gng_ref[h:h + 1, :]
        r = proj_ref[rows, off_r + h * head_v:off_r + (h + 1) * head_v].astype(F32)
        yg_ref[rows, h * head_v:(h + 1) * head_v] = (o * _silu(r)).astype(yg_ref.dtype)

    def token_of(y):
        bits = pltpu.bitcast(y, jnp.uint32)
        token = bits[:, :LANES]
        for l0 in range(LANES, bits.shape[1], LANES):
            token = token | bits[:, l0:l0 + LANES]
        return token

    def norm_strip(src, dst, rows):
        x = src[rows, :]
        ms = jnp.mean(x * x, axis=-1, keepdims=True)
        h = (x * lax.rsqrt(ms + EPS) * ng_ref[...]).astype(BF16)
        dst[rows, :] = h
        return token_of(h)

    def pin(tokens):
        acc = tokens[0]
        for t in tokens[1:]:
            acc = acc | t
        zero = pltpu.bitcast((acc >> 16) >> 16, F32)
        zero = jnp.concatenate([zero] * (STRIP // zero.shape[0]), axis=0)
        tile = h_ref[0:STRIP, 0:LANES].astype(F32)
        h_ref[0:STRIP, 0:LANES] = (tile + zero).astype(BF16)

    def proj_piece(c0, width):
        def piece():
            if c0 < w_ref.shape[0] * W_ROWS:
                w = w_ref[c0 // W_ROWS, :, c0 % W_ROWS:c0 % W_ROWS + width]
            else:
                w = w_tail_ref[...]
            proj_ref[:, c0:c0 + width] = jnp.dot(h_ref[...], w,
                                                 preferred_element_type=F32).astype(BF16)
        return piece

    def pieces(off, width):
        return [proj_piece(c0, min(MXU_N, off + width - c0)) for c0 in range(off, off + width, MXU_N)]

    @pl.when(s == 0)
    def _():
        proj_ref[:, :4 * conv_width] = jnp.zeros((t_rows, 4 * conv_width), BF16)

        n_full = in_cols // W_ROWS

        def chunk_copy(i, rows=W_ROWS):
            slot = i % W_SLOTS
            return pltpu.make_async_copy(w_hbm.at[pl.ds(i * W_ROWS, rows), :],
                                         stage_ref.at[slot, pl.ds(0, rows), :], dma_sem.at[slot])

        def transpose_chunk(slot, rows, dst):
            for k0 in range(0, d_model, MXU_N):
                blk = stage_ref[slot, 0:rows, k0:k0 + MXU_N]
                dst[k0:k0 + MXU_N, :] = blk.T.astype(BF16)

        for i in range(W_SLOTS - 1):
            chunk_copy(i).start()

        def load_chunk(i, c):
            @pl.when(i + W_SLOTS - 1 < n_full)
            def _():
                chunk_copy(i + W_SLOTS - 1).start()
            chunk_copy(i).wait()
            transpose_chunk(i % W_SLOTS, W_ROWS, w_ref.at[i])
            return c

        lax.fori_loop(0, n_full, load_chunk, 0)

        tail_rows = in_cols - n_full * W_ROWS
        stage_ref[n_full % W_SLOTS, tail_rows:LANES, :] = jnp.zeros((LANES - tail_rows, d_model), F32)
        chunk_copy(n_full, tail_rows).start()
        chunk_copy(n_full, tail_rows).wait()
        transpose_chunk(n_full % W_SLOTS, LANES, w_tail_ref)

        x0_copy = pltpu.make_async_copy(x_hbm.at[pl.ds(0, t_rows), :],
                                        stage_ref.at[0, pl.ds(0, t_rows), :], dma_sem.at[0])
        x0_copy.start()
        x0_copy.wait()

        def norm_first(i, c):
            norm_strip(stage_ref.at[0], hn_ref, pl.ds(pl.multiple_of(i * STRIP, STRIP), STRIP))
            return c

        lax.fori_loop(0, n_strips, norm_first, 0, unroll=2)

    @pl.when((s == 0) | ((s + tiles_per_seq - 1) % tiles_per_seq == 0))
    def _():
        carry_ref[...] = jnp.zeros_like(carry_ref)

    @pl.when(s < n_tiles)
    def _():
        @pl.when(s % tiles_per_seq == 0)
        def _():
            state_ref[...] = jnp.zeros_like(state_ref)

        wo_bf_ref[...] = wo_ref[...].astype(BF16)

        h_ref[...] = hn_ref[...]

        norm_units = [functools.partial(norm_strip, x_ref, hn_ref, strip_rows(i))
                      for i in range(n_strips)]
        conv_units = [functools.partial(conv_strip, strip_rows(i)) for i in range(n_strips)]
        gate_units = [functools.partial(gate_strip, strip_rows(i)) for i in range(n_strips)]
        kdec_units = [functools.partial(kdec_strip, strip_rows(i)) for i in range(n_strips)]
        n_steps = n_chunks * GLA_HEADS

        mxu = (pieces(off_gd, LANES) + pieces(off_k, dk_total) + pieces(off_q, dk_total)
               + pieces(off_v, gla_width) + pieces(off_r, gla_width))
        third, half_n = (n_strips + 2) // 3, n_strips // 2
        vpu = [conv_units[:2],
               [gate_pre] + gate_units[:third],
               gate_units[third:2 * third],
               gate_units[2 * third:],
               [later_sums, conv_units[2]],
               kdec_units[:half_n] + [conv_units[3]],
               kdec_units[half_n:] + [conv_units[4]]]
        rest = _spread(conv_units[5:], len(mxu) - len(vpu))
        rest[-1].append(functools.partial(state_update, 0))
        vpu += rest
        assert len(vpu) == len(mxu)

        mxu2 = pieces(off_h, 4 * conv_width)
        halves = []
        for step in range(n_steps):
            if step + 1 < n_steps:
                halves.append(functools.partial(state_update, step + 1))
            halves.append(functools.partial(readout, step))
        busy = len(mxu2) - IDLE_TAIL_PIECES
        vpu2 = [a + b for a, b in zip(_spread(halves, busy), _spread(norm_units, busy))]
        vpu2 += [[] for _ in range(IDLE_TAIL_PIECES)]

        tokens = []
        for piece, units in zip(mxu + mxu2, vpu + vpu2):
            if len(tokens) >= PIN_LAG and tokens[-PIN_LAG]:
                pin(tokens[-PIN_LAG])
            piece()
            tokens.append([t for t in [unit() for unit in units] if t is not None])

    @pl.when(s == n_tiles)
    def _():
        def step(i, c):
            conv_strip(pl.ds(pl.multiple_of(i * STRIP, STRIP), STRIP))
            return c
        lax.fori_loop(0, n_strips, step, 0, unroll=2)


def _proj_mix(x2d, norm_g, w_in_t, conv_w, conv_b, w_up_pad, b_gate, gla_norm_g, w_out, *,
              batch, seq, conv_width, dk_total, gla_width, n_cols):
    t = MIX_T
    nt = seq // t
    n_tiles = batch * nt
    d_model = x2d.shape[1]
    wo_rows = w_out.shape[0] // n_tiles
    assert w_out.shape[0] % n_tiles == 0 and wo_rows % STRIP == 0 and t <= W_ROWS
    main_cols = n_cols - LANES
    assert main_cols % W_ROWS == 0 and W_ROWS % MXU_N == 0 and 0 < w_in_t.shape[0] - main_cols <= LANES
    head_k = dk_total // GLA_HEADS
    head_v = gla_width // GLA_HEADS

    def whole(shape, **kw):
        return pl.BlockSpec(shape, lambda s: (0,) * len(shape), **kw)

    kern = functools.partial(_proj_mix_kernel, n_tiles=n_tiles, tiles_per_seq=nt,
                             conv_width=conv_width, dk_total=dk_total, gla_width=gla_width)
    return pl.pallas_call(
        kern,
        grid=(n_tiles + 1,),
        in_specs=[
            pl.BlockSpec((t, d_model), lambda s: (jnp.minimum(s + 1, n_tiles - 1), 0)),
            pl.BlockSpec(memory_space=pl.ANY),
            whole(norm_g.shape),
            pl.BlockSpec(memory_space=pl.ANY),
            whole(conv_w.shape), whole(conv_b.shape), whole(w_up_pad.shape), whole(b_gate.shape),
            whole(gla_norm_g.shape),
            pl.BlockSpec((wo_rows, w_out.shape[1]), lambda s: (jnp.minimum(s, n_tiles - 1), 0)),
        ],
        out_specs=[
            pl.BlockSpec((t, conv_width), lambda s: (jnp.maximum(s - 1, 0), 0)),
            pl.BlockSpec((t, gla_width), lambda s: (jnp.minimum(s, n_tiles - 1), 0)),
            pl.BlockSpec((wo_rows, w_out.shape[1]), lambda s: (jnp.minimum(s, n_tiles - 1), 0)),
        ],
        out_shape=[jax.ShapeDtypeStruct((batch * seq, conv_width), BF16),
                   jax.ShapeDtypeStruct((batch * seq, gla_width), BF16),
                   jax.ShapeDtypeStruct(w_out.shape, BF16)],
        scratch_shapes=[pltpu.VMEM((main_cols // W_ROWS, d_model, W_ROWS), BF16),
                        pltpu.VMEM((d_model, LANES), BF16),
                        pltpu.VMEM((W_SLOTS, W_ROWS, d_model), F32),
                        pltpu.SemaphoreType.DMA((W_SLOTS,)),
                        pltpu.VMEM((t, d_model), BF16),
                        pltpu.VMEM((t, d_model), BF16),
                        pltpu.VMEM((t, n_cols), BF16),
                        pltpu.VMEM((8, conv_width), F32),
                        pltpu.VMEM((GLA_HEADS, head_v, head_k), F32),
                        pltpu.VMEM((GLA_HEADS, head_v, head_k), BF16),
                        pltpu.VMEM((t, dk_total), F32),
                        pltpu.VMEM((t, 2 * dk_total), BF16),
                        pltpu.VMEM((t + 8, dk_total), F32),
                        pltpu.VMEM((t, dk_total), BF16)],
        compiler_params=pltpu.CompilerParams(
            dimension_semantics=("arbitrary",),
            vmem_limit_bytes=VMEM_LIMIT),
        name="proj_mix",
    )(x2d, x2d, norm_g, w_in_t, conv_w, conv_b, w_up_pad, b_gate, gla_norm_g, w_out)


def _out_proj_kernel(yc_ref, yg_ref, w_ref, x_ref, g_ref, o_ref, *, final_norm):
    kc = yc_ref.shape[1]
    z = (x_ref[...] + jnp.dot(yc_ref[...], w_ref[:kc, :], preferred_element_type=F32)
         + jnp.dot(yg_ref[...], w_ref[kc:, :], preferred_element_type=F32))
    if final_norm:
        ms = jnp.mean(z * z, axis=-1, keepdims=True)
        z = z * lax.rsqrt(ms + EPS) * g_ref[...]
    o_ref[...] = z


def _out_proj(y_conv, y_gla, w_out_bf16, x2d, final_g, *, final_norm):
    m, d = x2d.shape
    return pl.pallas_call(
        functools.partial(_out_proj_kernel, final_norm=final_norm),
        grid=(m // OUT_TM,),
        in_specs=[
            pl.BlockSpec((OUT_TM, y_conv.shape[1]), lambda i: (i, 0)),
            pl.BlockSpec((OUT_TM, y_gla.shape[1]), lambda i: (i, 0)),
            pl.BlockSpec(w_out_bf16.shape, lambda i: (0, 0)),
            pl.BlockSpec((OUT_TM, d), lambda i: (i, 0)),
            pl.BlockSpec((1, d), lambda i: (0, 0)),
        ],
        out_specs=pl.BlockSpec((OUT_TM, d), lambda i: (i, 0)),
        out_shape=jax.ShapeDtypeStruct((m, d), F32),
        compiler_params=pltpu.CompilerParams(
            dimension_semantics=("arbitrary",),
            vmem_limit_bytes=VMEM_LIMIT),
        name="out_proj",
    )(y_conv, y_gla, w_out_bf16, x2d, final_g)


def kernel(x, norm_g, w_in, conv_w, conv_b, gla_w_up, gla_b_gate, gla_norm_g, w_out, final_g):
    batch, seq, d_model = x.shape
    depth = norm_g.shape[0]
    conv_width = conv_w.shape[2]
    rank, dk_total = gla_w_up.shape[1], gla_w_up.shape[2]
    gla_width = gla_norm_g.shape[1] * gla_norm_g.shape[2]
    in_cols = w_in.shape[2]
    main_cols = in_cols - rank
    assert main_cols == 4 * conv_width + 2 * dk_total + 2 * gla_width
    assert main_cols % MXU_N == 0 and dk_total % MXU_N == 0 and rank <= LANES
    assert seq % MIX_T == 0 and MIX_T % CHUNK == 0 and (batch * seq) % OUT_TM == 0
    n_pad = main_cols + LANES

    x2d = x.reshape(batch * seq, d_model)
    for l in range(depth):
        w_up_p = jnp.pad(gla_w_up[l], ((0, LANES - rank), (0, 0))).astype(BF16)
        y_conv, y_gla, w_out_bf16 = _proj_mix(
            x2d, norm_g[l][None, :], w_in[l].T, conv_w[l], conv_b[l][None, :], w_up_p,
            gla_b_gate[l][None, :], gla_norm_g[l], w_out[l], batch=batch, seq=seq,
            conv_width=conv_width, dk_total=dk_total, gla_width=gla_width, n_cols=n_pad)
        x2d = _out_proj(y_conv, y_gla, w_out_bf16, x2d, final_g[None, :],
                        final_norm=(l == depth - 1))
    return x2d.reshape(batch, seq, d_model)
```

```python
import functools

import jax
import jax.numpy as jnp
from jax import lax
from jax.experimental import pallas as pl
from jax.experimental.pallas import tpu as pltpu

F32 = jnp.float32
BF16 = jnp.bfloat16

LANES = 128
MXU_N = 256
EPS = 1e-6
CHUNK = 64
GLA_HEADS = 4
GLA_TAU = 16.0

MIX_T = 256
STRIP = 16
COL_CHUNK = 256
W_ROWS = 256
W_SLOTS = 4
IDLE_TAIL_PIECES = 2
PIN_LAG = 1
OUT_TM = 512
VMEM_LIMIT = 56 * 1024 * 1024


def _silu(v):
    return v * jax.nn.sigmoid(v)


def _log_sigmoid(v):
    return -(jnp.maximum(-v, 0.0) + jnp.log1p(jnp.exp(-jnp.abs(v))))


def _spread(units, n_slots):
    out = [[] for _ in range(n_slots)]
    for j, u in enumerate(units):
        out[(j * n_slots) // len(units)].append(u)
    return out


def _proj_mix_kernel(x_ref, x_hbm, ng_ref, w_hbm, convw_ref, convb_ref, wup_ref, bgate_ref, gng_ref, wo_ref,
                     yc_ref, yg_ref, wo_bf_ref,
                     w_ref, w_tail_ref, stage_ref, dma_sem, h_ref, hn_ref, proj_ref, carry_ref, state_ref, sbf_ref,
                     glog_ref, hilo_ref, rev_ref, kdec_ref,
                     *, n_tiles, tiles_per_seq, conv_width, dk_total, gla_width):
    t_rows = x_ref.shape[0]
    in_cols, d_model = w_hbm.shape
    n_strips = t_rows // STRIP
    n_chunks = t_rows // CHUNK
    head_k = dk_total // GLA_HEADS
    head_v = gla_width // GLA_HEADS
    off_h, off_b, off_c, off_z = (i * conv_width for i in range(4))
    off_q = 4 * conv_width
    off_k = off_q + dk_total
    off_v = off_k + dk_total
    off_r = off_v + gla_width
    off_gd = off_r + gla_width
    s = pl.program_id(0)

    def strip_rows(i):
        return slice(i * STRIP, (i + 1) * STRIP)

    def conv_strip(rows):
        token = None
        for c0 in range(0, conv_width, COL_CHUNK):
            cols = slice(c0, c0 + COL_CHUNK)

            def col(off):
                return proj_ref[rows, off + c0:off + c0 + COL_CHUNK].astype(F32)

            u = col(off_c) * col(off_h)
            ext = jnp.concatenate([carry_ref[:, cols], u], axis=0)
            u1 = pltpu.roll(ext, 1, 0)[8:, :]
            u2 = pltpu.roll(ext, 2, 0)[8:, :]
            carry_ref[:, cols] = u[STRIP - 8:, :]
            conv = (convb_ref[:, cols] + convw_ref[0:1, cols] * u2 + convw_ref[1:2, cols] * u1
                    + convw_ref[2:3, cols] * u)
            y = (col(off_b) * conv * _silu(col(off_z))).astype(yc_ref.dtype)
            yc_ref[rows, cols] = y
            token = token_of(y) if token is None else token | token_of(y)
        return token

    def gate_pre():
        glog_ref[...] = jnp.dot(proj_ref[:, off_gd:off_gd + LANES], wup_ref[...],
                                preferred_element_type=F32)

    def gate_strip(rows):
        glog = _log_sigmoid(glog_ref[rows, :] + bgate_ref[...]) / GLA_TAU
        g_hi = glog.astype(BF16)
        hilo_ref[rows, :dk_total] = g_hi
        hilo_ref[rows, dk_total:] = (glog - g_hi.astype(F32)).astype(BF16)

    def later_sums():
        r_i = lax.broadcasted_iota(jnp.int32, (t_rows + 8, t_rows), 0)
        c_i = lax.broadcasted_iota(jnp.int32, (t_rows + 8, t_rows), 1)
        c_chunk = c_i // CHUNK
        ones = ((c_i > r_i) & (c_chunk == r_i // CHUNK)) | (c_chunk == r_i - t_rows)
        sums = jnp.dot(jnp.where(ones, 1.0, 0.0).astype(BF16), hilo_ref[...],
                       preferred_element_type=F32)
        rev_ref[...] = sums[:, :dk_total] + sums[:, dk_total:]

    def kdec_strip(rows):
        k = proj_ref[rows, off_k:off_k + dk_total].astype(F32)
        kdec_ref[rows, :] = (k * jnp.exp(rev_ref[rows, :])).astype(BF16)

    def state_update(step):
        c, h = divmod(step, GLA_HEADS)
        rows = slice(c * CHUNK, (c + 1) * CHUNK)
        ks = slice(h * head_k, (h + 1) * head_k)
        decay = jnp.exp(rev_ref[t_rows + c:t_rows + c + 1, ks])
        v = proj_ref[rows, off_v + h * head_v:off_v + (h + 1) * head_v]
        inc = lax.dot_general(v, kdec_ref[rows, ks], (((0,), (0,)), ((), ())),
                              preferred_element_type=F32)
        s_new = state_ref[h] * decay + inc
        state_ref[h] = s_new
        sbf_ref[h] = s_new.astype(BF16)

    def readout(step):
        c, h = divmod(step, GLA_HEADS)
        rows = slice(c * CHUNK, (c + 1) * CHUNK)
        q = proj_ref[rows, off_q + h * head_k:off_q + (h + 1) * head_k]
        o = lax.dot_general(q, sbf_ref[h], (((1,), (1,)), ((), ())),
                            preferred_element_type=F32) * (head_k ** -0.5)
        ms = jnp.mean(o * o, axis=-1, keepdims=True)
        o = o * lax.rsqrt(ms + EPS) * gng_ref[h:h + 1, :]
        r = proj_ref[rows, off_r + h * head_v:off_r + (h + 1) * head_v].astype(F32)
        yg_ref[rows, h * head_v:(h + 1) * head_v] = (o * _silu(r)).astype(yg_ref.dtype)

    def token_of(y):
        bits = pltpu.bitcast(y, jnp.uint32)
        token = bits[:, :LANES]
        for l0 in range(LANES, bits.shape[1], LANES):
            token = token | bits[:, l0:l0 + LANES]
        return token

    def norm_strip(src, dst, rows):
        x = src[rows, :]
        ms = jnp.mean(x * x, axis=-1, keepdims=True)
        h = (x * lax.rsqrt(ms + EPS) * ng_ref[...]).astype(BF16)
        dst[rows, :] = h
        return token_of(h)

    def pin(tokens):
        acc = tokens[0]
        for t in tokens[1:]:
            acc = acc | t
        zero = pltpu.bitcast((acc >> 16) >> 16, F32)
        zero = jnp.concatenate([zero] * (STRIP // zero.shape[0]), axis=0)
        tile = h_ref[0:STRIP, 0:LANES].astype(F32)
        h_ref[0:STRIP, 0:LANES] = (tile + zero).astype(BF16)

    def proj_piece(c0, width):
        def piece():
            if c0 < w_ref.shape[0] * W_ROWS:
                w = w_ref[c0 // W_ROWS, :, c0 % W_ROWS:c0 % W_ROWS + width]
            else:
                w = w_tail_ref[...]
            proj_ref[:, c0:c0 + width] = jnp.dot(h_ref[...], w,
                                                 preferred_element_type=F32).astype(BF16)
        return piece

    def pieces(off, width):
        return [proj_piece(c0, min(MXU_N, off + width - c0)) for c0 in range(off, off + width, MXU_N)]

    @pl.when(s == 0)
    def _():
        proj_ref[:, :4 * conv_width] = jnp.zeros((t_rows, 4 * conv_width), BF16)

        n_full = in_cols // W_ROWS

        def chunk_copy(i, rows=W_ROWS):
            slot = i % W_SLOTS
            return pltpu.make_async_copy(w_hbm.at[pl.ds(i * W_ROWS, rows), :],
                                         stage_ref.at[slot, pl.ds(0, rows), :], dma_sem.at[slot])

        def transpose_chunk(slot, rows, dst):
            for k0 in range(0, d_model, MXU_N):
                blk = stage_ref[slot, 0:rows, k0:k0 + MXU_N]
                dst[k0:k0 + MXU_N, :] = blk.T.astype(BF16)

        for i in range(W_SLOTS - 1):
            chunk_copy(i).start()

        def load_chunk(i, c):
            @pl.when(i + W_SLOTS - 1 < n_full)
            def _():
                chunk_copy(i + W_SLOTS - 1).start()
            chunk_copy(i).wait()
            transpose_chunk(i % W_SLOTS, W_ROWS, w_ref.at[i])
            return c

        lax.fori_loop(0, n_full, load_chunk, 0)

        tail_rows = in_cols - n_full * W_ROWS
        stage_ref[n_full % W_SLOTS, tail_rows:LANES, :] = jnp.zeros((LANES - tail_rows, d_model), F32)
        chunk_copy(n_full, tail_rows).start()
        chunk_copy(n_full, tail_rows).wait()
        transpose_chunk(n_full % W_SLOTS, LANES, w_tail_ref)

        x0_copy = pltpu.make_async_copy(x_hbm.at[pl.ds(0, t_rows), :],
                                        stage_ref.at[0, pl.ds(0, t_rows), :], dma_sem.at[0])
        x0_copy.start()
        x0_copy.wait()

        def norm_first(i, c):
            norm_strip(stage_ref.at[0], hn_ref, pl.ds(pl.multiple_of(i * STRIP, STRIP), STRIP))
            return c

        lax.fori_loop(0, n_strips, norm_first, 0, unroll=2)

    @pl.when((s == 0) | ((s + tiles_per_seq - 1) % tiles_per_seq == 0))
    def _():
        carry_ref[...] = jnp.zeros_like(carry_ref)

    @pl.when(s < n_tiles)
    def _():
        @pl.when(s % tiles_per_seq == 0)
        def _():
            state_ref[...] = jnp.zeros_like(state_ref)

        wo_bf_ref[...] = wo_ref[...].astype(BF16)

        h_ref[...] = hn_ref[...]

        norm_units = [functools.partial(norm_strip, x_ref, hn_ref, strip_rows(i))
                      for i in range(n_strips)]
        conv_units = [functools.partial(conv_strip, strip_rows(i)) for i in range(n_strips)]
        gate_units = [functools.partial(gate_strip, strip_rows(i)) for i in range(n_strips)]
        kdec_units = [functools.partial(kdec_strip, strip_rows(i)) for i in range(n_strips)]
        n_steps = n_chunks * GLA_HEADS

        mxu = (pieces(off_gd, LANES) + pieces(off_k, dk_total) + pieces(off_q, dk_total)
               + pieces(off_v, gla_width) + pieces(off_r, gla_width))
        n_k = dk_total // MXU_N
        vpu = [conv_units[:2]]
        vpu += _spread([gate_pre] + gate_units, n_k)
        vpu += [[later_sums, conv_units[2]]] + [kdec_units + [conv_units[3]]]
        vpu += [[] for _ in range(dk_total // MXU_N - 2)]
        vpu += _spread(conv_units[4:], len(mxu) - len(vpu))
        assert len(vpu) == len(mxu)

        mxu2 = pieces(off_h, 4 * conv_width)
        busy = len(mxu2) - IDLE_TAIL_PIECES
        vpu2 = _spread(norm_units, busy) + [[] for _ in range(IDLE_TAIL_PIECES)]

        halves = [functools.partial(state_update, 0)]
        for step in range(n_steps):
            if step + 1 < n_steps:
                halves.append(functools.partial(state_update, step + 1))
            halves.append(functools.partial(readout, step))
        first = 1 + 2 * n_k + gla_width // MXU_N
        slots = (vpu + vpu2)[first:len(vpu) + busy]
        for slot, extra in zip(slots, _spread(halves, len(slots))):
            slot.extend(extra)

        tokens = []
        for piece, units in zip(mxu + mxu2, vpu + vpu2):
            if len(tokens) >= PIN_LAG and tokens[-PIN_LAG]:
                pin(tokens[-PIN_LAG])
            piece()
            tokens.append([t for t in [unit() for unit in units] if t is not None])

    @pl.when(s == n_tiles)
    def _():
        def step(i, c):
            conv_strip(pl.ds(pl.multiple_of(i * STRIP, STRIP), STRIP))
            return c
        lax.fori_loop(0, n_strips, step, 0, unroll=2)


def _proj_mix(x2d, norm_g, w_in_t, conv_w, conv_b, w_up_pad, b_gate, gla_norm_g, w_out, *,
              batch, seq, conv_width, dk_total, gla_width, n_cols):
    t = MIX_T
    nt = seq // t
    n_tiles = batch * nt
    d_model = x2d.shape[1]
    wo_rows = w_out.shape[0] // n_tiles
    assert w_out.shape[0] % n_tiles == 0 and wo_rows % STRIP == 0 and t <= W_ROWS
    main_cols = n_cols - LANES
    assert main_cols % W_ROWS == 0 and W_ROWS % MXU_N == 0 and 0 < w_in_t.shape[0] - main_cols <= LANES
    head_k = dk_total // GLA_HEADS
    head_v = gla_width // GLA_HEADS

    def whole(shape, **kw):
        return pl.BlockSpec(shape, lambda s: (0,) * len(shape), **kw)

    kern = functools.partial(_proj_mix_kernel, n_tiles=n_tiles, tiles_per_seq=nt,
                             conv_width=conv_width, dk_total=dk_total, gla_width=gla_width)
    return pl.pallas_call(
        kern,
        grid=(n_tiles + 1,),
        in_specs=[
            pl.BlockSpec((t, d_model), lambda s: (jnp.minimum(s + 1, n_tiles - 1), 0)),
            pl.BlockSpec(memory_space=pl.ANY),
            whole(norm_g.shape),
            pl.BlockSpec(memory_space=pl.ANY),
            whole(conv_w.shape), whole(conv_b.shape), whole(w_up_pad.shape), whole(b_gate.shape),
            whole(gla_norm_g.shape),
            pl.BlockSpec((wo_rows, w_out.shape[1]), lambda s: (jnp.minimum(s, n_tiles - 1), 0)),
        ],
        out_specs=[
            pl.BlockSpec((t, conv_width), lambda s: (jnp.maximum(s - 1, 0), 0)),
            pl.BlockSpec((t, gla_width), lambda s: (jnp.minimum(s, n_tiles - 1), 0)),
            pl.BlockSpec((wo_rows, w_out.shape[1]), lambda s: (jnp.minimum(s, n_tiles - 1), 0)),
        ],
        out_shape=[jax.ShapeDtypeStruct((batch * seq, conv_width), BF16),
                   jax.ShapeDtypeStruct((batch * seq, gla_width), BF16),
                   jax.ShapeDtypeStruct(w_out.shape, BF16)],
        scratch_shapes=[pltpu.VMEM((main_cols // W_ROWS, d_model, W_ROWS), BF16),
                        pltpu.VMEM((d_model, LANES), BF16),
                        pltpu.VMEM((W_SLOTS, W_ROWS, d_model), F32),
                        pltpu.SemaphoreType.DMA((W_SLOTS,)),
                        pltpu.VMEM((t, d_model), BF16),
                        pltpu.VMEM((t, d_model), BF16),
                        pltpu.VMEM((t, n_cols), BF16),
                        pltpu.VMEM((8, conv_width), F32),
                        pltpu.VMEM((GLA_HEADS, head_v, head_k), F32),
                        pltpu.VMEM((GLA_HEADS, head_v, head_k), BF16),
                        pltpu.VMEM((t, dk_total), F32),
                        pltpu.VMEM((t, 2 * dk_total), BF16),
                        pltpu.VMEM((t + 8, dk_total), F32),
                        pltpu.VMEM((t, dk_total), BF16)],
        compiler_params=pltpu.CompilerParams(
            dimension_semantics=("arbitrary",),
            vmem_limit_bytes=VMEM_LIMIT),
        name="proj_mix",
    )(x2d, x2d, norm_g, w_in_t, conv_w, conv_b, w_up_pad, b_gate, gla_norm_g, w_out)


def _out_proj_kernel(yc_ref, yg_ref, w_ref, x_ref, g_ref, o_ref, *, final_norm):
    kc = yc_ref.shape[1]
    z = (x_ref[...] + jnp.dot(yc_ref[...], w_ref[:kc, :], preferred_element_type=F32)
         + jnp.dot(yg_ref[...], w_ref[kc:, :], preferred_element_type=F32))
    if final_norm:
        ms = jnp.mean(z * z, axis=-1, keepdims=True)
        z = z * lax.rsqrt(ms + EPS) * g_ref[...]
    o_ref[...] = z


def _out_proj(y_conv, y_gla, w_out_bf16, x2d, final_g, *, final_norm):
    m, d = x2d.shape
    return pl.pallas_call(
        functools.partial(_out_proj_kernel, final_norm=final_norm),
        grid=(m // OUT_TM,),
        in_specs=[
            pl.BlockSpec((OUT_TM, y_conv.shape[1]), lambda i: (i, 0)),
            pl.BlockSpec((OUT_TM, y_gla.shape[1]), lambda i: (i, 0)),
            pl.BlockSpec(w_out_bf16.shape, lambda i: (0, 0)),
            pl.BlockSpec((OUT_TM, d), lambda i: (i, 0)),
            pl.BlockSpec((1, d), lambda i: (0, 0)),
        ],
        out_specs=pl.BlockSpec((OUT_TM, d), lambda i: (i, 0)),
        out_shape=jax.ShapeDtypeStruct((m, d), F32),
        compiler_params=pltpu.CompilerParams(
            dimension_semantics=("arbitrary",),
            vmem_limit_bytes=VMEM_LIMIT),
        name="out_proj",
    )(y_conv, y_gla, w_out_bf16, x2d, final_g)


def kernel(x, norm_g, w_in, conv_w, conv_b, gla_w_up, gla_b_gate, gla_norm_g, w_out, final_g):
    batch, seq, d_model = x.shape
    depth = norm_g.shape[0]
    conv_width = conv_w.shape[2]
    rank, dk_total = gla_w_up.shape[1], gla_w_up.shape[2]
    gla_width = gla_norm_g.shape[1] * gla_norm_g.shape[2]
    in_cols = w_in.shape[2]
    main_cols = in_cols - rank
    assert main_cols == 4 * conv_width + 2 * dk_total + 2 * gla_width
    assert main_cols % MXU_N == 0 and dk_total % MXU_N == 0 and rank <= LANES
    assert seq % MIX_T == 0 and MIX_T % CHUNK == 0 and (batch * seq) % OUT_TM == 0
    n_pad = main_cols + LANES

    x2d = x.reshape(batch * seq, d_model)
    for l in range(depth):
        w_up_p = jnp.pad(gla_w_up[l], ((0, LANES - rank), (0, 0))).astype(BF16)
        y_conv, y_gla, w_out_bf16 = _proj_mix(
            x2d, norm_g[l][None, :], w_in[l].T, conv_w[l], conv_b[l][None, :], w_up_p,
            gla_b_gate[l][None, :], gla_norm_g[l], w_out[l], batch=batch, seq=seq,
            conv_width=conv_width, dk_total=dk_total, gla_width=gla_width, n_cols=n_pad)
        x2d = _out_proj(y_conv, y_gla, w_out_bf16, x2d, final_g[None, :],
                        final_norm=(l == depth - 1))
    return x2d.reshape(batch, seq, d_model)
```

```python
import functools

import jax
import jax.numpy as jnp
from jax import lax
from jax.experimental import pallas as pl
from jax.experimental.pallas import tpu as pltpu

F32 = jnp.float32
BF16 = jnp.bfloat16

LANES = 128
MXU_N = 256
EPS = 1e-6
CHUNK = 64
GLA_HEADS = 4
GLA_TAU = 16.0

MIX_T = 256
STRIP = 16
COL_CHUNK = 256
W_ROWS = 256
W_SLOTS = 4
IDLE_TAIL_PIECES = 2
PIN_LAG = 1
OUT_TM = 512
OUT_PIECE = 512
VMEM_LIMIT = 56 * 1024 * 1024


def _silu(v):
    return v * jax.nn.sigmoid(v)


def _log_sigmoid(v):
    return -(jnp.maximum(-v, 0.0) + jnp.log1p(jnp.exp(-jnp.abs(v))))


def _spread(units, n_slots):
    out = [[] for _ in range(n_slots)]
    for j, u in enumerate(units):
        out[(j * n_slots) // len(units)].append(u)
    return out


def _proj_mix_kernel(x_ref, x_hbm, ng_ref, w_hbm, convw_ref, convb_ref, wup_ref, bgate_ref, gng_ref, wo_ref,
                     yc_ref, yg_ref, wo_bf_ref,
                     w_ref, w_tail_ref, stage_ref, dma_sem, h_ref, hn_ref, proj_ref, carry_ref, state_ref, sbf_ref,
                     glog_ref, hilo_ref, rev_ref, kdec_ref,
                     *, n_tiles, tiles_per_seq, conv_width, dk_total, gla_width):
    t_rows = x_ref.shape[0]
    in_cols, d_model = w_hbm.shape
    n_strips = t_rows // STRIP
    n_chunks = t_rows // CHUNK
    head_k = dk_total // GLA_HEADS
    head_v = gla_width // GLA_HEADS
    off_h, off_b, off_c, off_z = (i * conv_width for i in range(4))
    off_q = 4 * conv_width
    off_k = off_q + dk_total
    off_v = off_k + dk_total
    off_r = off_v + gla_width
    off_gd = off_r + gla_width
    s = pl.program_id(0)

    def strip_rows(i):
        return slice(i * STRIP, (i + 1) * STRIP)

    def conv_strip(rows):
        token = None
        for c0 in range(0, conv_width, COL_CHUNK):
            cols = slice(c0, c0 + COL_CHUNK)

            def col(off):
                return proj_ref[rows, off + c0:off + c0 + COL_CHUNK].astype(F32)

            u = col(off_c) * col(off_h)
            ext = jnp.concatenate([carry_ref[:, cols], u], axis=0)
            u1 = pltpu.roll(ext, 1, 0)[8:, :]
            u2 = pltpu.roll(ext, 2, 0)[8:, :]
            carry_ref[:, cols] = u[STRIP - 8:, :]
            conv = (convb_ref[:, cols] + convw_ref[0:1, cols] * u2 + convw_ref[1:2, cols] * u1
                    + convw_ref[2:3, cols] * u)
            y = (col(off_b) * conv * _silu(col(off_z))).astype(yc_ref.dtype)
            yc_ref[rows, cols] = y
            token = token_of(y) if token is None else token | token_of(y)
        return token

    def gate_pre():
        glog_ref[...] = jnp.dot(proj_ref[:, off_gd:off_gd + LANES], wup_ref[...],
                                preferred_element_type=F32)

    def gate_strip(rows):
        glog = _log_sigmoid(glog_ref[rows, :] + bgate_ref[...]) / GLA_TAU
        g_hi = glog.astype(BF16)
        hilo_ref[rows, :dk_total] = g_hi
        hilo_ref[rows, dk_total:] = (glog - g_hi.astype(F32)).astype(BF16)

    def later_sums():
        r_i = lax.broadcasted_iota(jnp.int32, (t_rows + 8, t_rows), 0)
        c_i = lax.broadcasted_iota(jnp.int32, (t_rows + 8, t_rows), 1)
        c_chunk = c_i // CHUNK
        ones = ((c_i > r_i) & (c_chunk == r_i // CHUNK)) | (c_chunk == r_i - t_rows)
        sums = jnp.dot(jnp.where(ones, 1.0, 0.0).astype(BF16), hilo_ref[...],
                       preferred_element_type=F32)
        rev_ref[...] = sums[:, :dk_total] + sums[:, dk_total:]

    def kdec_strip(rows):
        k = proj_ref[rows, off_k:off_k + dk_total].astype(F32)
        kdec_ref[rows, :] = (k * jnp.exp(rev_ref[rows, :])).astype(BF16)

    def state_update(step):
        c, h = divmod(step, GLA_HEADS)
        rows = slice(c * CHUNK, (c + 1) * CHUNK)
        ks = slice(h * head_k, (h + 1) * head_k)
        decay = jnp.exp(rev_ref[t_rows + c:t_rows + c + 1, ks])
        v = proj_ref[rows, off_v + h * head_v:off_v + (h + 1) * head_v]
        inc = lax.dot_general(v, kdec_ref[rows, ks], (((0,), (0,)), ((), ())),
                              preferred_element_type=F32)
        s_new = state_ref[h] * decay + inc
        state_ref[h] = s_new
        sbf_ref[h] = s_new.astype(BF16)

    def readout(step):
        c, h = divmod(step, GLA_HEADS)
        rows = slice(c * CHUNK, (c + 1) * CHUNK)
        q = proj_ref[rows, off_q + h * head_k:off_q + (h + 1) * head_k]
        o = lax.dot_general(q, sbf_ref[h], (((1,), (1,)), ((), ())),
                            preferred_element_type=F32) * (head_k ** -0.5)
        ms = jnp.mean(o * o, axis=-1, keepdims=True)
        o = o * lax.rsqrt(ms + EPS) * gng_ref[h:h + 1, :]
        r = proj_ref[rows, off_r + h * head_v:off_r + (h + 1) * head_v].astype(F32)
        yg_ref[rows, h * head_v:(h + 1) * head_v] = (o * _silu(r)).astype(yg_ref.dtype)

    def token_of(y):
        bits = pltpu.bitcast(y, jnp.uint32)
        token = bits[:, :LANES]
        for l0 in range(LANES, bits.shape[1], LANES):
            token = token | bits[:, l0:l0 + LANES]
        return token

    def norm_strip(src, dst, rows):
        x = src[rows, :]
        ms = jnp.mean(x * x, axis=-1, keepdims=True)
        h = (x * lax.rsqrt(ms + EPS) * ng_ref[...]).astype(BF16)
        dst[rows, :] = h
        return token_of(h)

    def pin(tokens):
        acc = tokens[0]
        for t in tokens[1:]:
            acc = acc | t
        zero = pltpu.bitcast((acc >> 16) >> 16, F32)
        zero = jnp.concatenate([zero] * (STRIP // zero.shape[0]), axis=0)
        tile = h_ref[0:STRIP, 0:LANES].astype(F32)
        h_ref[0:STRIP, 0:LANES] = (tile + zero).astype(BF16)

    def proj_piece(c0, width):
        def piece():
            if c0 < w_ref.shape[0] * W_ROWS:
                w = w_ref[c0 // W_ROWS, :, c0 % W_ROWS:c0 % W_ROWS + width]
            else:
                w = w_tail_ref[...]
            proj_ref[:, c0:c0 + width] = jnp.dot(h_ref[...], w,
                                                 preferred_element_type=F32).astype(BF16)
        return piece

    def pieces(off, width):
        return [proj_piece(c0, min(MXU_N, off + width - c0)) for c0 in range(off, off + width, MXU_N)]

    @pl.when(s == 0)
    def _():
        proj_ref[:, :4 * conv_width] = jnp.zeros((t_rows, 4 * conv_width), BF16)

        n_full = in_cols // W_ROWS

        def chunk_copy(i, rows=W_ROWS):
            slot = i % W_SLOTS
            return pltpu.make_async_copy(w_hbm.at[pl.ds(i * W_ROWS, rows), :],
                                         stage_ref.at[slot, pl.ds(0, rows), :], dma_sem.at[slot])

        def transpose_chunk(slot, rows, dst):
            for k0 in range(0, d_model, MXU_N):
                blk = stage_ref[slot, 0:rows, k0:k0 + MXU_N]
                dst[k0:k0 + MXU_N, :] = blk.T.astype(BF16)

        for i in range(W_SLOTS - 1):
            chunk_copy(i).start()

        def load_chunk(i, c):
            @pl.when(i + W_SLOTS - 1 < n_full)
            def _():
                chunk_copy(i + W_SLOTS - 1).start()
            chunk_copy(i).wait()
            transpose_chunk(i % W_SLOTS, W_ROWS, w_ref.at[i])
            return c

        lax.fori_loop(0, n_full, load_chunk, 0)

        tail_rows = in_cols - n_full * W_ROWS
        stage_ref[n_full % W_SLOTS, tail_rows:LANES, :] = jnp.zeros((LANES - tail_rows, d_model), F32)
        chunk_copy(n_full, tail_rows).start()
        chunk_copy(n_full, tail_rows).wait()
        transpose_chunk(n_full % W_SLOTS, LANES, w_tail_ref)

        x0_copy = pltpu.make_async_copy(x_hbm.at[pl.ds(0, t_rows), :],
                                        stage_ref.at[0, pl.ds(0, t_rows), :], dma_sem.at[0])
        x0_copy.start()
        x0_copy.wait()

        def norm_first(i, c):
            norm_strip(stage_ref.at[0], hn_ref, pl.ds(pl.multiple_of(i * STRIP, STRIP), STRIP))
            return c

        lax.fori_loop(0, n_strips, norm_first, 0, unroll=2)

    @pl.when((s == 0) | ((s + tiles_per_seq - 1) % tiles_per_seq == 0))
    def _():
        carry_ref[...] = jnp.zeros_like(carry_ref)

    @pl.when(s < n_tiles)
    def _():
        @pl.when(s % tiles_per_seq == 0)
        def _():
            state_ref[...] = jnp.zeros_like(state_ref)

        wo_bf_ref[...] = wo_ref[...].astype(BF16)

        h_ref[...] = hn_ref[...]

        norm_units = [functools.partial(norm_strip, x_ref, hn_ref, strip_rows(i))
                      for i in range(n_strips)]
        conv_units = [functools.partial(conv_strip, strip_rows(i)) for i in range(n_strips)]
        gate_units = [functools.partial(gate_strip, strip_rows(i)) for i in range(n_strips)]
        kdec_units = [functools.partial(kdec_strip, strip_rows(i)) for i in range(n_strips)]
        n_steps = n_chunks * GLA_HEADS

        mxu = (pieces(off_gd, LANES) + pieces(off_k, dk_total) + pieces(off_q, dk_total)
               + pieces(off_v, gla_width) + pieces(off_r, gla_width))
        n_k = dk_total // MXU_N
        vpu = [conv_units[:2]]
        vpu += _spread([gate_pre] + gate_units, n_k)
        vpu += [[later_sums, conv_units[2]]] + [kdec_units + [conv_units[3]]]
        vpu += [[] for _ in range(dk_total // MXU_N - 2)]
        rest = _spread(conv_units[4:], len(mxu) - len(vpu))
        rest[-1].append(functools.partial(state_update, 0))
        vpu += rest
        assert len(vpu) == len(mxu)

        mxu2 = pieces(off_h, 4 * conv_width)
        halves = []
        for step in range(n_steps):
            if step + 1 < n_steps:
                halves.append(functools.partial(state_update, step + 1))
            halves.append(functools.partial(readout, step))
        busy = len(mxu2) - IDLE_TAIL_PIECES
        vpu2 = [a + b for a, b in zip(_spread(halves, busy), _spread(norm_units, busy))]
        vpu2 += [[] for _ in range(IDLE_TAIL_PIECES)]

        tokens = []
        for piece, units in zip(mxu + mxu2, vpu + vpu2):
            if len(tokens) >= PIN_LAG and tokens[-PIN_LAG]:
                pin(tokens[-PIN_LAG])
            piece()
            tokens.append([t for t in [unit() for unit in units] if t is not None])

    @pl.when(s == n_tiles)
    def _():
        def step(i, c):
            conv_strip(pl.ds(pl.multiple_of(i * STRIP, STRIP), STRIP))
            return c
        lax.fori_loop(0, n_strips, step, 0, unroll=2)


def _proj_mix(x2d, norm_g, w_in_t, conv_w, conv_b, w_up_pad, b_gate, gla_norm_g, w_out, *,
              batch, seq, conv_width, dk_total, gla_width, n_cols):
    t = MIX_T
    nt = seq // t
    n_tiles = batch * nt
    d_model = x2d.shape[1]
    wo_rows = w_out.shape[0] // n_tiles
    assert w_out.shape[0] % n_tiles == 0 and wo_rows % STRIP == 0 and t <= W_ROWS
    main_cols = n_cols - LANES
    assert main_cols % W_ROWS == 0 and W_ROWS % MXU_N == 0 and 0 < w_in_t.shape[0] - main_cols <= LANES
    head_k = dk_total // GLA_HEADS
    head_v = gla_width // GLA_HEADS

    def whole(shape, **kw):
        return pl.BlockSpec(shape, lambda s: (0,) * len(shape), **kw)

    kern = functools.partial(_proj_mix_kernel, n_tiles=n_tiles, tiles_per_seq=nt,
                             conv_width=conv_width, dk_total=dk_total, gla_width=gla_width)
    return pl.pallas_call(
        kern,
        grid=(n_tiles + 1,),
        in_specs=[
            pl.BlockSpec((t, d_model), lambda s: (jnp.minimum(s + 1, n_tiles - 1), 0)),
            pl.BlockSpec(memory_space=pl.ANY),
            whole(norm_g.shape),
            pl.BlockSpec(memory_space=pl.ANY),
            whole(conv_w.shape), whole(conv_b.shape), whole(w_up_pad.shape), whole(b_gate.shape),
            whole(gla_norm_g.shape),
            pl.BlockSpec((wo_rows, w_out.shape[1]), lambda s: (jnp.minimum(s, n_tiles - 1), 0)),
        ],
        out_specs=[
            pl.BlockSpec((t, conv_width), lambda s: (jnp.maximum(s - 1, 0), 0)),
            pl.BlockSpec((t, gla_width), lambda s: (jnp.minimum(s, n_tiles - 1), 0)),
            pl.BlockSpec((wo_rows, w_out.shape[1]), lambda s: (jnp.minimum(s, n_tiles - 1), 0)),
        ],
        out_shape=[jax.ShapeDtypeStruct((batch * seq, conv_width), BF16),
                   jax.ShapeDtypeStruct((batch * seq, gla_width), BF16),
                   jax.ShapeDtypeStruct(w_out.shape, BF16)],
        scratch_shapes=[pltpu.VMEM((main_cols // W_ROWS, d_model, W_ROWS), BF16),
                        pltpu.VMEM((d_model, LANES), BF16),
                        pltpu.VMEM((W_SLOTS, W_ROWS, d_model), F32),
                        pltpu.SemaphoreType.DMA((W_SLOTS,)),
                        pltpu.VMEM((t, d_model), BF16),
                        pltpu.VMEM((t, d_model), BF16),
                        pltpu.VMEM((t, n_cols), BF16),
                        pltpu.VMEM((8, conv_width), F32),
                        pltpu.VMEM((GLA_HEADS, head_v, head_k), F32),
                        pltpu.VMEM((GLA_HEADS, head_v, head_k), BF16),
                        pltpu.VMEM((t, dk_total), F32),
                        pltpu.VMEM((t, 2 * dk_total), BF16),
                        pltpu.VMEM((t + 8, dk_total), F32),
                        pltpu.VMEM((t, dk_total), BF16)],
        compiler_params=pltpu.CompilerParams(
            dimension_semantics=("arbitrary",),
            vmem_limit_bytes=VMEM_LIMIT),
        name="proj_mix",
    )(x2d, x2d, norm_g, w_in_t, conv_w, conv_b, w_up_pad, b_gate, gla_norm_g, w_out)


def _out_proj_kernel(yc_ref, yg_ref, w_ref, x_ref, g_ref, o_ref, *, final_norm):
    kc = yc_ref.shape[1]
    d = o_ref.shape[1]
    ssq = None
    for c0 in range(0, d, OUT_PIECE):
        cols = slice(c0, c0 + OUT_PIECE)
        z = (x_ref[:, cols] + jnp.dot(yc_ref[...], w_ref[:kc, cols], preferred_element_type=F32)
             + jnp.dot(yg_ref[...], w_ref[kc:, cols], preferred_element_type=F32))
        o_ref[:, cols] = z
        zz = z * z
        for l0 in range(0, OUT_PIECE, LANES):
            part = zz[:, l0:l0 + LANES]
            ssq = part if ssq is None else ssq + part
    if final_norm:
        ms = jnp.sum(ssq, axis=-1, keepdims=True) / d
        o_ref[...] = o_ref[...] * lax.rsqrt(ms + EPS) * g_ref[...]


def _out_proj(y_conv, y_gla, w_out_bf16, x2d, final_g, *, final_norm):
    m, d = x2d.shape
    return pl.pallas_call(
        functools.partial(_out_proj_kernel, final_norm=final_norm),
        grid=(m // OUT_TM,),
        in_specs=[
            pl.BlockSpec((OUT_TM, y_conv.shape[1]), lambda i: (i, 0)),
            pl.BlockSpec((OUT_TM, y_gla.shape[1]), lambda i: (i, 0)),
            pl.BlockSpec(w_out_bf16.shape, lambda i: (0, 0)),
            pl.BlockSpec((OUT_TM, d), lambda i: (i, 0)),
            pl.BlockSpec((1, d), lambda i: (0, 0)),
        ],
        out_specs=pl.BlockSpec((OUT_TM, d), lambda i: (i, 0)),
        out_shape=jax.ShapeDtypeStruct((m, d), F32),
        compiler_params=pltpu.CompilerParams(
            dimension_semantics=("arbitrary",),
            vmem_limit_bytes=VMEM_LIMIT),
        name="out_proj",
    )(y_conv, y_gla, w_out_bf16, x2d, final_g)


def kernel(x, norm_g, w_in, conv_w, conv_b, gla_w_up, gla_b_gate, gla_norm_g, w_out, final_g):
    batch, seq, d_model = x.shape
    depth = norm_g.shape[0]
    conv_width = conv_w.shape[2]
    rank, dk_total = gla_w_up.shape[1], gla_w_up.shape[2]
    gla_width = gla_norm_g.shape[1] * gla_norm_g.shape[2]
    in_cols = w_in.shape[2]
    main_cols = in_cols - rank
    assert main_cols == 4 * conv_width + 2 * dk_total + 2 * gla_width
    assert main_cols % MXU_N == 0 and dk_total % MXU_N == 0 and rank <= LANES
    assert seq % MIX_T == 0 and MIX_T % CHUNK == 0 and (batch * seq) % OUT_TM == 0
    n_pad = main_cols + LANES

    x2d = x.reshape(batch * seq, d_model)
    for l in range(depth):
        w_up_p = jnp.pad(gla_w_up[l], ((0, LANES - rank), (0, 0))).astype(BF16)
        y_conv, y_gla, w_out_bf16 = _proj_mix(
            x2d, norm_g[l][None, :], w_in[l].T, conv_w[l], conv_b[l][None, :], w_up_p,
            gla_b_gate[l][None, :], gla_norm_g[l], w_out[l], batch=batch, seq=seq,
            conv_width=conv_width, dk_total=dk_total, gla_width=gla_width, n_cols=n_pad)
        x2d = _out_proj(y_conv, y_gla, w_out_bf16, x2d, final_g[None, :],
                        final_norm=(l == depth - 1))
    return x2d.reshape(batch, seq, d_model)
```

```python
import functools

import jax
import jax.numpy as jnp
from jax import lax
from jax.experimental import pallas as pl
from jax.experimental.pallas import tpu as pltpu

F32 = jnp.float32
BF16 = jnp.bfloat16

LANES = 128
MXU_N = 256
EPS = 1e-6
CHUNK = 64
GLA_HEADS = 4
GLA_TAU = 16.0

MIX_T = 256
STRIP = 16
COL_CHUNK = 256
W_ROWS = 256
W_SLOTS = 4
IDLE_TAIL_PIECES = 2
PIN_LAG = 1
OUT_TM = 512
VMEM_LIMIT = 56 * 1024 * 1024


def _silu(v):
    return v * jax.nn.sigmoid(v)


def _log_sigmoid(v):
    return -(jnp.maximum(-v, 0.0) + jnp.log1p(jnp.exp(-jnp.abs(v))))


def _spread(units, n_slots):
    out = [[] for _ in range(n_slots)]
    for j, u in enumerate(units):
        out[(j * n_slots) // len(units)].append(u)
    return out


def _proj_mix_kernel(x_ref, x_hbm, ng_ref, w_hbm, convw_ref, convb_ref, wup_ref, bgate_ref, gng_ref, wo_ref,
                     yc_ref, yg_ref, wo_bf_ref,
                     w_ref, w_tail_ref, stage_ref, dma_sem, h_ref, hn_ref, proj_ref, carry_ref, state_ref, sbf_ref,
                     glog_ref, hilo_ref, rev_ref, kdec_ref,
                     *, n_tiles, tiles_per_seq, conv_width, dk_total, gla_width):
    t_rows = x_ref.shape[0]
    in_cols, d_model = w_hbm.shape
    n_strips = t_rows // STRIP
    n_chunks = t_rows // CHUNK
    head_k = dk_total // GLA_HEADS
    head_v = gla_width // GLA_HEADS
    off_h, off_b, off_c, off_z = (i * conv_width for i in range(4))
    off_q = 4 * conv_width
    off_k = off_q + dk_total
    off_v = off_k + dk_total
    off_r = off_v + gla_width
    off_gd = off_r + gla_width
    s = pl.program_id(0)

    def strip_rows(i):
        return slice(i * STRIP, (i + 1) * STRIP)

    def conv_strip(rows):
        token = None
        for c0 in range(0, conv_width, COL_CHUNK):
            cols = slice(c0, c0 + COL_CHUNK)

            def col(off):
                return proj_ref[rows, off + c0:off + c0 + COL_CHUNK].astype(F32)

            u = col(off_c) * col(off_h)
            ext = jnp.concatenate([carry_ref[:, cols], u], axis=0)
            u1 = pltpu.roll(ext, 1, 0)[8:, :]
            u2 = pltpu.roll(ext, 2, 0)[8:, :]
            carry_ref[:, cols] = u[STRIP - 8:, :]
            conv = (convb_ref[:, cols] + convw_ref[0:1, cols] * u2 + convw_ref[1:2, cols] * u1
                    + convw_ref[2:3, cols] * u)
            y = (col(off_b) * conv * _silu(col(off_z))).astype(yc_ref.dtype)
            yc_ref[rows, cols] = y
            token = token_of(y) if token is None else token | token_of(y)
        return token

    def gate_pre():
        glog_ref[...] = jnp.dot(proj_ref[:, off_gd:off_gd + LANES], wup_ref[...],
                                preferred_element_type=F32)

    def gate_strip(rows):
        glog = _log_sigmoid(glog_ref[rows, :] + bgate_ref[...]) / GLA_TAU
        g_hi = glog.astype(BF16)
        hilo_ref[rows, :dk_total] = g_hi
        hilo_ref[rows, dk_total:] = (glog - g_hi.astype(F32)).astype(BF16)

    def later_sums():
        r_i = lax.broadcasted_iota(jnp.int32, (t_rows + 8, t_rows), 0)
        c_i = lax.broadcasted_iota(jnp.int32, (t_rows + 8, t_rows), 1)
        c_chunk = c_i // CHUNK
        ones = ((c_i > r_i) & (c_chunk == r_i // CHUNK)) | (c_chunk == r_i - t_rows)
        sums = jnp.dot(jnp.where(ones, 1.0, 0.0).astype(BF16), hilo_ref[...],
                       preferred_element_type=F32)
        rev_ref[...] = sums[:, :dk_total] + sums[:, dk_total:]

    def kdec_strip(rows):
        k = proj_ref[rows, off_k:off_k + dk_total].astype(F32)
        kdec_ref[rows, :] = (k * jnp.exp(rev_ref[rows, :])).astype(BF16)

    def state_update(step):
        c, h = divmod(step, GLA_HEADS)
        rows = slice(c * CHUNK, (c + 1) * CHUNK)
        ks = slice(h * head_k, (h + 1) * head_k)
        decay = jnp.exp(rev_ref[t_rows + c:t_rows + c + 1, ks])
        v = proj_ref[rows, off_v + h * head_v:off_v + (h + 1) * head_v]
        inc = lax.dot_general(v, kdec_ref[rows, ks], (((0,), (0,)), ((), ())),
                              preferred_element_type=F32)
        s_new = state_ref[h] * decay + inc
        state_ref[h] = s_new
        sbf_ref[h] = s_new.T.astype(BF16)

    def readout(step):
        c, h = divmod(step, GLA_HEADS)
        rows = slice(c * CHUNK, (c + 1) * CHUNK)
        q = proj_ref[rows, off_q + h * head_k:off_q + (h + 1) * head_k]
        o = jnp.dot(q, sbf_ref[h], preferred_element_type=F32) * (head_k ** -0.5)
        ms = jnp.mean(o * o, axis=-1, keepdims=True)
        o = o * lax.rsqrt(ms + EPS) * gng_ref[h:h + 1, :]
        r = proj_ref[rows, off_r + h * head_v:off_r + (h + 1) * head_v].astype(F32)
        yg_ref[rows, h * head_v:(h + 1) * head_v] = (o * _silu(r)).astype(yg_ref.dtype)

    def token_of(y):
        bits = pltpu.bitcast(y, jnp.uint32)
        token = bits[:, :LANES]
        for l0 in range(LANES, bits.shape[1], LANES):
            token = token | bits[:, l0:l0 + LANES]
        return token

    def norm_strip(src, dst, rows):
        x = src[rows, :]
        ms = jnp.mean(x * x, axis=-1, keepdims=True)
        h = (x * lax.rsqrt(ms + EPS) * ng_ref[...]).astype(BF16)
        dst[rows, :] = h
        return token_of(h)

    def pin(tokens):
        acc = tokens[0]
        for t in tokens[1:]:
            acc = acc | t
        zero = pltpu.bitcast((acc >> 16) >> 16, F32)
        zero = jnp.concatenate([zero] * (STRIP // zero.shape[0]), axis=0)
        tile = h_ref[0:STRIP, 0:LANES].astype(F32)
        h_ref[0:STRIP, 0:LANES] = (tile + zero).astype(BF16)

    def proj_piece(c0, width):
        def piece():
            if c0 < w_ref.shape[0] * W_ROWS:
                w = w_ref[c0 // W_ROWS, :, c0 % W_ROWS:c0 % W_ROWS + width]
            else:
                w = w_tail_ref[...]
            proj_ref[:, c0:c0 + width] = jnp.dot(h_ref[...], w,
                                                 preferred_element_type=F32).astype(BF16)
        return piece

    def pieces(off, width):
        return [proj_piece(c0, min(MXU_N, off + width - c0)) for c0 in range(off, off + width, MXU_N)]

    @pl.when(s == 0)
    def _():
        proj_ref[:, :4 * conv_width] = jnp.zeros((t_rows, 4 * conv_width), BF16)

        n_full = in_cols // W_ROWS

        def chunk_copy(i, rows=W_ROWS):
            slot = i % W_SLOTS
            return pltpu.make_async_copy(w_hbm.at[pl.ds(i * W_ROWS, rows), :],
                                         stage_ref.at[slot, pl.ds(0, rows), :], dma_sem.at[slot])

        def transpose_chunk(slot, rows, dst):
            for k0 in range(0, d_model, MXU_N):
                blk = stage_ref[slot, 0:rows, k0:k0 + MXU_N]
                dst[k0:k0 + MXU_N, :] = blk.T.astype(BF16)

        for i in range(W_SLOTS - 1):
            chunk_copy(i).start()

        def load_chunk(i, c):
            @pl.when(i + W_SLOTS - 1 < n_full)
            def _():
                chunk_copy(i + W_SLOTS - 1).start()
            chunk_copy(i).wait()
            transpose_chunk(i % W_SLOTS, W_ROWS, w_ref.at[i])
            return c

        lax.fori_loop(0, n_full, load_chunk, 0)

        tail_rows = in_cols - n_full * W_ROWS
        stage_ref[n_full % W_SLOTS, tail_rows:LANES, :] = jnp.zeros((LANES - tail_rows, d_model), F32)
        chunk_copy(n_full, tail_rows).start()
        chunk_copy(n_full, tail_rows).wait()
        transpose_chunk(n_full % W_SLOTS, LANES, w_tail_ref)

        x0_copy = pltpu.make_async_copy(x_hbm.at[pl.ds(0, t_rows), :],
                                        stage_ref.at[0, pl.ds(0, t_rows), :], dma_sem.at[0])
        x0_copy.start()
        x0_copy.wait()

        def norm_first(i, c):
            norm_strip(stage_ref.at[0], hn_ref, pl.ds(pl.multiple_of(i * STRIP, STRIP), STRIP))
            return c

        lax.fori_loop(0, n_strips, norm_first, 0, unroll=2)

    @pl.when((s == 0) | ((s + tiles_per_seq - 1) % tiles_per_seq == 0))
    def _():
        carry_ref[...] = jnp.zeros_like(carry_ref)

    @pl.when(s < n_tiles)
    def _():
        @pl.when(s % tiles_per_seq == 0)
        def _():
            state_ref[...] = jnp.zeros_like(state_ref)

        wo_bf_ref[...] = wo_ref[...].astype(BF16)

        h_ref[...] = hn_ref[...]

        norm_units = [functools.partial(norm_strip, x_ref, hn_ref, strip_rows(i))
                      for i in range(n_strips)]
        conv_units = [functools.partial(conv_strip, strip_rows(i)) for i in range(n_strips)]
        gate_units = [functools.partial(gate_strip, strip_rows(i)) for i in range(n_strips)]
        kdec_units = [functools.partial(kdec_strip, strip_rows(i)) for i in range(n_strips)]
        n_steps = n_chunks * GLA_HEADS

        mxu = (pieces(off_gd, LANES) + pieces(off_k, dk_total) + pieces(off_q, dk_total)
               + pieces(off_v, gla_width) + pieces(off_r, gla_width))
        n_k = dk_total // MXU_N
        vpu = [conv_units[:2]]
        vpu += _spread([gate_pre] + gate_units, n_k)
        vpu += [[later_sums, conv_units[2]]] + [kdec_units + [conv_units[3]]]
        vpu += [[] for _ in range(dk_total // MXU_N - 2)]
        rest = _spread(conv_units[4:], len(mxu) - len(vpu))
        rest[-1].append(functools.partial(state_update, 0))
        vpu += rest
        assert len(vpu) == len(mxu)

        mxu2 = pieces(off_h, 4 * conv_width)
        halves = []
        for step in range(n_steps):
            if step + 1 < n_steps:
                halves.append(functools.partial(state_update, step + 1))
            halves.append(functools.partial(readout, step))
        busy = len(mxu2) - IDLE_TAIL_PIECES
        vpu2 = [a + b for a, b in zip(_spread(halves, busy), _spread(norm_units, busy))]
        vpu2 += [[] for _ in range(IDLE_TAIL_PIECES)]

        tokens = []
        for piece, units in zip(mxu + mxu2, vpu + vpu2):
            if len(tokens) >= PIN_LAG and tokens[-PIN_LAG]:
                pin(tokens[-PIN_LAG])
            piece()
            tokens.append([t for t in [unit() for unit in units] if t is not None])

    @pl.when(s == n_tiles)
    def _():
        def step(i, c):
            conv_strip(pl.ds(pl.multiple_of(i * STRIP, STRIP), STRIP))
            return c
        lax.fori_loop(0, n_strips, step, 0, unroll=2)


def _proj_mix(x2d, norm_g, w_in_t, conv_w, conv_b, w_up_pad, b_gate, gla_norm_g, w_out, *,
              batch, seq, conv_width, dk_total, gla_width, n_cols):
    t = MIX_T
    nt = seq // t
    n_tiles = batch * nt
    d_model = x2d.shape[1]
    wo_rows = w_out.shape[0] // n_tiles
    assert w_out.shape[0] % n_tiles == 0 and wo_rows % STRIP == 0 and t <= W_ROWS
    main_cols = n_cols - LANES
    assert main_cols % W_ROWS == 0 and W_ROWS % MXU_N == 0 and 0 < w_in_t.shape[0] - main_cols <= LANES
    head_k = dk_total // GLA_HEADS
    head_v = gla_width // GLA_HEADS

    def whole(shape, **kw):
        return pl.BlockSpec(shape, lambda s: (0,) * len(shape), **kw)

    kern = functools.partial(_proj_mix_kernel, n_tiles=n_tiles, tiles_per_seq=nt,
                             conv_width=conv_width, dk_total=dk_total, gla_width=gla_width)
    return pl.pallas_call(
        kern,
        grid=(n_tiles + 1,),
        in_specs=[
            pl.BlockSpec((t, d_model), lambda s: (jnp.minimum(s + 1, n_tiles - 1), 0)),
            pl.BlockSpec(memory_space=pl.ANY),
            whole(norm_g.shape),
            pl.BlockSpec(memory_space=pl.ANY),
            whole(conv_w.shape), whole(conv_b.shape), whole(w_up_pad.shape), whole(b_gate.shape),
            whole(gla_norm_g.shape),
            pl.BlockSpec((wo_rows, w_out.shape[1]), lambda s: (jnp.minimum(s, n_tiles - 1), 0)),
        ],
        out_specs=[
            pl.BlockSpec((t, conv_width), lambda s: (jnp.maximum(s - 1, 0), 0)),
            pl.BlockSpec((t, gla_width), lambda s: (jnp.minimum(s, n_tiles - 1), 0)),
            pl.BlockSpec((wo_rows, w_out.shape[1]), lambda s: (jnp.minimum(s, n_tiles - 1), 0)),
        ],
        out_shape=[jax.ShapeDtypeStruct((batch * seq, conv_width), BF16),
                   jax.ShapeDtypeStruct((batch * seq, gla_width), BF16),
                   jax.ShapeDtypeStruct(w_out.shape, BF16)],
        scratch_shapes=[pltpu.VMEM((main_cols // W_ROWS, d_model, W_ROWS), BF16),
                        pltpu.VMEM((d_model, LANES), BF16),
                        pltpu.VMEM((W_SLOTS, W_ROWS, d_model), F32),
                        pltpu.SemaphoreType.DMA((W_SLOTS,)),
                        pltpu.VMEM((t, d_model), BF16),
                        pltpu.VMEM((t, d_model), BF16),
                        pltpu.VMEM((t, n_cols), BF16),
                        pltpu.VMEM((8, conv_width), F32),
                        pltpu.VMEM((GLA_HEADS, head_v, head_k), F32),
                        pltpu.VMEM((GLA_HEADS, head_k, head_v), BF16),
                        pltpu.VMEM((t, dk_total), F32),
                        pltpu.VMEM((t, 2 * dk_total), BF16),
                        pltpu.VMEM((t + 8, dk_total), F32),
                        pltpu.VMEM((t, dk_total), BF16)],
        compiler_params=pltpu.CompilerParams(
            dimension_semantics=("arbitrary",),
            vmem_limit_bytes=VMEM_LIMIT),
        name="proj_mix",
    )(x2d, x2d, norm_g, w_in_t, conv_w, conv_b, w_up_pad, b_gate, gla_norm_g, w_out)


def _out_proj_kernel(yc_ref, yg_ref, w_ref, x_ref, g_ref, o_ref, *, final_norm):
    kc = yc_ref.shape[1]
    z = (x_ref[...] + jnp.dot(yc_ref[...], w_ref[:kc, :], preferred_element_type=F32)
         + jnp.dot(yg_ref[...], w_ref[kc:, :], preferred_element_type=F32))
    if final_norm:
        ms = jnp.mean(z * z, axis=-1, keepdims=True)
        z = z * lax.rsqrt(ms + EPS) * g_ref[...]
    o_ref[...] = z


def _out_proj(y_conv, y_gla, w_out_bf16, x2d, final_g, *, final_norm):
    m, d = x2d.shape
    return pl.pallas_call(
        functools.partial(_out_proj_kernel, final_norm=final_norm),
        grid=(m // OUT_TM,),
        in_specs=[
            pl.BlockSpec((OUT_TM, y_conv.shape[1]), lambda i: (i, 0)),
            pl.BlockSpec((OUT_TM, y_gla.shape[1]), lambda i: (i, 0)),
            pl.BlockSpec(w_out_bf16.shape, lambda i: (0, 0)),
            pl.BlockSpec((OUT_TM, d), lambda i: (i, 0)),
            pl.BlockSpec((1, d), lambda i: (0, 0)),
        ],
        out_specs=pl.BlockSpec((OUT_TM, d), lambda i: (i, 0)),
        out_shape=jax.ShapeDtypeStruct((m, d), F32),
        compiler_params=pltpu.CompilerParams(
            dimension_semantics=("arbitrary",),
            vmem_limit_bytes=VMEM_LIMIT),
        name="out_proj",
    )(y_conv, y_gla, w_out_bf16, x2d, final_g)


def kernel(x, norm_g, w_in, conv_w, conv_b, gla_w_up, gla_b_gate, gla_norm_g, w_out, final_g):
    batch, seq, d_model = x.shape
    depth = norm_g.shape[0]
    conv_width = conv_w.shape[2]
    rank, dk_total = gla_w_up.shape[1], gla_w_up.shape[2]
    gla_width = gla_norm_g.shape[1] * gla_norm_g.shape[2]
    in_cols = w_in.shape[2]
    main_cols = in_cols - rank
    assert main_cols == 4 * conv_width + 2 * dk_total + 2 * gla_width
    assert main_cols % MXU_N == 0 and dk_total % MXU_N == 0 and rank <= LANES
    assert seq % MIX_T == 0 and MIX_T % CHUNK == 0 and (batch * seq) % OUT_TM == 0
    n_pad = main_cols + LANES

    x2d = x.reshape(batch * seq, d_model)
    for l in range(depth):
        w_up_p = jnp.pad(gla_w_up[l], ((0, LANES - rank), (0, 0))).astype(BF16)
        y_conv, y_gla, w_out_bf16 = _proj_mix(
            x2d, norm_g[l][None, :], w_in[l].T, conv_w[l], conv_b[l][None, :], w_up_p,
            gla_b_gate[l][None, :], gla_norm_g[l], w_out[l], batch=batch, seq=seq,
            conv_width=conv_width, dk_total=dk_total, gla_width=gla_width, n_cols=n_pad)
        x2d = _out_proj(y_conv, y_gla, w_out_bf16, x2d, final_g[None, :],
                        final_norm=(l == depth - 1))
    return x2d.reshape(batch, seq, d_model)
```

```python
import functools

import jax
import jax.numpy as jnp
from jax import lax
from jax.experimental import pallas as pl
from jax.experimental.pallas import tpu as pltpu

F32 = jnp.float32
BF16 = jnp.bfloat16

LANES = 128
MXU_N = 256
EPS = 1e-6
CHUNK = 64
GLA_HEADS = 4
GLA_TAU = 16.0

MIX_T = 256
STRIP = 16
COL_CHUNK = 256
W_ROWS = 256
W_SLOTS = 4
IDLE_TAIL_PIECES = 2
PIN_LAG = 1
OUT_TM = 512
VMEM_LIMIT = 56 * 1024 * 1024


def _silu(v):
    return v * jax.nn.sigmoid(v)


def _log_sigmoid(v):
    return -(jnp.maximum(-v, 0.0) + jnp.log1p(jnp.exp(-jnp.abs(v))))


def _spread(units, n_slots):
    out = [[] for _ in range(n_slots)]
    for j, u in enumerate(units):
        out[(j * n_slots) // len(units)].append(u)
    return out


def _proj_mix_kernel(x_ref, x_hbm, ng_ref, w_hbm, convw_ref, convb_ref, wup_ref, bgate_ref, gng_ref, wo_ref,
                     yc_ref, yg_ref, wo_bf_ref,
                     w_ref, w_tail_ref, stage_ref, dma_sem, h_ref, hn_ref, proj_ref, carry_ref, state_ref, sbf_ref,
                     glog_ref, hilo_ref, rev_ref, kdec_ref,
                     *, n_tiles, tiles_per_seq, conv_width, dk_total, gla_width):
    t_rows = x_ref.shape[0]
    in_cols, d_model = w_hbm.shape
    n_strips = t_rows // STRIP
    n_chunks = t_rows // CHUNK
    head_k = dk_total // GLA_HEADS
    head_v = gla_width // GLA_HEADS
    off_h, off_b, off_c, off_z = (i * conv_width for i in range(4))
    off_q = 4 * conv_width
    off_k = off_q + dk_total
    off_v = off_k + dk_total
    off_r = off_v + gla_width
    off_gd = off_r + gla_width
    s = pl.program_id(0)

    def strip_rows(i):
        return slice(i * STRIP, (i + 1) * STRIP)

    def conv_strip(rows):
        token = None
        for c0 in range(0, conv_width, COL_CHUNK):
            cols = slice(c0, c0 + COL_CHUNK)

            def col(off):
                return proj_ref[rows, off + c0:off + c0 + COL_CHUNK].astype(F32)

            u = col(off_c) * col(off_h)
            ext = jnp.concatenate([carry_ref[:, cols], u], axis=0)
            u1 = pltpu.roll(ext, 1, 0)[8:, :]
            u2 = pltpu.roll(ext, 2, 0)[8:, :]
            carry_ref[:, cols] = u[STRIP - 8:, :]
            conv = (convb_ref[:, cols] + convw_ref[0:1, cols] * u2 + convw_ref[1:2, cols] * u1
                    + convw_ref[2:3, cols] * u)
            y = (col(off_b) * conv * _silu(col(off_z))).astype(yc_ref.dtype)
            yc_ref[rows, cols] = y
            token = token_of(y) if token is None else token | token_of(y)
        return token

    def gate_pre():
        glog_ref[...] = jnp.dot(proj_ref[:, off_gd:off_gd + LANES], wup_ref[...],
                                preferred_element_type=F32)

    def gate_strip(rows):
        glog = _log_sigmoid(glog_ref[rows, :] + bgate_ref[...]) / GLA_TAU
        g_hi = glog.astype(BF16)
        hilo_ref[rows, :dk_total] = g_hi
        hilo_ref[rows, dk_total:] = (glog - g_hi.astype(F32)).astype(BF16)

    def later_sums():
        r_i = lax.broadcasted_iota(jnp.int32, (t_rows + 8, t_rows), 0)
        c_i = lax.broadcasted_iota(jnp.int32, (t_rows + 8, t_rows), 1)
        c_chunk = c_i // CHUNK
        ones = ((c_i > r_i) & (c_chunk == r_i // CHUNK)) | (c_chunk == r_i - t_rows)
        sums = jnp.dot(jnp.where(ones, 1.0, 0.0).astype(BF16), hilo_ref[...],
                       preferred_element_type=F32)
        rev_ref[...] = sums[:, :dk_total] + sums[:, dk_total:]

    def kdec_strip(rows):
        k = proj_ref[rows, off_k:off_k + dk_total].astype(F32)
        kdec_ref[rows, :] = (k * jnp.exp(rev_ref[rows, :])).astype(BF16)

    def state_update(step):
        c, h = divmod(step, GLA_HEADS)
        rows = slice(c * CHUNK, (c + 1) * CHUNK)
        ks = slice(h * head_k, (h + 1) * head_k)
        decay = jnp.exp(rev_ref[t_rows + c:t_rows + c + 1, ks])
        v = proj_ref[rows, off_v + h * head_v:off_v + (h + 1) * head_v]
        inc = lax.dot_general(v, kdec_ref[rows, ks], (((0,), (0,)), ((), ())),
                              preferred_element_type=F32)
        s_new = state_ref[h] * decay + inc
        state_ref[h] = s_new
        sbf_ref[h] = s_new.astype(BF16)

    def readout(step):
        c, h = divmod(step, GLA_HEADS)
        rows = slice(c * CHUNK, (c + 1) * CHUNK)
        q = proj_ref[rows, off_q + h * head_k:off_q + (h + 1) * head_k]
        o = lax.dot_general(q, sbf_ref[h], (((1,), (1,)), ((), ())),
                            preferred_element_type=F32) * (head_k ** -0.5)
        ms = jnp.mean(o * o, axis=-1, keepdims=True)
        o = o * lax.rsqrt(ms + EPS) * gng_ref[h:h + 1, :]
        r = proj_ref[rows, off_r + h * head_v:off_r + (h + 1) * head_v].astype(F32)
        yg_ref[rows, h * head_v:(h + 1) * head_v] = (o * _silu(r)).astype(yg_ref.dtype)

    def token_of(y):
        bits = pltpu.bitcast(y, jnp.uint32)
        token = bits[:, :LANES]
        for l0 in range(LANES, bits.shape[1], LANES):
            token = token | bits[:, l0:l0 + LANES]
        return token

    def norm_strip(src, dst, rows):
        x = src[rows, :]
        ms = jnp.mean(x * x, axis=-1, keepdims=True)
        h = (x * lax.rsqrt(ms + EPS) * ng_ref[...]).astype(BF16)
        dst[rows, :] = h
        return token_of(h)

    def pin(tokens):
        acc = tokens[0]
        for t in tokens[1:]:
            acc = acc | t
        zero = pltpu.bitcast((acc >> 16) >> 16, F32)
        zero = jnp.concatenate([zero] * (STRIP // zero.shape[0]), axis=0)
        tile = h_ref[0:STRIP, 0:LANES].astype(F32)
        h_ref[0:STRIP, 0:LANES] = (tile + zero).astype(BF16)

    def proj_piece(c0, width):
        def piece():
            if c0 < w_ref.shape[0] * W_ROWS:
                w = w_ref[c0 // W_ROWS, :, c0 % W_ROWS:c0 % W_ROWS + width]
            else:
                w = w_tail_ref[...]
            proj_ref[:, c0:c0 + width] = jnp.dot(h_ref[...], w,
                                                 preferred_element_type=F32).astype(BF16)
        return piece

    def pieces(off, width):
        return [proj_piece(c0, min(MXU_N, off + width - c0)) for c0 in range(off, off + width, MXU_N)]

    @pl.when(s == 0)
    def _():
        proj_ref[:, :4 * conv_width] = jnp.zeros((t_rows, 4 * conv_width), BF16)

        n_full = in_cols // W_ROWS

        def chunk_copy(i, rows=W_ROWS):
            slot = i % W_SLOTS
            return pltpu.make_async_copy(w_hbm.at[pl.ds(i * W_ROWS, rows), :],
                                         stage_ref.at[slot, pl.ds(0, rows), :], dma_sem.at[slot])

        def transpose_chunk(slot, rows, dst):
            for k0 in range(0, d_model, MXU_N):
                blk = stage_ref[slot, 0:rows, k0:k0 + MXU_N]
                dst[k0:k0 + MXU_N, :] = blk.T.astype(BF16)

        for i in range(W_SLOTS - 1):
            chunk_copy(i).start()

        def load_chunk(i, c):
            @pl.when(i + W_SLOTS - 1 < n_full)
            def _():
                chunk_copy(i + W_SLOTS - 1).start()
            chunk_copy(i).wait()
            transpose_chunk(i % W_SLOTS, W_ROWS, w_ref.at[i])
            return c

        lax.fori_loop(0, n_full, load_chunk, 0)

        tail_rows = in_cols - n_full * W_ROWS
        stage_ref[n_full % W_SLOTS, tail_rows:LANES, :] = jnp.zeros((LANES - tail_rows, d_model), F32)
        chunk_copy(n_full, tail_rows).start()
        chunk_copy(n_full, tail_rows).wait()
        transpose_chunk(n_full % W_SLOTS, LANES, w_tail_ref)

        x0_copy = pltpu.make_async_copy(x_hbm.at[pl.ds(0, t_rows), :],
                                        stage_ref.at[0, pl.ds(0, t_rows), :], dma_sem.at[0])
        x0_copy.start()
        x0_copy.wait()

        def norm_first(i, c):
            norm_strip(stage_ref.at[0], hn_ref, pl.ds(pl.multiple_of(i * STRIP, STRIP), STRIP))
            return c

        lax.fori_loop(0, n_strips, norm_first, 0, unroll=2)

    @pl.when((s == 0) | ((s + tiles_per_seq - 1) % tiles_per_seq == 0))
    def _():
        carry_ref[...] = jnp.zeros_like(carry_ref)

    @pl.when(s < n_tiles)
    def _():
        @pl.when(s % tiles_per_seq == 0)
        def _():
            state_ref[...] = jnp.zeros_like(state_ref)

        wo_bf_ref[...] = wo_ref[...].astype(BF16)

        h_ref[...] = hn_ref[...]

        norm_units = [functools.partial(norm_strip, x_ref, hn_ref, strip_rows(i))
                      for i in range(n_strips)]
        conv_units = [functools.partial(conv_strip, strip_rows(i)) for i in range(n_strips)]
        gate_units = [functools.partial(gate_strip, strip_rows(i)) for i in range(n_strips)]
        kdec_units = [functools.partial(kdec_strip, strip_rows(i)) for i in range(n_strips)]
        n_steps = n_chunks * GLA_HEADS

        mxu = (pieces(off_gd, LANES) + pieces(off_k, dk_total) + pieces(off_q, dk_total)
               + pieces(off_v, gla_width) + pieces(off_r, gla_width))
        n_k = dk_total // MXU_N
        vpu = [conv_units[:2]]
        vpu += _spread([gate_pre] + gate_units, n_k)
        vpu += [[later_sums, conv_units[2]]] + [kdec_units + [conv_units[3]]]
        vpu += [[] for _ in range(dk_total // MXU_N - 2)]
        rest = _spread(conv_units[4:], len(mxu) - len(vpu))
        rest[-1].append(functools.partial(state_update, 0))
        vpu += rest
        assert len(vpu) == len(mxu)

        mxu2 = pieces(off_h, 4 * conv_width)
        halves = []
        for step in range(n_steps):
            if step + 1 < n_steps:
                halves.append(functools.partial(state_update, step + 1))
            halves.append(functools.partial(readout, step))
        busy = len(mxu2) - IDLE_TAIL_PIECES
        vpu2 = [a + b for a, b in zip(_spread(halves, busy), _spread(norm_units, busy))]
        vpu2 += [[] for _ in range(IDLE_TAIL_PIECES)]

        due = {}
        for k, (piece, units) in enumerate(zip(mxu + mxu2, vpu + vpu2)):
            if due.get(k):
                pin(due[k])
            piece()
            lag = PIN_LAG if k < len(mxu) else PIN_LAG + 1
            due.setdefault(k + lag, []).extend(t for t in [unit() for unit in units] if t is not None)

    @pl.when(s == n_tiles)
    def _():
        def step(i, c):
            conv_strip(pl.ds(pl.multiple_of(i * STRIP, STRIP), STRIP))
            return c
        lax.fori_loop(0, n_strips, step, 0, unroll=2)


def _proj_mix(x2d, norm_g, w_in_t, conv_w, conv_b, w_up_pad, b_gate, gla_norm_g, w_out, *,
              batch, seq, conv_width, dk_total, gla_width, n_cols):
    t = MIX_T
    nt = seq // t
    n_tiles = batch * nt
    d_model = x2d.shape[1]
    wo_rows = w_out.shape[0] // n_tiles
    assert w_out.shape[0] % n_tiles == 0 and wo_rows % STRIP == 0 and t <= W_ROWS
    main_cols = n_cols - LANES
    assert main_cols % W_ROWS == 0 and W_ROWS % MXU_N == 0 and 0 < w_in_t.shape[0] - main_cols <= LANES
    head_k = dk_total // GLA_HEADS
    head_v = gla_width // GLA_HEADS

    def whole(shape, **kw):
        return pl.BlockSpec(shape, lambda s: (0,) * len(shape), **kw)

    kern = functools.partial(_proj_mix_kernel, n_tiles=n_tiles, tiles_per_seq=nt,
                             conv_width=conv_width, dk_total=dk_total, gla_width=gla_width)
    return pl.pallas_call(
        kern,
        grid=(n_tiles + 1,),
        in_specs=[
            pl.BlockSpec((t, d_model), lambda s: (jnp.minimum(s + 1, n_tiles - 1), 0)),
            pl.BlockSpec(memory_space=pl.ANY),
            whole(norm_g.shape),
            pl.BlockSpec(memory_space=pl.ANY),
            whole(conv_w.shape), whole(conv_b.shape), whole(w_up_pad.shape), whole(b_gate.shape),
            whole(gla_norm_g.shape),
            pl.BlockSpec((wo_rows, w_out.shape[1]), lambda s: (jnp.minimum(s, n_tiles - 1), 0)),
        ],
        out_specs=[
            pl.BlockSpec((t, conv_width), lambda s: (jnp.maximum(s - 1, 0), 0)),
            pl.BlockSpec((t, gla_width), lambda s: (jnp.minimum(s, n_tiles - 1), 0)),
            pl.BlockSpec((wo_rows, w_out.shape[1]), lambda s: (jnp.minimum(s, n_tiles - 1), 0)),
        ],
        out_shape=[jax.ShapeDtypeStruct((batch * seq, conv_width), BF16),
                   jax.ShapeDtypeStruct((batch * seq, gla_width), BF16),
                   jax.ShapeDtypeStruct(w_out.shape, BF16)],
        scratch_shapes=[pltpu.VMEM((main_cols // W_ROWS, d_model, W_ROWS), BF16),
                        pltpu.VMEM((d_model, LANES), BF16),
                        pltpu.VMEM((W_SLOTS, W_ROWS, d_model), F32),
                        pltpu.SemaphoreType.DMA((W_SLOTS,)),
                        pltpu.VMEM((t, d_model), BF16),
                        pltpu.VMEM((t, d_model), BF16),
                        pltpu.VMEM((t, n_cols), BF16),
                        pltpu.VMEM((8, conv_width), F32),
                        pltpu.VMEM((GLA_HEADS, head_v, head_k), F32),
                        pltpu.VMEM((GLA_HEADS, head_v, head_k), BF16),
                        pltpu.VMEM((t, dk_total), F32),
                        pltpu.VMEM((t, 2 * dk_total), BF16),
                        pltpu.VMEM((t + 8, dk_total), F32),
                        pltpu.VMEM((t, dk_total), BF16)],
        compiler_params=pltpu.CompilerParams(
            dimension_semantics=("arbitrary",),
            vmem_limit_bytes=VMEM_LIMIT),
        name="proj_mix",
    )(x2d, x2d, norm_g, w_in_t, conv_w, conv_b, w_up_pad, b_gate, gla_norm_g, w_out)


def _out_proj_kernel(yc_ref, yg_ref, w_ref, x_ref, g_ref, o_ref, *, final_norm):
    kc = yc_ref.shape[1]
    z = (x_ref[...] + jnp.dot(yc_ref[...], w_ref[:kc, :], preferred_element_type=F32)
         + jnp.dot(yg_ref[...], w_ref[kc:, :], preferred_element_type=F32))
    if final_norm:
        ms = jnp.mean(z * z, axis=-1, keepdims=True)
        z = z * lax.rsqrt(ms + EPS) * g_ref[...]
    o_ref[...] = z


def _out_proj(y_conv, y_gla, w_out_bf16, x2d, final_g, *, final_norm):
    m, d = x2d.shape
    return pl.pallas_call(
        functools.partial(_out_proj_kernel, final_norm=final_norm),
        grid=(m // OUT_TM,),
        in_specs=[
            pl.BlockSpec((OUT_TM, y_conv.shape[1]), lambda i: (i, 0)),
            pl.BlockSpec((OUT_TM, y_gla.shape[1]), lambda i: (i, 0)),
            pl.BlockSpec(w_out_bf16.shape, lambda i: (0, 0)),
            pl.BlockSpec((OUT_TM, d), lambda i: (i, 0)),
            pl.BlockSpec((1, d), lambda i: (0, 0)),
        ],
        out_specs=pl.BlockSpec((OUT_TM, d), lambda i: (i, 0)),
        out_shape=jax.ShapeDtypeStruct((m, d), F32),
        compiler_params=pltpu.CompilerParams(
            dimension_semantics=("arbitrary",),
            vmem_limit_bytes=VMEM_LIMIT),
        name="out_proj",
    )(y_conv, y_gla, w_out_bf16, x2d, final_g)


def kernel(x, norm_g, w_in, conv_w, conv_b, gla_w_up, gla_b_gate, gla_norm_g, w_out, final_g):
    batch, seq, d_model = x.shape
    depth = norm_g.shape[0]
    conv_width = conv_w.shape[2]
    rank, dk_total = gla_w_up.shape[1], gla_w_up.shape[2]
    gla_width = gla_norm_g.shape[1] * gla_norm_g.shape[2]
    in_cols = w_in.shape[2]
    main_cols = in_cols - rank
    assert main_cols == 4 * conv_width + 2 * dk_total + 2 * gla_width
    assert main_cols % MXU_N == 0 and dk_total % MXU_N == 0 and rank <= LANES
    assert seq % MIX_T == 0 and MIX_T % CHUNK == 0 and (batch * seq) % OUT_TM == 0
    n_pad = main_cols + LANES

    x2d = x.reshape(batch * seq, d_model)
    for l in range(depth):
        w_up_p = jnp.pad(gla_w_up[l], ((0, LANES - rank), (0, 0))).astype(BF16)
        y_conv, y_gla, w_out_bf16 = _proj_mix(
            x2d, norm_g[l][None, :], w_in[l].T, conv_w[l], conv_b[l][None, :], w_up_p,
            gla_b_gate[l][None, :], gla_norm_g[l], w_out[l], batch=batch, seq=seq,
            conv_width=conv_width, dk_total=dk_total, gla_width=gla_width, n_cols=n_pad)
        x2d = _out_proj(y_conv, y_gla, w_out_bf16, x2d, final_g[None, :],
                        final_norm=(l == depth - 1))
    return x2d.reshape(batch, seq, d_model)
```

```python
import functools

import jax
import jax.numpy as jnp
from jax import lax
from jax.experimental import pallas as pl
from jax.experimental.pallas import tpu as pltpu

F32 = jnp.float32
BF16 = jnp.bfloat16

LANES = 128
MXU_N = 256
EPS = 1e-6
CHUNK = 64
GLA_HEADS = 4
GLA_TAU = 16.0

MIX_T = 256
STRIP = 16
COL_CHUNK = 256
W_ROWS = 256
W_SLOTS = 4
IDLE_TAIL_PIECES = 2
PIN_LAG = 1
OUT_TM = 1024
VMEM_LIMIT = 56 * 1024 * 1024
OUT_VMEM_LIMIT = 60 * 1024 * 1024


def _silu(v):
    return v * jax.nn.sigmoid(v)


def _log_sigmoid(v):
    return -(jnp.maximum(-v, 0.0) + jnp.log1p(jnp.exp(-jnp.abs(v))))


def _spread(units, n_slots):
    out = [[] for _ in range(n_slots)]
    for j, u in enumerate(units):
        out[(j * n_slots) // len(units)].append(u)
    return out


def _proj_mix_kernel(x_ref, x_hbm, ng_ref, w_hbm, convw_ref, convb_ref, wup_ref, bgate_ref, gng_ref, wo_ref,
                     yc_ref, yg_ref, wo_bf_ref,
                     w_ref, w_tail_ref, stage_ref, dma_sem, h_ref, hn_ref, proj_ref, carry_ref, state_ref, sbf_ref,
                     glog_ref, hilo_ref, rev_ref, kdec_ref,
                     *, n_tiles, tiles_per_seq, conv_width, dk_total, gla_width):
    t_rows = x_ref.shape[0]
    in_cols, d_model = w_hbm.shape
    n_strips = t_rows // STRIP
    n_chunks = t_rows // CHUNK
    head_k = dk_total // GLA_HEADS
    head_v = gla_width // GLA_HEADS
    off_h, off_b, off_c, off_z = (i * conv_width for i in range(4))
    off_q = 4 * conv_width
    off_k = off_q + dk_total
    off_v = off_k + dk_total
    off_r = off_v + gla_width
    off_gd = off_r + gla_width
    s = pl.program_id(0)

    def strip_rows(i):
        return slice(i * STRIP, (i + 1) * STRIP)

    def conv_strip(rows):
        token = None
        for c0 in range(0, conv_width, COL_CHUNK):
            cols = slice(c0, c0 + COL_CHUNK)

            def col(off):
                return proj_ref[rows, off + c0:off + c0 + COL_CHUNK].astype(F32)

            u = col(off_c) * col(off_h)
            ext = jnp.concatenate([carry_ref[:, cols], u], axis=0)
            u1 = pltpu.roll(ext, 1, 0)[8:, :]
            u2 = pltpu.roll(ext, 2, 0)[8:, :]
            carry_ref[:, cols] = u[STRIP - 8:, :]
            conv = (convb_ref[:, cols] + convw_ref[0:1, cols] * u2 + convw_ref[1:2, cols] * u1
                    + convw_ref[2:3, cols] * u)
            y = (col(off_b) * conv * _silu(col(off_z))).astype(yc_ref.dtype)
            yc_ref[rows, cols] = y
            token = token_of(y) if token is None else token | token_of(y)
        return token

    def gate_pre():
        glog_ref[...] = jnp.dot(proj_ref[:, off_gd:off_gd + LANES], wup_ref[...],
                                preferred_element_type=F32)

    def gate_strip(rows):
        glog = _log_sigmoid(glog_ref[rows, :] + bgate_ref[...]) / GLA_TAU
        g_hi = glog.astype(BF16)
        hilo_ref[rows, :dk_total] = g_hi
        hilo_ref[rows, dk_total:] = (glog - g_hi.astype(F32)).astype(BF16)

    def later_sums():
        r_i = lax.broadcasted_iota(jnp.int32, (t_rows + 8, t_rows), 0)
        c_i = lax.broadcasted_iota(jnp.int32, (t_rows + 8, t_rows), 1)
        c_chunk = c_i // CHUNK
        ones = ((c_i > r_i) & (c_chunk == r_i // CHUNK)) | (c_chunk == r_i - t_rows)
        sums = jnp.dot(jnp.where(ones, 1.0, 0.0).astype(BF16), hilo_ref[...],
                       preferred_element_type=F32)
        rev_ref[...] = sums[:, :dk_total] + sums[:, dk_total:]

    def kdec_strip(rows):
        k = proj_ref[rows, off_k:off_k + dk_total].astype(F32)
        kdec_ref[rows, :] = (k * jnp.exp(rev_ref[rows, :])).astype(BF16)

    def state_update(step):
        c, h = divmod(step, GLA_HEADS)
        rows = slice(c * CHUNK, (c + 1) * CHUNK)
        ks = slice(h * head_k, (h + 1) * head_k)
        decay = jnp.exp(rev_ref[t_rows + c:t_rows + c + 1, ks])
        v = proj_ref[rows, off_v + h * head_v:off_v + (h + 1) * head_v]
        inc = lax.dot_general(v, kdec_ref[rows, ks], (((0,), (0,)), ((), ())),
                              preferred_element_type=F32)
        s_new = state_ref[h] * decay + inc
        state_ref[h] = s_new
        sbf_ref[h] = s_new.astype(BF16)

    def readout(step):
        c, h = divmod(step, GLA_HEADS)
        rows = slice(c * CHUNK, (c + 1) * CHUNK)
        q = proj_ref[rows, off_q + h * head_k:off_q + (h + 1) * head_k]
        o = lax.dot_general(q, sbf_ref[h], (((1,), (1,)), ((), ())),
                            preferred_element_type=F32) * (head_k ** -0.5)
        ms = jnp.mean(o * o, axis=-1, keepdims=True)
        o = o * lax.rsqrt(ms + EPS) * gng_ref[h:h + 1, :]
        r = proj_ref[rows, off_r + h * head_v:off_r + (h + 1) * head_v].astype(F32)
        yg_ref[rows, h * head_v:(h + 1) * head_v] = (o * _silu(r)).astype(yg_ref.dtype)

    def token_of(y):
        bits = pltpu.bitcast(y, jnp.uint32)
        token = bits[:, :LANES]
        for l0 in range(LANES, bits.shape[1], LANES):
            token = token | bits[:, l0:l0 + LANES]
        return token

    def norm_strip(src, dst, rows):
        x = src[rows, :]
        ms = jnp.mean(x * x, axis=-1, keepdims=True)
        h = (x * lax.rsqrt(ms + EPS) * ng_ref[...]).astype(BF16)
        dst[rows, :] = h
        return token_of(h)

    def pin(tokens):
        acc = tokens[0]
        for t in tokens[1:]:
            acc = acc | t
        zero = pltpu.bitcast((acc >> 16) >> 16, F32)
        zero = jnp.concatenate([zero] * (STRIP // zero.shape[0]), axis=0)
        tile = h_ref[0:STRIP, 0:LANES].astype(F32)
        h_ref[0:STRIP, 0:LANES] = (tile + zero).astype(BF16)

    def proj_piece(c0, width):
        def piece():
            if c0 < w_ref.shape[0] * W_ROWS:
                w = w_ref[c0 // W_ROWS, :, c0 % W_ROWS:c0 % W_ROWS + width]
            else:
                w = w_tail_ref[...]
            proj_ref[:, c0:c0 + width] = jnp.dot(h_ref[...], w,
                                                 preferred_element_type=F32).astype(BF16)
        return piece

    def pieces(off, width):
        return [proj_piece(c0, min(MXU_N, off + width - c0)) for c0 in range(off, off + width, MXU_N)]

    @pl.when(s == 0)
    def _():
        proj_ref[:, :4 * conv_width] = jnp.zeros((t_rows, 4 * conv_width), BF16)

        n_full = in_cols // W_ROWS

        def chunk_copy(i, rows=W_ROWS):
            slot = i % W_SLOTS
            return pltpu.make_async_copy(w_hbm.at[pl.ds(i * W_ROWS, rows), :],
                                         stage_ref.at[slot, pl.ds(0, rows), :], dma_sem.at[slot])

        def transpose_chunk(slot, rows, dst):
            for k0 in range(0, d_model, MXU_N):
                blk = stage_ref[slot, 0:rows, k0:k0 + MXU_N]
                dst[k0:k0 + MXU_N, :] = blk.T.astype(BF16)

        for i in range(W_SLOTS - 1):
            chunk_copy(i).start()

        def load_chunk(i, c):
            @pl.when(i + W_SLOTS - 1 < n_full)
            def _():
                chunk_copy(i + W_SLOTS - 1).start()
            chunk_copy(i).wait()
            transpose_chunk(i % W_SLOTS, W_ROWS, w_ref.at[i])
            return c

        lax.fori_loop(0, n_full, load_chunk, 0)

        tail_rows = in_cols - n_full * W_ROWS
        stage_ref[n_full % W_SLOTS, tail_rows:LANES, :] = jnp.zeros((LANES - tail_rows, d_model), F32)
        chunk_copy(n_full, tail_rows).start()
        chunk_copy(n_full, tail_rows).wait()
        transpose_chunk(n_full % W_SLOTS, LANES, w_tail_ref)

        x0_copy = pltpu.make_async_copy(x_hbm.at[pl.ds(0, t_rows), :],
                                        stage_ref.at[0, pl.ds(0, t_rows), :], dma_sem.at[0])
        x0_copy.start()
        x0_copy.wait()

        def norm_first(i, c):
            norm_strip(stage_ref.at[0], hn_ref, pl.ds(pl.multiple_of(i * STRIP, STRIP), STRIP))
            return c

        lax.fori_loop(0, n_strips, norm_first, 0, unroll=2)

    @pl.when((s == 0) | ((s + tiles_per_seq - 1) % tiles_per_seq == 0))
    def _():
        carry_ref[...] = jnp.zeros_like(carry_ref)

    @pl.when(s < n_tiles)
    def _():
        @pl.when(s % tiles_per_seq == 0)
        def _():
            state_ref[...] = jnp.zeros_like(state_ref)

        wo_bf_ref[...] = wo_ref[...].astype(BF16)

        h_ref[...] = hn_ref[...]

        norm_units = [functools.partial(norm_strip, x_ref, hn_ref, strip_rows(i))
                      for i in range(n_strips)]
        conv_units = [functools.partial(conv_strip, strip_rows(i)) for i in range(n_strips)]
        gate_units = [functools.partial(gate_strip, strip_rows(i)) for i in range(n_strips)]
        kdec_units = [functools.partial(kdec_strip, strip_rows(i)) for i in range(n_strips)]
        n_steps = n_chunks * GLA_HEADS

        mxu = (pieces(off_gd, LANES) + pieces(off_k, dk_total) + pieces(off_q, dk_total)
               + pieces(off_v, gla_width) + pieces(off_r, gla_width))
        n_k = dk_total // MXU_N
        vpu = [conv_units[:2]]
        vpu += _spread([gate_pre] + gate_units, n_k)
        vpu += [[later_sums, conv_units[2]]] + [kdec_units + [conv_units[3]]]
        vpu += [[] for _ in range(dk_total // MXU_N - 2)]
        rest = _spread(conv_units[4:], len(mxu) - len(vpu))
        rest[-1].append(functools.partial(state_update, 0))
        vpu += rest
        assert len(vpu) == len(mxu)

        mxu2 = pieces(off_h, 4 * conv_width)
        halves = []
        for step in range(n_steps):
            if step + 1 < n_steps:
                halves.append(functools.partial(state_update, step + 1))
            halves.append(functools.partial(readout, step))
        busy = len(mxu2) - IDLE_TAIL_PIECES
        vpu2 = [a + b for a, b in zip(_spread(halves, busy), _spread(norm_units, busy))]
        vpu2 += [[] for _ in range(IDLE_TAIL_PIECES)]

        tokens = []
        for piece, units in zip(mxu + mxu2, vpu + vpu2):
            if len(tokens) >= PIN_LAG and tokens[-PIN_LAG]:
                pin(tokens[-PIN_LAG])
            piece()
            tokens.append([t for t in [unit() for unit in units] if t is not None])

    @pl.when(s == n_tiles)
    def _():
        def step(i, c):
            conv_strip(pl.ds(pl.multiple_of(i * STRIP, STRIP), STRIP))
            return c
        lax.fori_loop(0, n_strips, step, 0, unroll=2)


def _proj_mix(x2d, norm_g, w_in_t, conv_w, conv_b, w_up_pad, b_gate, gla_norm_g, w_out, *,
              batch, seq, conv_width, dk_total, gla_width, n_cols):
    t = MIX_T
    nt = seq // t
    n_tiles = batch * nt
    d_model = x2d.shape[1]
    wo_rows = w_out.shape[0] // n_tiles
    assert w_out.shape[0] % n_tiles == 0 and wo_rows % STRIP == 0 and t <= W_ROWS
    main_cols = n_cols - LANES
    assert main_cols % W_ROWS == 0 and W_ROWS % MXU_N == 0 and 0 < w_in_t.shape[0] - main_cols <= LANES
    head_k = dk_total // GLA_HEADS
    head_v = gla_width // GLA_HEADS

    def whole(shape, **kw):
        return pl.BlockSpec(shape, lambda s: (0,) * len(shape), **kw)

    kern = functools.partial(_proj_mix_kernel, n_tiles=n_tiles, tiles_per_seq=nt,
                             conv_width=conv_width, dk_total=dk_total, gla_width=gla_width)
    return pl.pallas_call(
        kern,
        grid=(n_tiles + 1,),
        in_specs=[
            pl.BlockSpec((t, d_model), lambda s: (jnp.minimum(s + 1, n_tiles - 1), 0)),
            pl.BlockSpec(memory_space=pl.ANY),
            whole(norm_g.shape),
            pl.BlockSpec(memory_space=pl.ANY),
            whole(conv_w.shape), whole(conv_b.shape), whole(w_up_pad.shape), whole(b_gate.shape),
            whole(gla_norm_g.shape),
            pl.BlockSpec((wo_rows, w_out.shape[1]), lambda s: (jnp.minimum(s, n_tiles - 1), 0)),
        ],
        out_specs=[
            pl.BlockSpec((t, conv_width), lambda s: (jnp.maximum(s - 1, 0), 0)),
            pl.BlockSpec((t, gla_width), lambda s: (jnp.minimum(s, n_tiles - 1), 0)),
            pl.BlockSpec((wo_rows, w_out.shape[1]), lambda s: (jnp.minimum(s, n_tiles - 1), 0)),
        ],
        out_shape=[jax.ShapeDtypeStruct((batch * seq, conv_width), BF16),
                   jax.ShapeDtypeStruct((batch * seq, gla_width), BF16),
                   jax.ShapeDtypeStruct(w_out.shape, BF16)],
        scratch_shapes=[pltpu.VMEM((main_cols // W_ROWS, d_model, W_ROWS), BF16),
                        pltpu.VMEM((d_model, LANES), BF16),
                        pltpu.VMEM((W_SLOTS, W_ROWS, d_model), F32),
                        pltpu.SemaphoreType.DMA((W_SLOTS,)),
                        pltpu.VMEM((t, d_model), BF16),
                        pltpu.VMEM((t, d_model), BF16),
                        pltpu.VMEM((t, n_cols), BF16),
                        pltpu.VMEM((8, conv_width), F32),
                        pltpu.VMEM((GLA_HEADS, head_v, head_k), F32),
                        pltpu.VMEM((GLA_HEADS, head_v, head_k), BF16),
                        pltpu.VMEM((t, dk_total), F32),
                        pltpu.VMEM((t, 2 * dk_total), BF16),
                        pltpu.VMEM((t + 8, dk_total), F32),
                        pltpu.VMEM((t, dk_total), BF16)],
        compiler_params=pltpu.CompilerParams(
            dimension_semantics=("arbitrary",),
            vmem_limit_bytes=VMEM_LIMIT),
        name="proj_mix",
    )(x2d, x2d, norm_g, w_in_t, conv_w, conv_b, w_up_pad, b_gate, gla_norm_g, w_out)


def _out_proj_kernel(yc_ref, yg_ref, w_ref, x_ref, g_ref, o_ref, *, final_norm):
    kc = yc_ref.shape[1]
    z = (x_ref[...] + jnp.dot(yc_ref[...], w_ref[:kc, :], preferred_element_type=F32)
         + jnp.dot(yg_ref[...], w_ref[kc:, :], preferred_element_type=F32))
    if final_norm:
        ms = jnp.mean(z * z, axis=-1, keepdims=True)
        z = z * lax.rsqrt(ms + EPS) * g_ref[...]
    o_ref[...] = z


def _out_proj(y_conv, y_gla, w_out_bf16, x2d, final_g, *, final_norm):
    m, d = x2d.shape
    return pl.pallas_call(
        functools.partial(_out_proj_kernel, final_norm=final_norm),
        grid=(m // OUT_TM,),
        in_specs=[
            pl.BlockSpec((OUT_TM, y_conv.shape[1]), lambda i: (i, 0)),
            pl.BlockSpec((OUT_TM, y_gla.shape[1]), lambda i: (i, 0)),
            pl.BlockSpec(w_out_bf16.shape, lambda i: (0, 0), pipeline_mode=pl.Buffered(1)),
            pl.BlockSpec((OUT_TM, d), lambda i: (i, 0)),
            pl.BlockSpec((1, d), lambda i: (0, 0)),
        ],
        out_specs=pl.BlockSpec((OUT_TM, d), lambda i: (i, 0)),
        out_shape=jax.ShapeDtypeStruct((m, d), F32),
        compiler_params=pltpu.CompilerParams(
            dimension_semantics=("arbitrary",),
            vmem_limit_bytes=OUT_VMEM_LIMIT),
        name="out_proj",
    )(y_conv, y_gla, w_out_bf16, x2d, final_g)


def kernel(x, norm_g, w_in, conv_w, conv_b, gla_w_up, gla_b_gate, gla_norm_g, w_out, final_g):
    batch, seq, d_model = x.shape
    depth = norm_g.shape[0]
    conv_width = conv_w.shape[2]
    rank, dk_total = gla_w_up.shape[1], gla_w_up.shape[2]
    gla_width = gla_norm_g.shape[1] * gla_norm_g.shape[2]
    in_cols = w_in.shape[2]
    main_cols = in_cols - rank
    assert main_cols == 4 * conv_width + 2 * dk_total + 2 * gla_width
    assert main_cols % MXU_N == 0 and dk_total % MXU_N == 0 and rank <= LANES
    assert seq % MIX_T == 0 and MIX_T % CHUNK == 0 and (batch * seq) % OUT_TM == 0
    n_pad = main_cols + LANES

    x2d = x.reshape(batch * seq, d_model)
    for l in range(depth):
        w_up_p = jnp.pad(gla_w_up[l], ((0, LANES - rank), (0, 0))).astype(BF16)
        y_conv, y_gla, w_out_bf16 = _proj_mix(
            x2d, norm_g[l][None, :], w_in[l].T, conv_w[l], conv_b[l][None, :], w_up_p,
            gla_b_gate[l][None, :], gla_norm_g[l], w_out[l], batch=batch, seq=seq,
            conv_width=conv_width, dk_total=dk_total, gla_width=gla_width, n_cols=n_pad)
        x2d = _out_proj(y_conv, y_gla, w_out_bf16, x2d, final_g[None, :],
                        final_norm=(l == depth - 1))
    return x2d.reshape(batch, seq, d_model)
```

```python
import functools

import jax
import jax.numpy as jnp
from jax import lax
from jax.experimental import pallas as pl
from jax.experimental.pallas import tpu as pltpu

F32 = jnp.float32
BF16 = jnp.bfloat16

LANES = 128
MXU_N = 256
EPS = 1e-6
CHUNK = 64
GLA_HEADS = 4
GLA_TAU = 16.0

MIX_T = 256
STRIP = 16
COL_CHUNK = 256
W_ROWS = 256
W_SLOTS = 4
IDLE_TAIL_PIECES = 2
PIN_LAG = 1
OUT_TM = 256
VMEM_LIMIT = 56 * 1024 * 1024


def _silu(v):
    return v * jax.nn.sigmoid(v)


def _log_sigmoid(v):
    return -(jnp.maximum(-v, 0.0) + jnp.log1p(jnp.exp(-jnp.abs(v))))


def _spread(units, n_slots):
    out = [[] for _ in range(n_slots)]
    for j, u in enumerate(units):
        out[(j * n_slots) // len(units)].append(u)
    return out


def _proj_mix_kernel(x_ref, x_hbm, ng_ref, w_hbm, convw_ref, convb_ref, wup_ref, bgate_ref, gng_ref, wo_ref,
                     yc_ref, yg_ref, wo_bf_ref,
                     w_ref, w_tail_ref, stage_ref, dma_sem, h_ref, hn_ref, proj_ref, carry_ref, state_ref, sbf_ref,
                     glog_ref, hilo_ref, rev_ref, kdec_ref,
                     *, n_tiles, tiles_per_seq, conv_width, dk_total, gla_width):
    t_rows = x_ref.shape[0]
    in_cols, d_model = w_hbm.shape
    n_strips = t_rows // STRIP
    n_chunks = t_rows // CHUNK
    head_k = dk_total // GLA_HEADS
    head_v = gla_width // GLA_HEADS
    off_h, off_b, off_c, off_z = (i * conv_width for i in range(4))
    off_q = 4 * conv_width
    off_k = off_q + dk_total
    off_v = off_k + dk_total
    off_r = off_v + gla_width
    off_gd = off_r + gla_width
    s = pl.program_id(0)

    def strip_rows(i):
        return slice(i * STRIP, (i + 1) * STRIP)

    def conv_strip(rows):
        token = None
        for c0 in range(0, conv_width, COL_CHUNK):
            cols = slice(c0, c0 + COL_CHUNK)

            def col(off):
                return proj_ref[rows, off + c0:off + c0 + COL_CHUNK].astype(F32)

            u = col(off_c) * col(off_h)
            ext = jnp.concatenate([carry_ref[:, cols], u], axis=0)
            u1 = pltpu.roll(ext, 1, 0)[8:, :]
            u2 = pltpu.roll(ext, 2, 0)[8:, :]
            carry_ref[:, cols] = u[STRIP - 8:, :]
            conv = (convb_ref[:, cols] + convw_ref[0:1, cols] * u2 + convw_ref[1:2, cols] * u1
                    + convw_ref[2:3, cols] * u)
            y = (col(off_b) * conv * _silu(col(off_z))).astype(yc_ref.dtype)
            yc_ref[rows, cols] = y
            token = token_of(y) if token is None else token | token_of(y)
        return token

    def gate_pre():
        glog_ref[...] = jnp.dot(proj_ref[:, off_gd:off_gd + LANES], wup_ref[...],
                                preferred_element_type=F32)

    def gate_strip(rows):
        glog = _log_sigmoid(glog_ref[rows, :] + bgate_ref[...]) / GLA_TAU
        g_hi = glog.astype(BF16)
        hilo_ref[rows, :dk_total] = g_hi
        hilo_ref[rows, dk_total:] = (glog - g_hi.astype(F32)).astype(BF16)

    def later_sums():
        r_i = lax.broadcasted_iota(jnp.int32, (t_rows + 8, t_rows), 0)
        c_i = lax.broadcasted_iota(jnp.int32, (t_rows + 8, t_rows), 1)
        c_chunk = c_i // CHUNK
        ones = ((c_i > r_i) & (c_chunk == r_i // CHUNK)) | (c_chunk == r_i - t_rows)
        sums = jnp.dot(jnp.where(ones, 1.0, 0.0).astype(BF16), hilo_ref[...],
                       preferred_element_type=F32)
        rev_ref[...] = sums[:, :dk_total] + sums[:, dk_total:]

    def kdec_strip(rows):
        k = proj_ref[rows, off_k:off_k + dk_total].astype(F32)
        kdec_ref[rows, :] = (k * jnp.exp(rev_ref[rows, :])).astype(BF16)

    def state_update(step):
        c, h = divmod(step, GLA_HEADS)
        rows = slice(c * CHUNK, (c + 1) * CHUNK)
        ks = slice(h * head_k, (h + 1) * head_k)
        decay = jnp.exp(rev_ref[t_rows + c:t_rows + c + 1, ks])
        v = proj_ref[rows, off_v + h * head_v:off_v + (h + 1) * head_v]
        inc = lax.dot_general(v, kdec_ref[rows, ks], (((0,), (0,)), ((), ())),
                              preferred_element_type=F32)
        s_new = state_ref[h] * decay + inc
        state_ref[h] = s_new
        sbf_ref[h] = s_new.astype(BF16)

    def readout(step):
        c, h = divmod(step, GLA_HEADS)
        rows = slice(c * CHUNK, (c + 1) * CHUNK)
        q = proj_ref[rows, off_q + h * head_k:off_q + (h + 1) * head_k]
        o = lax.dot_general(q, sbf_ref[h], (((1,), (1,)), ((), ())),
                            preferred_element_type=F32) * (head_k ** -0.5)
        ms = jnp.mean(o * o, axis=-1, keepdims=True)
        o = o * lax.rsqrt(ms + EPS) * gng_ref[h:h + 1, :]
        r = proj_ref[rows, off_r + h * head_v:off_r + (h + 1) * head_v].astype(F32)
        yg_ref[rows, h * head_v:(h + 1) * head_v] = (o * _silu(r)).astype(yg_ref.dtype)

    def token_of(y):
        bits = pltpu.bitcast(y, jnp.uint32)
        token = bits[:, :LANES]
        for l0 in range(LANES, bits.shape[1], LANES):
            token = token | bits[:, l0:l0 + LANES]
        return token

    def norm_strip(src, dst, rows):
        x = src[rows, :]
        ms = jnp.mean(x * x, axis=-1, keepdims=True)
        h = (x * lax.rsqrt(ms + EPS) * ng_ref[...]).astype(BF16)
        dst[rows, :] = h
        return token_of(h)

    def pin(tokens):
        acc = tokens[0]
        for t in tokens[1:]:
            acc = acc | t
        zero = pltpu.bitcast((acc >> 16) >> 16, F32)
        zero = jnp.concatenate([zero] * (STRIP // zero.shape[0]), axis=0)
        tile = h_ref[0:STRIP, 0:LANES].astype(F32)
        h_ref[0:STRIP, 0:LANES] = (tile + zero).astype(BF16)

    def proj_piece(c0, width):
        def piece():
            if c0 < w_ref.shape[0] * W_ROWS:
                w = w_ref[c0 // W_ROWS, :, c0 % W_ROWS:c0 % W_ROWS + width]
            else:
                w = w_tail_ref[...]
            proj_ref[:, c0:c0 + width] = jnp.dot(h_ref[...], w,
                                                 preferred_element_type=F32).astype(BF16)
        return piece

    def pieces(off, width):
        return [proj_piece(c0, min(MXU_N, off + width - c0)) for c0 in range(off, off + width, MXU_N)]

    @pl.when(s == 0)
    def _():
        proj_ref[:, :4 * conv_width] = jnp.zeros((t_rows, 4 * conv_width), BF16)

        n_full = in_cols // W_ROWS

        def chunk_copy(i, rows=W_ROWS):
            slot = i % W_SLOTS
            return pltpu.make_async_copy(w_hbm.at[pl.ds(i * W_ROWS, rows), :],
                                         stage_ref.at[slot, pl.ds(0, rows), :], dma_sem.at[slot])

        def transpose_chunk(slot, rows, dst):
            for k0 in range(0, d_model, MXU_N):
                blk = stage_ref[slot, 0:rows, k0:k0 + MXU_N]
                dst[k0:k0 + MXU_N, :] = blk.T.astype(BF16)

        for i in range(W_SLOTS - 1):
            chunk_copy(i).start()

        def load_chunk(i, c):
            @pl.when(i + W_SLOTS - 1 < n_full)
            def _():
                chunk_copy(i + W_SLOTS - 1).start()
            chunk_copy(i).wait()
            transpose_chunk(i % W_SLOTS, W_ROWS, w_ref.at[i])
            return c

        lax.fori_loop(0, n_full, load_chunk, 0)

        tail_rows = in_cols - n_full * W_ROWS
        stage_ref[n_full % W_SLOTS, tail_rows:LANES, :] = jnp.zeros((LANES - tail_rows, d_model), F32)
        chunk_copy(n_full, tail_rows).start()
        chunk_copy(n_full, tail_rows).wait()
        transpose_chunk(n_full % W_SLOTS, LANES, w_tail_ref)

        x0_copy = pltpu.make_async_copy(x_hbm.at[pl.ds(0, t_rows), :],
                                        stage_ref.at[0, pl.ds(0, t_rows), :], dma_sem.at[0])
        x0_copy.start()
        x0_copy.wait()

        def norm_first(i, c):
            norm_strip(stage_ref.at[0], hn_ref, pl.ds(pl.multiple_of(i * STRIP, STRIP), STRIP))
            return c

        lax.fori_loop(0, n_strips, norm_first, 0, unroll=2)

    @pl.when((s == 0) | ((s + tiles_per_seq - 1) % tiles_per_seq == 0))
    def _():
        carry_ref[...] = jnp.zeros_like(carry_ref)

    @pl.when(s < n_tiles)
    def _():
        @pl.when(s % tiles_per_seq == 0)
        def _():
            state_ref[...] = jnp.zeros_like(state_ref)

        wo_bf_ref[...] = wo_ref[...].astype(BF16)

        h_ref[...] = hn_ref[...]

        norm_units = [functools.partial(norm_strip, x_ref, hn_ref, strip_rows(i))
                      for i in range(n_strips)]
        conv_units = [functools.partial(conv_strip, strip_rows(i)) for i in range(n_strips)]
        gate_units = [functools.partial(gate_strip, strip_rows(i)) for i in range(n_strips)]
        kdec_units = [functools.partial(kdec_strip, strip_rows(i)) for i in range(n_strips)]
        n_steps = n_chunks * GLA_HEADS

        mxu = (pieces(off_gd, LANES) + pieces(off_k, dk_total) + pieces(off_q, dk_total)
               + pieces(off_v, gla_width) + pieces(off_r, gla_width))
        n_k = dk_total // MXU_N
        vpu = [conv_units[:2]]
        vpu += _spread([gate_pre] + gate_units, n_k)
        vpu += [[later_sums, conv_units[2]]] + [kdec_units + [conv_units[3]]]
        vpu += [[] for _ in range(dk_total // MXU_N - 2)]
        rest = _spread(conv_units[4:], len(mxu) - len(vpu))
        rest[-1].append(functools.partial(state_update, 0))
        vpu += rest
        assert len(vpu) == len(mxu)

        mxu2 = pieces(off_h, 4 * conv_width)
        halves = []
        for step in range(n_steps):
            if step + 1 < n_steps:
                halves.append(functools.partial(state_update, step + 1))
            halves.append(functools.partial(readout, step))
        busy = len(mxu2) - IDLE_TAIL_PIECES
        vpu2 = [a + b for a, b in zip(_spread(halves, busy), _spread(norm_units, busy))]
        vpu2 += [[] for _ in range(IDLE_TAIL_PIECES)]

        tokens = []
        for piece, units in zip(mxu + mxu2, vpu + vpu2):
            if len(tokens) >= PIN_LAG and tokens[-PIN_LAG]:
                pin(tokens[-PIN_LAG])
            piece()
            tokens.append([t for t in [unit() for unit in units] if t is not None])

    @pl.when(s == n_tiles)
    def _():
        def step(i, c):
            conv_strip(pl.ds(pl.multiple_of(i * STRIP, STRIP), STRIP))
            return c
        lax.fori_loop(0, n_strips, step, 0, unroll=2)


def _proj_mix(x2d, norm_g, w_in_t, conv_w, conv_b, w_up_pad, b_gate, gla_norm_g, w_out, *,
              batch, seq, conv_width, dk_total, gla_width, n_cols):
    t = MIX_T
    nt = seq // t
    n_tiles = batch * nt
    d_model = x2d.shape[1]
    wo_rows = w_out.shape[0] // n_tiles
    assert w_out.shape[0] % n_tiles == 0 and wo_rows % STRIP == 0 and t <= W_ROWS
    main_cols = n_cols - LANES
    assert main_cols % W_ROWS == 0 and W_ROWS % MXU_N == 0 and 0 < w_in_t.shape[0] - main_cols <= LANES
    head_k = dk_total // GLA_HEADS
    head_v = gla_width // GLA_HEADS

    def whole(shape, **kw):
        return pl.BlockSpec(shape, lambda s: (0,) * len(shape), **kw)

    kern = functools.partial(_proj_mix_kernel, n_tiles=n_tiles, tiles_per_seq=nt,
                             conv_width=conv_width, dk_total=dk_total, gla_width=gla_width)
    return pl.pallas_call(
        kern,
        grid=(n_tiles + 1,),
        in_specs=[
            pl.BlockSpec((t, d_model), lambda s: (jnp.minimum(s + 1, n_tiles - 1), 0)),
            pl.BlockSpec(memory_space=pl.ANY),
            whole(norm_g.shape),
            pl.BlockSpec(memory_space=pl.ANY),
            whole(conv_w.shape), whole(conv_b.shape), whole(w_up_pad.shape), whole(b_gate.shape),
            whole(gla_norm_g.shape),
            pl.BlockSpec((wo_rows, w_out.shape[1]), lambda s: (jnp.minimum(s, n_tiles - 1), 0)),
        ],
        out_specs=[
            pl.BlockSpec((t, conv_width), lambda s: (jnp.maximum(s - 1, 0), 0)),
            pl.BlockSpec((t, gla_width), lambda s: (jnp.minimum(s, n_tiles - 1), 0)),
            pl.BlockSpec((wo_rows, w_out.shape[1]), lambda s: (jnp.minimum(s, n_tiles - 1), 0)),
        ],
        out_shape=[jax.ShapeDtypeStruct((batch * seq, conv_width), BF16),
                   jax.ShapeDtypeStruct((batch * seq, gla_width), BF16),
                   jax.ShapeDtypeStruct(w_out.shape, BF16)],
        scratch_shapes=[pltpu.VMEM((main_cols // W_ROWS, d_model, W_ROWS), BF16),
                        pltpu.VMEM((d_model, LANES), BF16),
                        pltpu.VMEM((W_SLOTS, W_ROWS, d_model), F32),
                        pltpu.SemaphoreType.DMA((W_SLOTS,)),
                        pltpu.VMEM((t, d_model), BF16),
                        pltpu.VMEM((t, d_model), BF16),
                        pltpu.VMEM((t, n_cols), BF16),
                        pltpu.VMEM((8, conv_width), F32),
                        pltpu.VMEM((GLA_HEADS, head_v, head_k), F32),
                        pltpu.VMEM((GLA_HEADS, head_v, head_k), BF16),
                        pltpu.VMEM((t, dk_total), F32),
                        pltpu.VMEM((t, 2 * dk_total), BF16),
                        pltpu.VMEM((t + 8, dk_total), F32),
                        pltpu.VMEM((t, dk_total), BF16)],
        compiler_params=pltpu.CompilerParams(
            dimension_semantics=("arbitrary",),
            vmem_limit_bytes=VMEM_LIMIT),
        name="proj_mix",
    )(x2d, x2d, norm_g, w_in_t, conv_w, conv_b, w_up_pad, b_gate, gla_norm_g, w_out)


def _out_proj_kernel(yc_ref, yg_ref, w_ref, x_ref, g_ref, o_ref, *, final_norm):
    kc = yc_ref.shape[1]
    z = (x_ref[...] + jnp.dot(yc_ref[...], w_ref[:kc, :], preferred_element_type=F32)
         + jnp.dot(yg_ref[...], w_ref[kc:, :], preferred_element_type=F32))
    if final_norm:
        ms = jnp.mean(z * z, axis=-1, keepdims=True)
        z = z * lax.rsqrt(ms + EPS) * g_ref[...]
    o_ref[...] = z


def _out_proj(y_conv, y_gla, w_out_bf16, x2d, final_g, *, final_norm):
    m, d = x2d.shape
    return pl.pallas_call(
        functools.partial(_out_proj_kernel, final_norm=final_norm),
        grid=(m // OUT_TM,),
        in_specs=[
            pl.BlockSpec((OUT_TM, y_conv.shape[1]), lambda i: (i, 0)),
            pl.BlockSpec((OUT_TM, y_gla.shape[1]), lambda i: (i, 0)),
            pl.BlockSpec(w_out_bf16.shape, lambda i: (0, 0)),
            pl.BlockSpec((OUT_TM, d), lambda i: (i, 0)),
            pl.BlockSpec((1, d), lambda i: (0, 0)),
        ],
        out_specs=pl.BlockSpec((OUT_TM, d), lambda i: (i, 0)),
        out_shape=jax.ShapeDtypeStruct((m, d), F32),
        compiler_params=pltpu.CompilerParams(
            dimension_semantics=("arbitrary",),
            vmem_limit_bytes=VMEM_LIMIT),
        name="out_proj",
    )(y_conv, y_gla, w_out_bf16, x2d, final_g)


def kernel(x, norm_g, w_in, conv_w, conv_b, gla_w_up, gla_b_gate, gla_norm_g, w_out, final_g):
    batch, seq, d_model = x.shape
    depth = norm_g.shape[0]
    conv_width = conv_w.shape[2]
    rank, dk_total = gla_w_up.shape[1], gla_w_up.shape[2]
    gla_width = gla_norm_g.shape[1] * gla_norm_g.shape[2]
    in_cols = w_in.shape[2]
    main_cols = in_cols - rank
    assert main_cols == 4 * conv_width + 2 * dk_total + 2 * gla_width
    assert main_cols % MXU_N == 0 and dk_total % MXU_N == 0 and rank <= LANES
    assert seq % MIX_T == 0 and MIX_T % CHUNK == 0 and (batch * seq) % OUT_TM == 0
    n_pad = main_cols + LANES

    x2d = x.reshape(batch * seq, d_model)
    for l in range(depth):
        w_up_p = jnp.pad(gla_w_up[l], ((0, LANES - rank), (0, 0))).astype(BF16)
        y_conv, y_gla, w_out_bf16 = _proj_mix(
            x2d, norm_g[l][None, :], w_in[l].T, conv_w[l], conv_b[l][None, :], w_up_p,
            gla_b_gate[l][None, :], gla_norm_g[l], w_out[l], batch=batch, seq=seq,
            conv_width=conv_width, dk_total=dk_total, gla_width=gla_width, n_cols=n_pad)
        x2d = _out_proj(y_conv, y_gla, w_out_bf16, x2d, final_g[None, :],
                        final_norm=(l == depth - 1))
    return x2d.reshape(batch, seq, d_model)
```

```python
import functools

import jax
import jax.numpy as jnp
from jax import lax
from jax.experimental import pallas as pl
from jax.experimental.pallas import tpu as pltpu

F32 = jnp.float32
BF16 = jnp.bfloat16

LANES = 128
MXU_N = 256
EPS = 1e-6
CHUNK = 64
GLA_HEADS = 4
GLA_TAU = 16.0

MIX_T = 256
STRIP = 16
COL_CHUNK = 512
W_ROWS = 256
W_SLOTS = 4
IDLE_TAIL_PIECES = 2
PIN_LAG = 1
OUT_TM = 512
VMEM_LIMIT = 56 * 1024 * 1024


def _silu(v):
    return v * jax.nn.sigmoid(v)


def _log_sigmoid(v):
    return -(jnp.maximum(-v, 0.0) + jnp.log1p(jnp.exp(-jnp.abs(v))))


def _spread(units, n_slots):
    out = [[] for _ in range(n_slots)]
    for j, u in enumerate(units):
        out[(j * n_slots) // len(units)].append(u)
    return out


def _proj_mix_kernel(x_ref, x_hbm, ng_ref, w_hbm, convw_ref, convb_ref, wup_ref, bgate_ref, gng_ref, wo_ref,
                     yc_ref, yg_ref, wo_bf_ref,
                     w_ref, w_tail_ref, stage_ref, dma_sem, h_ref, hn_ref, proj_ref, carry_ref, state_ref, sbf_ref,
                     glog_ref, hilo_ref, rev_ref, kdec_ref,
                     *, n_tiles, tiles_per_seq, conv_width, dk_total, gla_width):
    t_rows = x_ref.shape[0]
    in_cols, d_model = w_hbm.shape
    n_strips = t_rows // STRIP
    n_chunks = t_rows // CHUNK
    head_k = dk_total // GLA_HEADS
    head_v = gla_width // GLA_HEADS
    off_h, off_b, off_c, off_z = (i * conv_width for i in range(4))
    off_q = 4 * conv_width
    off_k = off_q + dk_total
    off_v = off_k + dk_total
    off_r = off_v + gla_width
    off_gd = off_r + gla_width
    s = pl.program_id(0)

    def strip_rows(i):
        return slice(i * STRIP, (i + 1) * STRIP)

    def conv_strip(rows):
        token = None
        for c0 in range(0, conv_width, COL_CHUNK):
            cols = slice(c0, c0 + COL_CHUNK)

            def col(off):
                return proj_ref[rows, off + c0:off + c0 + COL_CHUNK].astype(F32)

            u = col(off_c) * col(off_h)
            ext = jnp.concatenate([carry_ref[:, cols], u], axis=0)
            u1 = pltpu.roll(ext, 1, 0)[8:, :]
            u2 = pltpu.roll(ext, 2, 0)[8:, :]
            carry_ref[:, cols] = u[STRIP - 8:, :]
            conv = (convb_ref[:, cols] + convw_ref[0:1, cols] * u2 + convw_ref[1:2, cols] * u1
                    + convw_ref[2:3, cols] * u)
            y = (col(off_b) * conv * _silu(col(off_z))).astype(yc_ref.dtype)
            yc_ref[rows, cols] = y
            token = token_of(y) if token is None else token | token_of(y)
        return token

    def gate_pre():
        glog_ref[...] = jnp.dot(proj_ref[:, off_gd:off_gd + LANES], wup_ref[...],
                                preferred_element_type=F32)

    def gate_strip(rows):
        glog = _log_sigmoid(glog_ref[rows, :] + bgate_ref[...]) / GLA_TAU
        g_hi = glog.astype(BF16)
        hilo_ref[rows, :dk_total] = g_hi
        hilo_ref[rows, dk_total:] = (glog - g_hi.astype(F32)).astype(BF16)

    def later_sums():
        r_i = lax.broadcasted_iota(jnp.int32, (t_rows + 8, t_rows), 0)
        c_i = lax.broadcasted_iota(jnp.int32, (t_rows + 8, t_rows), 1)
        c_chunk = c_i // CHUNK
        ones = ((c_i > r_i) & (c_chunk == r_i // CHUNK)) | (c_chunk == r_i - t_rows)
        sums = jnp.dot(jnp.where(ones, 1.0, 0.0).astype(BF16), hilo_ref[...],
                       preferred_element_type=F32)
        rev_ref[...] = sums[:, :dk_total] + sums[:, dk_total:]

    def kdec_strip(rows):
        k = proj_ref[rows, off_k:off_k + dk_total].astype(F32)
        kdec_ref[rows, :] = (k * jnp.exp(rev_ref[rows, :])).astype(BF16)

    def state_update(step):
        c, h = divmod(step, GLA_HEADS)
        rows = slice(c * CHUNK, (c + 1) * CHUNK)
        ks = slice(h * head_k, (h + 1) * head_k)
        decay = jnp.exp(rev_ref[t_rows + c:t_rows + c + 1, ks])
        v = proj_ref[rows, off_v + h * head_v:off_v + (h + 1) * head_v]
        inc = lax.dot_general(v, kdec_ref[rows, ks], (((0,), (0,)), ((), ())),
                              preferred_element_type=F32)
        s_new = state_ref[h] * decay + inc
        state_ref[h] = s_new
        sbf_ref[h] = s_new.astype(BF16)

    def readout(step):
        c, h = divmod(step, GLA_HEADS)
        rows = slice(c * CHUNK, (c + 1) * CHUNK)
        q = proj_ref[rows, off_q + h * head_k:off_q + (h + 1) * head_k]
        o = lax.dot_general(q, sbf_ref[h], (((1,), (1,)), ((), ())),
                            preferred_element_type=F32) * (head_k ** -0.5)
        ms = jnp.mean(o * o, axis=-1, keepdims=True)
        o = o * lax.rsqrt(ms + EPS) * gng_ref[h:h + 1, :]
        r = proj_ref[rows, off_r + h * head_v:off_r + (h + 1) * head_v].astype(F32)
        yg_ref[rows, h * head_v:(h + 1) * head_v] = (o * _silu(r)).astype(yg_ref.dtype)

    def token_of(y):
        bits = pltpu.bitcast(y, jnp.uint32)
        token = bits[:, :LANES]
        for l0 in range(LANES, bits.shape[1], LANES):
            token = token | bits[:, l0:l0 + LANES]
        return token

    def norm_strip(src, dst, rows):
        x = src[rows, :]
        ms = jnp.mean(x * x, axis=-1, keepdims=True)
        h = (x * lax.rsqrt(ms + EPS) * ng_ref[...]).astype(BF16)
        dst[rows, :] = h
        return token_of(h)

    def pin(tokens):
        acc = tokens[0]
        for t in tokens[1:]:
            acc = acc | t
        zero = pltpu.bitcast((acc >> 16) >> 16, F32)
        zero = jnp.concatenate([zero] * (STRIP // zero.shape[0]), axis=0)
        tile = h_ref[0:STRIP, 0:LANES].astype(F32)
        h_ref[0:STRIP, 0:LANES] = (tile + zero).astype(BF16)

    def proj_piece(c0, width):
        def piece():
            if c0 < w_ref.shape[0] * W_ROWS:
                w = w_ref[c0 // W_ROWS, :, c0 % W_ROWS:c0 % W_ROWS + width]
            else:
                w = w_tail_ref[...]
            proj_ref[:, c0:c0 + width] = jnp.dot(h_ref[...], w,
                                                 preferred_element_type=F32).astype(BF16)
        return piece

    def pieces(off, width):
        return [proj_piece(c0, min(MXU_N, off + width - c0)) for c0 in range(off, off + width, MXU_N)]

    @pl.when(s == 0)
    def _():
        proj_ref[:, :4 * conv_width] = jnp.zeros((t_rows, 4 * conv_width), BF16)

        n_full = in_cols // W_ROWS

        def chunk_copy(i, rows=W_ROWS):
            slot = i % W_SLOTS
            return pltpu.make_async_copy(w_hbm.at[pl.ds(i * W_ROWS, rows), :],
                                         stage_ref.at[slot, pl.ds(0, rows), :], dma_sem.at[slot])

        def transpose_chunk(slot, rows, dst):
            for k0 in range(0, d_model, MXU_N):
                blk = stage_ref[slot, 0:rows, k0:k0 + MXU_N]
                dst[k0:k0 + MXU_N, :] = blk.T.astype(BF16)

        for i in range(W_SLOTS - 1):
            chunk_copy(i).start()

        def load_chunk(i, c):
            @pl.when(i + W_SLOTS - 1 < n_full)
            def _():
                chunk_copy(i + W_SLOTS - 1).start()
            chunk_copy(i).wait()
            transpose_chunk(i % W_SLOTS, W_ROWS, w_ref.at[i])
            return c

        lax.fori_loop(0, n_full, load_chunk, 0)

        tail_rows = in_cols - n_full * W_ROWS
        stage_ref[n_full % W_SLOTS, tail_rows:LANES, :] = jnp.zeros((LANES - tail_rows, d_model), F32)
        chunk_copy(n_full, tail_rows).start()
        chunk_copy(n_full, tail_rows).wait()
        transpose_chunk(n_full % W_SLOTS, LANES, w_tail_ref)

        x0_copy = pltpu.make_async_copy(x_hbm.at[pl.ds(0, t_rows), :],
                                        stage_ref.at[0, pl.ds(0, t_rows), :], dma_sem.at[0])
        x0_copy.start()
        x0_copy.wait()

        def norm_first(i, c):
            norm_strip(stage_ref.at[0], hn_ref, pl.ds(pl.multiple_of(i * STRIP, STRIP), STRIP))
            return c

        lax.fori_loop(0, n_strips, norm_first, 0, unroll=2)

    @pl.when((s == 0) | ((s + tiles_per_seq - 1) % tiles_per_seq == 0))
    def _():
        carry_ref[...] = jnp.zeros_like(carry_ref)

    @pl.when(s < n_tiles)
    def _():
        @pl.when(s % tiles_per_seq == 0)
        def _():
            state_ref[...] = jnp.zeros_like(state_ref)

        wo_bf_ref[...] = wo_ref[...].astype(BF16)

        h_ref[...] = hn_ref[...]

        norm_units = [functools.partial(norm_strip, x_ref, hn_ref, strip_rows(i))
                      for i in range(n_strips)]
        conv_units = [functools.partial(conv_strip, strip_rows(i)) for i in range(n_strips)]
        gate_units = [functools.partial(gate_strip, strip_rows(i)) for i in range(n_strips)]
        kdec_units = [functools.partial(kdec_strip, strip_rows(i)) for i in range(n_strips)]
        n_steps = n_chunks * GLA_HEADS

        mxu = (pieces(off_gd, LANES) + pieces(off_k, dk_total) + pieces(off_q, dk_total)
               + pieces(off_v, gla_width) + pieces(off_r, gla_width))
        n_k = dk_total // MXU_N
        vpu = [conv_units[:2]]
        vpu += _spread([gate_pre] + gate_units, n_k)
        vpu += [[later_sums, conv_units[2]]] + [kdec_units + [conv_units[3]]]
        vpu += [[] for _ in range(dk_total // MXU_N - 2)]
        rest = _spread(conv_units[4:], len(mxu) - len(vpu))
        rest[-1].append(functools.partial(state_update, 0))
        vpu += rest
        assert len(vpu) == len(mxu)

        mxu2 = pieces(off_h, 4 * conv_width)
        halves = []
        for step in range(n_steps):
            if step + 1 < n_steps:
                halves.append(functools.partial(state_update, step + 1))
            halves.append(functools.partial(readout, step))
        busy = len(mxu2) - IDLE_TAIL_PIECES
        vpu2 = [a + b for a, b in zip(_spread(halves, busy), _spread(norm_units, busy))]
        vpu2 += [[] for _ in range(IDLE_TAIL_PIECES)]

        tokens = []
        for piece, units in zip(mxu + mxu2, vpu + vpu2):
            if len(tokens) >= PIN_LAG and tokens[-PIN_LAG]:
                pin(tokens[-PIN_LAG])
            piece()
            tokens.append([t for t in [unit() for unit in units] if t is not None])

    @pl.when(s == n_tiles)
    def _():
        def step(i, c):
            conv_strip(pl.ds(pl.multiple_of(i * STRIP, STRIP), STRIP))
            return c
        lax.fori_loop(0, n_strips, step, 0, unroll=2)


def _proj_mix(x2d, norm_g, w_in_t, conv_w, conv_b, w_up_pad, b_gate, gla_norm_g, w_out, *,
              batch, seq, conv_width, dk_total, gla_width, n_cols):
    t = MIX_T
    nt = seq // t
    n_tiles = batch * nt
    d_model = x2d.shape[1]
    wo_rows = w_out.shape[0] // n_tiles
    assert w_out.shape[0] % n_tiles == 0 and wo_rows % STRIP == 0 and t <= W_ROWS
    main_cols = n_cols - LANES
    assert main_cols % W_ROWS == 0 and W_ROWS % MXU_N == 0 and 0 < w_in_t.shape[0] - main_cols <= LANES
    head_k = dk_total // GLA_HEADS
    head_v = gla_width // GLA_HEADS

    def whole(shape, **kw):
        return pl.BlockSpec(shape, lambda s: (0,) * len(shape), **kw)

    kern = functools.partial(_proj_mix_kernel, n_tiles=n_tiles, tiles_per_seq=nt,
                             conv_width=conv_width, dk_total=dk_total, gla_width=gla_width)
    return pl.pallas_call(
        kern,
        grid=(n_tiles + 1,),
        in_specs=[
            pl.BlockSpec((t, d_model), lambda s: (jnp.minimum(s + 1, n_tiles - 1), 0)),
            pl.BlockSpec(memory_space=pl.ANY),
            whole(norm_g.shape),
            pl.BlockSpec(memory_space=pl.ANY),
            whole(conv_w.shape), whole(conv_b.shape), whole(w_up_pad.shape), whole(b_gate.shape),
            whole(gla_norm_g.shape),
            pl.BlockSpec((wo_rows, w_out.shape[1]), lambda s: (jnp.minimum(s, n_tiles - 1), 0)),
        ],
        out_specs=[
            pl.BlockSpec((t, conv_width), lambda s: (jnp.maximum(s - 1, 0), 0)),
            pl.BlockSpec((t, gla_width), lambda s: (jnp.minimum(s, n_tiles - 1), 0)),
            pl.BlockSpec((wo_rows, w_out.shape[1]), lambda s: (jnp.minimum(s, n_tiles - 1), 0)),
        ],
        out_shape=[jax.ShapeDtypeStruct((batch * seq, conv_width), BF16),
                   jax.ShapeDtypeStruct((batch * seq, gla_width), BF16),
                   jax.ShapeDtypeStruct(w_out.shape, BF16)],
        scratch_shapes=[pltpu.VMEM((main_cols // W_ROWS, d_model, W_ROWS), BF16),
                        pltpu.VMEM((d_model, LANES), BF16),
                        pltpu.VMEM((W_SLOTS, W_ROWS, d_model), F32),
                        pltpu.SemaphoreType.DMA((W_SLOTS,)),
                        pltpu.VMEM((t, d_model), BF16),
                        pltpu.VMEM((t, d_model), BF16),
                        pltpu.VMEM((t, n_cols), BF16),
                        pltpu.VMEM((8, conv_width), F32),
                        pltpu.VMEM((GLA_HEADS, head_v, head_k), F32),
                        pltpu.VMEM((GLA_HEADS, head_v, head_k), BF16),
                        pltpu.VMEM((t, dk_total), F32),
                        pltpu.VMEM((t, 2 * dk_total), BF16),
                        pltpu.VMEM((t + 8, dk_total), F32),
                        pltpu.VMEM((t, dk_total), BF16)],
        compiler_params=pltpu.CompilerParams(
            dimension_semantics=("arbitrary",),
            vmem_limit_bytes=VMEM_LIMIT),
        name="proj_mix",
    )(x2d, x2d, norm_g, w_in_t, conv_w, conv_b, w_up_pad, b_gate, gla_norm_g, w_out)


def _out_proj_kernel(yc_ref, yg_ref, w_ref, x_ref, g_ref, o_ref, *, final_norm):
    kc = yc_ref.shape[1]
    z = (x_ref[...] + jnp.dot(yc_ref[...], w_ref[:kc, :], preferred_element_type=F32)
         + jnp.dot(yg_ref[...], w_ref[kc:, :], preferred_element_type=F32))
    if final_norm:
        ms = jnp.mean(z * z, axis=-1, keepdims=True)
        z = z * lax.rsqrt(ms + EPS) * g_ref[...]
    o_ref[...] = z


def _out_proj(y_conv, y_gla, w_out_bf16, x2d, final_g, *, final_norm):
    m, d = x2d.shape
    return pl.pallas_call(
        functools.partial(_out_proj_kernel, final_norm=final_norm),
        grid=(m // OUT_TM,),
        in_specs=[
            pl.BlockSpec((OUT_TM, y_conv.shape[1]), lambda i: (i, 0)),
            pl.BlockSpec((OUT_TM, y_gla.shape[1]), lambda i: (i, 0)),
            pl.BlockSpec(w_out_bf16.shape, lambda i: (0, 0)),
            pl.BlockSpec((OUT_TM, d), lambda i: (i, 0)),
            pl.BlockSpec((1, d), lambda i: (0, 0)),
        ],
        out_specs=pl.BlockSpec((OUT_TM, d), lambda i: (i, 0)),
        out_shape=jax.ShapeDtypeStruct((m, d), F32),
        compiler_params=pltpu.CompilerParams(
            dimension_semantics=("arbitrary",),
            vmem_limit_bytes=VMEM_LIMIT),
        name="out_proj",
    )(y_conv, y_gla, w_out_bf16, x2d, final_g)


def kernel(x, norm_g, w_in, conv_w, conv_b, gla_w_up, gla_b_gate, gla_norm_g, w_out, final_g):
    batch, seq, d_model = x.shape
    depth = norm_g.shape[0]
    conv_width = conv_w.shape[2]
    rank, dk_total = gla_w_up.shape[1], gla_w_up.shape[2]
    gla_width = gla_norm_g.shape[1] * gla_norm_g.shape[2]
    in_cols = w_in.shape[2]
    main_cols = in_cols - rank
    assert main_cols == 4 * conv_width + 2 * dk_total + 2 * gla_width
    assert main_cols % MXU_N == 0 and dk_total % MXU_N == 0 and rank <= LANES
    assert seq % MIX_T == 0 and MIX_T % CHUNK == 0 and (batch * seq) % OUT_TM == 0
    n_pad = main_cols + LANES

    x2d = x.reshape(batch * seq, d_model)
    for l in range(depth):
        w_up_p = jnp.pad(gla_w_up[l], ((0, LANES - rank), (0, 0))).astype(BF16)
        y_conv, y_gla, w_out_bf16 = _proj_mix(
            x2d, norm_g[l][None, :], w_in[l].T, conv_w[l], conv_b[l][None, :], w_up_p,
            gla_b_gate[l][None, :], gla_norm_g[l], w_out[l], batch=batch, seq=seq,
            conv_width=conv_width, dk_total=dk_total, gla_width=gla_width, n_cols=n_pad)
        x2d = _out_proj(y_conv, y_gla, w_out_bf16, x2d, final_g[None, :],
                        final_norm=(l == depth - 1))
    return x2d.reshape(batch, seq, d_model)
```

```python
import functools

import jax
import jax.numpy as jnp
from jax import lax
from jax.experimental import pallas as pl
from jax.experimental.pallas import tpu as pltpu

F32 = jnp.float32
BF16 = jnp.bfloat16

LANES = 128
MXU_N = 256
EPS = 1e-6
CHUNK = 64
GLA_HEADS = 4
GLA_TAU = 16.0

MIX_T = 256
STRIP = 16
COL_CHUNK = 256
W_ROWS = 256
W_SLOTS = 4
IDLE_TAIL_PIECES = 2
PIN_LAG = 1
OUT_TM = 512
VMEM_LIMIT = 58 * 1024 * 1024


def _silu(v):
    return v * jax.nn.sigmoid(v)


def _log_sigmoid(v):
    return -(jnp.maximum(-v, 0.0) + jnp.log1p(jnp.exp(-jnp.abs(v))))


def _spread(units, n_slots):
    out = [[] for _ in range(n_slots)]
    for j, u in enumerate(units):
        out[(j * n_slots) // len(units)].append(u)
    return out


def _proj_mix_kernel(x_ref, x_hbm, ng_ref, w_hbm, convw_ref, convb_ref, wup_ref, bgate_ref, gng_ref, wo_ref,
                     yc_ref, yg_ref, wo_bf_ref,
                     w_ref, w_tail_ref, stage_ref, dma_sem, h_ref, hn_ref, pc_ref, pg_ref, carry_ref, state_ref, sbf_ref,
                     glog_ref, hilo_ref, rev_ref, kdec_ref,
                     *, n_tiles, tiles_per_seq, conv_width, dk_total, gla_width):
    t_rows = x_ref.shape[0]
    in_cols, d_model = w_hbm.shape
    n_strips = t_rows // STRIP
    n_chunks = t_rows // CHUNK
    head_k = dk_total // GLA_HEADS
    head_v = gla_width // GLA_HEADS
    off_h, off_b, off_c, off_z = (i * conv_width for i in range(4))
    gla_col0 = 4 * conv_width
    off_q = 0
    off_k = off_q + dk_total
    off_v = off_k + dk_total
    off_r = off_v + gla_width
    off_gd = off_r + gla_width
    s = pl.program_id(0)

    def strip_rows(i):
        return slice(i * STRIP, (i + 1) * STRIP)

    def conv_strip(rows):
        token = None
        for c0 in range(0, conv_width, COL_CHUNK):
            cols = slice(c0, c0 + COL_CHUNK)

            def col(off):
                return pc_ref[rows, off + c0:off + c0 + COL_CHUNK]

            u = col(off_c) * col(off_h)
            ext = jnp.concatenate([carry_ref[:, cols], u], axis=0)
            u1 = pltpu.roll(ext, 1, 0)[8:, :]
            u2 = pltpu.roll(ext, 2, 0)[8:, :]
            carry_ref[:, cols] = u[STRIP - 8:, :]
            conv = (convb_ref[:, cols] + convw_ref[0:1, cols] * u2 + convw_ref[1:2, cols] * u1
                    + convw_ref[2:3, cols] * u)
            y = (col(off_b) * conv * _silu(col(off_z))).astype(yc_ref.dtype)
            yc_ref[rows, cols] = y
            token = token_of(y) if token is None else token | token_of(y)
        return token

    def gate_pre():
        glog_ref[...] = jnp.dot(pg_ref[:, off_gd:off_gd + LANES], wup_ref[...],
                                preferred_element_type=F32)

    def gate_strip(rows):
        glog = _log_sigmoid(glog_ref[rows, :] + bgate_ref[...]) / GLA_TAU
        g_hi = glog.astype(BF16)
        hilo_ref[rows, :dk_total] = g_hi
        hilo_ref[rows, dk_total:] = (glog - g_hi.astype(F32)).astype(BF16)

    def later_sums():
        r_i = lax.broadcasted_iota(jnp.int32, (t_rows + 8, t_rows), 0)
        c_i = lax.broadcasted_iota(jnp.int32, (t_rows + 8, t_rows), 1)
        c_chunk = c_i // CHUNK
        ones = ((c_i > r_i) & (c_chunk == r_i // CHUNK)) | (c_chunk == r_i - t_rows)
        sums = jnp.dot(jnp.where(ones, 1.0, 0.0).astype(BF16), hilo_ref[...],
                       preferred_element_type=F32)
        rev_ref[...] = sums[:, :dk_total] + sums[:, dk_total:]

    def kdec_strip(rows):
        k = pg_ref[rows, off_k:off_k + dk_total].astype(F32)
        kdec_ref[rows, :] = (k * jnp.exp(rev_ref[rows, :])).astype(BF16)

    def state_update(step):
        c, h = divmod(step, GLA_HEADS)
        rows = slice(c * CHUNK, (c + 1) * CHUNK)
        ks = slice(h * head_k, (h + 1) * head_k)
        decay = jnp.exp(rev_ref[t_rows + c:t_rows + c + 1, ks])
        v = pg_ref[rows, off_v + h * head_v:off_v + (h + 1) * head_v]
        inc = lax.dot_general(v, kdec_ref[rows, ks], (((0,), (0,)), ((), ())),
                              preferred_element_type=F32)
        s_new = state_ref[h] * decay + inc
        state_ref[h] = s_new
        sbf_ref[h] = s_new.astype(BF16)

    def readout(step):
        c, h = divmod(step, GLA_HEADS)
        rows = slice(c * CHUNK, (c + 1) * CHUNK)
        q = pg_ref[rows, off_q + h * head_k:off_q + (h + 1) * head_k]
        o = lax.dot_general(q, sbf_ref[h], (((1,), (1,)), ((), ())),
                            preferred_element_type=F32) * (head_k ** -0.5)
        ms = jnp.mean(o * o, axis=-1, keepdims=True)
        o = o * lax.rsqrt(ms + EPS) * gng_ref[h:h + 1, :]
        r = pg_ref[rows, off_r + h * head_v:off_r + (h + 1) * head_v].astype(F32)
        yg_ref[rows, h * head_v:(h + 1) * head_v] = (o * _silu(r)).astype(yg_ref.dtype)

    def token_of(y):
        bits = pltpu.bitcast(y, jnp.uint32)
        token = bits[:, :LANES]
        for l0 in range(LANES, bits.shape[1], LANES):
            token = token | bits[:, l0:l0 + LANES]
        return token

    def norm_strip(src, dst, rows):
        x = src[rows, :]
        ms = jnp.mean(x * x, axis=-1, keepdims=True)
        h = (x * lax.rsqrt(ms + EPS) * ng_ref[...]).astype(BF16)
        dst[rows, :] = h
        return token_of(h)

    def pin(tokens):
        acc = tokens[0]
        for t in tokens[1:]:
            acc = acc | t
        zero = pltpu.bitcast((acc >> 16) >> 16, F32)
        zero = jnp.concatenate([zero] * (STRIP // zero.shape[0]), axis=0)
        tile = h_ref[0:STRIP, 0:LANES].astype(F32)
        h_ref[0:STRIP, 0:LANES] = (tile + zero).astype(BF16)

    def proj_piece(dst, dst_col0, c0, width):
        def piece():
            g0 = dst_col0 + c0
            if g0 < w_ref.shape[0] * W_ROWS:
                w = w_ref[g0 // W_ROWS, :, g0 % W_ROWS:g0 % W_ROWS + width]
            else:
                w = w_tail_ref[...]
            dst[:, c0:c0 + width] = jnp.dot(h_ref[...], w, preferred_element_type=F32).astype(dst.dtype)
        return piece

    def pieces(dst, dst_col0, off, width):
        return [proj_piece(dst, dst_col0, c0, min(MXU_N, off + width - c0))
                for c0 in range(off, off + width, MXU_N)]

    @pl.when(s == 0)
    def _():
        pc_ref[...] = jnp.zeros_like(pc_ref)

        n_full = in_cols // W_ROWS

        def chunk_copy(i, rows=W_ROWS):
            slot = i % W_SLOTS
            return pltpu.make_async_copy(w_hbm.at[pl.ds(i * W_ROWS, rows), :],
                                         stage_ref.at[slot, pl.ds(0, rows), :], dma_sem.at[slot])

        def transpose_chunk(slot, rows, dst):
            for k0 in range(0, d_model, MXU_N):
                blk = stage_ref[slot, 0:rows, k0:k0 + MXU_N]
                dst[k0:k0 + MXU_N, :] = blk.T.astype(BF16)

        for i in range(W_SLOTS - 1):
            chunk_copy(i).start()

        def load_chunk(i, c):
            @pl.when(i + W_SLOTS - 1 < n_full)
            def _():
                chunk_copy(i + W_SLOTS - 1).start()
            chunk_copy(i).wait()
            transpose_chunk(i % W_SLOTS, W_ROWS, w_ref.at[i])
            return c

        lax.fori_loop(0, n_full, load_chunk, 0)

        tail_rows = in_cols - n_full * W_ROWS
        stage_ref[n_full % W_SLOTS, tail_rows:LANES, :] = jnp.zeros((LANES - tail_rows, d_model), F32)
        chunk_copy(n_full, tail_rows).start()
        chunk_copy(n_full, tail_rows).wait()
        transpose_chunk(n_full % W_SLOTS, LANES, w_tail_ref)

        x0_copy = pltpu.make_async_copy(x_hbm.at[pl.ds(0, t_rows), :],
                                        stage_ref.at[0, pl.ds(0, t_rows), :], dma_sem.at[0])
        x0_copy.start()
        x0_copy.wait()

        def norm_first(i, c):
            norm_strip(stage_ref.at[0], hn_ref, pl.ds(pl.multiple_of(i * STRIP, STRIP), STRIP))
            return c

        lax.fori_loop(0, n_strips, norm_first, 0, unroll=2)

    @pl.when((s == 0) | ((s + tiles_per_seq - 1) % tiles_per_seq == 0))
    def _():
        carry_ref[...] = jnp.zeros_like(carry_ref)

    @pl.when(s < n_tiles)
    def _():
        @pl.when(s % tiles_per_seq == 0)
        def _():
            state_ref[...] = jnp.zeros_like(state_ref)

        wo_bf_ref[...] = wo_ref[...].astype(BF16)

        h_ref[...] = hn_ref[...]

        norm_units = [functools.partial(norm_strip, x_ref, hn_ref, strip_rows(i))
                      for i in range(n_strips)]
        conv_units = [functools.partial(conv_strip, strip_rows(i)) for i in range(n_strips)]
        gate_units = [functools.partial(gate_strip, strip_rows(i)) for i in range(n_strips)]
        kdec_units = [functools.partial(kdec_strip, strip_rows(i)) for i in range(n_strips)]
        n_steps = n_chunks * GLA_HEADS

        mxu = [p for off, width in ((off_gd, LANES), (off_k, dk_total), (off_q, dk_total),
                                    (off_v, gla_width), (off_r, gla_width))
               for p in pieces(pg_ref, gla_col0, off, width)]
        n_k = dk_total // MXU_N
        vpu = [conv_units[:2]]
        vpu += _spread([gate_pre] + gate_units, n_k)
        vpu += [[later_sums, conv_units[2]]] + [kdec_units + [conv_units[3]]]
        vpu += [[] for _ in range(dk_total // MXU_N - 2)]
        rest = _spread(conv_units[4:], len(mxu) - len(vpu))
        rest[-1].append(functools.partial(state_update, 0))
        vpu += rest
        assert len(vpu) == len(mxu)

        mxu2 = pieces(pc_ref, 0, off_h, 4 * conv_width)
        halves = []
        for step in range(n_steps):
            if step + 1 < n_steps:
                halves.append(functools.partial(state_update, step + 1))
            halves.append(functools.partial(readout, step))
        busy = len(mxu2) - IDLE_TAIL_PIECES
        vpu2 = [a + b for a, b in zip(_spread(halves, busy), _spread(norm_units, busy))]
        vpu2 += [[] for _ in range(IDLE_TAIL_PIECES)]

        tokens = []
        for piece, units in zip(mxu + mxu2, vpu + vpu2):
            if len(tokens) >= PIN_LAG and tokens[-PIN_LAG]:
                pin(tokens[-PIN_LAG])
            piece()
            tokens.append([t for t in [unit() for unit in units] if t is not None])

    @pl.when(s == n_tiles)
    def _():
        def step(i, c):
            conv_strip(pl.ds(pl.multiple_of(i * STRIP, STRIP), STRIP))
            return c
        lax.fori_loop(0, n_strips, step, 0, unroll=2)


def _proj_mix(x2d, norm_g, w_in_t, conv_w, conv_b, w_up_pad, b_gate, gla_norm_g, w_out, *,
              batch, seq, conv_width, dk_total, gla_width, n_cols):
    t = MIX_T
    nt = seq // t
    n_tiles = batch * nt
    d_model = x2d.shape[1]
    wo_rows = w_out.shape[0] // n_tiles
    assert w_out.shape[0] % n_tiles == 0 and wo_rows % STRIP == 0 and t <= W_ROWS
    main_cols = n_cols - LANES
    assert main_cols % W_ROWS == 0 and W_ROWS % MXU_N == 0 and 0 < w_in_t.shape[0] - main_cols <= LANES
    head_k = dk_total // GLA_HEADS
    head_v = gla_width // GLA_HEADS

    def whole(shape, **kw):
        return pl.BlockSpec(shape, lambda s: (0,) * len(shape), **kw)

    kern = functools.partial(_proj_mix_kernel, n_tiles=n_tiles, tiles_per_seq=nt,
                             conv_width=conv_width, dk_total=dk_total, gla_width=gla_width)
    return pl.pallas_call(
        kern,
        grid=(n_tiles + 1,),
        in_specs=[
            pl.BlockSpec((t, d_model), lambda s: (jnp.minimum(s + 1, n_tiles - 1), 0)),
            pl.BlockSpec(memory_space=pl.ANY),
            whole(norm_g.shape),
            pl.BlockSpec(memory_space=pl.ANY),
            whole(conv_w.shape), whole(conv_b.shape), whole(w_up_pad.shape), whole(b_gate.shape),
            whole(gla_norm_g.shape),
            pl.BlockSpec((wo_rows, w_out.shape[1]), lambda s: (jnp.minimum(s, n_tiles - 1), 0)),
        ],
        out_specs=[
            pl.BlockSpec((t, conv_width), lambda s: (jnp.maximum(s - 1, 0), 0)),
            pl.BlockSpec((t, gla_width), lambda s: (jnp.minimum(s, n_tiles - 1), 0)),
            pl.BlockSpec((wo_rows, w_out.shape[1]), lambda s: (jnp.minimum(s, n_tiles - 1), 0)),
        ],
        out_shape=[jax.ShapeDtypeStruct((batch * seq, conv_width), BF16),
                   jax.ShapeDtypeStruct((batch * seq, gla_width), BF16),
                   jax.ShapeDtypeStruct(w_out.shape, BF16)],
        scratch_shapes=[pltpu.VMEM((main_cols // W_ROWS, d_model, W_ROWS), BF16),
                        pltpu.VMEM((d_model, LANES), BF16),
                        pltpu.VMEM((W_SLOTS, W_ROWS, d_model), F32),
                        pltpu.SemaphoreType.DMA((W_SLOTS,)),
                        pltpu.VMEM((t, d_model), BF16),
                        pltpu.VMEM((t, d_model), BF16),
                        pltpu.VMEM((t, 4 * conv_width), F32),
                        pltpu.VMEM((t, n_cols - 4 * conv_width), BF16),
                        pltpu.VMEM((8, conv_width), F32),
                        pltpu.VMEM((GLA_HEADS, head_v, head_k), F32),
                        pltpu.VMEM((GLA_HEADS, head_v, head_k), BF16),
                        pltpu.VMEM((t, dk_total), F32),
                        pltpu.VMEM((t, 2 * dk_total), BF16),
                        pltpu.VMEM((t + 8, dk_total), F32),
                        pltpu.VMEM((t, dk_total), BF16)],
        compiler_params=pltpu.CompilerParams(
            dimension_semantics=("arbitrary",),
            vmem_limit_bytes=VMEM_LIMIT),
        name="proj_mix",
    )(x2d, x2d, norm_g, w_in_t, conv_w, conv_b, w_up_pad, b_gate, gla_norm_g, w_out)


def _out_proj_kernel(yc_ref, yg_ref, w_ref, x_ref, g_ref, o_ref, *, final_norm):
    kc = yc_ref.shape[1]
    z = (x_ref[...] + jnp.dot(yc_ref[...], w_ref[:kc, :], preferred_element_type=F32)
         + jnp.dot(yg_ref[...], w_ref[kc:, :], preferred_element_type=F32))
    if final_norm:
        ms = jnp.mean(z * z, axis=-1, keepdims=True)
        z = z * lax.rsqrt(ms + EPS) * g_ref[...]
    o_ref[...] = z


def _out_proj(y_conv, y_gla, w_out_bf16, x2d, final_g, *, final_norm):
    m, d = x2d.shape
    return pl.pallas_call(
        functools.partial(_out_proj_kernel, final_norm=final_norm),
        grid=(m // OUT_TM,),
        in_specs=[
            pl.BlockSpec((OUT_TM, y_conv.shape[1]), lambda i: (i, 0)),
            pl.BlockSpec((OUT_TM, y_gla.shape[1]), lambda i: (i, 0)),
            pl.BlockSpec(w_out_bf16.shape, lambda i: (0, 0)),
            pl.BlockSpec((OUT_TM, d), lambda i: (i, 0)),
            pl.BlockSpec((1, d), lambda i: (0, 0)),
        ],
        out_specs=pl.BlockSpec((OUT_TM, d), lambda i: (i, 0)),
        out_shape=jax.ShapeDtypeStruct((m, d), F32),
        compiler_params=pltpu.CompilerParams(
            dimension_semantics=("arbitrary",),
            vmem_limit_bytes=VMEM_LIMIT),
        name="out_proj",
    )(y_conv, y_gla, w_out_bf16, x2d, final_g)


def kernel(x, norm_g, w_in, conv_w, conv_b, gla_w_up, gla_b_gate, gla_norm_g, w_out, final_g):
    batch, seq, d_model = x.shape
    depth = norm_g.shape[0]
    conv_width = conv_w.shape[2]
    rank, dk_total = gla_w_up.shape[1], gla_w_up.shape[2]
    gla_width = gla_norm_g.shape[1] * gla_norm_g.shape[2]
    in_cols = w_in.shape[2]
    main_cols = in_cols - rank
    assert main_cols == 4 * conv_width + 2 * dk_total + 2 * gla_width
    assert main_cols % MXU_N == 0 and dk_total % MXU_N == 0 and rank <= LANES
    assert seq % MIX_T == 0 and MIX_T % CHUNK == 0 and (batch * seq) % OUT_TM == 0
    n_pad = main_cols + LANES

    x2d = x.reshape(batch * seq, d_model)
    for l in range(depth):
        w_up_p = jnp.pad(gla_w_up[l], ((0, LANES - rank), (0, 0))).astype(BF16)
        y_conv, y_gla, w_out_bf16 = _proj_mix(
            x2d, norm_g[l][None, :], w_in[l].T, conv_w[l], conv_b[l][None, :], w_up_p,
            gla_b_gate[l][None, :], gla_norm_g[l], w_out[l], batch=batch, seq=seq,
            conv_width=conv_width, dk_total=dk_total, gla_width=gla_width, n_cols=n_pad)
        x2d = _out_proj(y_conv, y_gla, w_out_bf16, x2d, final_g[None, :],
                        final_norm=(l == depth - 1))
    return x2d.reshape(batch, seq, d_model)
```

```python
import functools

import jax
import jax.numpy as jnp
from jax import lax
from jax.experimental import pallas as pl
from jax.experimental.pallas import tpu as pltpu

F32 = jnp.float32
BF16 = jnp.bfloat16

LANES = 128
MXU_N = 256
EPS = 1e-6
CHUNK = 64
GLA_HEADS = 4
GLA_TAU = 16.0

MIX_T = 256
STRIP = 16
COL_CHUNK = 256
W_ROWS = 256
W_SLOTS = 4
IDLE_TAIL_PIECES = 2
PIN_LAG = 1
OUT_TM = 512
VMEM_LIMIT = 58 * 1024 * 1024


def _silu(v):
    return v * jax.nn.sigmoid(v)


def _log_sigmoid(v):
    return -(jnp.maximum(-v, 0.0) + jnp.log1p(jnp.exp(-jnp.abs(v))))


def _spread(units, n_slots):
    out = [[] for _ in range(n_slots)]
    for j, u in enumerate(units):
        out[(j * n_slots) // len(units)].append(u)
    return out


def _proj_mix_kernel(x_ref, x_hbm, ng_ref, w_hbm, convw_ref, convb_ref, wup_ref, bgate_ref, gng_ref, wo_ref,
                     yc_ref, yg_ref, wo_bf_ref,
                     w_ref, w_tail_ref, stage_ref, dma_sem, h_ref, hn_ref, pc_ref, pe_ref, pg_ref, carry_ref, state_ref, sbf_ref,
                     glog_ref, hilo_ref, rev_ref, kdec_ref,
                     *, n_tiles, tiles_per_seq, conv_width, dk_total, gla_width):
    t_rows = x_ref.shape[0]
    in_cols, d_model = w_hbm.shape
    n_strips = t_rows // STRIP
    n_chunks = t_rows // CHUNK
    head_k = dk_total // GLA_HEADS
    head_v = gla_width // GLA_HEADS
    off_h, off_b, off_c, off_z = (i * conv_width for i in range(4))
    col_q = 4 * conv_width
    col_k = col_q + dk_total
    col_v = col_k + dk_total
    col_r = col_v + gla_width
    col_gd = col_r + gla_width
    loc_k, loc_r = 0, dk_total
    loc_q, loc_v, loc_gd = 0, dk_total, dk_total + gla_width
    s = pl.program_id(0)

    def strip_rows(i):
        return slice(i * STRIP, (i + 1) * STRIP)

    def conv_strip(rows):
        token = None
        for c0 in range(0, conv_width, COL_CHUNK):
            cols = slice(c0, c0 + COL_CHUNK)

            def col(off):
                return pc_ref[rows, off + c0:off + c0 + COL_CHUNK]

            u = col(off_c) * col(off_h)
            ext = jnp.concatenate([carry_ref[:, cols], u], axis=0)
            u1 = pltpu.roll(ext, 1, 0)[8:, :]
            u2 = pltpu.roll(ext, 2, 0)[8:, :]
            carry_ref[:, cols] = u[STRIP - 8:, :]
            conv = (convb_ref[:, cols] + convw_ref[0:1, cols] * u2 + convw_ref[1:2, cols] * u1
                    + convw_ref[2:3, cols] * u)
            y = (col(off_b) * conv * _silu(col(off_z))).astype(yc_ref.dtype)
            yc_ref[rows, cols] = y
            token = token_of(y) if token is None else token | token_of(y)
        return token

    def gate_pre():
        glog_ref[...] = jnp.dot(pg_ref[:, loc_gd:loc_gd + LANES], wup_ref[...],
                                preferred_element_type=F32)

    def gate_strip(rows):
        glog = _log_sigmoid(glog_ref[rows, :] + bgate_ref[...]) / GLA_TAU
        g_hi = glog.astype(BF16)
        hilo_ref[rows, :dk_total] = g_hi
        hilo_ref[rows, dk_total:] = (glog - g_hi.astype(F32)).astype(BF16)

    def later_sums():
        r_i = lax.broadcasted_iota(jnp.int32, (t_rows + 8, t_rows), 0)
        c_i = lax.broadcasted_iota(jnp.int32, (t_rows + 8, t_rows), 1)
        c_chunk = c_i // CHUNK
        ones = ((c_i > r_i) & (c_chunk == r_i // CHUNK)) | (c_chunk == r_i - t_rows)
        sums = jnp.dot(jnp.where(ones, 1.0, 0.0).astype(BF16), hilo_ref[...],
                       preferred_element_type=F32)
        rev_ref[...] = sums[:, :dk_total] + sums[:, dk_total:]

    def kdec_strip(rows):
        k = pe_ref[rows, loc_k:loc_k + dk_total]
        kdec_ref[rows, :] = (k * jnp.exp(rev_ref[rows, :])).astype(BF16)

    def state_update(step):
        c, h = divmod(step, GLA_HEADS)
        rows = slice(c * CHUNK, (c + 1) * CHUNK)
        ks = slice(h * head_k, (h + 1) * head_k)
        decay = jnp.exp(rev_ref[t_rows + c:t_rows + c + 1, ks])
        v = pg_ref[rows, loc_v + h * head_v:loc_v + (h + 1) * head_v]
        inc = lax.dot_general(v, kdec_ref[rows, ks], (((0,), (0,)), ((), ())),
                              preferred_element_type=F32)
        s_new = state_ref[h] * decay + inc
        state_ref[h] = s_new
        sbf_ref[h] = s_new.astype(BF16)

    def readout(step):
        c, h = divmod(step, GLA_HEADS)
        rows = slice(c * CHUNK, (c + 1) * CHUNK)
        q = pg_ref[rows, loc_q + h * head_k:loc_q + (h + 1) * head_k]
        o = lax.dot_general(q, sbf_ref[h], (((1,), (1,)), ((), ())),
                            preferred_element_type=F32) * (head_k ** -0.5)
        ms = jnp.mean(o * o, axis=-1, keepdims=True)
        o = o * lax.rsqrt(ms + EPS) * gng_ref[h:h + 1, :]
        r = pe_ref[rows, loc_r + h * head_v:loc_r + (h + 1) * head_v]
        yg_ref[rows, h * head_v:(h + 1) * head_v] = (o * _silu(r)).astype(yg_ref.dtype)

    def token_of(y):
        bits = pltpu.bitcast(y, jnp.uint32)
        token = bits[:, :LANES]
        for l0 in range(LANES, bits.shape[1], LANES):
            token = token | bits[:, l0:l0 + LANES]
        return token

    def norm_strip(src, dst, rows):
        x = src[rows, :]
        ms = jnp.mean(x * x, axis=-1, keepdims=True)
        h = (x * lax.rsqrt(ms + EPS) * ng_ref[...]).astype(BF16)
        dst[rows, :] = h
        return token_of(h)

    def pin(tokens):
        acc = tokens[0]
        for t in tokens[1:]:
            acc = acc | t
        zero = pltpu.bitcast((acc >> 16) >> 16, F32)
        zero = jnp.concatenate([zero] * (STRIP // zero.shape[0]), axis=0)
        tile = h_ref[0:STRIP, 0:LANES].astype(F32)
        h_ref[0:STRIP, 0:LANES] = (tile + zero).astype(BF16)

    def proj_piece(dst, c0, g0, width):
        def piece():
            if g0 < w_ref.shape[0] * W_ROWS:
                w = w_ref[g0 // W_ROWS, :, g0 % W_ROWS:g0 % W_ROWS + width]
            else:
                w = w_tail_ref[...]
            dst[:, c0:c0 + width] = jnp.dot(h_ref[...], w, preferred_element_type=F32).astype(dst.dtype)
        return piece

    def pieces(dst, loc, col, width):
        return [proj_piece(dst, loc + i, col + i, min(MXU_N, width - i)) for i in range(0, width, MXU_N)]

    @pl.when(s == 0)
    def _():
        pc_ref[...] = jnp.zeros_like(pc_ref)

        n_full = in_cols // W_ROWS

        def chunk_copy(i, rows=W_ROWS):
            slot = i % W_SLOTS
            return pltpu.make_async_copy(w_hbm.at[pl.ds(i * W_ROWS, rows), :],
                                         stage_ref.at[slot, pl.ds(0, rows), :], dma_sem.at[slot])

        def transpose_chunk(slot, rows, dst):
            for k0 in range(0, d_model, MXU_N):
                blk = stage_ref[slot, 0:rows, k0:k0 + MXU_N]
                dst[k0:k0 + MXU_N, :] = blk.T.astype(BF16)

        for i in range(W_SLOTS - 1):
            chunk_copy(i).start()

        def load_chunk(i, c):
            @pl.when(i + W_SLOTS - 1 < n_full)
            def _():
                chunk_copy(i + W_SLOTS - 1).start()
            chunk_copy(i).wait()
            transpose_chunk(i % W_SLOTS, W_ROWS, w_ref.at[i])
            return c

        lax.fori_loop(0, n_full, load_chunk, 0)

        tail_rows = in_cols - n_full * W_ROWS
        stage_ref[n_full % W_SLOTS, tail_rows:LANES, :] = jnp.zeros((LANES - tail_rows, d_model), F32)
        chunk_copy(n_full, tail_rows).start()
        chunk_copy(n_full, tail_rows).wait()
        transpose_chunk(n_full % W_SLOTS, LANES, w_tail_ref)

        x0_copy = pltpu.make_async_copy(x_hbm.at[pl.ds(0, t_rows), :],
                                        stage_ref.at[0, pl.ds(0, t_rows), :], dma_sem.at[0])
        x0_copy.start()
        x0_copy.wait()

        def norm_first(i, c):
            norm_strip(stage_ref.at[0], hn_ref, pl.ds(pl.multiple_of(i * STRIP, STRIP), STRIP))
            return c

        lax.fori_loop(0, n_strips, norm_first, 0, unroll=2)

    @pl.when((s == 0) | ((s + tiles_per_seq - 1) % tiles_per_seq == 0))
    def _():
        carry_ref[...] = jnp.zeros_like(carry_ref)

    @pl.when(s < n_tiles)
    def _():
        @pl.when(s % tiles_per_seq == 0)
        def _():
            state_ref[...] = jnp.zeros_like(state_ref)

        wo_bf_ref[...] = wo_ref[...].astype(BF16)

        h_ref[...] = hn_ref[...]

        norm_units = [functools.partial(norm_strip, x_ref, hn_ref, strip_rows(i))
                      for i in range(n_strips)]
        conv_units = [functools.partial(conv_strip, strip_rows(i)) for i in range(n_strips)]
        gate_units = [functools.partial(gate_strip, strip_rows(i)) for i in range(n_strips)]
        kdec_units = [functools.partial(kdec_strip, strip_rows(i)) for i in range(n_strips)]
        n_steps = n_chunks * GLA_HEADS

        mxu = (pieces(pg_ref, loc_gd, col_gd, LANES) + pieces(pe_ref, loc_k, col_k, dk_total)
               + pieces(pg_ref, loc_q, col_q, dk_total) + pieces(pg_ref, loc_v, col_v, gla_width)
               + pieces(pe_ref, loc_r, col_r, gla_width))
        n_k = dk_total // MXU_N
        vpu = [conv_units[:2]]
        vpu += _spread([gate_pre] + gate_units, n_k)
        vpu += [[later_sums, conv_units[2]]] + [kdec_units + [conv_units[3]]]
        vpu += [[] for _ in range(dk_total // MXU_N - 2)]
        rest = _spread(conv_units[4:], len(mxu) - len(vpu))
        rest[-1].append(functools.partial(state_update, 0))
        vpu += rest
        assert len(vpu) == len(mxu)

        mxu2 = pieces(pc_ref, 0, 0, 4 * conv_width)
        halves = []
        for step in range(n_steps):
            if step + 1 < n_steps:
                halves.append(functools.partial(state_update, step + 1))
            halves.append(functools.partial(readout, step))
        busy = len(mxu2) - IDLE_TAIL_PIECES
        vpu2 = [a + b for a, b in zip(_spread(halves, busy), _spread(norm_units, busy))]
        vpu2 += [[] for _ in range(IDLE_TAIL_PIECES)]

        tokens = []
        for piece, units in zip(mxu + mxu2, vpu + vpu2):
            if len(tokens) >= PIN_LAG and tokens[-PIN_LAG]:
                pin(tokens[-PIN_LAG])
            piece()
            tokens.append([t for t in [unit() for unit in units] if t is not None])

    @pl.when(s == n_tiles)
    def _():
        def step(i, c):
            conv_strip(pl.ds(pl.multiple_of(i * STRIP, STRIP), STRIP))
            return c
        lax.fori_loop(0, n_strips, step, 0, unroll=2)


def _proj_mix(x2d, norm_g, w_in_t, conv_w, conv_b, w_up_pad, b_gate, gla_norm_g, w_out, *,
              batch, seq, conv_width, dk_total, gla_width, n_cols):
    t = MIX_T
    nt = seq // t
    n_tiles = batch * nt
    d_model = x2d.shape[1]
    wo_rows = w_out.shape[0] // n_tiles
    assert w_out.shape[0] % n_tiles == 0 and wo_rows % STRIP == 0 and t <= W_ROWS
    main_cols = n_cols - LANES
    assert main_cols % W_ROWS == 0 and W_ROWS % MXU_N == 0 and 0 < w_in_t.shape[0] - main_cols <= LANES
    head_k = dk_total // GLA_HEADS
    head_v = gla_width // GLA_HEADS

    def whole(shape, **kw):
        return pl.BlockSpec(shape, lambda s: (0,) * len(shape), **kw)

    kern = functools.partial(_proj_mix_kernel, n_tiles=n_tiles, tiles_per_seq=nt,
                             conv_width=conv_width, dk_total=dk_total, gla_width=gla_width)
    return pl.pallas_call(
        kern,
        grid=(n_tiles + 1,),
        in_specs=[
            pl.BlockSpec((t, d_model), lambda s: (jnp.minimum(s + 1, n_tiles - 1), 0)),
            pl.BlockSpec(memory_space=pl.ANY),
            whole(norm_g.shape),
            pl.BlockSpec(memory_space=pl.ANY),
            whole(conv_w.shape), whole(conv_b.shape), whole(w_up_pad.shape), whole(b_gate.shape),
            whole(gla_norm_g.shape),
            pl.BlockSpec((wo_rows, w_out.shape[1]), lambda s: (jnp.minimum(s, n_tiles - 1), 0)),
        ],
        out_specs=[
            pl.BlockSpec((t, conv_width), lambda s: (jnp.maximum(s - 1, 0), 0)),
            pl.BlockSpec((t, gla_width), lambda s: (jnp.minimum(s, n_tiles - 1), 0)),
            pl.BlockSpec((wo_rows, w_out.shape[1]), lambda s: (jnp.minimum(s, n_tiles - 1), 0)),
        ],
        out_shape=[jax.ShapeDtypeStruct((batch * seq, conv_width), BF16),
                   jax.ShapeDtypeStruct((batch * seq, gla_width), BF16),
                   jax.ShapeDtypeStruct(w_out.shape, BF16)],
        scratch_shapes=[pltpu.VMEM((main_cols // W_ROWS, d_model, W_ROWS), BF16),
                        pltpu.VMEM((d_model, LANES), BF16),
                        pltpu.VMEM((W_SLOTS, W_ROWS, d_model), F32),
                        pltpu.SemaphoreType.DMA((W_SLOTS,)),
                        pltpu.VMEM((t, d_model), BF16),
                        pltpu.VMEM((t, d_model), BF16),
                        pltpu.VMEM((t, 4 * conv_width), F32),
                        pltpu.VMEM((t, dk_total + gla_width), F32),
                        pltpu.VMEM((t, dk_total + gla_width + LANES), BF16),
                        pltpu.VMEM((8, conv_width), F32),
                        pltpu.VMEM((GLA_HEADS, head_v, head_k), F32),
                        pltpu.VMEM((GLA_HEADS, head_v, head_k), BF16),
                        pltpu.VMEM((t, dk_total), F32),
                        pltpu.VMEM((t, 2 * dk_total), BF16),
                        pltpu.VMEM((t + 8, dk_total), F32),
                        pltpu.VMEM((t, dk_total), BF16)],
        compiler_params=pltpu.CompilerParams(
            dimension_semantics=("arbitrary",),
            vmem_limit_bytes=VMEM_LIMIT),
        name="proj_mix",
    )(x2d, x2d, norm_g, w_in_t, conv_w, conv_b, w_up_pad, b_gate, gla_norm_g, w_out)


def _out_proj_kernel(yc_ref, yg_ref, w_ref, x_ref, g_ref, o_ref, *, final_norm):
    kc = yc_ref.shape[1]
    z = (x_ref[...] + jnp.dot(yc_ref[...], w_ref[:kc, :], preferred_element_type=F32)
         + jnp.dot(yg_ref[...], w_ref[kc:, :], preferred_element_type=F32))
    if final_norm:
        ms = jnp.mean(z * z, axis=-1, keepdims=True)
        z = z * lax.rsqrt(ms + EPS) * g_ref[...]
    o_ref[...] = z


def _out_proj(y_conv, y_gla, w_out_bf16, x2d, final_g, *, final_norm):
    m, d = x2d.shape
    return pl.pallas_call(
        functools.partial(_out_proj_kernel, final_norm=final_norm),
        grid=(m // OUT_TM,),
        in_specs=[
            pl.BlockSpec((OUT_TM, y_conv.shape[1]), lambda i: (i, 0)),
            pl.BlockSpec((OUT_TM, y_gla.shape[1]), lambda i: (i, 0)),
            pl.BlockSpec(w_out_bf16.shape, lambda i: (0, 0)),
            pl.BlockSpec((OUT_TM, d), lambda i: (i, 0)),
            pl.BlockSpec((1, d), lambda i: (0, 0)),
        ],
        out_specs=pl.BlockSpec((OUT_TM, d), lambda i: (i, 0)),
        out_shape=jax.ShapeDtypeStruct((m, d), F32),
        compiler_params=pltpu.CompilerParams(
            dimension_semantics=("arbitrary",),
            vmem_limit_bytes=VMEM_LIMIT),
        name="out_proj",
    )(y_conv, y_gla, w_out_bf16, x2d, final_g)


def kernel(x, norm_g, w_in, conv_w, conv_b, gla_w_up, gla_b_gate, gla_norm_g, w_out, final_g):
    batch, seq, d_model = x.shape
    depth = norm_g.shape[0]
    conv_width = conv_w.shape[2]
    rank, dk_total = gla_w_up.shape[1], gla_w_up.shape[2]
    gla_width = gla_norm_g.shape[1] * gla_norm_g.shape[2]
    in_cols = w_in.shape[2]
    main_cols = in_cols - rank
    assert main_cols == 4 * conv_width + 2 * dk_total + 2 * gla_width
    assert main_cols % MXU_N == 0 and dk_total % MXU_N == 0 and rank <= LANES
    assert seq % MIX_T == 0 and MIX_T % CHUNK == 0 and (batch * seq) % OUT_TM == 0
    n_pad = main_cols + LANES

    x2d = x.reshape(batch * seq, d_model)
    for l in range(depth):
        w_up_p = jnp.pad(gla_w_up[l], ((0, LANES - rank), (0, 0))).astype(BF16)
        y_conv, y_gla, w_out_bf16 = _proj_mix(
            x2d, norm_g[l][None, :], w_in[l].T, conv_w[l], conv_b[l][None, :], w_up_p,
            gla_b_gate[l][None, :], gla_norm_g[l], w_out[l], batch=batch, seq=seq,
            conv_width=conv_width, dk_total=dk_total, gla_width=gla_width, n_cols=n_pad)
        x2d = _out_proj(y_conv, y_gla, w_out_bf16, x2d, final_g[None, :],
                        final_norm=(l == depth - 1))
    return x2d.reshape(batch, seq, d_model)
```

```python
import functools

import jax
import jax.numpy as jnp
from jax import lax
from jax.experimental import pallas as pl
from jax.experimental.pallas import tpu as pltpu

F32 = jnp.float32
BF16 = jnp.bfloat16

LANES = 128
MXU_N = 256
EPS = 1e-6
CHUNK = 64
GLA_HEADS = 4
GLA_TAU = 16.0

MIX_T = 256
STRIP = 16
COL_CHUNK = 256
W_ROWS = 256
W_SLOTS = 4
IDLE_TAIL_PIECES = 1
PIN_LAG = 1
OUT_TM = 512
VMEM_LIMIT = 58 * 1024 * 1024


def _silu(v):
    return v * jax.nn.sigmoid(v)


def _log_sigmoid(v):
    return -(jnp.maximum(-v, 0.0) + jnp.log1p(jnp.exp(-jnp.abs(v))))


def _spread(units, n_slots):
    out = [[] for _ in range(n_slots)]
    for j, u in enumerate(units):
        out[(j * n_slots) // len(units)].append(u)
    return out


def _proj_mix_kernel(x_ref, x_hbm, ng_ref, w_hbm, convw_ref, convb_ref, wup_ref, bgate_ref, gng_ref, wo_ref,
                     yc_ref, yg_ref, wo_bf_ref,
                     w_ref, w_tail_ref, stage_ref, dma_sem, h_ref, hn_ref, pc_ref, pg_ref, carry_ref, state_ref, sbf_ref,
                     glog_ref, hilo_ref, rev_ref, kdec_ref,
                     *, n_tiles, tiles_per_seq, conv_width, dk_total, gla_width):
    t_rows = x_ref.shape[0]
    in_cols, d_model = w_hbm.shape
    n_strips = t_rows // STRIP
    n_chunks = t_rows // CHUNK
    head_k = dk_total // GLA_HEADS
    head_v = gla_width // GLA_HEADS
    off_h, off_b, off_c, off_z = (i * conv_width for i in range(4))
    gla_col0 = 4 * conv_width
    off_q = 0
    off_k = off_q + dk_total
    off_v = off_k + dk_total
    off_r = off_v + gla_width
    off_gd = off_r + gla_width
    s = pl.program_id(0)

    def strip_rows(i):
        return slice(i * STRIP, (i + 1) * STRIP)

    def conv_strip(rows):
        token = None
        for c0 in range(0, conv_width, COL_CHUNK):
            cols = slice(c0, c0 + COL_CHUNK)

            def col(off):
                return pc_ref[rows, off + c0:off + c0 + COL_CHUNK]

            u = col(off_c) * col(off_h)
            ext = jnp.concatenate([carry_ref[:, cols], u], axis=0)
            u1 = pltpu.roll(ext, 1, 0)[8:, :]
            u2 = pltpu.roll(ext, 2, 0)[8:, :]
            carry_ref[:, cols] = u[STRIP - 8:, :]
            conv = (convb_ref[:, cols] + convw_ref[0:1, cols] * u2 + convw_ref[1:2, cols] * u1
                    + convw_ref[2:3, cols] * u)
            y = (col(off_b) * conv * _silu(col(off_z))).astype(yc_ref.dtype)
            yc_ref[rows, cols] = y
            token = token_of(y) if token is None else token | token_of(y)
        return token

    def gate_pre():
        glog_ref[...] = jnp.dot(pg_ref[:, off_gd:off_gd + LANES], wup_ref[...],
                                preferred_element_type=F32)

    def gate_strip(rows):
        glog = _log_sigmoid(glog_ref[rows, :] + bgate_ref[...]) / GLA_TAU
        g_hi = glog.astype(BF16)
        hilo_ref[rows, :dk_total] = g_hi
        hilo_ref[rows, dk_total:] = (glog - g_hi.astype(F32)).astype(BF16)

    def later_sums():
        r_i = lax.broadcasted_iota(jnp.int32, (t_rows + 8, t_rows), 0)
        c_i = lax.broadcasted_iota(jnp.int32, (t_rows + 8, t_rows), 1)
        c_chunk = c_i // CHUNK
        ones = ((c_i > r_i) & (c_chunk == r_i // CHUNK)) | (c_chunk == r_i - t_rows)
        sums = jnp.dot(jnp.where(ones, 1.0, 0.0).astype(BF16), hilo_ref[...],
                       preferred_element_type=F32)
        rev_ref[...] = sums[:, :dk_total] + sums[:, dk_total:]

    def kdec_strip(rows):
        k = pg_ref[rows, off_k:off_k + dk_total].astype(F32)
        kdec_ref[rows, :] = (k * jnp.exp(rev_ref[rows, :])).astype(BF16)

    def state_update(step):
        c, h = divmod(step, GLA_HEADS)
        rows = slice(c * CHUNK, (c + 1) * CHUNK)
        ks = slice(h * head_k, (h + 1) * head_k)
        decay = jnp.exp(rev_ref[t_rows + c:t_rows + c + 1, ks])
        v = pg_ref[rows, off_v + h * head_v:off_v + (h + 1) * head_v]
        inc = lax.dot_general(v, kdec_ref[rows, ks], (((0,), (0,)), ((), ())),
                              preferred_element_type=F32)
        s_new = state_ref[h] * decay + inc
        state_ref[h] = s_new
        sbf_ref[h] = s_new.astype(BF16)

    def readout(step):
        c, h = divmod(step, GLA_HEADS)
        rows = slice(c * CHUNK, (c + 1) * CHUNK)
        q = pg_ref[rows, off_q + h * head_k:off_q + (h + 1) * head_k]
        o = lax.dot_general(q, sbf_ref[h], (((1,), (1,)), ((), ())),
                            preferred_element_type=F32) * (head_k ** -0.5)
        ms = jnp.mean(o * o, axis=-1, keepdims=True)
        o = o * lax.rsqrt(ms + EPS) * gng_ref[h:h + 1, :]
        r = pg_ref[rows, off_r + h * head_v:off_r + (h + 1) * head_v].astype(F32)
        yg_ref[rows, h * head_v:(h + 1) * head_v] = (o * _silu(r)).astype(yg_ref.dtype)

    def token_of(y):
        bits = pltpu.bitcast(y, jnp.uint32)
        token = bits[:, :LANES]
        for l0 in range(LANES, bits.shape[1], LANES):
            token = token | bits[:, l0:l0 + LANES]
        return token

    def norm_strip(src, dst, rows):
        x = src[rows, :]
        ms = jnp.mean(x * x, axis=-1, keepdims=True)
        h = (x * lax.rsqrt(ms + EPS) * ng_ref[...]).astype(BF16)
        dst[rows, :] = h
        return token_of(h)

    def pin(tokens):
        acc = tokens[0]
        for t in tokens[1:]:
            acc = acc | t
        zero = pltpu.bitcast((acc >> 16) >> 16, F32)
        zero = jnp.concatenate([zero] * (STRIP // zero.shape[0]), axis=0)
        tile = h_ref[0:STRIP, 0:LANES].astype(F32)
        h_ref[0:STRIP, 0:LANES] = (tile + zero).astype(BF16)

    def proj_piece(dst, dst_col0, c0, width):
        def piece():
            g0 = dst_col0 + c0
            if g0 < w_ref.shape[0] * W_ROWS:
                w = w_ref[g0 // W_ROWS, :, g0 % W_ROWS:g0 % W_ROWS + width]
            else:
                w = w_tail_ref[...]
            dst[:, c0:c0 + width] = jnp.dot(h_ref[...], w, preferred_element_type=F32).astype(dst.dtype)
        return piece

    def pieces(dst, dst_col0, off, width):
        return [proj_piece(dst, dst_col0, c0, min(MXU_N, off + width - c0))
                for c0 in range(off, off + width, MXU_N)]

    @pl.when(s == 0)
    def _():
        pc_ref[...] = jnp.zeros_like(pc_ref)

        n_full = in_cols // W_ROWS

        def chunk_copy(i, rows=W_ROWS):
            slot = i % W_SLOTS
            return pltpu.make_async_copy(w_hbm.at[pl.ds(i * W_ROWS, rows), :],
                                         stage_ref.at[slot, pl.ds(0, rows), :], dma_sem.at[slot])

        def transpose_chunk(slot, rows, dst):
            for k0 in range(0, d_model, MXU_N):
                blk = stage_ref[slot, 0:rows, k0:k0 + MXU_N]
                dst[k0:k0 + MXU_N, :] = blk.T.astype(BF16)

        for i in range(W_SLOTS - 1):
            chunk_copy(i).start()

        def load_chunk(i, c):
            @pl.when(i + W_SLOTS - 1 < n_full)
            def _():
                chunk_copy(i + W_SLOTS - 1).start()
            chunk_copy(i).wait()
            transpose_chunk(i % W_SLOTS, W_ROWS, w_ref.at[i])
            return c

        lax.fori_loop(0, n_full, load_chunk, 0)

        tail_rows = in_cols - n_full * W_ROWS
        stage_ref[n_full % W_SLOTS, tail_rows:LANES, :] = jnp.zeros((LANES - tail_rows, d_model), F32)
        chunk_copy(n_full, tail_rows).start()
        chunk_copy(n_full, tail_rows).wait()
        transpose_chunk(n_full % W_SLOTS, LANES, w_tail_ref)

        x0_copy = pltpu.make_async_copy(x_hbm.at[pl.ds(0, t_rows), :],
                                        stage_ref.at[0, pl.ds(0, t_rows), :], dma_sem.at[0])
        x0_copy.start()
        x0_copy.wait()

        def norm_first(i, c):
            norm_strip(stage_ref.at[0], hn_ref, pl.ds(pl.multiple_of(i * STRIP, STRIP), STRIP))
            return c

        lax.fori_loop(0, n_strips, norm_first, 0, unroll=2)

    @pl.when((s == 0) | ((s + tiles_per_seq - 1) % tiles_per_seq == 0))
    def _():
        carry_ref[...] = jnp.zeros_like(carry_ref)

    @pl.when(s < n_tiles)
    def _():
        @pl.when(s % tiles_per_seq == 0)
        def _():
            state_ref[...] = jnp.zeros_like(state_ref)

        wo_bf_ref[...] = wo_ref[...].astype(BF16)

        h_ref[...] = hn_ref[...]

        norm_units = [functools.partial(norm_strip, x_ref, hn_ref, strip_rows(i))
                      for i in range(n_strips)]
        conv_units = [functools.partial(conv_strip, strip_rows(i)) for i in range(n_strips)]
        gate_units = [functools.partial(gate_strip, strip_rows(i)) for i in range(n_strips)]
        kdec_units = [functools.partial(kdec_strip, strip_rows(i)) for i in range(n_strips)]
        n_steps = n_chunks * GLA_HEADS

        mxu = [p for off, width in ((off_gd, LANES), (off_k, dk_total), (off_q, dk_total),
                                    (off_v, gla_width), (off_r, gla_width))
               for p in pieces(pg_ref, gla_col0, off, width)]
        n_k = dk_total // MXU_N
        vpu = [conv_units[:2]]
        vpu += _spread([gate_pre] + gate_units, n_k)
        vpu += [[later_sums, conv_units[2]]] + [kdec_units + [conv_units[3]]]
        vpu += [[] for _ in range(dk_total // MXU_N - 2)]
        rest = _spread(conv_units[4:], len(mxu) - len(vpu))
        rest[-1].append(functools.partial(state_update, 0))
        vpu += rest
        assert len(vpu) == len(mxu)

        mxu2 = pieces(pc_ref, 0, off_h, 4 * conv_width)
        halves = []
        for step in range(n_steps):
            if step + 1 < n_steps:
                halves.append(functools.partial(state_update, step + 1))
            halves.append(functools.partial(readout, step))
        busy = len(mxu2) - IDLE_TAIL_PIECES
        vpu2 = [a + b for a, b in zip(_spread(halves, busy), _spread(norm_units, busy))]
        vpu2 += [[] for _ in range(IDLE_TAIL_PIECES)]

        tokens = []
        for piece, units in zip(mxu + mxu2, vpu + vpu2):
            if len(tokens) >= PIN_LAG and tokens[-PIN_LAG]:
                pin(tokens[-PIN_LAG])
            piece()
            tokens.append([t for t in [unit() for unit in units] if t is not None])

    @pl.when(s == n_tiles)
    def _():
        def step(i, c):
            conv_strip(pl.ds(pl.multiple_of(i * STRIP, STRIP), STRIP))
            return c
        lax.fori_loop(0, n_strips, step, 0, unroll=2)


def _proj_mix(x2d, norm_g, w_in_t, conv_w, conv_b, w_up_pad, b_gate, gla_norm_g, w_out, *,
              batch, seq, conv_width, dk_total, gla_width, n_cols):
    t = MIX_T
    nt = seq // t
    n_tiles = batch * nt
    d_model = x2d.shape[1]
    wo_rows = w_out.shape[0] // n_tiles
    assert w_out.shape[0] % n_tiles == 0 and wo_rows % STRIP == 0 and t <= W_ROWS
    main_cols = n_cols - LANES
    assert main_cols % W_ROWS == 0 and W_ROWS % MXU_N == 0 and 0 < w_in_t.shape[0] - main_cols <= LANES
    head_k = dk_total // GLA_HEADS
    head_v = gla_width // GLA_HEADS

    def whole(shape, **kw):
        return pl.BlockSpec(shape, lambda s: (0,) * len(shape), **kw)

    kern = functools.partial(_proj_mix_kernel, n_tiles=n_tiles, tiles_per_seq=nt,
                             conv_width=conv_width, dk_total=dk_total, gla_width=gla_width)
    return pl.pallas_call(
        kern,
        grid=(n_tiles + 1,),
        in_specs=[
            pl.BlockSpec((t, d_model), lambda s: (jnp.minimum(s + 1, n_tiles - 1), 0)),
            pl.BlockSpec(memory_space=pl.ANY),
            whole(norm_g.shape),
            pl.BlockSpec(memory_space=pl.ANY),
            whole(conv_w.shape), whole(conv_b.shape), whole(w_up_pad.shape), whole(b_gate.shape),
            whole(gla_norm_g.shape),
            pl.BlockSpec((wo_rows, w_out.shape[1]), lambda s: (jnp.minimum(s, n_tiles - 1), 0)),
        ],
        out_specs=[
            pl.BlockSpec((t, conv_width), lambda s: (jnp.maximum(s - 1, 0), 0)),
            pl.BlockSpec((t, gla_width), lambda s: (jnp.minimum(s, n_tiles - 1), 0)),
            pl.BlockSpec((wo_rows, w_out.shape[1]), lambda s: (jnp.minimum(s, n_tiles - 1), 0)),
        ],
        out_shape=[jax.ShapeDtypeStruct((batch * seq, conv_width), BF16),
                   jax.ShapeDtypeStruct((batch * seq, gla_width), BF16),
                   jax.ShapeDtypeStruct(w_out.shape, BF16)],
        scratch_shapes=[pltpu.VMEM((main_cols // W_ROWS, d_model, W_ROWS), BF16),
                        pltpu.VMEM((d_model, LANES), BF16),
                        pltpu.VMEM((W_SLOTS, W_ROWS, d_model), F32),
                        pltpu.SemaphoreType.DMA((W_SLOTS,)),
                        pltpu.VMEM((t, d_model), BF16),
                        pltpu.VMEM((t, d_model), BF16),
                        pltpu.VMEM((t, 4 * conv_width), F32),
                        pltpu.VMEM((t, n_cols - 4 * conv_width), BF16),
                        pltpu.VMEM((8, conv_width), F32),
                        pltpu.VMEM((GLA_HEADS, head_v, head_k), F32),
                        pltpu.VMEM((GLA_HEADS, head_v, head_k), BF16),
                        pltpu.VMEM((t, dk_total), F32),
                        pltpu.VMEM((t, 2 * dk_total), BF16),
                        pltpu.VMEM((t + 8, dk_total), F32),
                        pltpu.VMEM((t, dk_total), BF16)],
        compiler_params=pltpu.CompilerParams(
            dimension_semantics=("arbitrary",),
            vmem_limit_bytes=VMEM_LIMIT),
        name="proj_mix",
    )(x2d, x2d, norm_g, w_in_t, conv_w, conv_b, w_up_pad, b_gate, gla_norm_g, w_out)


def _out_proj_kernel(yc_ref, yg_ref, w_ref, x_ref, g_ref, o_ref, *, final_norm):
    kc = yc_ref.shape[1]
    z = (x_ref[...] + jnp.dot(yc_ref[...], w_ref[:kc, :], preferred_element_type=F32)
         + jnp.dot(yg_ref[...], w_ref[kc:, :], preferred_element_type=F32))
    if final_norm:
        ms = jnp.mean(z * z, axis=-1, keepdims=True)
        z = z * lax.rsqrt(ms + EPS) * g_ref[...]
    o_ref[...] = z


def _out_proj(y_conv, y_gla, w_out_bf16, x2d, final_g, *, final_norm):
    m, d = x2d.shape
    return pl.pallas_call(
        functools.partial(_out_proj_kernel, final_norm=final_norm),
        grid=(m // OUT_TM,),
        in_specs=[
            pl.BlockSpec((OUT_TM, y_conv.shape[1]), lambda i: (i, 0)),
            pl.BlockSpec((OUT_TM, y_gla.shape[1]), lambda i: (i, 0)),
            pl.BlockSpec(w_out_bf16.shape, lambda i: (0, 0)),
            pl.BlockSpec((OUT_TM, d), lambda i: (i, 0)),
            pl.BlockSpec((1, d), lambda i: (0, 0)),
        ],
        out_specs=pl.BlockSpec((OUT_TM, d), lambda i: (i, 0)),
        out_shape=jax.ShapeDtypeStruct((m, d), F32),
        compiler_params=pltpu.CompilerParams(
            dimension_semantics=("arbitrary",),
            vmem_limit_bytes=VMEM_LIMIT),
        name="out_proj",
    )(y_conv, y_gla, w_out_bf16, x2d, final_g)


def kernel(x, norm_g, w_in, conv_w, conv_b, gla_w_up, gla_b_gate, gla_norm_g, w_out, final_g):
    batch, seq, d_model = x.shape
    depth = norm_g.shape[0]
    conv_width = conv_w.shape[2]
    rank, dk_total = gla_w_up.shape[1], gla_w_up.shape[2]
    gla_width = gla_norm_g.shape[1] * gla_norm_g.shape[2]
    in_cols = w_in.shape[2]
    main_cols = in_cols - rank
    assert main_cols == 4 * conv_width + 2 * dk_total + 2 * gla_width
    assert main_cols % MXU_N == 0 and dk_total % MXU_N == 0 and rank <= LANES
    assert seq % MIX_T == 0 and MIX_T % CHUNK == 0 and (batch * seq) % OUT_TM == 0
    n_pad = main_cols + LANES

    x2d = x.reshape(batch * seq, d_model)
    for l in range(depth):
        w_up_p = jnp.pad(gla_w_up[l], ((0, LANES - rank), (0, 0))).astype(BF16)
        y_conv, y_gla, w_out_bf16 = _proj_mix(
            x2d, norm_g[l][None, :], w_in[l].T, conv_w[l], conv_b[l][None, :], w_up_p,
            gla_b_gate[l][None, :], gla_norm_g[l], w_out[l], batch=batch, seq=seq,
            conv_width=conv_width, dk_total=dk_total, gla_width=gla_width, n_cols=n_pad)
        x2d = _out_proj(y_conv, y_gla, w_out_bf16, x2d, final_g[None, :],
                        final_norm=(l == depth - 1))
    return x2d.reshape(batch, seq, d_model)
```

```python
import functools

import jax
import jax.numpy as jnp
from jax import lax
from jax.experimental import pallas as pl
from jax.experimental.pallas import tpu as pltpu

F32 = jnp.float32
BF16 = jnp.bfloat16

LANES = 128
MXU_N = 256
EPS = 1e-6
CHUNK = 64
GLA_HEADS = 4
GLA_TAU = 16.0

MIX_T = 256
STRIP = 16
COL_CHUNK = 256
W_ROWS = 256
W_SLOTS = 4
IDLE_TAIL_PIECES = 0
PIN_LAG = 1
OUT_TM = 512
VMEM_LIMIT = 58 * 1024 * 1024


def _silu(v):
    return v * jax.nn.sigmoid(v)


def _log_sigmoid(v):
    return -(jnp.maximum(-v, 0.0) + jnp.log1p(jnp.exp(-jnp.abs(v))))


def _spread(units, n_slots):
    out = [[] for _ in range(n_slots)]
    for j, u in enumerate(units):
        out[(j * n_slots) // len(units)].append(u)
    return out


def _proj_mix_kernel(x_ref, x_hbm, ng_ref, w_hbm, convw_ref, convb_ref, wup_ref, bgate_ref, gng_ref, wo_ref,
                     yc_ref, yg_ref, wo_bf_ref,
                     w_ref, w_tail_ref, stage_ref, dma_sem, h_ref, hn_ref, pc_ref, pg_ref, carry_ref, state_ref, sbf_ref,
                     glog_ref, hilo_ref, rev_ref, kdec_ref,
                     *, n_tiles, tiles_per_seq, conv_width, dk_total, gla_width):
    t_rows = x_ref.shape[0]
    in_cols, d_model = w_hbm.shape
    n_strips = t_rows // STRIP
    n_chunks = t_rows // CHUNK
    head_k = dk_total // GLA_HEADS
    head_v = gla_width // GLA_HEADS
    off_h, off_b, off_c, off_z = (i * conv_width for i in range(4))
    gla_col0 = 4 * conv_width
    off_q = 0
    off_k = off_q + dk_total
    off_v = off_k + dk_total
    off_r = off_v + gla_width
    off_gd = off_r + gla_width
    s = pl.program_id(0)

    def strip_rows(i):
        return slice(i * STRIP, (i + 1) * STRIP)

    def conv_strip(rows):
        token = None
        for c0 in range(0, conv_width, COL_CHUNK):
            cols = slice(c0, c0 + COL_CHUNK)

            def col(off):
                return pc_ref[rows, off + c0:off + c0 + COL_CHUNK]

            u = col(off_c) * col(off_h)
            ext = jnp.concatenate([carry_ref[:, cols], u], axis=0)
            u1 = pltpu.roll(ext, 1, 0)[8:, :]
            u2 = pltpu.roll(ext, 2, 0)[8:, :]
            carry_ref[:, cols] = u[STRIP - 8:, :]
            conv = (convb_ref[:, cols] + convw_ref[0:1, cols] * u2 + convw_ref[1:2, cols] * u1
                    + convw_ref[2:3, cols] * u)
            y = (col(off_b) * conv * _silu(col(off_z))).astype(yc_ref.dtype)
            yc_ref[rows, cols] = y
            token = token_of(y) if token is None else token | token_of(y)
        return token

    def gate_pre():
        glog_ref[...] = jnp.dot(pg_ref[:, off_gd:off_gd + LANES], wup_ref[...],
                                preferred_element_type=F32)

    def gate_strip(rows):
        glog = _log_sigmoid(glog_ref[rows, :] + bgate_ref[...]) / GLA_TAU
        g_hi = glog.astype(BF16)
        hilo_ref[rows, :dk_total] = g_hi
        hilo_ref[rows, dk_total:] = (glog - g_hi.astype(F32)).astype(BF16)

    def later_sums():
        r_i = lax.broadcasted_iota(jnp.int32, (t_rows + 8, t_rows), 0)
        c_i = lax.broadcasted_iota(jnp.int32, (t_rows + 8, t_rows), 1)
        c_chunk = c_i // CHUNK
        ones = ((c_i > r_i) & (c_chunk == r_i // CHUNK)) | (c_chunk == r_i - t_rows)
        sums = jnp.dot(jnp.where(ones, 1.0, 0.0).astype(BF16), hilo_ref[...],
                       preferred_element_type=F32)
        rev_ref[...] = sums[:, :dk_total] + sums[:, dk_total:]

    def kdec_strip(rows):
        k = pg_ref[rows, off_k:off_k + dk_total].astype(F32)
        kdec_ref[rows, :] = (k * jnp.exp(rev_ref[rows, :])).astype(BF16)

    def state_update(step):
        c, h = divmod(step, GLA_HEADS)
        rows = slice(c * CHUNK, (c + 1) * CHUNK)
        ks = slice(h * head_k, (h + 1) * head_k)
        decay = jnp.exp(rev_ref[t_rows + c:t_rows + c + 1, ks])
        v = pg_ref[rows, off_v + h * head_v:off_v + (h + 1) * head_v]
        inc = lax.dot_general(v, kdec_ref[rows, ks], (((0,), (0,)), ((), ())),
                              preferred_element_type=F32)
        s_new = state_ref[h] * decay + inc
        state_ref[h] = s_new
        sbf_ref[h] = s_new.astype(BF16)

    def readout(step):
        c, h = divmod(step, GLA_HEADS)
        rows = slice(c * CHUNK, (c + 1) * CHUNK)
        q = pg_ref[rows, off_q + h * head_k:off_q + (h + 1) * head_k]
        o = lax.dot_general(q, sbf_ref[h], (((1,), (1,)), ((), ())),
                            preferred_element_type=F32) * (head_k ** -0.5)
        ms = jnp.mean(o * o, axis=-1, keepdims=True)
        o = o * lax.rsqrt(ms + EPS) * gng_ref[h:h + 1, :]
        r = pg_ref[rows, off_r + h * head_v:off_r + (h + 1) * head_v].astype(F32)
        yg_ref[rows, h * head_v:(h + 1) * head_v] = (o * _silu(r)).astype(yg_ref.dtype)

    def token_of(y):
        bits = pltpu.bitcast(y, jnp.uint32)
        token = bits[:, :LANES]
        for l0 in range(LANES, bits.shape[1], LANES):
            token = token | bits[:, l0:l0 + LANES]
        return token

    def norm_strip(src, dst, rows):
        x = src[rows, :]
        ms = jnp.mean(x * x, axis=-1, keepdims=True)
        h = (x * lax.rsqrt(ms + EPS) * ng_ref[...]).astype(BF16)
        dst[rows, :] = h
        return token_of(h)

    def pin(tokens):
        acc = tokens[0]
        for t in tokens[1:]:
            acc = acc | t
        zero = pltpu.bitcast((acc >> 16) >> 16, F32)
        zero = jnp.concatenate([zero] * (STRIP // zero.shape[0]), axis=0)
        tile = h_ref[0:STRIP, 0:LANES].astype(F32)
        h_ref[0:STRIP, 0:LANES] = (tile + zero).astype(BF16)

    def proj_piece(dst, dst_col0, c0, width):
        def piece():
            g0 = dst_col0 + c0
            if g0 < w_ref.shape[0] * W_ROWS:
                w = w_ref[g0 // W_ROWS, :, g0 % W_ROWS:g0 % W_ROWS + width]
            else:
                w = w_tail_ref[...]
            dst[:, c0:c0 + width] = jnp.dot(h_ref[...], w, preferred_element_type=F32).astype(dst.dtype)
        return piece

    def pieces(dst, dst_col0, off, width):
        return [proj_piece(dst, dst_col0, c0, min(MXU_N, off + width - c0))
                for c0 in range(off, off + width, MXU_N)]

    @pl.when(s == 0)
    def _():
        pc_ref[...] = jnp.zeros_like(pc_ref)

        n_full = in_cols // W_ROWS

        def chunk_copy(i, rows=W_ROWS):
            slot = i % W_SLOTS
            return pltpu.make_async_copy(w_hbm.at[pl.ds(i * W_ROWS, rows), :],
                                         stage_ref.at[slot, pl.ds(0, rows), :], dma_sem.at[slot])

        def transpose_chunk(slot, rows, dst):
            for k0 in range(0, d_model, MXU_N):
                blk = stage_ref[slot, 0:rows, k0:k0 + MXU_N]
                dst[k0:k0 + MXU_N, :] = blk.T.astype(BF16)

        for i in range(W_SLOTS - 1):
            chunk_copy(i).start()

        def load_chunk(i, c):
            @pl.when(i + W_SLOTS - 1 < n_full)
            def _():
                chunk_copy(i + W_SLOTS - 1).start()
            chunk_copy(i).wait()
            transpose_chunk(i % W_SLOTS, W_ROWS, w_ref.at[i])
            return c

        lax.fori_loop(0, n_full, load_chunk, 0)

        tail_rows = in_cols - n_full * W_ROWS
        stage_ref[n_full % W_SLOTS, tail_rows:LANES, :] = jnp.zeros((LANES - tail_rows, d_model), F32)
        chunk_copy(n_full, tail_rows).start()
        chunk_copy(n_full, tail_rows).wait()
        transpose_chunk(n_full % W_SLOTS, LANES, w_tail_ref)

        x0_copy = pltpu.make_async_copy(x_hbm.at[pl.ds(0, t_rows), :],
                                        stage_ref.at[0, pl.ds(0, t_rows), :], dma_sem.at[0])
        x0_copy.start()
        x0_copy.wait()

        def norm_first(i, c):
            norm_strip(stage_ref.at[0], hn_ref, pl.ds(pl.multiple_of(i * STRIP, STRIP), STRIP))
            return c

        lax.fori_loop(0, n_strips, norm_first, 0, unroll=2)

    @pl.when((s == 0) | ((s + tiles_per_seq - 1) % tiles_per_seq == 0))
    def _():
        carry_ref[...] = jnp.zeros_like(carry_ref)

    @pl.when(s < n_tiles)
    def _():
        @pl.when(s % tiles_per_seq == 0)
        def _():
            state_ref[...] = jnp.zeros_like(state_ref)

        wo_bf_ref[...] = wo_ref[...].astype(BF16)

        h_ref[...] = hn_ref[...]

        norm_units = [functools.partial(norm_strip, x_ref, hn_ref, strip_rows(i))
                      for i in range(n_strips)]
        conv_units = [functools.partial(conv_strip, strip_rows(i)) for i in range(n_strips)]
        gate_units = [functools.partial(gate_strip, strip_rows(i)) for i in range(n_strips)]
        kdec_units = [functools.partial(kdec_strip, strip_rows(i)) for i in range(n_strips)]
        n_steps = n_chunks * GLA_HEADS

        mxu = [p for off, width in ((off_gd, LANES), (off_k, dk_total), (off_q, dk_total),
                                    (off_v, gla_width), (off_r, gla_width))
               for p in pieces(pg_ref, gla_col0, off, width)]
        n_k = dk_total // MXU_N
        vpu = [conv_units[:2]]
        vpu += _spread([gate_pre] + gate_units, n_k)
        vpu += [[later_sums, conv_units[2]]] + [kdec_units + [conv_units[3]]]
        vpu += [[] for _ in range(dk_total // MXU_N - 2)]
        rest = _spread(conv_units[4:], len(mxu) - len(vpu))
        rest[-1].append(functools.partial(state_update, 0))
        vpu += rest
        assert len(vpu) == len(mxu)

        mxu2 = pieces(pc_ref, 0, off_h, 4 * conv_width)
        halves = []
        for step in range(n_steps):
            if step + 1 < n_steps:
                halves.append(functools.partial(state_update, step + 1))
            halves.append(functools.partial(readout, step))
        busy = len(mxu2) - IDLE_TAIL_PIECES
        vpu2 = [a + b for a, b in zip(_spread(halves, busy), _spread(norm_units, busy))]
        vpu2 += [[] for _ in range(IDLE_TAIL_PIECES)]

        tokens = []
        for piece, units in zip(mxu + mxu2, vpu + vpu2):
            if len(tokens) >= PIN_LAG and tokens[-PIN_LAG]:
                pin(tokens[-PIN_LAG])
            piece()
            tokens.append([t for t in [unit() for unit in units] if t is not None])

    @pl.when(s == n_tiles)
    def _():
        def step(i, c):
            conv_strip(pl.ds(pl.multiple_of(i * STRIP, STRIP), STRIP))
            return c
        lax.fori_loop(0, n_strips, step, 0, unroll=2)


def _proj_mix(x2d, norm_g, w_in_t, conv_w, conv_b, w_up_pad, b_gate, gla_norm_g, w_out, *,
              batch, seq, conv_width, dk_total, gla_width, n_cols):
    t = MIX_T
    nt = seq // t
    n_tiles = batch * nt
    d_model = x2d.shape[1]
    wo_rows = w_out.shape[0] // n_tiles
    assert w_out.shape[0] % n_tiles == 0 and wo_rows % STRIP == 0 and t <= W_ROWS
    main_cols = n_cols - LANES
    assert main_cols % W_ROWS == 0 and W_ROWS % MXU_N == 0 and 0 < w_in_t.shape[0] - main_cols <= LANES
    head_k = dk_total // GLA_HEADS
    head_v = gla_width // GLA_HEADS

    def whole(shape, **kw):
        return pl.BlockSpec(shape, lambda s: (0,) * len(shape), **kw)

    kern = functools.partial(_proj_mix_kernel, n_tiles=n_tiles, tiles_per_seq=nt,
                             conv_width=conv_width, dk_total=dk_total, gla_width=gla_width)
    return pl.pallas_call(
        kern,
        grid=(n_tiles + 1,),
        in_specs=[
            pl.BlockSpec((t, d_model), lambda s: (jnp.minimum(s + 1, n_tiles - 1), 0)),
            pl.BlockSpec(memory_space=pl.ANY),
            whole(norm_g.shape),
            pl.BlockSpec(memory_space=pl.ANY),
            whole(conv_w.shape), whole(conv_b.shape), whole(w_up_pad.shape), whole(b_gate.shape),
            whole(gla_norm_g.shape),
            pl.BlockSpec((wo_rows, w_out.shape[1]), lambda s: (jnp.minimum(s, n_tiles - 1), 0)),
        ],
        out_specs=[
            pl.BlockSpec((t, conv_width), lambda s: (jnp.maximum(s - 1, 0), 0)),
            pl.BlockSpec((t, gla_width), lambda s: (jnp.minimum(s, n_tiles - 1), 0)),
            pl.BlockSpec((wo_rows, w_out.shape[1]), lambda s: (jnp.minimum(s, n_tiles - 1), 0)),
        ],
        out_shape=[jax.ShapeDtypeStruct((batch * seq, conv_width), BF16),
                   jax.ShapeDtypeStruct((batch * seq, gla_width), BF16),
                   jax.ShapeDtypeStruct(w_out.shape, BF16)],
        scratch_shapes=[pltpu.VMEM((main_cols // W_ROWS, d_model, W_ROWS), BF16),
                        pltpu.VMEM((d_model, LANES), BF16),
                        pltpu.VMEM((W_SLOTS, W_ROWS, d_model), F32),
                        pltpu.SemaphoreType.DMA((W_SLOTS,)),
                        pltpu.VMEM((t, d_model), BF16),
                        pltpu.VMEM((t, d_model), BF16),
                        pltpu.VMEM((t, 4 * conv_width), F32),
                        pltpu.VMEM((t, n_cols - 4 * conv_width), BF16),
                        pltpu.VMEM((8, conv_width), F32),
                        pltpu.VMEM((GLA_HEADS, head_v, head_k), F32),
                        pltpu.VMEM((GLA_HEADS, head_v, head_k), BF16),
                        pltpu.VMEM((t, dk_total), F32),
                        pltpu.VMEM((t, 2 * dk_total), BF16),
                        pltpu.VMEM((t + 8, dk_total), F32),
                        pltpu.VMEM((t, dk_total), BF16)],
        compiler_params=pltpu.CompilerParams(
            dimension_semantics=("arbitrary",),
            vmem_limit_bytes=VMEM_LIMIT),
        name="proj_mix",
    )(x2d, x2d, norm_g, w_in_t, conv_w, conv_b, w_up_pad, b_gate, gla_norm_g, w_out)


def _out_proj_kernel(yc_ref, yg_ref, w_ref, x_ref, g_ref, o_ref, *, final_norm):
    kc = yc_ref.shape[1]
    z = (x_ref[...] + jnp.dot(yc_ref[...], w_ref[:kc, :], preferred_element_type=F32)
         + jnp.dot(yg_ref[...], w_ref[kc:, :], preferred_element_type=F32))
    if final_norm:
        ms = jnp.mean(z * z, axis=-1, keepdims=True)
        z = z * lax.rsqrt(ms + EPS) * g_ref[...]
    o_ref[...] = z


def _out_proj(y_conv, y_gla, w_out_bf16, x2d, final_g, *, final_norm):
    m, d = x2d.shape
    return pl.pallas_call(
        functools.partial(_out_proj_kernel, final_norm=final_norm),
        grid=(m // OUT_TM,),
        in_specs=[
            pl.BlockSpec((OUT_TM, y_conv.shape[1]), lambda i: (i, 0)),
            pl.BlockSpec((OUT_TM, y_gla.shape[1]), lambda i: (i, 0)),
            pl.BlockSpec(w_out_bf16.shape, lambda i: (0, 0)),
            pl.BlockSpec((OUT_TM, d), lambda i: (i, 0)),
            pl.BlockSpec((1, d), lambda i: (0, 0)),
        ],
        out_specs=pl.BlockSpec((OUT_TM, d), lambda i: (i, 0)),
        out_shape=jax.ShapeDtypeStruct((m, d), F32),
        compiler_params=pltpu.CompilerParams(
            dimension_semantics=("arbitrary",),
            vmem_limit_bytes=VMEM_LIMIT),
        name="out_proj",
    )(y_conv, y_gla, w_out_bf16, x2d, final_g)


def kernel(x, norm_g, w_in, conv_w, conv_b, gla_w_up, gla_b_gate, gla_norm_g, w_out, final_g):
    batch, seq, d_model = x.shape
    depth = norm_g.shape[0]
    conv_width = conv_w.shape[2]
    rank, dk_total = gla_w_up.shape[1], gla_w_up.shape[2]
    gla_width = gla_norm_g.shape[1] * gla_norm_g.shape[2]
    in_cols = w_in.shape[2]
    main_cols = in_cols - rank
    assert main_cols == 4 * conv_width + 2 * dk_total + 2 * gla_width
    assert main_cols % MXU_N == 0 and dk_total % MXU_N == 0 and rank <= LANES
    assert seq % MIX_T == 0 and MIX_T % CHUNK == 0 and (batch * seq) % OUT_TM == 0
    n_pad = main_cols + LANES

    x2d = x.reshape(batch * seq, d_model)
    for l in range(depth):
        w_up_p = jnp.pad(gla_w_up[l], ((0, LANES - rank), (0, 0))).astype(BF16)
        y_conv, y_gla, w_out_bf16 = _proj_mix(
            x2d, norm_g[l][None, :], w_in[l].T, conv_w[l], conv_b[l][None, :], w_up_p,
            gla_b_gate[l][None, :], gla_norm_g[l], w_out[l], batch=batch, seq=seq,
            conv_width=conv_width, dk_total=dk_total, gla_width=gla_width, n_cols=n_pad)
        x2d = _out_proj(y_conv, y_gla, w_out_bf16, x2d, final_g[None, :],
                        final_norm=(l == depth - 1))
    return x2d.reshape(batch, seq, d_model)
```

```python
import functools

import jax
import jax.numpy as jnp
from jax import lax
from jax.experimental import pallas as pl
from jax.experimental.pallas import tpu as pltpu

F32 = jnp.float32
BF16 = jnp.bfloat16

LANES = 128
MXU_N = 256
EPS = 1e-6
CHUNK = 64
GLA_HEADS = 4
GLA_TAU = 16.0

MIX_T = 256
STRIP = 16
COL_CHUNK = 512
W_ROWS = 256
W_SLOTS = 4
IDLE_TAIL_PIECES = 1
PIN_LAG = 1
OUT_TM = 512
VMEM_LIMIT = 58 * 1024 * 1024


def _silu(v):
    return v * jax.nn.sigmoid(v)


def _log_sigmoid(v):
    return -(jnp.maximum(-v, 0.0) + jnp.log1p(jnp.exp(-jnp.abs(v))))


def _spread(units, n_slots):
    out = [[] for _ in range(n_slots)]
    for j, u in enumerate(units):
        out[(j * n_slots) // len(units)].append(u)
    return out


def _proj_mix_kernel(x_ref, x_hbm, ng_ref, w_hbm, convw_ref, convb_ref, wup_ref, bgate_ref, gng_ref, wo_ref,
                     yc_ref, yg_ref, wo_bf_ref,
                     w_ref, w_tail_ref, stage_ref, dma_sem, h_ref, hn_ref, pc_ref, pg_ref, carry_ref, state_ref, sbf_ref,
                     glog_ref, hilo_ref, rev_ref, kdec_ref,
                     *, n_tiles, tiles_per_seq, conv_width, dk_total, gla_width):
    t_rows = x_ref.shape[0]
    in_cols, d_model = w_hbm.shape
    n_strips = t_rows // STRIP
    n_chunks = t_rows // CHUNK
    head_k = dk_total // GLA_HEADS
    head_v = gla_width // GLA_HEADS
    off_h, off_b, off_c, off_z = (i * conv_width for i in range(4))
    gla_col0 = 4 * conv_width
    off_q = 0
    off_k = off_q + dk_total
    off_v = off_k + dk_total
    off_r = off_v + gla_width
    off_gd = off_r + gla_width
    s = pl.program_id(0)

    def strip_rows(i):
        return slice(i * STRIP, (i + 1) * STRIP)

    def conv_strip(rows):
        token = None
        for c0 in range(0, conv_width, COL_CHUNK):
            cols = slice(c0, c0 + COL_CHUNK)

            def col(off):
                return pc_ref[rows, off + c0:off + c0 + COL_CHUNK]

            u = col(off_c) * col(off_h)
            ext = jnp.concatenate([carry_ref[:, cols], u], axis=0)
            u1 = pltpu.roll(ext, 1, 0)[8:, :]
            u2 = pltpu.roll(ext, 2, 0)[8:, :]
            carry_ref[:, cols] = u[STRIP - 8:, :]
            conv = (convb_ref[:, cols] + convw_ref[0:1, cols] * u2 + convw_ref[1:2, cols] * u1
                    + convw_ref[2:3, cols] * u)
            y = (col(off_b) * conv * _silu(col(off_z))).astype(yc_ref.dtype)
            yc_ref[rows, cols] = y
            token = token_of(y) if token is None else token | token_of(y)
        return token

    def gate_pre():
        glog_ref[...] = jnp.dot(pg_ref[:, off_gd:off_gd + LANES], wup_ref[...],
                                preferred_element_type=F32)

    def gate_strip(rows):
        glog = _log_sigmoid(glog_ref[rows, :] + bgate_ref[...]) / GLA_TAU
        g_hi = glog.astype(BF16)
        hilo_ref[rows, :dk_total] = g_hi
        hilo_ref[rows, dk_total:] = (glog - g_hi.astype(F32)).astype(BF16)

    def later_sums():
        r_i = lax.broadcasted_iota(jnp.int32, (t_rows + 8, t_rows), 0)
        c_i = lax.broadcasted_iota(jnp.int32, (t_rows + 8, t_rows), 1)
        c_chunk = c_i // CHUNK
        ones = ((c_i > r_i) & (c_chunk == r_i // CHUNK)) | (c_chunk == r_i - t_rows)
        sums = jnp.dot(jnp.where(ones, 1.0, 0.0).astype(BF16), hilo_ref[...],
                       preferred_element_type=F32)
        rev_ref[...] = sums[:, :dk_total] + sums[:, dk_total:]

    def kdec_strip(rows):
        k = pg_ref[rows, off_k:off_k + dk_total].astype(F32)
        kdec_ref[rows, :] = (k * jnp.exp(rev_ref[rows, :])).astype(BF16)

    def state_update(step):
        c, h = divmod(step, GLA_HEADS)
        rows = slice(c * CHUNK, (c + 1) * CHUNK)
        ks = slice(h * head_k, (h + 1) * head_k)
        decay = jnp.exp(rev_ref[t_rows + c:t_rows + c + 1, ks])
        v = pg_ref[rows, off_v + h * head_v:off_v + (h + 1) * head_v]
        inc = lax.dot_general(v, kdec_ref[rows, ks], (((0,), (0,)), ((), ())),
                              preferred_element_type=F32)
        s_new = state_ref[h] * decay + inc
        state_ref[h] = s_new
        sbf_ref[h] = s_new.astype(BF16)

    def readout(step):
        c, h = divmod(step, GLA_HEADS)
        rows = slice(c * CHUNK, (c + 1) * CHUNK)
        q = pg_ref[rows, off_q + h * head_k:off_q + (h + 1) * head_k]
        o = lax.dot_general(q, sbf_ref[h], (((1,), (1,)), ((), ())),
                            preferred_element_type=F32) * (head_k ** -0.5)
        ms = jnp.mean(o * o, axis=-1, keepdims=True)
        o = o * lax.rsqrt(ms + EPS) * gng_ref[h:h + 1, :]
        r = pg_ref[rows, off_r + h * head_v:off_r + (h + 1) * head_v].astype(F32)
        yg_ref[rows, h * head_v:(h + 1) * head_v] = (o * _silu(r)).astype(yg_ref.dtype)

    def token_of(y):
        bits = pltpu.bitcast(y, jnp.uint32)
        token = bits[:, :LANES]
        for l0 in range(LANES, bits.shape[1], LANES):
            token = token | bits[:, l0:l0 + LANES]
        return token

    def norm_strip(src, dst, rows):
        x = src[rows, :]
        ms = jnp.mean(x * x, axis=-1, keepdims=True)
        h = (x * lax.rsqrt(ms + EPS) * ng_ref[...]).astype(BF16)
        dst[rows, :] = h
        return token_of(h)

    def pin(tokens):
        acc = tokens[0]
        for t in tokens[1:]:
            acc = acc | t
        zero = pltpu.bitcast((acc >> 16) >> 16, F32)
        zero = jnp.concatenate([zero] * (STRIP // zero.shape[0]), axis=0)
        tile = h_ref[0:STRIP, 0:LANES].astype(F32)
        h_ref[0:STRIP, 0:LANES] = (tile + zero).astype(BF16)

    def proj_piece(dst, dst_col0, c0, width):
        def piece():
            g0 = dst_col0 + c0
            if g0 < w_ref.shape[0] * W_ROWS:
                w = w_ref[g0 // W_ROWS, :, g0 % W_ROWS:g0 % W_ROWS + width]
            else:
                w = w_tail_ref[...]
            dst[:, c0:c0 + width] = jnp.dot(h_ref[...], w, preferred_element_type=F32).astype(dst.dtype)
        return piece

    def pieces(dst, dst_col0, off, width):
        return [proj_piece(dst, dst_col0, c0, min(MXU_N, off + width - c0))
                for c0 in range(off, off + width, MXU_N)]

    @pl.when(s == 0)
    def _():
        pc_ref[...] = jnp.zeros_like(pc_ref)

        n_full = in_cols // W_ROWS

        def chunk_copy(i, rows=W_ROWS):
            slot = i % W_SLOTS
            return pltpu.make_async_copy(w_hbm.at[pl.ds(i * W_ROWS, rows), :],
                                         stage_ref.at[slot, pl.ds(0, rows), :], dma_sem.at[slot])

        def transpose_chunk(slot, rows, dst):
            for k0 in range(0, d_model, MXU_N):
                blk = stage_ref[slot, 0:rows, k0:k0 + MXU_N]
                dst[k0:k0 + MXU_N, :] = blk.T.astype(BF16)

        for i in range(W_SLOTS - 1):
            chunk_copy(i).start()

        def load_chunk(i, c):
            @pl.when(i + W_SLOTS - 1 < n_full)
            def _():
                chunk_copy(i + W_SLOTS - 1).start()
            chunk_copy(i).wait()
            transpose_chunk(i % W_SLOTS, W_ROWS, w_ref.at[i])
            return c

        lax.fori_loop(0, n_full, load_chunk, 0)

        tail_rows = in_cols - n_full * W_ROWS
        stage_ref[n_full % W_SLOTS, tail_rows:LANES, :] = jnp.zeros((LANES - tail_rows, d_model), F32)
        chunk_copy(n_full, tail_rows).start()
        chunk_copy(n_full, tail_rows).wait()
        transpose_chunk(n_full % W_SLOTS, LANES, w_tail_ref)

        x0_copy = pltpu.make_async_copy(x_hbm.at[pl.ds(0, t_rows), :],
                                        stage_ref.at[0, pl.ds(0, t_rows), :], dma_sem.at[0])
        x0_copy.start()
        x0_copy.wait()

        def norm_first(i, c):
            norm_strip(stage_ref.at[0], hn_ref, pl.ds(pl.multiple_of(i * STRIP, STRIP), STRIP))
            return c

        lax.fori_loop(0, n_strips, norm_first, 0, unroll=2)

    @pl.when((s == 0) | ((s + tiles_per_seq - 1) % tiles_per_seq == 0))
    def _():
        carry_ref[...] = jnp.zeros_like(carry_ref)

    @pl.when(s < n_tiles)
    def _():
        @pl.when(s % tiles_per_seq == 0)
        def _():
            state_ref[...] = jnp.zeros_like(state_ref)

        wo_bf_ref[...] = wo_ref[...].astype(BF16)

        h_ref[...] = hn_ref[...]

        norm_units = [functools.partial(norm_strip, x_ref, hn_ref, strip_rows(i))
                      for i in range(n_strips)]
        conv_units = [functools.partial(conv_strip, strip_rows(i)) for i in range(n_strips)]
        gate_units = [functools.partial(gate_strip, strip_rows(i)) for i in range(n_strips)]
        kdec_units = [functools.partial(kdec_strip, strip_rows(i)) for i in range(n_strips)]
        n_steps = n_chunks * GLA_HEADS

        mxu = [p for off, width in ((off_gd, LANES), (off_k, dk_total), (off_q, dk_total),
                                    (off_v, gla_width), (off_r, gla_width))
               for p in pieces(pg_ref, gla_col0, off, width)]
        n_k = dk_total // MXU_N
        vpu = [conv_units[:2]]
        vpu += _spread([gate_pre] + gate_units, n_k)
        vpu += [[later_sums, conv_units[2]]] + [kdec_units + [conv_units[3]]]
        vpu += [[] for _ in range(dk_total // MXU_N - 2)]
        rest = _spread(conv_units[4:], len(mxu) - len(vpu))
        rest[-1].append(functools.partial(state_update, 0))
        vpu += rest
        assert len(vpu) == len(mxu)

        mxu2 = pieces(pc_ref, 0, off_h, 4 * conv_width)
        halves = []
        for step in range(n_steps):
            if step + 1 < n_steps:
                halves.append(functools.partial(state_update, step + 1))
            halves.append(functools.partial(readout, step))
        busy = len(mxu2) - IDLE_TAIL_PIECES
        vpu2 = [a + b for a, b in zip(_spread(halves, busy), _spread(norm_units, busy))]
        vpu2 += [[] for _ in range(IDLE_TAIL_PIECES)]

        tokens = []
        for piece, units in zip(mxu + mxu2, vpu + vpu2):
            if len(tokens) >= PIN_LAG and tokens[-PIN_LAG]:
                pin(tokens[-PIN_LAG])
            piece()
            tokens.append([t for t in [unit() for unit in units] if t is not None])

    @pl.when(s == n_tiles)
    def _():
        def step(i, c):
            conv_strip(pl.ds(pl.multiple_of(i * STRIP, STRIP), STRIP))
            return c
        lax.fori_loop(0, n_strips, step, 0, unroll=2)


def _proj_mix(x2d, norm_g, w_in_t, conv_w, conv_b, w_up_pad, b_gate, gla_norm_g, w_out, *,
              batch, seq, conv_width, dk_total, gla_width, n_cols):
    t = MIX_T
    nt = seq // t
    n_tiles = batch * nt
    d_model = x2d.shape[1]
    wo_rows = w_out.shape[0] // n_tiles
    assert w_out.shape[0] % n_tiles == 0 and wo_rows % STRIP == 0 and t <= W_ROWS
    main_cols = n_cols - LANES
    assert main_cols % W_ROWS == 0 and W_ROWS % MXU_N == 0 and 0 < w_in_t.shape[0] - main_cols <= LANES
    head_k = dk_total // GLA_HEADS
    head_v = gla_width // GLA_HEADS

    def whole(shape, **kw):
        return pl.BlockSpec(shape, lambda s: (0,) * len(shape), **kw)

    kern = functools.partial(_proj_mix_kernel, n_tiles=n_tiles, tiles_per_seq=nt,
                             conv_width=conv_width, dk_total=dk_total, gla_width=gla_width)
    return pl.pallas_call(
        kern,
        grid=(n_tiles + 1,),
        in_specs=[
            pl.BlockSpec((t, d_model), lambda s: (jnp.minimum(s + 1, n_tiles - 1), 0)),
            pl.BlockSpec(memory_space=pl.ANY),
            whole(norm_g.shape),
            pl.BlockSpec(memory_space=pl.ANY),
            whole(conv_w.shape), whole(conv_b.shape), whole(w_up_pad.shape), whole(b_gate.shape),
            whole(gla_norm_g.shape),
            pl.BlockSpec((wo_rows, w_out.shape[1]), lambda s: (jnp.minimum(s, n_tiles - 1), 0)),
        ],
        out_specs=[
            pl.BlockSpec((t, conv_width), lambda s: (jnp.maximum(s - 1, 0), 0)),
            pl.BlockSpec((t, gla_width), lambda s: (jnp.minimum(s, n_tiles - 1), 0)),
            pl.BlockSpec((wo_rows, w_out.shape[1]), lambda s: (jnp.minimum(s, n_tiles - 1), 0)),
        ],
        out_shape=[jax.ShapeDtypeStruct((batch * seq, conv_width), BF16),
                   jax.ShapeDtypeStruct((batch * seq, gla_width), BF16),
                   jax.ShapeDtypeStruct(w_out.shape, BF16)],
        scratch_shapes=[pltpu.VMEM((main_cols // W_ROWS, d_model, W_ROWS), BF16),
                        pltpu.VMEM((d_model, LANES), BF16),
                        pltpu.VMEM((W_SLOTS, W_ROWS, d_model), F32),
                        pltpu.SemaphoreType.DMA((W_SLOTS,)),
                        pltpu.VMEM((t, d_model), BF16),
                        pltpu.VMEM((t, d_model), BF16),
                        pltpu.VMEM((t, 4 * conv_width), F32),
                        pltpu.VMEM((t, n_cols - 4 * conv_width), BF16),
                        pltpu.VMEM((8, conv_width), F32),
                        pltpu.VMEM((GLA_HEADS, head_v, head_k), F32),
                        pltpu.VMEM((GLA_HEADS, head_v, head_k), BF16),
                        pltpu.VMEM((t, dk_total), F32),
                        pltpu.VMEM((t, 2 * dk_total), BF16),
                        pltpu.VMEM((t + 8, dk_total), F32),
                        pltpu.VMEM((t, dk_total), BF16)],
        compiler_params=pltpu.CompilerParams(
            dimension_semantics=("arbitrary",),
            vmem_limit_bytes=VMEM_LIMIT),
        name="proj_mix",
    )(x2d, x2d, norm_g, w_in_t, conv_w, conv_b, w_up_pad, b_gate, gla_norm_g, w_out)


def _out_proj_kernel(yc_ref, yg_ref, w_ref, x_ref, g_ref, o_ref, *, final_norm):
    kc = yc_ref.shape[1]
    z = (x_ref[...] + jnp.dot(yc_ref[...], w_ref[:kc, :], preferred_element_type=F32)
         + jnp.dot(yg_ref[...], w_ref[kc:, :], preferred_element_type=F32))
    if final_norm:
        ms = jnp.mean(z * z, axis=-1, keepdims=True)
        z = z * lax.rsqrt(ms + EPS) * g_ref[...]
    o_ref[...] = z


def _out_proj(y_conv, y_gla, w_out_bf16, x2d, final_g, *, final_norm):
    m, d = x2d.shape
    return pl.pallas_call(
        functools.partial(_out_proj_kernel, final_norm=final_norm),
        grid=(m // OUT_TM,),
        in_specs=[
            pl.BlockSpec((OUT_TM, y_conv.shape[1]), lambda i: (i, 0)),
            pl.BlockSpec((OUT_TM, y_gla.shape[1]), lambda i: (i, 0)),
            pl.BlockSpec(w_out_bf16.shape, lambda i: (0, 0)),
            pl.BlockSpec((OUT_TM, d), lambda i: (i, 0)),
            pl.BlockSpec((1, d), lambda i: (0, 0)),
        ],
        out_specs=pl.BlockSpec((OUT_TM, d), lambda i: (i, 0)),
        out_shape=jax.ShapeDtypeStruct((m, d), F32),
        compiler_params=pltpu.CompilerParams(
            dimension_semantics=("arbitrary",),
            vmem_limit_bytes=VMEM_LIMIT),
        name="out_proj",
    )(y_conv, y_gla, w_out_bf16, x2d, final_g)


def kernel(x, norm_g, w_in, conv_w, conv_b, gla_w_up, gla_b_gate, gla_norm_g, w_out, final_g):
    batch, seq, d_model = x.shape
    depth = norm_g.shape[0]
    conv_width = conv_w.shape[2]
    rank, dk_total = gla_w_up.shape[1], gla_w_up.shape[2]
    gla_width = gla_norm_g.shape[1] * gla_norm_g.shape[2]
    in_cols = w_in.shape[2]
    main_cols = in_cols - rank
    assert main_cols == 4 * conv_width + 2 * dk_total + 2 * gla_width
    assert main_cols % MXU_N == 0 and dk_total % MXU_N == 0 and rank <= LANES
    assert seq % MIX_T == 0 and MIX_T % CHUNK == 0 and (batch * seq) % OUT_TM == 0
    n_pad = main_cols + LANES

    x2d = x.reshape(batch * seq, d_model)
    for l in range(depth):
        w_up_p = jnp.pad(gla_w_up[l], ((0, LANES - rank), (0, 0))).astype(BF16)
        y_conv, y_gla, w_out_bf16 = _proj_mix(
            x2d, norm_g[l][None, :], w_in[l].T, conv_w[l], conv_b[l][None, :], w_up_p,
            gla_b_gate[l][None, :], gla_norm_g[l], w_out[l], batch=batch, seq=seq,
            conv_width=conv_width, dk_total=dk_total, gla_width=gla_width, n_cols=n_pad)
        x2d = _out_proj(y_conv, y_gla, w_out_bf16, x2d, final_g[None, :],
                        final_norm=(l == depth - 1))
    return x2d.reshape(batch, seq, d_model)
```

```python
import functools

import jax
import jax.numpy as jnp
from jax import lax
from jax.experimental import pallas as pl
from jax.experimental.pallas import tpu as pltpu

F32 = jnp.float32
BF16 = jnp.bfloat16

LANES = 128
MXU_N = 256
EPS = 1e-6
CHUNK = 64
GLA_HEADS = 4
GLA_TAU = 16.0

MIX_T = 256
STRIP = 16
COL_CHUNK = 256
W_ROWS = 256
W_SLOTS = 4
IDLE_TAIL_PIECES = 1
PIN_LAG = 1
OUT_TM = 512
VMEM_LIMIT = 58 * 1024 * 1024


def _silu(v):
    return v * jax.nn.sigmoid(v)


def _log_sigmoid(v):
    return -(jnp.maximum(-v, 0.0) + jnp.log1p(jnp.exp(-jnp.abs(v))))


def _spread(units, n_slots):
    out = [[] for _ in range(n_slots)]
    for j, u in enumerate(units):
        out[(j * n_slots) // len(units)].append(u)
    return out


def _proj_mix_kernel(x_ref, x_hbm, ng_ref, w_hbm, convw_ref, convb_ref, wup_ref, bgate_ref, gng_ref, wo_ref,
                     yc_ref, yg_ref, wo_bf_ref,
                     w_ref, w_tail_ref, stage_ref, dma_sem, h_ref, hn_ref, pc_ref, pg_ref, carry_ref, state_ref, sbf_ref,
                     glog_ref, hilo_ref, rev_ref, kdec_ref, wupb_ref,
                     *, n_tiles, tiles_per_seq, conv_width, dk_total, gla_width):
    t_rows = x_ref.shape[0]
    in_cols, d_model = w_hbm.shape
    n_strips = t_rows // STRIP
    n_chunks = t_rows // CHUNK
    head_k = dk_total // GLA_HEADS
    head_v = gla_width // GLA_HEADS
    off_h, off_b, off_c, off_z = (i * conv_width for i in range(4))
    gla_col0 = 4 * conv_width
    off_q = 0
    off_k = off_q + dk_total
    off_v = off_k + dk_total
    off_r = off_v + gla_width
    off_gd = off_r + gla_width
    s = pl.program_id(0)

    def strip_rows(i):
        return slice(i * STRIP, (i + 1) * STRIP)

    def conv_strip(rows):
        token = None
        for c0 in range(0, conv_width, COL_CHUNK):
            cols = slice(c0, c0 + COL_CHUNK)

            def col(off):
                return pc_ref[rows, off + c0:off + c0 + COL_CHUNK]

            u = col(off_c) * col(off_h)
            ext = jnp.concatenate([carry_ref[:, cols], u], axis=0)
            u1 = pltpu.roll(ext, 1, 0)[8:, :]
            u2 = pltpu.roll(ext, 2, 0)[8:, :]
            carry_ref[:, cols] = u[STRIP - 8:, :]
            conv = (convb_ref[:, cols] + convw_ref[0:1, cols] * u2 + convw_ref[1:2, cols] * u1
                    + convw_ref[2:3, cols] * u)
            y = (col(off_b) * conv * _silu(col(off_z))).astype(yc_ref.dtype)
            yc_ref[rows, cols] = y
            token = token_of(y) if token is None else token | token_of(y)
        return token

    def gate_pre():
        glog_ref[...] = jnp.dot(pg_ref[:, off_gd:off_gd + LANES], wupb_ref[...],
                                preferred_element_type=F32)

    def gate_strip(rows):
        glog = _log_sigmoid(glog_ref[rows, :] + bgate_ref[...]) / GLA_TAU
        g_hi = glog.astype(BF16)
        hilo_ref[rows, :dk_total] = g_hi
        hilo_ref[rows, dk_total:] = (glog - g_hi.astype(F32)).astype(BF16)

    def later_sums():
        r_i = lax.broadcasted_iota(jnp.int32, (t_rows + 8, t_rows), 0)
        c_i = lax.broadcasted_iota(jnp.int32, (t_rows + 8, t_rows), 1)
        c_chunk = c_i // CHUNK
        ones = ((c_i > r_i) & (c_chunk == r_i // CHUNK)) | (c_chunk == r_i - t_rows)
        sums = jnp.dot(jnp.where(ones, 1.0, 0.0).astype(BF16), hilo_ref[...],
                       preferred_element_type=F32)
        rev_ref[...] = sums[:, :dk_total] + sums[:, dk_total:]

    def kdec_strip(rows):
        k = pg_ref[rows, off_k:off_k + dk_total].astype(F32)
        kdec_ref[rows, :] = (k * jnp.exp(rev_ref[rows, :])).astype(BF16)

    def state_update(step):
        c, h = divmod(step, GLA_HEADS)
        rows = slice(c * CHUNK, (c + 1) * CHUNK)
        ks = slice(h * head_k, (h + 1) * head_k)
        decay = jnp.exp(rev_ref[t_rows + c:t_rows + c + 1, ks])
        v = pg_ref[rows, off_v + h * head_v:off_v + (h + 1) * head_v]
        inc = lax.dot_general(v, kdec_ref[rows, ks], (((0,), (0,)), ((), ())),
                              preferred_element_type=F32)
        s_new = state_ref[h] * decay + inc
        state_ref[h] = s_new
        sbf_ref[h] = s_new.astype(BF16)

    def readout(step):
        c, h = divmod(step, GLA_HEADS)
        rows = slice(c * CHUNK, (c + 1) * CHUNK)
        q = pg_ref[rows, off_q + h * head_k:off_q + (h + 1) * head_k]
        o = lax.dot_general(q, sbf_ref[h], (((1,), (1,)), ((), ())),
                            preferred_element_type=F32) * (head_k ** -0.5)
        ms = jnp.mean(o * o, axis=-1, keepdims=True)
        o = o * lax.rsqrt(ms + EPS) * gng_ref[h:h + 1, :]
        r = pg_ref[rows, off_r + h * head_v:off_r + (h + 1) * head_v].astype(F32)
        yg_ref[rows, h * head_v:(h + 1) * head_v] = (o * _silu(r)).astype(yg_ref.dtype)

    def token_of(y):
        bits = pltpu.bitcast(y, jnp.uint32)
        token = bits[:, :LANES]
        for l0 in range(LANES, bits.shape[1], LANES):
            token = token | bits[:, l0:l0 + LANES]
        return token

    def norm_strip(src, dst, rows):
        x = src[rows, :]
        ms = jnp.mean(x * x, axis=-1, keepdims=True)
        h = (x * lax.rsqrt(ms + EPS) * ng_ref[...]).astype(BF16)
        dst[rows, :] = h
        return token_of(h)

    def pin(tokens):
        acc = tokens[0]
        for t in tokens[1:]:
            acc = acc | t
        zero = pltpu.bitcast((acc >> 16) >> 16, F32)
        zero = jnp.concatenate([zero] * (STRIP // zero.shape[0]), axis=0)
        tile = h_ref[0:STRIP, 0:LANES].astype(F32)
        h_ref[0:STRIP, 0:LANES] = (tile + zero).astype(BF16)

    def proj_piece(dst, dst_col0, c0, width):
        def piece():
            g0 = dst_col0 + c0
            if g0 < w_ref.shape[0] * W_ROWS:
                w = w_ref[g0 // W_ROWS, :, g0 % W_ROWS:g0 % W_ROWS + width]
            else:
                w = w_tail_ref[...]
            dst[:, c0:c0 + width] = jnp.dot(h_ref[...], w, preferred_element_type=F32).astype(dst.dtype)
        return piece

    def pieces(dst, dst_col0, off, width):
        return [proj_piece(dst, dst_col0, c0, min(MXU_N, off + width - c0))
                for c0 in range(off, off + width, MXU_N)]

    @pl.when(s == 0)
    def _():
        pc_ref[...] = jnp.zeros_like(pc_ref)
        wupb_ref[...] = jnp.zeros_like(wupb_ref)
        wupb_ref[0:wup_ref.shape[0], :] = wup_ref[...].astype(BF16)

        n_full = in_cols // W_ROWS

        def chunk_copy(i, rows=W_ROWS):
            slot = i % W_SLOTS
            return pltpu.make_async_copy(w_hbm.at[pl.ds(i * W_ROWS, rows), :],
                                         stage_ref.at[slot, pl.ds(0, rows), :], dma_sem.at[slot])

        def transpose_chunk(slot, rows, dst):
            for k0 in range(0, d_model, MXU_N):
                blk = stage_ref[slot, 0:rows, k0:k0 + MXU_N]
                dst[k0:k0 + MXU_N, :] = blk.T.astype(BF16)

        for i in range(W_SLOTS - 1):
            chunk_copy(i).start()

        def load_chunk(i, c):
            @pl.when(i + W_SLOTS - 1 < n_full)
            def _():
                chunk_copy(i + W_SLOTS - 1).start()
            chunk_copy(i).wait()
            transpose_chunk(i % W_SLOTS, W_ROWS, w_ref.at[i])
            return c

        lax.fori_loop(0, n_full, load_chunk, 0)

        tail_rows = in_cols - n_full * W_ROWS
        stage_ref[n_full % W_SLOTS, tail_rows:LANES, :] = jnp.zeros((LANES - tail_rows, d_model), F32)
        chunk_copy(n_full, tail_rows).start()
        chunk_copy(n_full, tail_rows).wait()
        transpose_chunk(n_full % W_SLOTS, LANES, w_tail_ref)

        x0_copy = pltpu.make_async_copy(x_hbm.at[pl.ds(0, t_rows), :],
                                        stage_ref.at[0, pl.ds(0, t_rows), :], dma_sem.at[0])
        x0_copy.start()
        x0_copy.wait()

        def norm_first(i, c):
            norm_strip(stage_ref.at[0], hn_ref, pl.ds(pl.multiple_of(i * STRIP, STRIP), STRIP))
            return c

        lax.fori_loop(0, n_strips, norm_first, 0, unroll=2)

    @pl.when((s == 0) | ((s + tiles_per_seq - 1) % tiles_per_seq == 0))
    def _():
        carry_ref[...] = jnp.zeros_like(carry_ref)

    @pl.when(s < n_tiles)
    def _():
        @pl.when(s % tiles_per_seq == 0)
        def _():
            state_ref[...] = jnp.zeros_like(state_ref)

        wo_bf_ref[...] = wo_ref[...].astype(BF16)

        h_ref[...] = hn_ref[...]

        norm_units = [functools.partial(norm_strip, x_ref, hn_ref, strip_rows(i))
                      for i in range(n_strips)]
        conv_units = [functools.partial(conv_strip, strip_rows(i)) for i in range(n_strips)]
        gate_units = [functools.partial(gate_strip, strip_rows(i)) for i in range(n_strips)]
        kdec_units = [functools.partial(kdec_strip, strip_rows(i)) for i in range(n_strips)]
        n_steps = n_chunks * GLA_HEADS

        mxu = [p for off, width in ((off_gd, LANES), (off_k, dk_total), (off_q, dk_total),
                                    (off_v, gla_width), (off_r, gla_width))
               for p in pieces(pg_ref, gla_col0, off, width)]
        n_k = dk_total // MXU_N
        vpu = [conv_units[:2]]
        vpu += _spread([gate_pre] + gate_units, n_k)
        vpu += [[later_sums, conv_units[2]]] + [kdec_units + [conv_units[3]]]
        vpu += [[] for _ in range(dk_total // MXU_N - 2)]
        rest = _spread(conv_units[4:], len(mxu) - len(vpu))
        rest[-1].append(functools.partial(state_update, 0))
        vpu += rest
        assert len(vpu) == len(mxu)

        mxu2 = pieces(pc_ref, 0, off_h, 4 * conv_width)
        halves = []
        for step in range(n_steps):
            if step + 1 < n_steps:
                halves.append(functools.partial(state_update, step + 1))
            halves.append(functools.partial(readout, step))
        busy = len(mxu2) - IDLE_TAIL_PIECES
        vpu2 = [a + b for a, b in zip(_spread(halves, busy), _spread(norm_units, busy))]
        vpu2 += [[] for _ in range(IDLE_TAIL_PIECES)]

        tokens = []
        for piece, units in zip(mxu + mxu2, vpu + vpu2):
            if len(tokens) >= PIN_LAG and tokens[-PIN_LAG]:
                pin(tokens[-PIN_LAG])
            piece()
            tokens.append([t for t in [unit() for unit in units] if t is not None])

    @pl.when(s == n_tiles)
    def _():
        def step(i, c):
            conv_strip(pl.ds(pl.multiple_of(i * STRIP, STRIP), STRIP))
            return c
        lax.fori_loop(0, n_strips, step, 0, unroll=2)


def _proj_mix(x2d, norm_g, w_in_t, conv_w, conv_b, w_up_pad, b_gate, gla_norm_g, w_out, *,
              batch, seq, conv_width, dk_total, gla_width, n_cols):
    t = MIX_T
    nt = seq // t
    n_tiles = batch * nt
    d_model = x2d.shape[1]
    wo_rows = w_out.shape[0] // n_tiles
    assert w_out.shape[0] % n_tiles == 0 and wo_rows % STRIP == 0 and t <= W_ROWS
    main_cols = n_cols - LANES
    assert main_cols % W_ROWS == 0 and W_ROWS % MXU_N == 0 and 0 < w_in_t.shape[0] - main_cols <= LANES
    head_k = dk_total // GLA_HEADS
    head_v = gla_width // GLA_HEADS

    def whole(shape, **kw):
        return pl.BlockSpec(shape, lambda s: (0,) * len(shape), **kw)

    kern = functools.partial(_proj_mix_kernel, n_tiles=n_tiles, tiles_per_seq=nt,
                             conv_width=conv_width, dk_total=dk_total, gla_width=gla_width)
    return pl.pallas_call(
        kern,
        grid=(n_tiles + 1,),
        in_specs=[
            pl.BlockSpec((t, d_model), lambda s: (jnp.minimum(s + 1, n_tiles - 1), 0)),
            pl.BlockSpec(memory_space=pl.ANY),
            whole(norm_g.shape),
            pl.BlockSpec(memory_space=pl.ANY),
            whole(conv_w.shape), whole(conv_b.shape), whole(w_up_pad.shape), whole(b_gate.shape),
            whole(gla_norm_g.shape),
            pl.BlockSpec((wo_rows, w_out.shape[1]), lambda s: (jnp.minimum(s, n_tiles - 1), 0)),
        ],
        out_specs=[
            pl.BlockSpec((t, conv_width), lambda s: (jnp.maximum(s - 1, 0), 0)),
            pl.BlockSpec((t, gla_width), lambda s: (jnp.minimum(s, n_tiles - 1), 0)),
            pl.BlockSpec((wo_rows, w_out.shape[1]), lambda s: (jnp.minimum(s, n_tiles - 1), 0)),
        ],
        out_shape=[jax.ShapeDtypeStruct((batch * seq, conv_width), BF16),
                   jax.ShapeDtypeStruct((batch * seq, gla_width), BF16),
                   jax.ShapeDtypeStruct(w_out.shape, BF16)],
        scratch_shapes=[pltpu.VMEM((main_cols // W_ROWS, d_model, W_ROWS), BF16),
                        pltpu.VMEM((d_model, LANES), BF16),
                        pltpu.VMEM((W_SLOTS, W_ROWS, d_model), F32),
                        pltpu.SemaphoreType.DMA((W_SLOTS,)),
                        pltpu.VMEM((t, d_model), BF16),
                        pltpu.VMEM((t, d_model), BF16),
                        pltpu.VMEM((t, 4 * conv_width), F32),
                        pltpu.VMEM((t, n_cols - 4 * conv_width), BF16),
                        pltpu.VMEM((8, conv_width), F32),
                        pltpu.VMEM((GLA_HEADS, head_v, head_k), F32),
                        pltpu.VMEM((GLA_HEADS, head_v, head_k), BF16),
                        pltpu.VMEM((t, dk_total), F32),
                        pltpu.VMEM((t, 2 * dk_total), BF16),
                        pltpu.VMEM((t + 8, dk_total), F32),
                        pltpu.VMEM((t, dk_total), BF16),
                        pltpu.VMEM((LANES, dk_total), BF16)],
        compiler_params=pltpu.CompilerParams(
            dimension_semantics=("arbitrary",),
            vmem_limit_bytes=VMEM_LIMIT),
        name="proj_mix",
    )(x2d, x2d, norm_g, w_in_t, conv_w, conv_b, w_up_pad, b_gate, gla_norm_g, w_out)


def _out_proj_kernel(yc_ref, yg_ref, w_ref, x_ref, g_ref, o_ref, *, final_norm):
    kc = yc_ref.shape[1]
    z = (x_ref[...] + jnp.dot(yc_ref[...], w_ref[:kc, :], preferred_element_type=F32)
         + jnp.dot(yg_ref[...], w_ref[kc:, :], preferred_element_type=F32))
    if final_norm:
        ms = jnp.mean(z * z, axis=-1, keepdims=True)
        z = z * lax.rsqrt(ms + EPS) * g_ref[...]
    o_ref[...] = z


def _out_proj(y_conv, y_gla, w_out_bf16, x2d, final_g, *, final_norm):
    m, d = x2d.shape
    return pl.pallas_call(
        functools.partial(_out_proj_kernel, final_norm=final_norm),
        grid=(m // OUT_TM,),
        in_specs=[
            pl.BlockSpec((OUT_TM, y_conv.shape[1]), lambda i: (i, 0)),
            pl.BlockSpec((OUT_TM, y_gla.shape[1]), lambda i: (i, 0)),
            pl.BlockSpec(w_out_bf16.shape, lambda i: (0, 0)),
            pl.BlockSpec((OUT_TM, d), lambda i: (i, 0)),
            pl.BlockSpec((1, d), lambda i: (0, 0)),
        ],
        out_specs=pl.BlockSpec((OUT_TM, d), lambda i: (i, 0)),
        out_shape=jax.ShapeDtypeStruct((m, d), F32),
        compiler_params=pltpu.CompilerParams(
            dimension_semantics=("arbitrary",),
            vmem_limit_bytes=VMEM_LIMIT),
        name="out_proj",
    )(y_conv, y_gla, w_out_bf16, x2d, final_g)


def kernel(x, norm_g, w_in, conv_w, conv_b, gla_w_up, gla_b_gate, gla_norm_g, w_out, final_g):
    batch, seq, d_model = x.shape
    depth = norm_g.shape[0]
    conv_width = conv_w.shape[2]
    rank, dk_total = gla_w_up.shape[1], gla_w_up.shape[2]
    gla_width = gla_norm_g.shape[1] * gla_norm_g.shape[2]
    in_cols = w_in.shape[2]
    main_cols = in_cols - rank
    assert main_cols == 4 * conv_width + 2 * dk_total + 2 * gla_width
    assert main_cols % MXU_N == 0 and dk_total % MXU_N == 0 and rank <= LANES
    assert seq % MIX_T == 0 and MIX_T % CHUNK == 0 and (batch * seq) % OUT_TM == 0
    n_pad = main_cols + LANES

    x2d = x.reshape(batch * seq, d_model)
    for l in range(depth):
        assert rank % STRIP == 0
        y_conv, y_gla, w_out_bf16 = _proj_mix(
            x2d, norm_g[l][None, :], w_in[l].T, conv_w[l], conv_b[l][None, :], gla_w_up[l],
            gla_b_gate[l][None, :], gla_norm_g[l], w_out[l], batch=batch, seq=seq,
            conv_width=conv_width, dk_total=dk_total, gla_width=gla_width, n_cols=n_pad)
        x2d = _out_proj(y_conv, y_gla, w_out_bf16, x2d, final_g[None, :],
                        final_norm=(l == depth - 1))
    return x2d.reshape(batch, seq, d_model)
```
